```python
import jax
import jax.numpy as jnp
from jax import lax
import numpy as np

D_MODEL = 1024
BATCH = 8
SEQ = 4096
DEPTH = 2

CTX_LEN = 256
GRID_W = 64

BRANCH_WIDTH = 512
N_BRANCH = 3

POOL_WINDOWS = (2, 4, 8, 16)
POOL_WIDTH = BRANCH_WIDTH
POOL_GROUP = POOL_WIDTH // len(POOL_WINDOWS)

N_Q_HEADS = 8
N_KV_HEADS = 2
HEAD_DIM = 64
GQA_GROUP = N_Q_HEADS // N_KV_HEADS
Q_WIDTH = N_Q_HEADS * HEAD_DIM
KV_WIDTH = N_KV_HEADS * HEAD_DIM
Q_BLOCK = 128
ROPE_THETA = 10000.0
ATTN_SCALE = HEAD_DIM ** -0.5

LRU_WIDTH = BRANCH_WIDTH
LRU_BLOCKS = 8
LRU_BLOCK_DIM = LRU_WIDTH // LRU_BLOCKS
CONV_WIDTH = 4
RG_C = 8.0

OFF_Q = POOL_WIDTH
OFF_K = OFF_Q + Q_WIDTH
OFF_V = OFF_K + KV_WIDTH
OFF_LX = OFF_V + KV_WIDTH
OFF_LG = OFF_LX + LRU_WIDTH
OFF_GATE = OFF_LG + LRU_WIDTH
W_IN = OFF_GATE + N_BRANCH * D_MODEL
SPLITS = (OFF_Q, OFF_K, OFF_V, OFF_LX, OFF_LG, OFF_GATE)

D_FF = 2816
N_EXPERTS = 8
TOP_K = 2
D_EXPERT = 3584
EXPERT_BLOCK = 256
N_DENSE = (DEPTH + 1) // 2
N_MOE = DEPTH // 2

ALPHA = (2 * DEPTH) ** 0.25
BETA = (8 * DEPTH) ** -0.25
LN_EPS = 1e-5
RMS_EPS = 1e-6
F32 = jnp.float32

kernel_name = 'hybrid_pool_gqa_rglru_moe_dit'


def layer_norm(x, g, b):
    xf = x.astype(F32)
    mu = jnp.mean(xf, axis=-1, keepdims=True)
    var = jnp.mean(jnp.square(xf - mu), axis=-1, keepdims=True)
    return ((xf - mu) * lax.rsqrt(var + LN_EPS) * g.astype(F32) + b.astype(F32)).astype(x.dtype)


def rms_norm(x, g):
    xf = x.astype(F32)
    return (xf * lax.rsqrt(jnp.mean(xf * xf, axis=-1, keepdims=True) + RMS_EPS) * g.astype(F32)).astype(x.dtype)


def axial_rope(x, row_pos, col_pos):
    half = HEAD_DIM // 2
    nf = half // 2
    inv = ROPE_THETA ** (-jnp.arange(nf, dtype=F32) / nf)
    xf = x.astype(F32)

    def rotate(xa, pos):
        ang = pos[:, None] * inv
        cos = jnp.cos(ang)[None, :, None, :]
        sin = jnp.sin(ang)[None, :, None, :]
        x1, x2 = xa[..., :nf], xa[..., nf:]
        return jnp.concatenate([x1 * cos - x2 * sin, x2 * cos + x1 * sin], axis=-1)

    out = jnp.concatenate([rotate(xf[..., :half], row_pos), rotate(xf[..., half:], col_pos)], axis=-1)
    return out.astype(x.dtype)


def attend(q, k, v):
    B, Lq = q.shape[:2]
    qg = q.reshape(B, Lq, N_KV_HEADS, GQA_GROUP, HEAD_DIM)
    s = jnp.einsum('bqhgd,bkhd->bhgqk', qg, k, preferred_element_type=F32) * ATTN_SCALE
    p = jax.nn.softmax(s, axis=-1).astype(v.dtype)
    o = jnp.einsum('bhgqk,bkhd->bqhgd', p, v)
    return o.reshape(B, Lq, Q_WIDTH)


def blocked_attention(q, k, v):
    B, S = q.shape[:2]
    nb = S // Q_BLOCK
    qb = jnp.moveaxis(q.reshape(B, nb, Q_BLOCK, N_Q_HEADS, HEAD_DIM), 1, 0)
    ob = lax.map(lambda qq: attend(qq, k, v), qb)
    return jnp.moveaxis(ob, 0, 1).reshape(B, S, Q_WIDTH)


def multiscale_pool(x, pool_w, pool_scale):
    B, L, _ = x.shape
    xf = x.astype(F32)
    cs = jnp.concatenate([jnp.zeros((B, 1, POOL_WIDTH), F32), jnp.cumsum(xf, axis=1)], axis=1)
    t = jnp.arange(L)
    outs = []
    for g, w in enumerate(POOL_WINDOWS):
        lo = w // 2
        start = jnp.clip(t - lo, 0, L)
        end = jnp.clip(t - lo + w, 0, L)
        sl = slice(g * POOL_GROUP, (g + 1) * POOL_GROUP)
        csg = cs[..., sl]
        mean = (jnp.take(csg, end, axis=1) - jnp.take(csg, start, axis=1)) / (end - start).astype(F32)[None, :, None]
        outs.append(jnp.einsum('blc,cd->bld', (mean - xf[..., sl]).astype(x.dtype), pool_w[g]))
    return jnp.concatenate(outs, axis=-1) * pool_scale


def centred_dwconv(x, w, b):
    L = x.shape[1]
    lo = CONV_WIDTH // 2
    hi = CONV_WIDTH - 1 - lo
    xp = jnp.pad(x, ((0, 0), (lo, hi), (0, 0)))
    return sum(xp[:, k:k + L] * w[k] for k in range(CONV_WIDTH)) + b


def _lin_combine(left, right):
    a1, b1 = left
    a2, b2 = right
    return a1 * a2, a2 * b1 + b2


def rglru(x, wa, ba, wx, bx, lam, h0, reverse):
    B, L, _ = x.shape
    xb = x.reshape(B, L, LRU_BLOCKS, LRU_BLOCK_DIM)
    r = jax.nn.sigmoid(jnp.einsum('blhi,hij->blhj', xb, wa).reshape(B, L, LRU_WIDTH).astype(F32) + ba.astype(F32))
    i = jax.nn.sigmoid(jnp.einsum('blhi,hij->blhj', xb, wx).reshape(B, L, LRU_WIDTH).astype(F32) + bx.astype(F32))
    log_a = -RG_C * r * jax.nn.softplus(-lam.astype(F32))
    a = jnp.exp(log_a)
    u = jnp.sqrt(-jnp.expm1(2.0 * log_a)) * i * x.astype(F32)
    if h0 is not None:
        edge = -1 if reverse else 0
        u = u.at[:, edge].add(a[:, edge] * h0)
    _, h = lax.associative_scan(_lin_combine, (a, u), reverse=reverse, axis=1)
    return h


def merge_branches(pool_y, attn_y, lru_y, gates, w_branch, w_out):
    ys = jnp.stack([pool_y, attn_y, lru_y], axis=2)
    branch = jnp.einsum('blnc,ncd->blnd', ys, w_branch)
    return jnp.sum(gates.astype(branch.dtype) * branch, axis=2) @ w_out


def token_mixer(h_ctx, h_lat, row_pos, col_pos, w_in, b_merge, pool_w, pool_scale, q_norm, k_norm,
                conv_w, conv_b, lru_wa, lru_ba, lru_wx, lru_bx, lru_lambda, w_branch, w_out, need_ctx):
    dtype = h_lat.dtype

    def project(h):
        B, L, _ = h.shape
        z = h @ w_in
        p_in, q, k, v, lx, lg, gz = jnp.split(z, SPLITS, axis=-1)
        q = rms_norm(q.reshape(B, L, N_Q_HEADS, HEAD_DIM), q_norm)
        k = rms_norm(k.reshape(B, L, N_KV_HEADS, HEAD_DIM), k_norm)
        v = v.reshape(B, L, N_KV_HEADS, HEAD_DIM)
        lx = centred_dwconv(lx, conv_w, conv_b)
        gates = jax.nn.sigmoid(gz.reshape(B, L, N_BRANCH, D_MODEL).astype(F32) + b_merge.astype(F32))
        return p_in, q, k, v, lx, lg, gates

    p_c, q_c, k_c, v_c, lx_c, lg_c, g_c = project(h_ctx)
    p_l, q_l, k_l, v_l, lx_l, lg_l, g_l = project(h_lat)

    q_l = axial_rope(q_l, row_pos, col_pos)
    k_l = axial_rope(k_l, row_pos, col_pos)
    k_all = jnp.concatenate([k_c, k_l], axis=1)
    v_all = jnp.concatenate([v_c, v_l], axis=1)
    attn_l = blocked_attention(q_l, k_all, v_all)

    h_cf = rglru(lx_c, lru_wa[0], lru_ba[0], lru_wx[0], lru_bx[0], lru_lambda[0], None, False)
    h_cb = rglru(lx_c, lru_wa[1], lru_ba[1], lru_wx[1], lru_bx[1], lru_lambda[1], None, True)
    h_lf = rglru(lx_l, lru_wa[0], lru_ba[0], lru_wx[0], lru_bx[0], lru_lambda[0], h_cf[:, -1], False)
    h_lb = rglru(lx_l, lru_wa[1], lru_ba[1], lru_wx[1], lru_bx[1], lru_lambda[1], h_cb[:, 0], True)
    lru_l = ((h_lf + h_lb) * jax.nn.gelu(lg_l.astype(F32))).astype(dtype)

    y_lat = merge_branches(multiscale_pool(p_l, pool_w, pool_scale), attn_l, lru_l, g_l, w_branch, w_out)
    if need_ctx:
        attn_c = attend(q_c, k_c, v_c)
        lru_c = ((h_cf + h_cb) * jax.nn.gelu(lg_c.astype(F32))).astype(dtype)
        y_ctx = merge_branches(multiscale_pool(p_c, pool_w, pool_scale), attn_c, lru_c, g_c, w_branch, w_out)
    else:
        y_ctx = None
    return y_ctx, y_lat


def swiglu(x, w_gate, w_up, w_down):
    return (jax.nn.silu(x @ w_gate) * (x @ w_up)) @ w_down


def moe_swiglu(t, router, router_b, w_gate, w_up, w_down):
    N, D = t.shape
    logits = (t @ router + router_b).astype(F32)
    top_v, top_i = lax.top_k(logits, TOP_K)
    probs = jax.nn.softmax(top_v, axis=-1)
    M = N * TOP_K
    flat_e = top_i.reshape(M)
    order = jnp.argsort(flat_e)
    sorted_e = flat_e[order]
    sorted_tok = order // TOP_K
    sorted_p = probs.reshape(M)[order]
    counts = jnp.bincount(flat_e, length=N_EXPERTS)
    padded = (counts + EXPERT_BLOCK - 1) // EXPERT_BLOCK * EXPERT_BLOCK
    start = jnp.cumsum(counts) - counts
    ends_p = jnp.cumsum(padded)
    start_p = ends_p - padded
    dest = start_p[sorted_e] + jnp.arange(M) - start[sorted_e]
    n_blocks = -(-(M + N_EXPERTS * (EXPERT_BLOCK - 1)) // EXPERT_BLOCK)
    P = n_blocks * EXPERT_BLOCK
    buf_tok = jnp.full((P,), N, jnp.int32).at[dest].set(sorted_tok.astype(jnp.int32))
    buf_p = jnp.zeros((P,), F32).at[dest].set(sorted_p)
    block_e = jnp.minimum(jnp.searchsorted(ends_p, jnp.arange(n_blocks) * EXPERT_BLOCK, side='right'), N_EXPERTS - 1)
    t_pad = jnp.concatenate([t, jnp.zeros((1, D), t.dtype)], axis=0)
    xb = t_pad[buf_tok].reshape(n_blocks, EXPERT_BLOCK, D)

    def expert_block(args):
        xe, e = args
        return swiglu(xe, w_gate[e], w_up[e], w_down[e])

    yb = lax.map(expert_block, (xb, block_e))
    y = yb.reshape(P, D) * buf_p[:, None].astype(t.dtype)
    return jnp.zeros((N + 1, D), t.dtype).at[buf_tok].add(y)[:N]


def setup_inputs(seed: int = 0) -> dict:
    key = jax.random.key(seed)
    ks = jax.random.split(key, 40)

    def nrm(i, shape, scale):
        return jax.random.normal(ks[i], shape, F32) * scale

    u = jax.random.uniform(ks[30], (DEPTH, 2, LRU_WIDTH), F32, 0.9, 0.999)
    a0 = u ** (1.0 / RG_C)
    lru_lambda = jnp.log(a0) - jnp.log1p(-a0)
    return {
        'x': nrm(0, (BATCH, SEQ, D_MODEL), 1.0),
        'c': nrm(1, (BATCH, D_MODEL), 1.0),
        'ctx': nrm(2, (BATCH, CTX_LEN, D_MODEL), 1.0),
        'c_ctx': nrm(3, (D_MODEL,), 1.0),
        'w_mod': nrm(4, (DEPTH, D_MODEL, 6 * D_MODEL), 0.5 * D_MODEL ** -0.5),
        'b_mod': nrm(5, (DEPTH, 6 * D_MODEL), 0.02),
        'w_in': nrm(6, (DEPTH, D_MODEL, W_IN), D_MODEL ** -0.5),
        'b_merge': nrm(7, (DEPTH, N_BRANCH, D_MODEL), 0.1),
        'pool_w': nrm(8, (DEPTH, len(POOL_WINDOWS), POOL_GROUP, POOL_GROUP), POOL_GROUP ** -0.5),
        'pool_scale': 1.0 + nrm(9, (DEPTH, POOL_WIDTH), 0.1),
        'q_norm': 1.0 + nrm(10, (DEPTH, HEAD_DIM), 0.1),
        'k_norm': 1.0 + nrm(11, (DEPTH, HEAD_DIM), 0.1),
        'conv_w': nrm(12, (DEPTH, CONV_WIDTH, LRU_WIDTH), CONV_WIDTH ** -0.5),
        'conv_b': nrm(13, (DEPTH, LRU_WIDTH), 0.02),
        'lru_wa': nrm(14, (DEPTH, 2, LRU_BLOCKS, LRU_BLOCK_DIM, LRU_BLOCK_DIM), LRU_BLOCK_DIM ** -0.5),
        'lru_ba': nrm(15, (DEPTH, 2, LRU_WIDTH), 0.1),
        'lru_wx': nrm(16, (DEPTH, 2, LRU_BLOCKS, LRU_BLOCK_DIM, LRU_BLOCK_DIM), LRU_BLOCK_DIM ** -0.5),
        'lru_bx': nrm(17, (DEPTH, 2, LRU_WIDTH), 0.1),
        'lru_lambda': lru_lambda,
        'w_branch': nrm(18, (DEPTH, N_BRANCH, BRANCH_WIDTH, D_MODEL), BRANCH_WIDTH ** -0.5),
        'w_out': nrm(19, (DEPTH, D_MODEL, D_MODEL), BETA * D_MODEL ** -0.5),
        'ln1_g': 1.0 + nrm(20, (DEPTH, D_MODEL), 0.1),
        'ln1_b': nrm(21, (DEPTH, D_MODEL), 0.02),
        'ffn_w_gate': nrm(22, (N_DENSE, D_MODEL, D_FF), D_MODEL ** -0.5),
        'ffn_w_up': nrm(23, (N_DENSE, D_MODEL, D_FF), D_MODEL ** -0.5),
        'ffn_w_down': nrm(24, (N_DENSE, D_FF, D_MODEL), BETA * D_FF ** -0.5),
        'moe_router': nrm(25, (N_MOE, D_MODEL, N_EXPERTS), D_MODEL ** -0.5),
        'moe_router_b': nrm(26, (N_MOE, N_EXPERTS), 0.01),
        'moe_w_gate': nrm(27, (N_MOE, N_EXPERTS, D_MODEL, D_EXPERT), D_MODEL ** -0.5),
        'moe_w_up': nrm(28, (N_MOE, N_EXPERTS, D_MODEL, D_EXPERT), D_MODEL ** -0.5),
        'moe_w_down': nrm(29, (N_MOE, N_EXPERTS, D_EXPERT, D_MODEL), BETA * D_EXPERT ** -0.5),
        'ln2_g': 1.0 + nrm(31, (DEPTH, D_MODEL), 0.1),
        'ln2_b': nrm(32, (DEPTH, D_MODEL), 0.02),
    }


def reference(x, c, ctx, c_ctx, w_mod, b_mod, w_in, b_merge, pool_w, pool_scale, q_norm, k_norm,
              conv_w, conv_b, lru_wa, lru_ba, lru_wx, lru_bx, lru_lambda, w_branch, w_out, ln1_g, ln1_b,
              ffn_w_gate, ffn_w_up, ffn_w_down, moe_router, moe_router_b, moe_w_gate, moe_w_up, moe_w_down,
              ln2_g, ln2_b):
    B, S, D = x.shape
    rows = S // GRID_W
    row_pos = jnp.repeat(jnp.arange(rows, dtype=F32), GRID_W)
    col_pos = jnp.tile(jnp.arange(GRID_W, dtype=F32), rows)
    x_lat, x_ctx = x, ctx
    for l in range(DEPTH):
        last = l == DEPTH - 1
        m_lat = jnp.split((jax.nn.silu(c) @ w_mod[l] + b_mod[l])[:, None, :], 6, axis=-1)
        m_ctx = jnp.split(jax.nn.silu(c_ctx) @ w_mod[l] + b_mod[l], 6, axis=-1)

        h_lat = x_lat * (1 + m_lat[1]) + m_lat[0]
        h_ctx = x_ctx * (1 + m_ctx[1]) + m_ctx[0]
        y_ctx, y_lat = token_mixer(h_ctx, h_lat, row_pos, col_pos, w_in[l], b_merge[l], pool_w[l], pool_scale[l],
                                   q_norm[l], k_norm[l], conv_w[l], conv_b[l], lru_wa[l], lru_ba[l], lru_wx[l],
                                   lru_bx[l], lru_lambda[l], w_branch[l], w_out[l], not last)
        x_lat = layer_norm(ALPHA * x_lat + m_lat[2] * y_lat, ln1_g[l], ln1_b[l])
        if not last:
            x_ctx = layer_norm(ALPHA * x_ctx + m_ctx[2] * y_ctx, ln1_g[l], ln1_b[l])

        h2_lat = x_lat * (1 + m_lat[4]) + m_lat[3]
        if last:
            tokens = h2_lat.reshape(B * S, D)
        else:
            h2_ctx = x_ctx * (1 + m_ctx[4]) + m_ctx[3]
            tokens = jnp.concatenate([h2_ctx.reshape(-1, D), h2_lat.reshape(B * S, D)], axis=0)
        if l % 2 == 0:
            j = l // 2
            f = swiglu(tokens, ffn_w_gate[j], ffn_w_up[j], ffn_w_down[j])
        else:
            j = l // 2
            f = moe_swiglu(tokens, moe_router[j], moe_router_b[j], moe_w_gate[j], moe_w_up[j], moe_w_down[j])
        n_ctx_tok = tokens.shape[0] - B * S
        x_lat = layer_norm(ALPHA * x_lat + m_lat[5] * f[n_ctx_tok:].reshape(B, S, D), ln2_g[l], ln2_b[l])
        if not last:
            f_ctx = f[:n_ctx_tok].reshape(x_ctx.shape)
            x_ctx = layer_norm(ALPHA * x_ctx + m_ctx[5] * f_ctx, ln2_g[l], ln2_b[l])
    return x_lat
```

```python
import functools

import jax
import jax.numpy as jnp
from jax import lax
from jax.experimental import pallas as pl
from jax.experimental.pallas import tpu as pltpu

F32 = jnp.float32
BF16 = jnp.bfloat16

GRID_W = 64
POOL_WINDOWS = (2, 4, 8, 16)
BRANCH_WIDTH = 512
POOL_GROUP = BRANCH_WIDTH // len(POOL_WINDOWS)
N_Q_HEADS = 8
N_KV_HEADS = 2
HEAD_DIM = 64
GQA_GROUP = N_Q_HEADS // N_KV_HEADS
Q_WIDTH = N_Q_HEADS * HEAD_DIM
KV_WIDTH = N_KV_HEADS * HEAD_DIM
ROPE_THETA = 10000.0
ATTN_SCALE = HEAD_DIM ** -0.5
LRU_WIDTH = BRANCH_WIDTH
LRU_BLOCKS = 8
CONV_WIDTH = 4
RG_C = 8.0
N_BRANCH = 3
N_EXPERTS = 8
TOP_K = 2
LN_EPS = 1e-5
RMS_EPS = 1e-6
OFF_GATE = BRANCH_WIDTH + Q_WIDTH + 2 * KV_WIDTH + 2 * LRU_WIDTH

LANES = 128
SUBLANES = 8
ROW_TILE = 256
Q_TILE = 128
FFN_ROWS = 512
HALO = SUBLANES
VMEM_LIMIT = 56 * 1024 * 1024
GATHER_WINDOW = 64


def _cparams(*sem):
    return pltpu.CompilerParams(dimension_semantics=sem, vmem_limit_bytes=VMEM_LIMIT)


def _sigmoid(x):
    return 1.0 / (1.0 + jnp.exp(-x))


def _silu(x):
    return x * _sigmoid(x)


def _layer_norm(v, g, b):
    mu = jnp.mean(v, axis=-1, keepdims=True)
    d = v - mu
    var = jnp.mean(d * d, axis=-1, keepdims=True)
    return d * lax.rsqrt(var + LN_EPS) * g + b


def _mod_kernel(c_ref, w_ref, b_ref, o_ref):
    s = _silu(c_ref[...])
    o_ref[0, 0] = jnp.dot(s, w_ref[0], precision=lax.Precision.HIGHEST,
                          preferred_element_type=F32) + b_ref[0, 0]


def _modulation(cc, w_mod, b_mod):
    depth, d, _ = w_mod.shape
    r = cc.shape[0]
    b6 = b_mod.reshape(depth, 6, 1, d)
    return pl.pallas_call(
        _mod_kernel,
        grid=(depth, 6),
        in_specs=[pl.BlockSpec((r, d), lambda l, j: (0, 0)),
                  pl.BlockSpec((1, d, d), lambda l, j: (l, 0, j)),
                  pl.BlockSpec((1, 1, 1, d), lambda l, j: (l, j, 0, 0))],
        out_specs=pl.BlockSpec((1, 1, r, d), lambda l, j: (l, j, 0, 0)),
        out_shape=jax.ShapeDtypeStruct((depth, 6, r, d), F32),
        compiler_params=_cparams("arbitrary", "arbitrary"),
        name="modulation",
    )(cc, w_mod, b6)


def _norm_rope(zc, bd, g, cos, sin):
    zz = zc * zc
    hi = zz.astype(BF16)
    lo = (zz - hi.astype(F32)).astype(BF16)
    ms = jnp.dot(hi, bd, preferred_element_type=F32) + jnp.dot(lo, bd, preferred_element_type=F32)
    y = zc * lax.rsqrt(ms + RMS_EPS) * g
    nf = HEAD_DIM // 4
    up = pltpu.roll(y, LANES - nf, 1)
    dn = pltpu.roll(y, nf, 1)
    lane = lax.broadcasted_iota(jnp.int32, y.shape, 1)
    partner = jnp.where((lane % (2 * nf)) < nf, up, dn)
    return y * cos + partner * sin


def _proj_kernel(x_ref, mod_ref, w_ref, bd_ref, qg_ref, kg_ref, cos_ref, sin_ref,
                 p_ref, q_ref, k_ref, v_ref, lx_ref, lg_ref):
    x = x_ref[0]
    m = mod_ref[0, 0]
    h = (x * (1.0 + m[1:2]) + m[0:1]).astype(BF16)
    z = jnp.dot(h, w_ref[...], preferred_element_type=F32)
    o = 0
    p_ref[0] = z[:, o:o + BRANCH_WIDTH]
    o += BRANCH_WIDTH
    bd = bd_ref[...]
    cos = cos_ref[...]
    sin = sin_ref[...]
    for c in range(Q_WIDTH // LANES):
        y = _norm_rope(z[:, o:o + LANES], bd, qg_ref[...], cos, sin) * ATTN_SCALE
        q_ref[0, 2 * c] = y[:, :HEAD_DIM].astype(BF16)
        q_ref[0, 2 * c + 1] = y[:, HEAD_DIM:].astype(BF16)
        o += LANES
    for c in range(KV_WIDTH // LANES):
        y = _norm_rope(z[:, o:o + LANES], bd, kg_ref[...], cos, sin)
        k_ref[0, 2 * c] = y[:, :HEAD_DIM].astype(BF16)
        k_ref[0, 2 * c + 1] = y[:, HEAD_DIM:].astype(BF16)
        o += LANES
    for c in range(KV_WIDTH // LANES):
        y = z[:, o:o + LANES]
        v_ref[0, 2 * c] = y[:, :HEAD_DIM].astype(BF16)
        v_ref[0, 2 * c + 1] = y[:, HEAD_DIM:].astype(BF16)
        o += LANES
    lx_ref[0] = z[:, o:o + LRU_WIDTH]
    o += LRU_WIDTH
    lg_ref[0] = z[:, o:o + LRU_WIDTH]


def _project(x, mods, w_a, bd, qg, kg, cos_t, sin_t, nct):
    b, l, d = x.shape
    nt = l // ROW_TILE
    wa = w_a.shape[1]
    tok = lambda w: pl.BlockSpec((1, ROW_TILE, w), lambda bi, i: (bi, i, 0))
    head = lambda n: pl.BlockSpec((1, n, ROW_TILE, HEAD_DIM), lambda bi, i: (bi, 0, i, 0))
    const = lambda shp: pl.BlockSpec(shp, lambda bi, i: (0,) * len(shp))
    return pl.pallas_call(
        _proj_kernel,
        grid=(b, nt),
        in_specs=[tok(d),
                  pl.BlockSpec((1, 1, SUBLANES, d), lambda bi, i: (bi, (i >= nct).astype(jnp.int32), 0, 0)),
                  const((d, wa)), const((LANES, LANES)), const((1, LANES)), const((1, LANES)),
                  pl.BlockSpec((ROW_TILE, LANES), lambda bi, i: (i, 0)),
                  pl.BlockSpec((ROW_TILE, LANES), lambda bi, i: (i, 0))],
        out_specs=[tok(BRANCH_WIDTH), head(N_Q_HEADS), head(N_KV_HEADS), head(N_KV_HEADS),
                   tok(LRU_WIDTH), tok(LRU_WIDTH)],
        out_shape=[jax.ShapeDtypeStruct((b, l, BRANCH_WIDTH), F32),
                   jax.ShapeDtypeStruct((b, N_Q_HEADS, l, HEAD_DIM), BF16),
                   jax.ShapeDtypeStruct((b, N_KV_HEADS, l, HEAD_DIM), BF16),
                   jax.ShapeDtypeStruct((b, N_KV_HEADS, l, HEAD_DIM), BF16),
                   jax.ShapeDtypeStruct((b, l, LRU_WIDTH), F32),
                   jax.ShapeDtypeStruct((b, l, LRU_WIDTH), F32)],
        compiler_params=_cparams("parallel", "parallel"),
        name="in_proj",
    )(x, mods, w_a, bd, qg, kg, cos_t, sin_t)


def _softmax_pv(q, k, v):
    s = lax.dot_general(q, k, (((1,), (1,)), ((), ())), preferred_element_type=F32)
    m = jnp.max(s, axis=1, keepdims=True)
    p = jnp.exp(s - m)
    den = jnp.sum(p, axis=1, keepdims=True)
    return jnp.dot(p.astype(BF16), v, preferred_element_type=F32) / den


def _attn_kernel(q_ref, k_ref, v_ref, o_ref, *, n_ctx_tiles, ctx_len):
    g, tq, hd = q_ref.shape[1:]
    q = q_ref[0].reshape(g * tq, hd)

    def emit(o):
        for j in range(g):
            o_ref[0, :, j * hd:(j + 1) * hd] = o[j * tq:(j + 1) * tq].astype(o_ref.dtype)

    if n_ctx_tiles:
        @pl.when(pl.program_id(2) < n_ctx_tiles)
        def _():
            emit(_softmax_pv(q, k_ref[0, 0, :ctx_len], v_ref[0, 0, :ctx_len]))

        @pl.when(pl.program_id(2) >= n_ctx_tiles)
        def _():
            emit(_softmax_pv(q, k_ref[0, 0], v_ref[0, 0]))
    else:
        emit(_softmax_pv(q, k_ref[0, 0], v_ref[0, 0]))


def _attention(q, k, v, ctx_len, with_ctx):
    b, _, l, hd = q.shape
    nct = ctx_len // Q_TILE
    rows = l if with_ctx else l - ctx_len
    q_off = 0 if with_ctx else nct
    return pl.pallas_call(
        functools.partial(_attn_kernel, n_ctx_tiles=nct if with_ctx else 0, ctx_len=ctx_len),
        grid=(b, N_KV_HEADS, rows // Q_TILE),
        in_specs=[pl.BlockSpec((1, GQA_GROUP, Q_TILE, hd), lambda bi, h, i: (bi, h, i + q_off, 0)),
                  pl.BlockSpec((1, 1, l, hd), lambda bi, h, i: (bi, h, 0, 0)),
                  pl.BlockSpec((1, 1, l, hd), lambda bi, h, i: (bi, h, 0, 0))],
        out_specs=pl.BlockSpec((1, Q_TILE, GQA_GROUP * hd), lambda bi, h, i: (bi, i, h)),
        out_shape=jax.ShapeDtypeStruct((b, rows, Q_WIDTH), BF16),
        compiler_params=_cparams("parallel", "parallel", "parallel"),
        name="attention",
    )(q, k, v)


def _fill_ext(ext_ref, main, prev, nxt, first, last):
    t = main.shape[0]
    ext_ref[0:HALO] = jnp.where(first, 0.0, prev)
    ext_ref[HALO:HALO + t] = main
    ext_ref[HALO + t:2 * HALO + t] = jnp.where(last, 0.0, nxt)


def _seg_flags(tile, nct, nt):
    first = jnp.logical_or(tile == 0, tile == nct)
    last = jnp.logical_or(tile == nct - 1, tile == nt - 1)
    return first, last


def _halo_specs(width, tile_of, l):
    rb = ROW_TILE // HALO
    last_blk = l // HALO - 1
    main = pl.BlockSpec((1, ROW_TILE, width), lambda bi, i: (bi, tile_of(i), 0))
    prev = pl.BlockSpec((1, HALO, width), lambda bi, i: (bi, jnp.maximum(tile_of(i) * rb - 1, 0), 0))
    nxt = pl.BlockSpec((1, HALO, width), lambda bi, i: (bi, jnp.minimum((tile_of(i) + 1) * rb, last_blk), 0))
    return main, prev, nxt


def _softplus(z):
    return jnp.maximum(z, 0.0) + jnp.log1p(jnp.exp(-jnp.abs(z)))


def _gelu_tanh(x):
    return 0.5 * x * (1.0 + jnp.tanh(0.7978845608028654 * (x + 0.044715 * (x * x * x))))


def _lru_kernel(*refs, reverse, nct, nt):
    if reverse:
        (lx_ref, lxp_ref, lxn_ref, w_ref, ba_ref, bx_ref, lam_ref, cw_ref, cb_ref, hf_ref, lg_ref,
         o_ref, ext_ref, a_ref, u_ref, h_ref) = refs
    else:
        (lx_ref, lxp_ref, lxn_ref, w_ref, ba_ref, bx_ref, lam_ref, cw_ref, cb_ref,
         o_ref, ext_ref, a_ref, u_ref, h_ref) = refs
    j = pl.program_id(1)
    tile = _lru_tile(j, reverse, nct, nt)
    t = lx_ref.shape[1]

    @pl.when(j == 0)
    def _():
        h_ref[...] = jnp.zeros_like(h_ref)

    first, last = _seg_flags(tile, nct, nt)
    _fill_ext(ext_ref, lx_ref[0], lxp_ref[0], lxn_ref[0], first, last)
    lo = CONV_WIDTH // 2
    xc = cb_ref[...]
    for kk in range(CONV_WIDTH):
        xc = xc + ext_ref[pl.ds(HALO + kk - lo, t), :] * cw_ref[kk:kk + 1, :]
    zz = jnp.dot(xc.astype(BF16), w_ref[...], preferred_element_type=F32)
    r = _sigmoid(zz[:, :LRU_WIDTH] + ba_ref[...])
    gi = _sigmoid(zz[:, LRU_WIDTH:] + bx_ref[...])
    log_a = (-RG_C * _softplus(-lam_ref[...])) * r
    a = jnp.exp(log_a)
    a_ref[...] = a
    u_ref[...] = jnp.sqrt(1.0 - a * a) * gi * xc

    row = lax.broadcasted_iota(jnp.int32, (SUBLANES, LRU_WIDTH), 0)
    n_sub = t // SUBLANES

    def sub(s, h):
        blk = (n_sub - 1 - s) if reverse else s
        r0 = pl.multiple_of(blk * SUBLANES, SUBLANES)
        aa = a_ref[pl.ds(r0, SUBLANES), :]
        uu = u_ref[pl.ds(r0, SUBLANES), :]
        for dd in (1, 2, 4):
            if reverse:
                keep = row < SUBLANES - dd
                sh = SUBLANES - dd
            else:
                keep = row >= dd
                sh = dd
            a_sh = jnp.where(keep, pltpu.roll(aa, sh, 0), 1.0)
            u_sh = jnp.where(keep, pltpu.roll(uu, sh, 0), 0.0)
            uu = aa * u_sh + uu
            aa = aa * a_sh
        hh = aa * h + uu
        if reverse:
            o_ref[0, pl.ds(r0, SUBLANES), :] = (
                (hf_ref[0, pl.ds(r0, SUBLANES), :] + hh)
                * _gelu_tanh(lg_ref[0, pl.ds(r0, SUBLANES), :])).astype(o_ref.dtype)
            edge = hh[0:1]
        else:
            o_ref[0, pl.ds(r0, SUBLANES), :] = hh
            edge = hh[SUBLANES - 1:SUBLANES]
        return jnp.broadcast_to(edge, (SUBLANES, LRU_WIDTH))

    h_ref[...] = lax.fori_loop(0, n_sub, sub, h_ref[...], unroll=4)


def _lru_tile(j, reverse, nct, nt):
    if not reverse:
        return j
    return jnp.where(j < nct, nct - 1 - j, nt - 1 - (j - nct))


def _lru(lx, w, ba, bx, lam, cw, cb, nct, reverse, hf=None, lg=None):
    b, l, wd = lx.shape
    nt = l // ROW_TILE
    tile_of = lambda i: _lru_tile(i, reverse, nct, nt)
    main, prev, nxt = _halo_specs(wd, tile_of, l)
    const = lambda shp: pl.BlockSpec(shp, lambda bi, i: (0,) * len(shp))
    in_specs = [main, prev, nxt, const(w.shape), const((1, wd)), const((1, wd)), const((1, wd)),
                const((SUBLANES, wd)), const((1, wd))]
    args = [lx, lx, lx, w, ba, bx, lam, cw, cb]
    if reverse:
        in_specs += [main, main]
        args += [hf, lg]
    return pl.pallas_call(
        functools.partial(_lru_kernel, reverse=reverse, nct=nct, nt=nt),
        grid=(b, nt),
        in_specs=in_specs,
        out_specs=main,
        out_shape=jax.ShapeDtypeStruct((b, l, wd), BF16 if reverse else F32),
        scratch_shapes=[pltpu.VMEM((ROW_TILE + 2 * HALO, wd), F32), pltpu.VMEM((ROW_TILE, wd), F32),
                        pltpu.VMEM((ROW_TILE, wd), F32), pltpu.VMEM((SUBLANES, wd), F32)],
        compiler_params=_cparams("parallel", "arbitrary"),
        name="lru_bwd" if reverse else "lru_fwd",
    )(*args)


def _merge_kernel(x_ref, mod_ref, p_ref, pp_ref, pn_ref, at_ref, lr_ref, wg_ref, bm_ref, pw_ref, ps_ref,
                  wb_ref, wo_ref, g_ref, b_ref, x1_ref, h2_ref, ext_ref, *, nct, nt, tile_off, ctx_len, alpha):
    tile = pl.program_id(1) + tile_off
    t = x_ref.shape[1]
    x = x_ref[0]
    m = mod_ref[0, 0]
    h = (x * (1.0 + m[1:2]) + m[0:1]).astype(BF16)

    first, last = _seg_flags(tile, nct, nt)
    _fill_ext(ext_ref, p_ref[0], pp_ref[0], pn_ref[0], first, last)
    in_ctx = tile < nct
    seg_len = jnp.where(in_ctx, ctx_len, nt * t - ctx_len)
    pos = (tile - jnp.where(in_ctx, 0, nct)) * t + lax.broadcasted_iota(jnp.int32, (t, 1), 0)
    pooled = []
    for gi, w in enumerate(POOL_WINDOWS):
        lo = w // 2
        cols = slice(gi * POOL_GROUP, (gi + 1) * POOL_GROUP)
        acc = ext_ref[pl.ds(HALO - lo, t), cols]
        for kk in range(1, w):
            acc = acc + ext_ref[pl.ds(HALO - lo + kk, t), cols]
        cnt = jnp.clip(pos - lo + w, 0, seg_len) - jnp.clip(pos - lo, 0, seg_len)
        mean = acc / cnt.astype(F32)
        dlt = (mean - ext_ref[pl.ds(HALO, t), cols]).astype(BF16)
        pooled.append(jnp.dot(dlt, pw_ref[gi], preferred_element_type=F32))
    pool_y = (jnp.concatenate(pooled, axis=1) * ps_ref[...]).astype(BF16)

    d = x.shape[1]
    mix = None
    for n, ys in enumerate((pool_y, at_ref[0], lr_ref[0])):
        gz = jnp.dot(h, wg_ref[:, n * d:(n + 1) * d], preferred_element_type=F32) + bm_ref[n:n + 1, :]
        br = jnp.dot(ys, wb_ref[n], preferred_element_type=F32)
        term = _sigmoid(gz) * br
        mix = term if mix is None else mix + term
    y = jnp.dot(mix.astype(BF16), wo_ref[...], preferred_element_type=F32)
    x1 = _layer_norm(alpha * x + m[2:3] * y, g_ref[...], b_ref[...])
    x1_ref[0] = x1
    h2_ref[0] = (x1 * (1.0 + m[4:5]) + m[3:4]).astype(h2_ref.dtype)


def _merge(x, mods, p, attn, lru, w_gate, b_merge, pool_w, pool_scale, w_branch, w_out, ln_g, ln_b,
           nct, lat_only, alpha, h2_dtype):
    b, l, d = x.shape
    nt = l // ROW_TILE
    off = nct if lat_only else 0
    rows = l - off * ROW_TILE
    tile_of = lambda i: i + off
    tok_l = lambda w: pl.BlockSpec((1, ROW_TILE, w), lambda bi, i: (bi, i + off, 0))
    tok_o = lambda w: pl.BlockSpec((1, ROW_TILE, w), lambda bi, i: (bi, i, 0))
    const = lambda shp: pl.BlockSpec(shp, lambda bi, i: (0,) * len(shp))
    pm, pp, pn = _halo_specs(BRANCH_WIDTH, tile_of, l)
    return pl.pallas_call(
        functools.partial(_merge_kernel, nct=nct, nt=nt, tile_off=off, ctx_len=nct * ROW_TILE, alpha=alpha),
        grid=(b, rows // ROW_TILE),
        in_specs=[tok_l(d),
                  pl.BlockSpec((1, 1, SUBLANES, d), lambda bi, i: (bi, (i + off >= nct).astype(jnp.int32), 0, 0)),
                  pm, pp, pn,
                  tok_o(Q_WIDTH) if lat_only else tok_l(Q_WIDTH),
                  tok_l(LRU_WIDTH),
                  const(w_gate.shape), const(b_merge.shape), const(pool_w.shape), const(pool_scale.shape),
                  const(w_branch.shape), const(w_out.shape), const((1, d)), const((1, d))],
        out_specs=[tok_o(d), tok_o(d)],
        out_shape=[jax.ShapeDtypeStruct((b, rows, d), F32), jax.ShapeDtypeStruct((b, rows, d), h2_dtype)],
        scratch_shapes=[pltpu.VMEM((ROW_TILE + 2 * HALO, BRANCH_WIDTH), F32)],
        compiler_params=_cparams("parallel", "parallel"),
        name="merge_ln1",
    )(x, mods, p, p, p, attn, lru, w_gate, b_merge, pool_w, pool_scale, w_branch, w_out, ln_g, ln_b)


def _swiglu_kernel(be_ref, nu_ref, x_ref, wg_ref, wu_ref, wd_ref, o_ref, acc_ref):
    i = pl.program_id(0)
    f = pl.program_id(1)
    nf = pl.num_programs(1)

    @pl.when(f == 0)
    def _():
        acc_ref[...] = jnp.zeros_like(acc_ref)

    @pl.when(i < nu_ref[0])
    def _():
        x = x_ref[...].astype(BF16)
        g = jnp.dot(x, wg_ref[0], preferred_element_type=F32)
        u = jnp.dot(x, wu_ref[0], preferred_element_type=F32)
        a = (_silu(g) * u).astype(BF16)
        acc_ref[...] += jnp.dot(a, wd_ref[0], preferred_element_type=F32)

    @pl.when(f == nf - 1)
    def _():
        o_ref[...] = acc_ref[...]


def _swiglu_blocks(x, block_e, n_used, w_gate, w_up, w_down, tf):
    r, d = x.shape
    ff = w_gate.shape[2]
    nb = r // FFN_ROWS
    nf = ff // tf

    def fcol(i, f, nu):
        return jnp.where(i < nu[0], f, nf - 1)

    return pl.pallas_call(
        _swiglu_kernel,
        grid_spec=pltpu.PrefetchScalarGridSpec(
            num_scalar_prefetch=2,
            grid=(nb, nf),
            in_specs=[pl.BlockSpec((FFN_ROWS, d), lambda i, f, be, nu: (i, 0)),
                      pl.BlockSpec((1, d, tf), lambda i, f, be, nu: (be[i], 0, fcol(i, f, nu))),
                      pl.BlockSpec((1, d, tf), lambda i, f, be, nu: (be[i], 0, fcol(i, f, nu))),
                      pl.BlockSpec((1, tf, d), lambda i, f, be, nu: (be[i], fcol(i, f, nu), 0))],
            out_specs=pl.BlockSpec((FFN_ROWS, d), lambda i, f, be, nu: (i, 0)),
            scratch_shapes=[pltpu.VMEM((FFN_ROWS, d), F32)]),
        out_shape=jax.ShapeDtypeStruct((r, d), F32),
        compiler_params=_cparams("parallel", "arbitrary"),
        name="swiglu",
    )(block_e, n_used, x, w_gate, w_up, w_down)


def _res_ln_kernel(x_ref, mod_ref, g_ref, b_ref, *rest, alpha, top_k):
    if top_k:
        y_ref, p_ref, o_ref = rest
        d = x_ref.shape[2]
        f = y_ref[0, :, 0:d] * p_ref[0, :, 0:1]
        for kk in range(1, top_k):
            f = f + y_ref[0, :, kk * d:(kk + 1) * d] * p_ref[0, :, kk:kk + 1]
    else:
        f_ref, o_ref = rest
        f = f_ref[0]
    m = mod_ref[0, 0]
    o_ref[0] = _layer_norm(alpha * x_ref[0] + m[5:6] * f, g_ref[...], b_ref[...])


def _res_ln(x1, mods, ln_g, ln_b, f, probs, nct, lat_only, alpha):
    b, rows, d = x1.shape
    off = nct if lat_only else 0
    tok = lambda w: pl.BlockSpec((1, ROW_TILE, w), lambda bi, i: (bi, i, 0))
    const = lambda shp: pl.BlockSpec(shp, lambda bi, i: (0,) * len(shp))
    in_specs = [tok(d),
                pl.BlockSpec((1, 1, SUBLANES, d), lambda bi, i: (bi, (i + off >= nct).astype(jnp.int32), 0, 0)),
                const((1, d)), const((1, d)), tok(f.shape[2])]
    args = [x1, mods, ln_g, ln_b, f]
    top_k = 0
    if probs is not None:
        top_k = probs.shape[2]
        in_specs.append(tok(top_k))
        args.append(probs)
    return pl.pallas_call(
        functools.partial(_res_ln_kernel, alpha=alpha, top_k=top_k),
        grid=(b, rows // ROW_TILE),
        in_specs=in_specs,
        out_specs=tok(d),
        out_shape=jax.ShapeDtypeStruct((b, rows, d), F32),
        compiler_params=_cparams("parallel", "parallel"),
        name="res_ln2",
    )(*args)


def _router_kernel(x_ref, w_ref, b_ref, o_ref):
    o_ref[...] = jnp.dot(x_ref[...], w_ref[...], precision=lax.Precision.HIGHEST,
                         preferred_element_type=F32) + b_ref[...]


def _router_logits(t, w, bias):
    n, d = t.shape
    wp = jnp.zeros((d, LANES), F32).at[:, :w.shape[1]].set(w)
    bp = jnp.zeros((1, LANES), F32).at[0, :w.shape[1]].set(bias)
    out = pl.pallas_call(
        _router_kernel,
        grid=(n // FFN_ROWS,),
        in_specs=[pl.BlockSpec((FFN_ROWS, d), lambda i: (i, 0)),
                  pl.BlockSpec((d, LANES), lambda i: (0, 0)),
                  pl.BlockSpec((1, LANES), lambda i: (0, 0))],
        out_specs=pl.BlockSpec((FFN_ROWS, LANES), lambda i: (i, 0)),
        out_shape=jax.ShapeDtypeStruct((n, LANES), F32),
        compiler_params=_cparams("parallel"),
        name="router",
    )(t, wp, bp)
    return out[:, :w.shape[1]]


def _gather_kernel(idx_ref, src_ref, o_ref, sem):
    n = o_ref.shape[0]

    def copy(r, row):
        return pltpu.make_async_copy(src_ref.at[pl.ds(row, 1)], o_ref.at[pl.ds(r, 1)], sem)

    def body(r, carry):
        copy(r, idx_ref[r]).start()

        @pl.when(r >= GATHER_WINDOW)
        def _():
            copy(r - GATHER_WINDOW, 0).wait()

        return carry

    lax.fori_loop(0, n, body, 0)

    def drain(r, carry):
        copy(r, 0).wait()
        return carry

    lax.fori_loop(max(n - GATHER_WINDOW, 0), n, drain, 0)


def _gather_rows(src, idx):
    n = idx.shape[0]
    return pl.pallas_call(
        _gather_kernel,
        grid_spec=pltpu.PrefetchScalarGridSpec(
            num_scalar_prefetch=1,
            grid=(1,),
            in_specs=[pl.BlockSpec(memory_space=pl.ANY)],
            out_specs=pl.BlockSpec(memory_space=pl.ANY),
            scratch_shapes=[pltpu.SemaphoreType.DMA]),
        out_shape=jax.ShapeDtypeStruct((n, src.shape[1]), src.dtype),
        compiler_params=pltpu.CompilerParams(dimension_semantics=("arbitrary",), has_side_effects=True),
        name="row_gather",
    )(idx, src)


def _moe(t, router, router_b, w_gate, w_up, w_down, tf):
    n, d = t.shape
    n_e = router.shape[1]
    logits = _router_logits(t, router, router_b)
    top_v, top_i = lax.top_k(logits, TOP_K)
    probs = jax.nn.softmax(top_v, axis=-1)
    m = n * TOP_K
    flat_e = top_i.reshape(m)
    onehot = (flat_e[:, None] == jnp.arange(n_e, dtype=flat_e.dtype)[None, :]).astype(jnp.int32)
    csum = jnp.cumsum(onehot, axis=0)
    rank = jnp.take_along_axis(csum, flat_e[:, None], axis=1)[:, 0] - 1
    counts = csum[-1]
    padded = (counts + FFN_ROWS - 1) // FFN_ROWS * FFN_ROWS
    ends_p = jnp.cumsum(padded)
    start_p = ends_p - padded
    dest = (start_p[flat_e] + rank).astype(jnp.int32)
    nb = -(-(m + n_e * (FFN_ROWS - 1)) // FFN_ROWS)
    slot_tok = jnp.zeros((nb * FFN_ROWS,), jnp.int32).at[dest].set(jnp.arange(m, dtype=jnp.int32) // TOP_K)
    block_e = jnp.minimum(jnp.searchsorted(ends_p, jnp.arange(nb) * FFN_ROWS, side='right'),
                          n_e - 1).astype(jnp.int32)
    n_used = (ends_p[-1:] // FFN_ROWS).astype(jnp.int32)
    xs = _gather_rows(t, slot_tok)
    ys = _swiglu_blocks(xs, block_e, n_used, w_gate, w_up, w_down, tf)
    yg = _gather_rows(ys, dest)
    return yg.reshape(n, TOP_K * d), probs


def _block_diag(w):
    nb, bi, bj = w.shape
    eye = jnp.eye(nb, dtype=w.dtype)
    return (eye[:, None, :, None] * w[:, :, None, :]).reshape(nb * bi, nb * bj)


def _rope_tables(ctx_len, s):
    nf = HEAD_DIM // 4
    inv = ROPE_THETA ** (-jnp.arange(nf, dtype=F32) / nf)
    t = jnp.arange(s)
    row = (t // GRID_W).astype(F32)[:, None] * inv
    col = (t % GRID_W).astype(F32)[:, None] * inv
    ang = jnp.concatenate([row, row, col, col], axis=1)
    sign = jnp.tile(jnp.concatenate([-jnp.ones((nf,), F32), jnp.ones((nf,), F32)]), 2)
    cos = jnp.concatenate([jnp.ones((ctx_len, HEAD_DIM), F32), jnp.cos(ang)], axis=0)
    sin = jnp.concatenate([jnp.zeros((ctx_len, HEAD_DIM), F32), jnp.sin(ang) * sign], axis=0)
    reps = LANES // HEAD_DIM
    return jnp.tile(cos, (1, reps)), jnp.tile(sin, (1, reps))


def _pad_rows(a, rows):
    return jnp.zeros((rows,) + a.shape[1:], a.dtype).at[:a.shape[0]].set(a)


def kernel(x, c, ctx, c_ctx, w_mod, b_mod, w_in, b_merge, pool_w, pool_scale, q_norm, k_norm, conv_w, conv_b, lru_wa, lru_ba, lru_wx, lru_bx, lru_lambda, w_branch, w_out, ln1_g, ln1_b, ffn_w_gate, ffn_w_up, ffn_w_down, moe_router, moe_router_b, moe_w_gate, moe_w_up, moe_w_down, ln2_g, ln2_b):
    b, s, d = x.shape
    ctx_len = ctx.shape[1]
    depth = w_in.shape[0]
    l = ctx_len + s
    assert ctx_len % ROW_TILE == 0 and s % ROW_TILE == 0 and (b * s) % FFN_ROWS == 0
    assert (b * l) % FFN_ROWS == 0 and ctx_len % Q_TILE == 0 and s % Q_TILE == 0
    nct = ctx_len // ROW_TILE
    alpha = (2 * depth) ** 0.25

    cc = _pad_rows(jnp.concatenate([c, c_ctx[None, :]], axis=0), -(-(b + 1) // SUBLANES) * SUBLANES)
    mod_all = _modulation(cc, w_mod, b_mod)

    cos_t, sin_t = _rope_tables(ctx_len, s)
    bd = _block_diag(jnp.full((LANES // HEAD_DIM, HEAD_DIM, HEAD_DIM), 1.0 / HEAD_DIM, F32)).astype(BF16)
    reps = LANES // HEAD_DIM

    xs = jnp.concatenate([ctx, x], axis=1)
    for li in range(depth):
        last = li == depth - 1
        ml = jnp.transpose(mod_all[li, :, :b], (1, 0, 2))
        mc = jnp.broadcast_to(mod_all[li, :, b][None], (b, 6, d))
        mods = jnp.stack([mc, ml], axis=1)
        mods = jnp.concatenate([mods, jnp.zeros((b, 2, SUBLANES - 6, d), F32)], axis=2)

        w_l = w_in[li].astype(BF16)
        p, q, k, v, lx, lg = _project(
            xs, mods, w_l[:, :OFF_GATE], bd,
            jnp.tile(q_norm[li], reps)[None], jnp.tile(k_norm[li], reps)[None], cos_t, sin_t, nct)
        attn = _attention(q, k, v, ctx_len, with_ctx=not last)

        cw = _pad_rows(conv_w[li], SUBLANES)
        lru_w = [jnp.concatenate([_block_diag(lru_wa[li, dr]), _block_diag(lru_wx[li, dr])], axis=1).astype(BF16)
                 for dr in range(2)]
        hf = _lru(lx, lru_w[0], lru_ba[li, 0][None], lru_bx[li, 0][None], lru_lambda[li, 0][None],
                  cw, conv_b[li][None], nct, reverse=False)
        lru_y = _lru(lx, lru_w[1], lru_ba[li, 1][None], lru_bx[li, 1][None], lru_lambda[li, 1][None],
                     cw, conv_b[li][None], nct, reverse=True, hf=hf, lg=lg)

        x1, h2 = _merge(xs, mods, p, attn, lru_y, w_l[:, OFF_GATE:], _pad_rows(b_merge[li], SUBLANES),
                        pool_w[li].astype(BF16), pool_scale[li][None], w_branch[li].astype(BF16),
                        w_out[li].astype(BF16), ln1_g[li][None], ln1_b[li][None],
                        nct, lat_only=last, alpha=alpha, h2_dtype=F32 if li % 2 else BF16)
        rows = x1.shape[1]
        if li % 2 == 0:
            jf = li // 2
            nb = b * rows // FFN_ROWS
            f = _swiglu_blocks(h2.reshape(b * rows, d), jnp.zeros((nb,), jnp.int32), jnp.full((1,), nb, jnp.int32),
                               ffn_w_gate[jf][None].astype(BF16), ffn_w_up[jf][None].astype(BF16),
                               ffn_w_down[jf][None].astype(BF16), tf=ffn_w_gate.shape[2] // 2)
            xs = _res_ln(x1, mods, ln2_g[li][None], ln2_b[li][None], f.reshape(b, rows, d), None,
                         nct, last, alpha)
        else:
            jf = li // 2
            if last:
                t = h2.reshape(b * rows, d)
            else:
                t = jnp.concatenate([h2[:, :ctx_len].reshape(-1, d), h2[:, ctx_len:].reshape(-1, d)], axis=0)
            yg, probs = _moe(t, moe_router[jf], moe_router_b[jf], moe_w_gate[jf].astype(BF16),
                             moe_w_up[jf].astype(BF16), moe_w_down[jf].astype(BF16),
                             tf=moe_w_gate.shape[3] // 2)
            if not last:
                nc = b * ctx_len
                yg = jnp.concatenate([yg[:nc].reshape(b, ctx_len, -1), yg[nc:].reshape(b, s, -1)], axis=1)
                probs = jnp.concatenate([probs[:nc].reshape(b, ctx_len, -1), probs[nc:].reshape(b, s, -1)], axis=1)
            else:
                yg = yg.reshape(b, rows, -1)
                probs = probs.reshape(b, rows, -1)
            xs = _res_ln(x1, mods, ln2_g[li][None], ln2_b[li][None], yg, probs, nct, last, alpha)
    return xs
```

```python
import functools

import jax
import jax.numpy as jnp
from jax import lax
from jax.experimental import pallas as pl
from jax.experimental.pallas import tpu as pltpu

F32 = jnp.float32
BF16 = jnp.bfloat16

GRID_W = 64
POOL_WINDOWS = (2, 4, 8, 16)
BRANCH_WIDTH = 512
POOL_GROUP = BRANCH_WIDTH // len(POOL_WINDOWS)
N_Q_HEADS = 8
N_KV_HEADS = 2
HEAD_DIM = 64
GQA_GROUP = N_Q_HEADS // N_KV_HEADS
Q_WIDTH = N_Q_HEADS * HEAD_DIM
KV_WIDTH = N_KV_HEADS * HEAD_DIM
ROPE_THETA = 10000.0
ATTN_SCALE = HEAD_DIM ** -0.5
LOG2_E = 1.4426950408889634
LRU_WIDTH = BRANCH_WIDTH
LRU_BLOCKS = 8
CONV_WIDTH = 4
RG_C = 8.0
N_BRANCH = 3
N_EXPERTS = 8
TOP_K = 2
LN_EPS = 1e-5
RMS_EPS = 1e-6
OFF_GATE = BRANCH_WIDTH + Q_WIDTH + 2 * KV_WIDTH + 2 * LRU_WIDTH

LANES = 128
SUBLANES = 8
ROW_TILE = 256
Q_TILE = 256
ATTN_CHAIN = 256
FFN_ROWS = 512
HALO = SUBLANES
VMEM_LIMIT = 56 * 1024 * 1024


def _cparams(*sem):
    return pltpu.CompilerParams(dimension_semantics=sem, vmem_limit_bytes=VMEM_LIMIT)


def _sigmoid(x):
    return 1.0 / (1.0 + jnp.exp(-x))


def _silu(x):
    return x * _sigmoid(x)


def _layer_norm(v, g, b):
    mu = jnp.mean(v, axis=-1, keepdims=True)
    d = v - mu
    var = jnp.mean(d * d, axis=-1, keepdims=True)
    return d * lax.rsqrt(var + LN_EPS) * g + b


def _mod_kernel(c_ref, w_ref, b_ref, o_ref):
    s = _silu(c_ref[...])
    o_ref[0, 0] = jnp.dot(s, w_ref[0], precision=lax.Precision.HIGHEST,
                          preferred_element_type=F32) + b_ref[0, 0]


def _modulation(cc, w_mod, b_mod):
    depth, d, _ = w_mod.shape
    r = cc.shape[0]
    b6 = b_mod.reshape(depth, 6, 1, d)
    return pl.pallas_call(
        _mod_kernel,
        grid=(depth, 6),
        in_specs=[pl.BlockSpec((r, d), lambda l, j: (0, 0)),
                  pl.BlockSpec((1, d, d), lambda l, j: (l, 0, j)),
                  pl.BlockSpec((1, 1, 1, d), lambda l, j: (l, j, 0, 0))],
        out_specs=pl.BlockSpec((1, 1, r, d), lambda l, j: (l, j, 0, 0)),
        out_shape=jax.ShapeDtypeStruct((depth, 6, r, d), F32),
        compiler_params=_cparams("arbitrary", "arbitrary"),
        name="modulation",
    )(cc, w_mod, b6)


def _norm_rope(zc, bd, g, cos, sin):
    zz = zc * zc
    hi = zz.astype(BF16)
    lo = (zz - hi.astype(F32)).astype(BF16)
    ms = jnp.dot(hi, bd, preferred_element_type=F32) + jnp.dot(lo, bd, preferred_element_type=F32)
    y = zc * lax.rsqrt(ms + RMS_EPS) * g
    nf = HEAD_DIM // 4
    up = pltpu.roll(y, LANES - nf, 1)
    dn = pltpu.roll(y, nf, 1)
    lane = lax.broadcasted_iota(jnp.int32, y.shape, 1)
    partner = jnp.where((lane % (2 * nf)) < nf, up, dn)
    return y * cos + partner * sin


def _proj_kernel(x_ref, mod_ref, w_ref, bd_ref, qg_ref, kg_ref, cos_ref, sin_ref,
                 p_ref, q_ref, k_ref, v_ref, lx_ref, lg_ref):
    x = x_ref[0]
    m = mod_ref[0, 0]
    h = (x * (1.0 + m[1:2]) + m[0:1]).astype(BF16)
    z = jnp.dot(h, w_ref[...], preferred_element_type=F32)
    o = 0
    p_ref[0] = z[:, o:o + BRANCH_WIDTH]
    o += BRANCH_WIDTH
    bd = bd_ref[...]
    cos = cos_ref[...]
    sin = sin_ref[...]
    for c in range(Q_WIDTH // LANES):
        y = _norm_rope(z[:, o:o + LANES], bd, qg_ref[...], cos, sin) * (ATTN_SCALE * LOG2_E)
        q_ref[0, 2 * c] = y[:, :HEAD_DIM].astype(BF16)
        q_ref[0, 2 * c + 1] = y[:, HEAD_DIM:].astype(BF16)
        o += LANES
    for c in range(KV_WIDTH // LANES):
        y = _norm_rope(z[:, o:o + LANES], bd, kg_ref[...], cos, sin)
        k_ref[0, 2 * c] = y[:, :HEAD_DIM].astype(BF16)
        k_ref[0, 2 * c + 1] = y[:, HEAD_DIM:].astype(BF16)
        o += LANES
    for c in range(KV_WIDTH // LANES):
        y = z[:, o:o + LANES]
        ones = jnp.ones((y.shape[0], LANES - HEAD_DIM), F32)
        v_ref[0, 2 * c] = jnp.concatenate([y[:, :HEAD_DIM], ones], axis=1).astype(BF16)
        v_ref[0, 2 * c + 1] = jnp.concatenate([y[:, HEAD_DIM:], ones], axis=1).astype(BF16)
        o += LANES
    lx_ref[0] = z[:, o:o + LRU_WIDTH]
    o += LRU_WIDTH
    lg_ref[0] = z[:, o:o + LRU_WIDTH]


def _project(x, mods, w_a, bd, qg, kg, cos_t, sin_t, nct):
    b, l, d = x.shape
    nt = l // ROW_TILE
    wa = w_a.shape[1]
    tok = lambda w: pl.BlockSpec((1, ROW_TILE, w), lambda bi, i: (bi, i, 0))
    head = lambda n, w=HEAD_DIM: pl.BlockSpec((1, n, ROW_TILE, w), lambda bi, i: (bi, 0, i, 0))
    const = lambda shp: pl.BlockSpec(shp, lambda bi, i: (0,) * len(shp))
    return pl.pallas_call(
        _proj_kernel,
        grid=(b, nt),
        in_specs=[tok(d),
                  pl.BlockSpec((1, 1, SUBLANES, d), lambda bi, i: (bi, (i >= nct).astype(jnp.int32), 0, 0)),
                  const((d, wa)), const((LANES, LANES)), const((1, LANES)), const((1, LANES)),
                  pl.BlockSpec((ROW_TILE, LANES), lambda bi, i: (i, 0)),
                  pl.BlockSpec((ROW_TILE, LANES), lambda bi, i: (i, 0))],
        out_specs=[tok(BRANCH_WIDTH), head(N_Q_HEADS), head(N_KV_HEADS), head(N_KV_HEADS, LANES),
                   tok(LRU_WIDTH), tok(LRU_WIDTH)],
        out_shape=[jax.ShapeDtypeStruct((b, l, BRANCH_WIDTH), F32),
                   jax.ShapeDtypeStruct((b, N_Q_HEADS, l, HEAD_DIM), BF16),
                   jax.ShapeDtypeStruct((b, N_KV_HEADS, l, HEAD_DIM), BF16),
                   jax.ShapeDtypeStruct((b, N_KV_HEADS, l, LANES), BF16),
                   jax.ShapeDtypeStruct((b, l, LRU_WIDTH), F32),
                   jax.ShapeDtypeStruct((b, l, LRU_WIDTH), F32)],
        compiler_params=_cparams("parallel", "parallel"),
        name="in_proj",
    )(x, mods, w_a, bd, qg, kg, cos_t, sin_t)


def _softmax_pv(q, k, v):
    hd = q.shape[1]
    s = lax.dot_general(q, k, (((1,), (1,)), ((), ())), preferred_element_type=F32)
    m = jnp.max(s, axis=1, keepdims=True)
    p = jnp.exp2(s - m).astype(BF16)
    r = jnp.dot(p, v, preferred_element_type=F32)
    return r[:, :hd] / r[:, hd:2 * hd]


def _attn_kernel(q_ref, k_ref, v_ref, o_ref, *, n_ctx_tiles, ctx_len):
    g, tq, hd = q_ref.shape[1:]

    def run(lk):
        for j in range(g):
            for r0 in range(0, tq, ATTN_CHAIN):
                o = _softmax_pv(q_ref[0, j, r0:r0 + ATTN_CHAIN, :], k_ref[0, 0, :lk], v_ref[0, 0, :lk])
                o_ref[0, r0:r0 + ATTN_CHAIN, j * hd:(j + 1) * hd] = o.astype(o_ref.dtype)

    if n_ctx_tiles:
        @pl.when(pl.program_id(2) < n_ctx_tiles)
        def _():
            run(ctx_len)

        @pl.when(pl.program_id(2) >= n_ctx_tiles)
        def _():
            run(k_ref.shape[2])
    else:
        run(k_ref.shape[2])


def _attention(q, k, v, ctx_len, with_ctx):
    b, _, l, hd = q.shape
    nct = ctx_len // Q_TILE
    rows = l if with_ctx else l - ctx_len
    q_off = 0 if with_ctx else nct
    return pl.pallas_call(
        functools.partial(_attn_kernel, n_ctx_tiles=nct if with_ctx else 0, ctx_len=ctx_len),
        grid=(b, N_KV_HEADS, rows // Q_TILE),
        in_specs=[pl.BlockSpec((1, GQA_GROUP, Q_TILE, hd), lambda bi, h, i: (bi, h, i + q_off, 0)),
                  pl.BlockSpec((1, 1, l, hd), lambda bi, h, i: (bi, h, 0, 0)),
                  pl.BlockSpec((1, 1, l, v.shape[3]), lambda bi, h, i: (bi, h, 0, 0))],
        out_specs=pl.BlockSpec((1, Q_TILE, GQA_GROUP * hd), lambda bi, h, i: (bi, i, h)),
        out_shape=jax.ShapeDtypeStruct((b, rows, Q_WIDTH), BF16),
        compiler_params=_cparams("parallel", "parallel", "parallel"),
        name="attention",
    )(q, k, v)


def _fill_ext(ext_ref, main, prev, nxt, first, last):
    t = main.shape[0]
    ext_ref[0:HALO] = jnp.where(first, 0.0, prev)
    ext_ref[HALO:HALO + t] = main
    ext_ref[HALO + t:2 * HALO + t] = jnp.where(last, 0.0, nxt)


def _seg_flags(tile, nct, nt):
    first = jnp.logical_or(tile == 0, tile == nct)
    last = jnp.logical_or(tile == nct - 1, tile == nt - 1)
    return first, last


def _halo_specs(width, tile_of, l):
    rb = ROW_TILE // HALO
    last_blk = l // HALO - 1
    main = pl.BlockSpec((1, ROW_TILE, width), lambda bi, i: (bi, tile_of(i), 0))
    prev = pl.BlockSpec((1, HALO, width), lambda bi, i: (bi, jnp.maximum(tile_of(i) * rb - 1, 0), 0))
    nxt = pl.BlockSpec((1, HALO, width), lambda bi, i: (bi, jnp.minimum((tile_of(i) + 1) * rb, last_blk), 0))
    return main, prev, nxt


def _softplus(z):
    return jnp.maximum(z, 0.0) + jnp.log1p(jnp.exp(-jnp.abs(z)))


def _gelu_tanh(x):
    return 0.5 * x * (1.0 + jnp.tanh(0.7978845608028654 * (x + 0.044715 * (x * x * x))))


def _lru_kernel(*refs, reverse, nct, nt):
    if reverse:
        (lx_ref, lxp_ref, lxn_ref, w_ref, ba_ref, bx_ref, lam_ref, cw_ref, cb_ref, hf_ref, lg_ref,
         o_ref, ext_ref, a_ref, u_ref, h_ref) = refs
    else:
        (lx_ref, lxp_ref, lxn_ref, w_ref, ba_ref, bx_ref, lam_ref, cw_ref, cb_ref,
         o_ref, ext_ref, a_ref, u_ref, h_ref) = refs
    j = pl.program_id(1)
    tile = _lru_tile(j, reverse, nct, nt)
    t = lx_ref.shape[1]

    @pl.when(j == 0)
    def _():
        h_ref[...] = jnp.zeros_like(h_ref)

    first, last = _seg_flags(tile, nct, nt)
    _fill_ext(ext_ref, lx_ref[0], lxp_ref[0], lxn_ref[0], first, last)
    lo = CONV_WIDTH // 2
    xc = cb_ref[...]
    for kk in range(CONV_WIDTH):
        xc = xc + ext_ref[pl.ds(HALO + kk - lo, t), :] * cw_ref[kk:kk + 1, :]
    zz = jnp.dot(xc.astype(BF16), w_ref[...], preferred_element_type=F32)
    r = _sigmoid(zz[:, :LRU_WIDTH] + ba_ref[...])
    gi = _sigmoid(zz[:, LRU_WIDTH:] + bx_ref[...])
    log_a = (-RG_C * _softplus(-lam_ref[...])) * r
    a = jnp.exp(log_a)
    a_ref[...] = a
    u_ref[...] = jnp.sqrt(1.0 - a * a) * gi * xc

    row = lax.broadcasted_iota(jnp.int32, (SUBLANES, LRU_WIDTH), 0)
    n_sub = t // SUBLANES

    def sub(s, h):
        blk = (n_sub - 1 - s) if reverse else s
        r0 = pl.multiple_of(blk * SUBLANES, SUBLANES)
        aa = a_ref[pl.ds(r0, SUBLANES), :]
        uu = u_ref[pl.ds(r0, SUBLANES), :]
        for dd in (1, 2, 4):
            if reverse:
                keep = row < SUBLANES - dd
                sh = SUBLANES - dd
            else:
                keep = row >= dd
                sh = dd
            a_sh = jnp.where(keep, pltpu.roll(aa, sh, 0), 1.0)
            u_sh = jnp.where(keep, pltpu.roll(uu, sh, 0), 0.0)
            uu = aa * u_sh + uu
            aa = aa * a_sh
        hh = aa * h + uu
        if reverse:
            o_ref[0, pl.ds(r0, SUBLANES), :] = (
                (hf_ref[0, pl.ds(r0, SUBLANES), :] + hh)
                * _gelu_tanh(lg_ref[0, pl.ds(r0, SUBLANES), :])).astype(o_ref.dtype)
            edge = hh[0:1]
        else:
            o_ref[0, pl.ds(r0, SUBLANES), :] = hh
            edge = hh[SUBLANES - 1:SUBLANES]
        return jnp.broadcast_to(edge, (SUBLANES, LRU_WIDTH))

    h_ref[...] = lax.fori_loop(0, n_sub, sub, h_ref[...], unroll=4)


def _lru_tile(j, reverse, nct, nt):
    if not reverse:
        return j
    return jnp.where(j < nct, nct - 1 - j, nt - 1 - (j - nct))


def _lru(lx, w, ba, bx, lam, cw, cb, nct, reverse, hf=None, lg=None):
    b, l, wd = lx.shape
    nt = l // ROW_TILE
    tile_of = lambda i: _lru_tile(i, reverse, nct, nt)
    main, prev, nxt = _halo_specs(wd, tile_of, l)
    const = lambda shp: pl.BlockSpec(shp, lambda bi, i: (0,) * len(shp))
    in_specs = [main, prev, nxt, const(w.shape), const((1, wd)), const((1, wd)), const((1, wd)),
                const((SUBLANES, wd)), const((1, wd))]
    args = [lx, lx, lx, w, ba, bx, lam, cw, cb]
    if reverse:
        in_specs += [main, main]
        args += [hf, lg]
    return pl.pallas_call(
        functools.partial(_lru_kernel, reverse=reverse, nct=nct, nt=nt),
        grid=(b, nt),
        in_specs=in_specs,
        out_specs=main,
        out_shape=jax.ShapeDtypeStruct((b, l, wd), BF16 if reverse else F32),
        scratch_shapes=[pltpu.VMEM((ROW_TILE + 2 * HALO, wd), F32), pltpu.VMEM((ROW_TILE, wd), F32),
                        pltpu.VMEM((ROW_TILE, wd), F32), pltpu.VMEM((SUBLANES, wd), F32)],
        compiler_params=_cparams("parallel", "arbitrary"),
        name="lru_bwd" if reverse else "lru_fwd",
    )(*args)


def _merge_kernel(x_ref, mod_ref, p_ref, pp_ref, pn_ref, at_ref, lr_ref, wg_ref, bm_ref, pw_ref, ps_ref,
                  wb_ref, wo_ref, g_ref, b_ref, x1_ref, h2_ref, ext_ref, *, nct, nt, tile_off, ctx_len, alpha):
    tile = pl.program_id(1) + tile_off
    t = x_ref.shape[1]
    x = x_ref[0]
    m = mod_ref[0, 0]
    h = (x * (1.0 + m[1:2]) + m[0:1]).astype(BF16)

    first, last = _seg_flags(tile, nct, nt)
    _fill_ext(ext_ref, p_ref[0], pp_ref[0], pn_ref[0], first, last)
    in_ctx = tile < nct
    seg_len = jnp.where(in_ctx, ctx_len, nt * t - ctx_len)
    pos = (tile - jnp.where(in_ctx, 0, nct)) * t + lax.broadcasted_iota(jnp.int32, (t, 1), 0)
    pooled = []
    for gi, w in enumerate(POOL_WINDOWS):
        lo = w // 2
        cols = slice(gi * POOL_GROUP, (gi + 1) * POOL_GROUP)
        acc = ext_ref[pl.ds(HALO - lo, t), cols]
        for kk in range(1, w):
            acc = acc + ext_ref[pl.ds(HALO - lo + kk, t), cols]
        cnt = jnp.clip(pos - lo + w, 0, seg_len) - jnp.clip(pos - lo, 0, seg_len)
        mean = acc / cnt.astype(F32)
        dlt = (mean - ext_ref[pl.ds(HALO, t), cols]).astype(BF16)
        pooled.append(jnp.dot(dlt, pw_ref[gi], preferred_element_type=F32))
    pool_y = (jnp.concatenate(pooled, axis=1) * ps_ref[...]).astype(BF16)

    d = x.shape[1]
    mix = None
    for n, ys in enumerate((pool_y, at_ref[0], lr_ref[0])):
        gz = jnp.dot(h, wg_ref[:, n * d:(n + 1) * d], preferred_element_type=F32) + bm_ref[n:n + 1, :]
        br = jnp.dot(ys, wb_ref[n], preferred_element_type=F32)
        term = _sigmoid(gz) * br
        mix = term if mix is None else mix + term
    y = jnp.dot(mix.astype(BF16), wo_ref[...], preferred_element_type=F32)
    x1 = _layer_norm(alpha * x + m[2:3] * y, g_ref[...], b_ref[...])
    x1_ref[0] = x1
    h2_ref[0] = (x1 * (1.0 + m[4:5]) + m[3:4]).astype(h2_ref.dtype)


def _merge(x, mods, p, attn, lru, w_gate, b_merge, pool_w, pool_scale, w_branch, w_out, ln_g, ln_b,
           nct, lat_only, alpha, h2_dtype):
    b, l, d = x.shape
    nt = l // ROW_TILE
    off = nct if lat_only else 0
    rows = l - off * ROW_TILE
    tile_of = lambda i: i + off
    tok_l = lambda w: pl.BlockSpec((1, ROW_TILE, w), lambda bi, i: (bi, i + off, 0))
    tok_o = lambda w: pl.BlockSpec((1, ROW_TILE, w), lambda bi, i: (bi, i, 0))
    const = lambda shp: pl.BlockSpec(shp, lambda bi, i: (0,) * len(shp))
    pm, pp, pn = _halo_specs(BRANCH_WIDTH, tile_of, l)
    return pl.pallas_call(
        functools.partial(_merge_kernel, nct=nct, nt=nt, tile_off=off, ctx_len=nct * ROW_TILE, alpha=alpha),
        grid=(b, rows // ROW_TILE),
        in_specs=[tok_l(d),
                  pl.BlockSpec((1, 1, SUBLANES, d), lambda bi, i: (bi, (i + off >= nct).astype(jnp.int32), 0, 0)),
                  pm, pp, pn,
                  tok_o(Q_WIDTH) if lat_only else tok_l(Q_WIDTH),
                  tok_l(LRU_WIDTH),
                  const(w_gate.shape), const(b_merge.shape), const(pool_w.shape), const(pool_scale.shape),
                  const(w_branch.shape), const(w_out.shape), const((1, d)), const((1, d))],
        out_specs=[tok_o(d), tok_o(d)],
        out_shape=[jax.ShapeDtypeStruct((b, rows, d), F32), jax.ShapeDtypeStruct((b, rows, d), h2_dtype)],
        scratch_shapes=[pltpu.VMEM((ROW_TILE + 2 * HALO, BRANCH_WIDTH), F32)],
        compiler_params=_cparams("parallel", "parallel"),
        name="merge_ln1",
    )(x, mods, p, p, p, attn, lru, w_gate, b_merge, pool_w, pool_scale, w_branch, w_out, ln_g, ln_b)


def _swiglu_kernel(x_ref, wg_ref, wu_ref, wd_ref, o_ref, acc_ref):
    f = pl.program_id(1)
    nf = pl.num_programs(1)
    x = x_ref[...]
    g = jnp.dot(x, wg_ref[...], preferred_element_type=F32)
    u = jnp.dot(x, wu_ref[...], preferred_element_type=F32)
    part = jnp.dot((_silu(g) * u).astype(BF16), wd_ref[...], preferred_element_type=F32)

    @pl.when(f == 0)
    def _():
        acc_ref[...] = part

    @pl.when(f > 0)
    def _():
        acc_ref[...] += part

    @pl.when(f == nf - 1)
    def _():
        o_ref[...] = acc_ref[...]


def _swiglu_dense(x, w_gate, w_up, w_down, tf):
    r, d = x.shape
    nf = w_gate.shape[1] // tf
    return pl.pallas_call(
        _swiglu_kernel,
        grid=(r // FFN_ROWS, nf),
        in_specs=[pl.BlockSpec((FFN_ROWS, d), lambda i, f: (i, 0)),
                  pl.BlockSpec((d, tf), lambda i, f: (0, f)),
                  pl.BlockSpec((d, tf), lambda i, f: (0, f)),
                  pl.BlockSpec((tf, d), lambda i, f: (f, 0))],
        out_specs=pl.BlockSpec((FFN_ROWS, d), lambda i, f: (i, 0)),
        out_shape=jax.ShapeDtypeStruct((r, d), F32),
        scratch_shapes=[pltpu.VMEM((FFN_ROWS, d), F32)],
        compiler_params=_cparams("parallel", "arbitrary"),
        name="swiglu_dense",
    )(x, w_gate, w_up, w_down)


def _moe_kernel(be_ref, nu_ref, tok_ref, dst_ref, t_hbm, wg_ref, wu_ref, wd_ref, y_hbm,
                xbuf, acc_ref, obuf, sem_in, sem_out, sem_fill):
    i = pl.program_id(0)
    f = pl.program_id(1)
    nb = pl.num_programs(0)
    nf = pl.num_programs(1)
    nu = nu_ref[0]
    rows = xbuf.shape[1]
    slot = i % 2

    def gather_copy(blk, sl, r):
        return pltpu.make_async_copy(t_hbm.at[pl.ds(tok_ref[blk * rows + r], 1)], xbuf.at[sl, pl.ds(r, 1)],
                                     sem_in.at[sl])

    def scatter_copy(blk, sl, r):
        return pltpu.make_async_copy(obuf.at[sl, pl.ds(r, 1)], y_hbm.at[pl.ds(dst_ref[blk * rows + r], 1)],
                                     sem_out.at[sl])

    def for_rows(fn):
        def body(r, c):
            fn(r)
            return c
        lax.fori_loop(0, rows, body, 0, unroll=8)

    @pl.when(f == 0)
    def _():
        @pl.when(i == 0)
        def _():
            for_rows(lambda r: gather_copy(0, 0, r).start())

        @pl.when(i + 1 < nu)
        def _():
            for_rows(lambda r: gather_copy(i + 1, 1 - slot, r).start())

        @pl.when(i < nu)
        def _():
            for_rows(lambda r: gather_copy(i, slot, r).wait())

    @pl.when(i < nu)
    def _():
        x = xbuf[slot].astype(BF16)
        g = jnp.dot(x, wg_ref[0], preferred_element_type=F32)
        u = jnp.dot(x, wu_ref[0], preferred_element_type=F32)
        part = jnp.dot((_silu(g) * u).astype(BF16), wd_ref[0], preferred_element_type=F32)

        @pl.when(f == 0)
        def _():
            acc_ref[...] = part

        @pl.when(jnp.logical_and(f > 0, f < nf - 1))
        def _():
            acc_ref[...] += part

        @pl.when(f == nf - 1)
        def _():
            @pl.when(i >= 2)
            def _():
                for_rows(lambda r: scatter_copy(i - 2, slot, r).wait())

            obuf[slot] = acc_ref[...] + part
            for_rows(lambda r: scatter_copy(i, slot, r).start())

    @pl.when(jnp.logical_and(i >= nu, f == nf - 1))
    def _():
        acc_ref[...] = jnp.zeros_like(acc_ref)
        fill = pltpu.make_async_copy(acc_ref, y_hbm.at[pl.ds(i * rows, rows)], sem_fill)
        fill.start()
        fill.wait()

    @pl.when(jnp.logical_and(i == nb - 1, f == nf - 1))
    def _():
        @pl.when(nu >= 2)
        def _():
            for_rows(lambda r: scatter_copy(nu - 2, nu % 2, r).wait())

        for_rows(lambda r: scatter_copy(nu - 1, (nu - 1) % 2, r).wait())


def _swiglu_experts(t, block_e, n_used, slot_tok, slot_dst, w_gate, w_up, w_down, tf):
    n, d = t.shape
    p = slot_tok.shape[0]
    nf = w_gate.shape[2] // tf
    assert nf >= 2

    def fcol(i, f, nu):
        return jnp.where(i < nu[0], f, nf - 1)

    return pl.pallas_call(
        _moe_kernel,
        grid_spec=pltpu.PrefetchScalarGridSpec(
            num_scalar_prefetch=4,
            grid=(p // FFN_ROWS, nf),
            in_specs=[pl.BlockSpec(memory_space=pl.ANY),
                      pl.BlockSpec((1, d, tf), lambda i, f, be, nu, st, sd: (be[i], 0, fcol(i, f, nu))),
                      pl.BlockSpec((1, d, tf), lambda i, f, be, nu, st, sd: (be[i], 0, fcol(i, f, nu))),
                      pl.BlockSpec((1, tf, d), lambda i, f, be, nu, st, sd: (be[i], fcol(i, f, nu), 0))],
            out_specs=pl.BlockSpec(memory_space=pl.ANY),
            scratch_shapes=[pltpu.VMEM((2, FFN_ROWS, d), F32), pltpu.VMEM((FFN_ROWS, d), F32),
                            pltpu.VMEM((2, FFN_ROWS, d), F32),
                            pltpu.SemaphoreType.DMA((2,)), pltpu.SemaphoreType.DMA((2,)),
                            pltpu.SemaphoreType.DMA]),
        out_shape=jax.ShapeDtypeStruct((p, d), F32),
        compiler_params=_cparams("arbitrary", "arbitrary"),
        name="swiglu_experts",
    )(block_e, n_used, slot_tok, slot_dst, t, w_gate, w_up, w_down)


def _res_ln_kernel(x_ref, mod_ref, g_ref, b_ref, *rest, alpha, top_k):
    if top_k:
        y_refs, p_ref, o_ref = rest[:top_k], rest[top_k], rest[top_k + 1]
        f = y_refs[0][...] * p_ref[0, :, 0:1]
        for kk in range(1, top_k):
            f = f + y_refs[kk][...] * p_ref[0, :, kk:kk + 1]
    else:
        f_ref, o_ref = rest
        f = f_ref[0]
    m = mod_ref[0, 0]
    o_ref[0] = _layer_norm(alpha * x_ref[0] + m[5:6] * f, g_ref[...], b_ref[...])


def _res_ln(x1, mods, ln_g, ln_b, f, probs, nct, lat_only, alpha):
    b, rows, d = x1.shape
    off = nct if lat_only else 0
    nt = rows // ROW_TILE
    tok = lambda w: pl.BlockSpec((1, ROW_TILE, w), lambda bi, i: (bi, i, 0))
    const = lambda shp: pl.BlockSpec(shp, lambda bi, i: (0,) * len(shp))
    in_specs = [tok(d),
                pl.BlockSpec((1, 1, SUBLANES, d), lambda bi, i: (bi, (i + off >= nct).astype(jnp.int32), 0, 0)),
                const((1, d)), const((1, d))]
    args = [x1, mods, ln_g, ln_b]
    top_k = 0
    if probs is None:
        in_specs.append(tok(d))
        args.append(f)
    else:
        top_k = probs.shape[2]
        for kk in range(top_k):
            in_specs.append(pl.BlockSpec((ROW_TILE, d), lambda bi, i, kk=kk: ((kk * b + bi) * nt + i, 0)))
            args.append(f)
        in_specs.append(tok(top_k))
        args.append(probs)
    return pl.pallas_call(
        functools.partial(_res_ln_kernel, alpha=alpha, top_k=top_k),
        grid=(b, nt),
        in_specs=in_specs,
        out_specs=tok(d),
        out_shape=jax.ShapeDtypeStruct((b, rows, d), F32),
        compiler_params=_cparams("parallel", "parallel"),
        name="res_ln2",
    )(*args)


def _router_kernel(x_ref, w_ref, b_ref, o_ref):
    o_ref[...] = jnp.dot(x_ref[...], w_ref[...], precision=lax.Precision.HIGHEST,
                         preferred_element_type=F32) + b_ref[...]


def _router_logits(t, w, bias):
    n, d = t.shape
    wp = jnp.zeros((d, LANES), F32).at[:, :w.shape[1]].set(w)
    bp = jnp.zeros((1, LANES), F32).at[0, :w.shape[1]].set(bias)
    out = pl.pallas_call(
        _router_kernel,
        grid=(n // FFN_ROWS,),
        in_specs=[pl.BlockSpec((FFN_ROWS, d), lambda i: (i, 0)),
                  pl.BlockSpec((d, LANES), lambda i: (0, 0)),
                  pl.BlockSpec((1, LANES), lambda i: (0, 0))],
        out_specs=pl.BlockSpec((FFN_ROWS, LANES), lambda i: (i, 0)),
        out_shape=jax.ShapeDtypeStruct((n, LANES), F32),
        compiler_params=_cparams("parallel"),
        name="router",
    )(t, wp, bp)
    return out[:, :w.shape[1]]


def _moe(t, router, router_b, w_gate, w_up, w_down, tf):
    n, d = t.shape
    n_e = router.shape[1]
    logits = _router_logits(t, router, router_b)
    top_v, top_i = lax.top_k(logits, TOP_K)
    probs = jax.nn.softmax(top_v, axis=-1)
    m = n * TOP_K
    flat_e = top_i.reshape(m)
    onehot = (flat_e[:, None] == jnp.arange(n_e, dtype=flat_e.dtype)[None, :]).astype(jnp.int32)
    csum = jnp.cumsum(onehot, axis=0)
    rank = jnp.take_along_axis(csum, flat_e[:, None], axis=1)[:, 0] - 1
    counts = csum[-1]
    padded = (counts + FFN_ROWS - 1) // FFN_ROWS * FFN_ROWS
    ends_p = jnp.cumsum(padded)
    start_p = ends_p - padded
    slot = (start_p[flat_e] + rank).astype(jnp.int32)
    nb = -(-(m + n_e * (FFN_ROWS - 1)) // FFN_ROWS)
    p = nb * FFN_ROWS
    pair = jnp.arange(m, dtype=jnp.int32)
    slot_pair = jnp.full((p,), -1, jnp.int32).at[slot].set(pair)
    is_pad = slot_pair < 0
    slot_tok = jnp.where(is_pad, 0, slot_pair // TOP_K)
    slot_dst = jnp.where(is_pad, m + jnp.cumsum(is_pad.astype(jnp.int32)) - 1,
                         (slot_pair % TOP_K) * n + slot_pair // TOP_K)
    block_e = jnp.minimum(jnp.searchsorted(ends_p, jnp.arange(nb) * FFN_ROWS, side='right'),
                          n_e - 1).astype(jnp.int32)
    n_used = (ends_p[-1:] // FFN_ROWS).astype(jnp.int32)
    y = _swiglu_experts(t, block_e, n_used, slot_tok, slot_dst, w_gate, w_up, w_down, tf)
    return y, probs


def _block_diag(w):
    nb, bi, bj = w.shape
    eye = jnp.eye(nb, dtype=w.dtype)
    return (eye[:, None, :, None] * w[:, :, None, :]).reshape(nb * bi, nb * bj)


def _rope_tables(ctx_len, s):
    nf = HEAD_DIM // 4
    inv = ROPE_THETA ** (-jnp.arange(nf, dtype=F32) / nf)
    t = jnp.arange(s)
    row = (t // GRID_W).astype(F32)[:, None] * inv
    col = (t % GRID_W).astype(F32)[:, None] * inv
    ang = jnp.concatenate([row, row, col, col], axis=1)
    sign = jnp.tile(jnp.concatenate([-jnp.ones((nf,), F32), jnp.ones((nf,), F32)]), 2)
    cos = jnp.concatenate([jnp.ones((ctx_len, HEAD_DIM), F32), jnp.cos(ang)], axis=0)
    sin = jnp.concatenate([jnp.zeros((ctx_len, HEAD_DIM), F32), jnp.sin(ang) * sign], axis=0)
    reps = LANES // HEAD_DIM
    return jnp.tile(cos, (1, reps)), jnp.tile(sin, (1, reps))


def _pad_rows(a, rows):
    return jnp.zeros((rows,) + a.shape[1:], a.dtype).at[:a.shape[0]].set(a)


def kernel(x, c, ctx, c_ctx, w_mod, b_mod, w_in, b_merge, pool_w, pool_scale, q_norm, k_norm, conv_w, conv_b, lru_wa, lru_ba, lru_wx, lru_bx, lru_lambda, w_branch, w_out, ln1_g, ln1_b, ffn_w_gate, ffn_w_up, ffn_w_down, moe_router, moe_router_b, moe_w_gate, moe_w_up, moe_w_down, ln2_g, ln2_b):
    b, s, d = x.shape
    ctx_len = ctx.shape[1]
    depth = w_in.shape[0]
    l = ctx_len + s
    assert ctx_len % ROW_TILE == 0 and s % ROW_TILE == 0 and (b * s) % FFN_ROWS == 0
    assert (b * l) % FFN_ROWS == 0 and ctx_len % Q_TILE == 0 and s % Q_TILE == 0
    nct = ctx_len // ROW_TILE
    alpha = (2 * depth) ** 0.25

    cc = _pad_rows(jnp.concatenate([c, c_ctx[None, :]], axis=0), -(-(b + 1) // SUBLANES) * SUBLANES)
    mod_all = _modulation(cc, w_mod, b_mod)

    cos_t, sin_t = _rope_tables(ctx_len, s)
    bd = _block_diag(jnp.full((LANES // HEAD_DIM, HEAD_DIM, HEAD_DIM), 1.0 / HEAD_DIM, F32)).astype(BF16)
    reps = LANES // HEAD_DIM

    xs = jnp.concatenate([ctx, x], axis=1)
    for li in range(depth):
        last = li == depth - 1
        ml = jnp.transpose(mod_all[li, :, :b], (1, 0, 2))
        mc = jnp.broadcast_to(mod_all[li, :, b][None], (b, 6, d))
        mods = jnp.stack([mc, ml], axis=1)
        mods = jnp.concatenate([mods, jnp.zeros((b, 2, SUBLANES - 6, d), F32)], axis=2)

        w_l = w_in[li].astype(BF16)
        p, q, k, v, lx, lg = _project(
            xs, mods, w_l[:, :OFF_GATE], bd,
            jnp.tile(q_norm[li], reps)[None], jnp.tile(k_norm[li], reps)[None], cos_t, sin_t, nct)
        attn = _attention(q, k, v, ctx_len, with_ctx=not last)

        cw = _pad_rows(conv_w[li], SUBLANES)
        lru_w = [jnp.concatenate([_block_diag(lru_wa[li, dr]), _block_diag(lru_wx[li, dr])], axis=1).astype(BF16)
                 for dr in range(2)]
        hf = _lru(lx, lru_w[0], lru_ba[li, 0][None], lru_bx[li, 0][None], lru_lambda[li, 0][None],
                  cw, conv_b[li][None], nct, reverse=False)
        lru_y = _lru(lx, lru_w[1], lru_ba[li, 1][None], lru_bx[li, 1][None], lru_lambda[li, 1][None],
                     cw, conv_b[li][None], nct, reverse=True, hf=hf, lg=lg)

        x1, h2 = _merge(xs, mods, p, attn, lru_y, w_l[:, OFF_GATE:], _pad_rows(b_merge[li], SUBLANES),
                        pool_w[li].astype(BF16), pool_scale[li][None], w_branch[li].astype(BF16),
                        w_out[li].astype(BF16), ln1_g[li][None], ln1_b[li][None],
                        nct, lat_only=last, alpha=alpha, h2_dtype=F32 if li % 2 else BF16)
        rows = x1.shape[1]
        if li % 2 == 0:
            jf = li // 2
            f = _swiglu_dense(h2.reshape(b * rows, d), ffn_w_gate[jf].astype(BF16), ffn_w_up[jf].astype(BF16),
                              ffn_w_down[jf].astype(BF16), tf=ffn_w_gate.shape[2] // 2)
            xs = _res_ln(x1, mods, ln2_g[li][None], ln2_b[li][None], f.reshape(b, rows, d), None,
                         nct, last, alpha)
        else:
            jf = li // 2
            assert last, "the expert layer is expected to be the last layer (latent tokens only)"
            y, probs = _moe(h2.reshape(b * rows, d), moe_router[jf], moe_router_b[jf], moe_w_gate[jf].astype(BF16),
                            moe_w_up[jf].astype(BF16), moe_w_down[jf].astype(BF16),
                            tf=moe_w_gate.shape[3] // 2)
            xs = _res_ln(x1, mods, ln2_g[li][None], ln2_b[li][None], y, probs.reshape(b, rows, -1),
                         nct, last, alpha)
    return xs
```

```python
import functools

import jax
import jax.numpy as jnp
from jax import lax
from jax.experimental import pallas as pl
from jax.experimental.pallas import tpu as pltpu

F32 = jnp.float32
BF16 = jnp.bfloat16

GRID_W = 64
POOL_WINDOWS = (2, 4, 8, 16)
BRANCH_WIDTH = 512
POOL_GROUP = BRANCH_WIDTH // len(POOL_WINDOWS)
N_Q_HEADS = 8
N_KV_HEADS = 2
HEAD_DIM = 64
GQA_GROUP = N_Q_HEADS // N_KV_HEADS
Q_WIDTH = N_Q_HEADS * HEAD_DIM
KV_WIDTH = N_KV_HEADS * HEAD_DIM
ROPE_THETA = 10000.0
ATTN_SCALE = HEAD_DIM ** -0.5
LOG2_E = 1.4426950408889634
LRU_WIDTH = BRANCH_WIDTH
LRU_BLOCKS = 8
CONV_WIDTH = 4
RG_C = 8.0
N_BRANCH = 3
N_EXPERTS = 8
TOP_K = 2
LN_EPS = 1e-5
RMS_EPS = 1e-6
OFF_GATE = BRANCH_WIDTH + Q_WIDTH + 2 * KV_WIDTH + 2 * LRU_WIDTH

LANES = 128
SUBLANES = 8
ROW_TILE = 256
Q_TILE = 256
ATTN_CHAIN = 256
FFN_ROWS = 512
HALO = SUBLANES
VMEM_LIMIT = 56 * 1024 * 1024


def _cparams(*sem):
    return pltpu.CompilerParams(dimension_semantics=sem, vmem_limit_bytes=VMEM_LIMIT)


def _sigmoid(x):
    return 0.5 * jnp.tanh(0.5 * x) + 0.5


def _store_row_tiles(ref, val):
    t, d = val.shape
    for j in range(d // LANES):
        ref[pl.ds(j, t, stride=d // LANES), :] = val[:, j * LANES:(j + 1) * LANES]


def _load_row_tiles(ref, t):
    n = ref.shape[0] // t
    return jnp.concatenate([ref[pl.ds(j, t, stride=n), :] for j in range(n)], axis=1)


def _silu(x):
    return x * _sigmoid(x)


def _layer_norm(v, g, b):
    mu = jnp.mean(v, axis=-1, keepdims=True)
    d = v - mu
    var = jnp.mean(d * d, axis=-1, keepdims=True)
    return d * lax.rsqrt(var + LN_EPS) * g + b


def _mod_kernel(c_ref, w_ref, b_ref, o_ref):
    s = _silu(c_ref[...])
    o_ref[0, 0] = jnp.dot(s, w_ref[0], precision=lax.Precision.HIGHEST,
                          preferred_element_type=F32) + b_ref[0, 0]


def _modulation(cc, w_mod, b_mod):
    depth, d, _ = w_mod.shape
    r = cc.shape[0]
    b6 = b_mod.reshape(depth, 6, 1, d)
    return pl.pallas_call(
        _mod_kernel,
        grid=(depth, 6),
        in_specs=[pl.BlockSpec((r, d), lambda l, j: (0, 0)),
                  pl.BlockSpec((1, d, d), lambda l, j: (l, 0, j)),
                  pl.BlockSpec((1, 1, 1, d), lambda l, j: (l, j, 0, 0))],
        out_specs=pl.BlockSpec((1, 1, r, d), lambda l, j: (l, j, 0, 0)),
        out_shape=jax.ShapeDtypeStruct((depth, 6, r, d), F32),
        compiler_params=_cparams("arbitrary", "arbitrary"),
        name="modulation",
    )(cc, w_mod, b6)


def _norm_rope(zc, bd, g, cos, sin):
    zz = zc * zc
    hi = zz.astype(BF16)
    lo = (zz - hi.astype(F32)).astype(BF16)
    ms = jnp.dot(hi, bd, preferred_element_type=F32) + jnp.dot(lo, bd, preferred_element_type=F32)
    y = zc * lax.rsqrt(ms + RMS_EPS) * g
    nf = HEAD_DIM // 4
    up = pltpu.roll(y, LANES - nf, 1)
    dn = pltpu.roll(y, nf, 1)
    lane = lax.broadcasted_iota(jnp.int32, y.shape, 1)
    partner = jnp.where((lane % (2 * nf)) < nf, up, dn)
    return y * cos + partner * sin


def _proj_kernel(x_ref, mod_ref, w_ref, bd_ref, qg_ref, kg_ref, cos_ref, sin_ref,
                 p_ref, q_ref, k_ref, v_ref, lx_ref, lg_ref):
    x = x_ref[0]
    m = mod_ref[0, 0]
    h = (x * (1.0 + m[1:2]) + m[0:1]).astype(BF16)
    z = jnp.dot(h, w_ref[...], preferred_element_type=F32)
    o = 0
    p_ref[0] = z[:, o:o + BRANCH_WIDTH]
    o += BRANCH_WIDTH
    bd = bd_ref[...]
    cos = cos_ref[...]
    sin = sin_ref[...]
    for c in range(Q_WIDTH // LANES):
        y = _norm_rope(z[:, o:o + LANES], bd, qg_ref[...], cos, sin) * (ATTN_SCALE * LOG2_E)
        q_ref[0, 2 * c] = y[:, :HEAD_DIM].astype(BF16)
        q_ref[0, 2 * c + 1] = y[:, HEAD_DIM:].astype(BF16)
        o += LANES
    for c in range(KV_WIDTH // LANES):
        y = _norm_rope(z[:, o:o + LANES], bd, kg_ref[...], cos, sin)
        k_ref[0, 2 * c] = y[:, :HEAD_DIM].astype(BF16)
        k_ref[0, 2 * c + 1] = y[:, HEAD_DIM:].astype(BF16)
        o += LANES
    for c in range(KV_WIDTH // LANES):
        y = z[:, o:o + LANES]
        ones = jnp.ones((y.shape[0], LANES - HEAD_DIM), F32)
        v_ref[0, 2 * c] = jnp.concatenate([y[:, :HEAD_DIM], ones], axis=1).astype(BF16)
        v_ref[0, 2 * c + 1] = jnp.concatenate([y[:, HEAD_DIM:], ones], axis=1).astype(BF16)
        o += LANES
    lx_ref[0] = z[:, o:o + LRU_WIDTH]
    o += LRU_WIDTH
    lg_ref[0] = z[:, o:o + LRU_WIDTH]


def _project(x, mods, w_a, bd, qg, kg, cos_t, sin_t, nct):
    b, l, d = x.shape
    nt = l // ROW_TILE
    wa = w_a.shape[1]
    tok = lambda w: pl.BlockSpec((1, ROW_TILE, w), lambda bi, i: (bi, i, 0))
    head = lambda n, w=HEAD_DIM: pl.BlockSpec((1, n, ROW_TILE, w), lambda bi, i: (bi, 0, i, 0))
    const = lambda shp: pl.BlockSpec(shp, lambda bi, i: (0,) * len(shp))
    return pl.pallas_call(
        _proj_kernel,
        grid=(b, nt),
        in_specs=[tok(d),
                  pl.BlockSpec((1, 1, SUBLANES, d), lambda bi, i: (bi, (i >= nct).astype(jnp.int32), 0, 0)),
                  const((d, wa)), const((LANES, LANES)), const((1, LANES)), const((1, LANES)),
                  pl.BlockSpec((ROW_TILE, LANES), lambda bi, i: (i, 0)),
                  pl.BlockSpec((ROW_TILE, LANES), lambda bi, i: (i, 0))],
        out_specs=[tok(BRANCH_WIDTH), head(N_Q_HEADS), head(N_KV_HEADS), head(N_KV_HEADS, LANES),
                   tok(LRU_WIDTH), tok(LRU_WIDTH)],
        out_shape=[jax.ShapeDtypeStruct((b, l, BRANCH_WIDTH), F32),
                   jax.ShapeDtypeStruct((b, N_Q_HEADS, l, HEAD_DIM), BF16),
                   jax.ShapeDtypeStruct((b, N_KV_HEADS, l, HEAD_DIM), BF16),
                   jax.ShapeDtypeStruct((b, N_KV_HEADS, l, LANES), BF16),
                   jax.ShapeDtypeStruct((b, l, LRU_WIDTH), F32),
                   jax.ShapeDtypeStruct((b, l, LRU_WIDTH), F32)],
        compiler_params=_cparams("parallel", "parallel"),
        name="in_proj",
    )(x, mods, w_a, bd, qg, kg, cos_t, sin_t)


def _softmax_pv(q, k, v):
    hd = q.shape[1]
    s = lax.dot_general(q, k, (((1,), (1,)), ((), ())), preferred_element_type=F32)
    m = jnp.max(s, axis=1, keepdims=True)
    p = jnp.exp2(s - m).astype(BF16)
    r = jnp.dot(p, v, preferred_element_type=F32)
    return r[:, :hd] / r[:, hd:2 * hd]


def _attn_kernel(q_ref, k_ref, v_ref, o_ref, *, n_ctx_tiles, ctx_len):
    g, tq, hd = q_ref.shape[1:]

    def run(lk):
        for j in range(g):
            for r0 in range(0, tq, ATTN_CHAIN):
                o = _softmax_pv(q_ref[0, j, r0:r0 + ATTN_CHAIN, :], k_ref[0, 0, :lk], v_ref[0, 0, :lk])
                o_ref[0, r0:r0 + ATTN_CHAIN, j * hd:(j + 1) * hd] = o.astype(o_ref.dtype)

    if n_ctx_tiles:
        @pl.when(pl.program_id(2) < n_ctx_tiles)
        def _():
            run(ctx_len)

        @pl.when(pl.program_id(2) >= n_ctx_tiles)
        def _():
            run(k_ref.shape[2])
    else:
        run(k_ref.shape[2])


def _attention(q, k, v, ctx_len, with_ctx):
    b, _, l, hd = q.shape
    nct = ctx_len // Q_TILE
    rows = l if with_ctx else l - ctx_len
    q_off = 0 if with_ctx else nct
    return pl.pallas_call(
        functools.partial(_attn_kernel, n_ctx_tiles=nct if with_ctx else 0, ctx_len=ctx_len),
        grid=(b, N_KV_HEADS, rows // Q_TILE),
        in_specs=[pl.BlockSpec((1, GQA_GROUP, Q_TILE, hd), lambda bi, h, i: (bi, h, i + q_off, 0)),
                  pl.BlockSpec((1, 1, l, hd), lambda bi, h, i: (bi, h, 0, 0)),
                  pl.BlockSpec((1, 1, l, v.shape[3]), lambda bi, h, i: (bi, h, 0, 0))],
        out_specs=pl.BlockSpec((1, Q_TILE, GQA_GROUP * hd), lambda bi, h, i: (bi, i, h)),
        out_shape=jax.ShapeDtypeStruct((b, rows, Q_WIDTH), BF16),
        compiler_params=_cparams("parallel", "parallel", "parallel"),
        name="attention",
    )(q, k, v)


def _fill_ext(ext_ref, main, prev, nxt, first, last):
    t = main.shape[0]
    ext_ref[0:HALO] = jnp.where(first, 0.0, prev)
    ext_ref[HALO:HALO + t] = main
    ext_ref[HALO + t:2 * HALO + t] = jnp.where(last, 0.0, nxt)


def _seg_flags(tile, nct, nt):
    first = jnp.logical_or(tile == 0, tile == nct)
    last = jnp.logical_or(tile == nct - 1, tile == nt - 1)
    return first, last


def _halo_specs(width, tile_of, l):
    rb = ROW_TILE // HALO
    last_blk = l // HALO - 1
    main = pl.BlockSpec((1, ROW_TILE, width), lambda bi, i: (bi, tile_of(i), 0))
    prev = pl.BlockSpec((1, HALO, width), lambda bi, i: (bi, jnp.maximum(tile_of(i) * rb - 1, 0), 0))
    nxt = pl.BlockSpec((1, HALO, width), lambda bi, i: (bi, jnp.minimum((tile_of(i) + 1) * rb, last_blk), 0))
    return main, prev, nxt


def _softplus(z):
    return jnp.maximum(z, 0.0) + jnp.log1p(jnp.exp(-jnp.abs(z)))


def _gelu_tanh(x):
    return 0.5 * x * (1.0 + jnp.tanh(0.7978845608028654 * (x + 0.044715 * (x * x * x))))


def _lru_kernel(*refs, reverse, nct, nt):
    if reverse:
        (lx_ref, lxp_ref, lxn_ref, w_ref, ba_ref, bx_ref, lam_ref, cw_ref, cb_ref, hf_ref, lg_ref,
         o_ref, ext_ref, a_ref, u_ref, h_ref) = refs
    else:
        (lx_ref, lxp_ref, lxn_ref, w_ref, ba_ref, bx_ref, lam_ref, cw_ref, cb_ref,
         o_ref, ext_ref, a_ref, u_ref, h_ref) = refs
    j = pl.program_id(1)
    tile = _lru_tile(j, reverse, nct, nt)
    t = lx_ref.shape[1]

    @pl.when(j == 0)
    def _():
        h_ref[...] = jnp.zeros_like(h_ref)

    first, last = _seg_flags(tile, nct, nt)
    _fill_ext(ext_ref, lx_ref[0], lxp_ref[0], lxn_ref[0], first, last)
    lo = CONV_WIDTH // 2
    xc = cb_ref[...]
    for kk in range(CONV_WIDTH):
        xc = xc + ext_ref[pl.ds(HALO + kk - lo, t), :] * cw_ref[kk:kk + 1, :]
    zz = jnp.dot(xc.astype(BF16), w_ref[...], preferred_element_type=F32)
    r = _sigmoid(zz[:, :LRU_WIDTH] + ba_ref[...])
    gi = _sigmoid(zz[:, LRU_WIDTH:] + bx_ref[...])
    log_a = (-RG_C * _softplus(-lam_ref[...])) * r
    a = jnp.exp(log_a)
    a_ref[...] = a
    u_ref[...] = jnp.sqrt(1.0 - a * a) * gi * xc

    row = lax.broadcasted_iota(jnp.int32, (SUBLANES, LRU_WIDTH), 0)
    n_sub = t // SUBLANES

    def sub(s, h):
        blk = (n_sub - 1 - s) if reverse else s
        r0 = pl.multiple_of(blk * SUBLANES, SUBLANES)
        aa = a_ref[pl.ds(r0, SUBLANES), :]
        uu = u_ref[pl.ds(r0, SUBLANES), :]
        for dd in (1, 2, 4):
            if reverse:
                keep = row < SUBLANES - dd
                sh = SUBLANES - dd
            else:
                keep = row >= dd
                sh = dd
            a_sh = jnp.where(keep, pltpu.roll(aa, sh, 0), 1.0)
            u_sh = jnp.where(keep, pltpu.roll(uu, sh, 0), 0.0)
            uu = aa * u_sh + uu
            aa = aa * a_sh
        hh = aa * h + uu
        if reverse:
            o_ref[0, pl.ds(r0, SUBLANES), :] = (
                (hf_ref[0, pl.ds(r0, SUBLANES), :] + hh)
                * _gelu_tanh(lg_ref[0, pl.ds(r0, SUBLANES), :])).astype(o_ref.dtype)
            edge = hh[0:1]
        else:
            o_ref[0, pl.ds(r0, SUBLANES), :] = hh
            edge = hh[SUBLANES - 1:SUBLANES]
        return jnp.broadcast_to(edge, (SUBLANES, LRU_WIDTH))

    h_ref[...] = lax.fori_loop(0, n_sub, sub, h_ref[...], unroll=4)


def _lru_tile(j, reverse, nct, nt):
    if not reverse:
        return j
    return jnp.where(j < nct, nct - 1 - j, nt - 1 - (j - nct))


def _lru(lx, w, ba, bx, lam, cw, cb, nct, reverse, hf=None, lg=None):
    b, l, wd = lx.shape
    nt = l // ROW_TILE
    tile_of = lambda i: _lru_tile(i, reverse, nct, nt)
    main, prev, nxt = _halo_specs(wd, tile_of, l)
    const = lambda shp: pl.BlockSpec(shp, lambda bi, i: (0,) * len(shp))
    in_specs = [main, prev, nxt, const(w.shape), const((1, wd)), const((1, wd)), const((1, wd)),
                const((SUBLANES, wd)), const((1, wd))]
    args = [lx, lx, lx, w, ba, bx, lam, cw, cb]
    if reverse:
        in_specs += [main, main]
        args += [hf, lg]
    return pl.pallas_call(
        functools.partial(_lru_kernel, reverse=reverse, nct=nct, nt=nt),
        grid=(b, nt),
        in_specs=in_specs,
        out_specs=main,
        out_shape=jax.ShapeDtypeStruct((b, l, wd), BF16 if reverse else F32),
        scratch_shapes=[pltpu.VMEM((ROW_TILE + 2 * HALO, wd), F32), pltpu.VMEM((ROW_TILE, wd), F32),
                        pltpu.VMEM((ROW_TILE, wd), F32), pltpu.VMEM((SUBLANES, wd), F32)],
        compiler_params=_cparams("parallel", "arbitrary"),
        name="lru_bwd" if reverse else "lru_fwd",
    )(*args)


def _merge_kernel(x_ref, mod_ref, p_ref, pp_ref, pn_ref, at_ref, lr_ref, wg_ref, bm_ref, pw_ref, ps_ref,
                  wb_ref, wo_ref, g_ref, b_ref, *rest, nct, nt, tile_off, ctx_len, alpha, route):
    if route:
        rw_ref, rb_ref, x1_ref, h2_ref, logit_ref, ext_ref = rest
    else:
        x1_ref, h2_ref, ext_ref = rest
    tile = pl.program_id(1) + tile_off
    t = x_ref.shape[1]
    x = x_ref[0]
    m = mod_ref[0, 0]
    h = (x * (1.0 + m[1:2]) + m[0:1]).astype(BF16)

    first, last = _seg_flags(tile, nct, nt)
    _fill_ext(ext_ref, p_ref[0], pp_ref[0], pn_ref[0], first, last)
    in_ctx = tile < nct
    seg_len = jnp.where(in_ctx, ctx_len, nt * t - ctx_len)
    pos = (tile - jnp.where(in_ctx, 0, nct)) * t + lax.broadcasted_iota(jnp.int32, (t, 1), 0)
    pooled = []
    for gi, w in enumerate(POOL_WINDOWS):
        lo = w // 2
        cols = slice(gi * POOL_GROUP, (gi + 1) * POOL_GROUP)
        acc = ext_ref[pl.ds(HALO - lo, t), cols]
        for kk in range(1, w):
            acc = acc + ext_ref[pl.ds(HALO - lo + kk, t), cols]
        cnt = jnp.clip(pos - lo + w, 0, seg_len) - jnp.clip(pos - lo, 0, seg_len)
        mean = acc / cnt.astype(F32)
        dlt = (mean - ext_ref[pl.ds(HALO, t), cols]).astype(BF16)
        pooled.append(jnp.dot(dlt, pw_ref[gi], preferred_element_type=F32))
    pool_y = (jnp.concatenate(pooled, axis=1) * ps_ref[...]).astype(BF16)

    d = x.shape[1]
    mix = None
    for n, ys in enumerate((pool_y, at_ref[0], lr_ref[0])):
        gz = jnp.dot(h, wg_ref[:, n * d:(n + 1) * d], preferred_element_type=F32) + bm_ref[n:n + 1, :]
        br = jnp.dot(ys, wb_ref[n], preferred_element_type=F32)
        term = _sigmoid(gz) * br
        mix = term if mix is None else mix + term
    y = jnp.dot(mix.astype(BF16), wo_ref[...], preferred_element_type=F32)
    x1 = _layer_norm(alpha * x + m[2:3] * y, g_ref[...], b_ref[...])
    x1_ref[0] = x1
    h2 = x1 * (1.0 + m[4:5]) + m[3:4]
    if route:
        logit_ref[0] = jnp.dot(h2, rw_ref[...], precision=lax.Precision.HIGHEST,
                               preferred_element_type=F32) + rb_ref[...]
        _store_row_tiles(h2_ref.at[0], h2)
    else:
        h2_ref[0] = h2.astype(h2_ref.dtype)


def _merge(x, mods, p, attn, lru, w_gate, b_merge, pool_w, pool_scale, w_branch, w_out, ln_g, ln_b,
           nct, lat_only, alpha, router=None):
    b, l, d = x.shape
    nt = l // ROW_TILE
    off = nct if lat_only else 0
    rows = l - off * ROW_TILE
    tile_of = lambda i: i + off
    tok_l = lambda w: pl.BlockSpec((1, ROW_TILE, w), lambda bi, i: (bi, i + off, 0))
    tok_o = lambda w: pl.BlockSpec((1, ROW_TILE, w), lambda bi, i: (bi, i, 0))
    const = lambda shp: pl.BlockSpec(shp, lambda bi, i: (0,) * len(shp))
    pm, pp, pn = _halo_specs(BRANCH_WIDTH, tile_of, l)
    in_specs = [tok_l(d),
                pl.BlockSpec((1, 1, SUBLANES, d), lambda bi, i: (bi, (i + off >= nct).astype(jnp.int32), 0, 0)),
                pm, pp, pn,
                tok_o(Q_WIDTH) if lat_only else tok_l(Q_WIDTH),
                tok_l(LRU_WIDTH),
                const(w_gate.shape), const(b_merge.shape), const(pool_w.shape), const(pool_scale.shape),
                const(w_branch.shape), const(w_out.shape), const((1, d)), const((1, d))]
    args = [x, mods, p, p, p, attn, lru, w_gate, b_merge, pool_w, pool_scale, w_branch, w_out, ln_g, ln_b]
    out_specs = [tok_o(d)]
    out_shape = [jax.ShapeDtypeStruct((b, rows, d), F32)]
    if router is None:
        out_specs.append(tok_o(d))
        out_shape.append(jax.ShapeDtypeStruct((b, rows, d), BF16))
    else:
        n_sub = d // LANES
        in_specs += [const(router[0].shape), const(router[1].shape)]
        args += list(router)
        out_specs += [pl.BlockSpec((1, ROW_TILE * n_sub, LANES), lambda bi, i: (bi, i, 0)), tok_o(LANES)]
        out_shape += [jax.ShapeDtypeStruct((b, rows * n_sub, LANES), F32),
                      jax.ShapeDtypeStruct((b, rows, LANES), F32)]
    return pl.pallas_call(
        functools.partial(_merge_kernel, nct=nct, nt=nt, tile_off=off, ctx_len=nct * ROW_TILE, alpha=alpha,
                          route=router is not None),
        grid=(b, rows // ROW_TILE),
        in_specs=in_specs,
        out_specs=out_specs,
        out_shape=out_shape,
        scratch_shapes=[pltpu.VMEM((ROW_TILE + 2 * HALO, BRANCH_WIDTH), F32)],
        compiler_params=_cparams("parallel", "parallel"),
        name="merge_ln1",
    )(*args)


def _swiglu_kernel(x_ref, wg_ref, wu_ref, wd_ref, x1_ref, *rest, alpha):
    mod_refs, (g_ref, b_ref, o_ref, acc_ref) = rest[:-4], rest[-4:]
    f = pl.program_id(1)
    nf = pl.num_programs(1)
    x = x_ref[...]
    g = jnp.dot(x, wg_ref[...], preferred_element_type=F32)
    u = jnp.dot(x, wu_ref[...], preferred_element_type=F32)
    part = jnp.dot((_silu(g) * u).astype(BF16), wd_ref[...], preferred_element_type=F32)

    @pl.when(f == 0)
    def _():
        acc_ref[...] = part

    @pl.when(jnp.logical_and(f > 0, f < nf - 1))
    def _():
        acc_ref[...] += part

    @pl.when(f == nf - 1)
    def _():
        for hh, mod_ref in enumerate(mod_refs):
            rs = slice(hh * ROW_TILE, (hh + 1) * ROW_TILE)
            ff = acc_ref[rs, :] + part[rs]
            o_ref[rs, :] = _layer_norm(alpha * x1_ref[rs, :] + mod_ref[0, 0, 5:6, :] * ff, g_ref[...], b_ref[...])


def _swiglu_dense_ln(h2, x1, mods, ln_g, ln_b, w_gate, w_up, w_down, tf, nct, alpha):
    b, l, d = x1.shape
    r = b * l
    nf = w_gate.shape[1] // tf
    assert nf >= 2
    ntl = l // ROW_TILE
    halves = FFN_ROWS // ROW_TILE

    def mod_spec(hh):
        def idx(i, f):
            tile = i * halves + hh
            return (tile // ntl, (tile % ntl >= nct).astype(jnp.int32), 0, 0)
        return pl.BlockSpec((1, 1, SUBLANES, d), idx)

    out = pl.pallas_call(
        functools.partial(_swiglu_kernel, alpha=alpha),
        grid=(r // FFN_ROWS, nf),
        in_specs=[pl.BlockSpec((FFN_ROWS, d), lambda i, f: (i, 0)),
                  pl.BlockSpec((d, tf), lambda i, f: (0, f)),
                  pl.BlockSpec((d, tf), lambda i, f: (0, f)),
                  pl.BlockSpec((tf, d), lambda i, f: (f, 0)),
                  pl.BlockSpec((FFN_ROWS, d), lambda i, f: (i, 0))]
                 + [mod_spec(hh) for hh in range(halves)]
                 + [pl.BlockSpec((1, d), lambda i, f: (0, 0)), pl.BlockSpec((1, d), lambda i, f: (0, 0))],
        out_specs=pl.BlockSpec((FFN_ROWS, d), lambda i, f: (i, 0)),
        out_shape=jax.ShapeDtypeStruct((r, d), F32),
        scratch_shapes=[pltpu.VMEM((FFN_ROWS, d), F32)],
        compiler_params=_cparams("parallel", "arbitrary"),
        name="swiglu_dense_ln2",
    )(h2.reshape(r, d), w_gate, w_up, w_down, x1.reshape(r, d), *([mods] * halves), ln_g, ln_b)
    return out.reshape(b, l, d)


def _moe_kernel(be_ref, nu_ref, tok_ref, dst_ref, t_hbm, wg_ref, wu_ref, wd_ref, y_hbm,
                xbuf, acc_ref, obuf, sem_in, sem_out, sem_fill):
    i = pl.program_id(0)
    f = pl.program_id(1)
    nb = pl.num_programs(0)
    nf = pl.num_programs(1)
    nu = nu_ref[0]
    rows = acc_ref.shape[0]
    sub = xbuf.shape[1] // rows
    slot = i % 2

    def tile_rows(start):
        return pl.ds(pl.multiple_of(start, sub), sub)

    def gather_copy(blk, sl, r):
        return pltpu.make_async_copy(t_hbm.at[tile_rows(tok_ref[blk * rows + r])], xbuf.at[sl, tile_rows(r * sub)],
                                     sem_in.at[sl])

    def scatter_copy(blk, sl, r):
        return pltpu.make_async_copy(obuf.at[sl, tile_rows(r * sub)], y_hbm.at[tile_rows(dst_ref[blk * rows + r])],
                                     sem_out.at[sl])

    def for_rows(fn):
        def body(r, c):
            fn(r)
            return c
        lax.fori_loop(0, rows, body, 0, unroll=8)

    @pl.when(f == 0)
    def _():
        @pl.when(i == 0)
        def _():
            for_rows(lambda r: gather_copy(0, 0, r).start())

        @pl.when(i + 1 < nu)
        def _():
            for_rows(lambda r: gather_copy(i + 1, 1 - slot, r).start())

        @pl.when(i < nu)
        def _():
            for_rows(lambda r: gather_copy(i, slot, r).wait())

    @pl.when(i < nu)
    def _():
        x = _load_row_tiles(xbuf.at[slot], rows).astype(BF16)
        g = jnp.dot(x, wg_ref[0], preferred_element_type=F32)
        u = jnp.dot(x, wu_ref[0], preferred_element_type=F32)
        part = jnp.dot((_silu(g) * u).astype(BF16), wd_ref[0], preferred_element_type=F32)

        @pl.when(f == 0)
        def _():
            acc_ref[...] = part

        @pl.when(jnp.logical_and(f > 0, f < nf - 1))
        def _():
            acc_ref[...] += part

        @pl.when(f == nf - 1)
        def _():
            @pl.when(i >= 2)
            def _():
                for_rows(lambda r: scatter_copy(i - 2, slot, r).wait())

            _store_row_tiles(obuf.at[slot], acc_ref[...] + part)
            for_rows(lambda r: scatter_copy(i, slot, r).start())

    @pl.when(jnp.logical_and(i >= nu, f == nf - 1))
    def _():
        xbuf[0] = jnp.zeros(xbuf.shape[1:], xbuf.dtype)
        fill = pltpu.make_async_copy(xbuf.at[0], y_hbm.at[pl.ds(i * rows * sub, rows * sub)], sem_fill)
        fill.start()
        fill.wait()

    @pl.when(jnp.logical_and(i == nb - 1, f == nf - 1))
    def _():
        @pl.when(nu >= 2)
        def _():
            for_rows(lambda r: scatter_copy(nu - 2, nu % 2, r).wait())

        for_rows(lambda r: scatter_copy(nu - 1, (nu - 1) % 2, r).wait())


def _swiglu_experts(t, block_e, n_used, slot_tok, slot_dst, w_gate, w_up, w_down, tf):
    d = w_gate.shape[1]
    sub = d // LANES
    p = slot_tok.shape[0]
    nf = w_gate.shape[2] // tf
    assert nf >= 2

    def fcol(i, f, nu):
        return jnp.where(i < nu[0], f, nf - 1)

    return pl.pallas_call(
        _moe_kernel,
        grid_spec=pltpu.PrefetchScalarGridSpec(
            num_scalar_prefetch=4,
            grid=(p // FFN_ROWS, nf),
            in_specs=[pl.BlockSpec(memory_space=pl.ANY),
                      pl.BlockSpec((1, d, tf), lambda i, f, be, nu, st, sd: (be[i], 0, fcol(i, f, nu))),
                      pl.BlockSpec((1, d, tf), lambda i, f, be, nu, st, sd: (be[i], 0, fcol(i, f, nu))),
                      pl.BlockSpec((1, tf, d), lambda i, f, be, nu, st, sd: (be[i], fcol(i, f, nu), 0))],
            out_specs=pl.BlockSpec(memory_space=pl.ANY),
            scratch_shapes=[pltpu.VMEM((2, FFN_ROWS * sub, LANES), F32), pltpu.VMEM((FFN_ROWS, d), F32),
                            pltpu.VMEM((2, FFN_ROWS * sub, LANES), F32),
                            pltpu.SemaphoreType.DMA((2,)), pltpu.SemaphoreType.DMA((2,)),
                            pltpu.SemaphoreType.DMA]),
        out_shape=jax.ShapeDtypeStruct((p * sub, LANES), F32),
        compiler_params=_cparams("arbitrary", "arbitrary"),
        name="swiglu_experts",
    )(block_e, n_used, slot_tok, slot_dst, t, w_gate, w_up, w_down)


def _combine_ln_kernel(x_ref, mod_ref, g_ref, b_ref, *rest, alpha):
    y_refs, p_ref, o_ref = rest[:-2], rest[-2], rest[-1]
    t = x_ref.shape[1]
    f = None
    for kk, y_ref in enumerate(y_refs):
        term = _load_row_tiles(y_ref, t) * p_ref[0, :, kk:kk + 1]
        f = term if f is None else f + term
    m = mod_ref[0, 0]
    o_ref[0] = _layer_norm(alpha * x_ref[0] + m[5:6] * f, g_ref[...], b_ref[...])


def _combine_ln(x1, mods, ln_g, ln_b, y, probs, alpha):
    b, rows, d = x1.shape
    nt = rows // ROW_TILE
    sub = d // LANES
    top_k = probs.shape[2]
    tok = lambda w: pl.BlockSpec((1, ROW_TILE, w), lambda bi, i: (bi, i, 0))
    const = lambda shp: pl.BlockSpec(shp, lambda bi, i: (0,) * len(shp))
    y_specs = [pl.BlockSpec((ROW_TILE * sub, LANES), lambda bi, i, kk=kk: ((kk * b + bi) * nt + i, 0))
               for kk in range(top_k)]
    return pl.pallas_call(
        functools.partial(_combine_ln_kernel, alpha=alpha),
        grid=(b, nt),
        in_specs=[tok(d), pl.BlockSpec((1, 1, SUBLANES, d), lambda bi, i: (bi, 1, 0, 0)),
                  const((1, d)), const((1, d))] + y_specs + [tok(top_k)],
        out_specs=tok(d),
        out_shape=jax.ShapeDtypeStruct((b, rows, d), F32),
        compiler_params=_cparams("parallel", "parallel"),
        name="combine_ln2",
    )(x1, mods, ln_g, ln_b, *([y] * top_k), probs)


def _moe(t, logits, w_gate, w_up, w_down, tf):
    n, n_e = logits.shape
    sub = t.shape[0] // n
    top_v, top_i = lax.top_k(logits, TOP_K)
    probs = jax.nn.softmax(top_v, axis=-1)
    m = n * TOP_K
    flat_e = top_i.reshape(m)
    onehot = (flat_e[:, None] == jnp.arange(n_e, dtype=flat_e.dtype)[None, :]).astype(jnp.int32)
    csum = jnp.cumsum(onehot, axis=0)
    rank = jnp.take_along_axis(csum, flat_e[:, None], axis=1)[:, 0] - 1
    counts = csum[-1]
    padded = (counts + FFN_ROWS - 1) // FFN_ROWS * FFN_ROWS
    ends_p = jnp.cumsum(padded)
    start_p = ends_p - padded
    slot = (start_p[flat_e] + rank).astype(jnp.int32)
    nb = -(-(m + n_e * (FFN_ROWS - 1)) // FFN_ROWS)
    p = nb * FFN_ROWS
    pair = jnp.arange(m, dtype=jnp.int32)
    slot_pair = jnp.full((p,), -1, jnp.int32).at[slot].set(pair)
    is_pad = slot_pair < 0
    slot_tok = jnp.where(is_pad, 0, slot_pair // TOP_K) * sub
    slot_dst = jnp.where(is_pad, m + jnp.cumsum(is_pad.astype(jnp.int32)) - 1,
                         (slot_pair % TOP_K) * n + slot_pair // TOP_K) * sub
    block_e = jnp.minimum(jnp.searchsorted(ends_p, jnp.arange(nb) * FFN_ROWS, side='right'),
                          n_e - 1).astype(jnp.int32)
    n_used = (ends_p[-1:] // FFN_ROWS).astype(jnp.int32)
    y = _swiglu_experts(t, block_e, n_used, slot_tok, slot_dst, w_gate, w_up, w_down, tf)
    return y, probs


def _block_diag(w):
    nb, bi, bj = w.shape
    eye = jnp.eye(nb, dtype=w.dtype)
    return (eye[:, None, :, None] * w[:, :, None, :]).reshape(nb * bi, nb * bj)


def _rope_tables(ctx_len, s):
    nf = HEAD_DIM // 4
    inv = ROPE_THETA ** (-jnp.arange(nf, dtype=F32) / nf)
    t = jnp.arange(s)
    row = (t // GRID_W).astype(F32)[:, None] * inv
    col = (t % GRID_W).astype(F32)[:, None] * inv
    ang = jnp.concatenate([row, row, col, col], axis=1)
    sign = jnp.tile(jnp.concatenate([-jnp.ones((nf,), F32), jnp.ones((nf,), F32)]), 2)
    cos = jnp.concatenate([jnp.ones((ctx_len, HEAD_DIM), F32), jnp.cos(ang)], axis=0)
    sin = jnp.concatenate([jnp.zeros((ctx_len, HEAD_DIM), F32), jnp.sin(ang) * sign], axis=0)
    reps = LANES // HEAD_DIM
    return jnp.tile(cos, (1, reps)), jnp.tile(sin, (1, reps))


def _pad_rows(a, rows):
    return jnp.zeros((rows,) + a.shape[1:], a.dtype).at[:a.shape[0]].set(a)


def kernel(x, c, ctx, c_ctx, w_mod, b_mod, w_in, b_merge, pool_w, pool_scale, q_norm, k_norm, conv_w, conv_b, lru_wa, lru_ba, lru_wx, lru_bx, lru_lambda, w_branch, w_out, ln1_g, ln1_b, ffn_w_gate, ffn_w_up, ffn_w_down, moe_router, moe_router_b, moe_w_gate, moe_w_up, moe_w_down, ln2_g, ln2_b):
    b, s, d = x.shape
    ctx_len = ctx.shape[1]
    depth = w_in.shape[0]
    l = ctx_len + s
    assert ctx_len % ROW_TILE == 0 and s % ROW_TILE == 0 and (b * s) % FFN_ROWS == 0
    assert (b * l) % FFN_ROWS == 0 and ctx_len % Q_TILE == 0 and s % Q_TILE == 0
    nct = ctx_len // ROW_TILE
    alpha = (2 * depth) ** 0.25

    cc = _pad_rows(jnp.concatenate([c, c_ctx[None, :]], axis=0), -(-(b + 1) // SUBLANES) * SUBLANES)
    mod_all = _modulation(cc, w_mod, b_mod)

    cos_t, sin_t = _rope_tables(ctx_len, s)
    bd = _block_diag(jnp.full((LANES // HEAD_DIM, HEAD_DIM, HEAD_DIM), 1.0 / HEAD_DIM, F32)).astype(BF16)
    reps = LANES // HEAD_DIM

    xs = jnp.concatenate([ctx, x], axis=1)
    for li in range(depth):
        last = li == depth - 1
        ml = jnp.transpose(mod_all[li, :, :b], (1, 0, 2))
        mc = jnp.broadcast_to(mod_all[li, :, b][None], (b, 6, d))
        mods = jnp.stack([mc, ml], axis=1)
        mods = jnp.concatenate([mods, jnp.zeros((b, 2, SUBLANES - 6, d), F32)], axis=2)

        w_l = w_in[li].astype(BF16)
        p, q, k, v, lx, lg = _project(
            xs, mods, w_l[:, :OFF_GATE], bd,
            jnp.tile(q_norm[li], reps)[None], jnp.tile(k_norm[li], reps)[None], cos_t, sin_t, nct)
        attn = _attention(q, k, v, ctx_len, with_ctx=not last)

        cw = _pad_rows(conv_w[li], SUBLANES)
        lru_w = [jnp.concatenate([_block_diag(lru_wa[li, dr]), _block_diag(lru_wx[li, dr])], axis=1).astype(BF16)
                 for dr in range(2)]
        hf = _lru(lx, lru_w[0], lru_ba[li, 0][None], lru_bx[li, 0][None], lru_lambda[li, 0][None],
                  cw, conv_b[li][None], nct, reverse=False)
        lru_y = _lru(lx, lru_w[1], lru_ba[li, 1][None], lru_bx[li, 1][None], lru_lambda[li, 1][None],
                     cw, conv_b[li][None], nct, reverse=True, hf=hf, lg=lg)

        jf = li // 2
        merge_args = (xs, mods, p, attn, lru_y, w_l[:, OFF_GATE:], _pad_rows(b_merge[li], SUBLANES),
                      pool_w[li].astype(BF16), pool_scale[li][None], w_branch[li].astype(BF16),
                      w_out[li].astype(BF16), ln1_g[li][None], ln1_b[li][None])
        if li % 2 == 0:
            assert not last, "the dense layer is expected to carry the context tokens along"
            x1, h2 = _merge(*merge_args, nct, lat_only=False, alpha=alpha)
            xs = _swiglu_dense_ln(h2, x1, mods, ln2_g[li][None], ln2_b[li][None], ffn_w_gate[jf].astype(BF16),
                                  ffn_w_up[jf].astype(BF16), ffn_w_down[jf].astype(BF16),
                                  tf=ffn_w_gate.shape[2] // 2, nct=nct, alpha=alpha)
        else:
            assert last, "the expert layer is expected to be the last layer (latent tokens only)"
            n_e = moe_router.shape[2]
            rw = jnp.zeros((d, LANES), F32).at[:, :n_e].set(moe_router[jf])
            rb = jnp.zeros((1, LANES), F32).at[0, :n_e].set(moe_router_b[jf])
            x1, h2, logits = _merge(*merge_args, nct, lat_only=True, alpha=alpha, router=(rw, rb))
            y, probs = _moe(h2.reshape(-1, LANES), logits.reshape(b * s, LANES)[:, :n_e],
                            moe_w_gate[jf].astype(BF16), moe_w_up[jf].astype(BF16), moe_w_down[jf].astype(BF16),
                            tf=moe_w_gate.shape[3] // 2)
            xs = _combine_ln(x1, mods, ln2_g[li][None], ln2_b[li][None], y, probs.reshape(b, s, -1), alpha)
    return xs
```

```python
import functools

import jax
import jax.numpy as jnp
from jax import lax
from jax.experimental import pallas as pl
from jax.experimental.pallas import tpu as pltpu

F32 = jnp.float32
BF16 = jnp.bfloat16

GRID_W = 64
POOL_WINDOWS = (2, 4, 8, 16)
BRANCH_WIDTH = 512
POOL_GROUP = BRANCH_WIDTH // len(POOL_WINDOWS)
N_Q_HEADS = 8
N_KV_HEADS = 2
HEAD_DIM = 64
GQA_GROUP = N_Q_HEADS // N_KV_HEADS
Q_WIDTH = N_Q_HEADS * HEAD_DIM
KV_WIDTH = N_KV_HEADS * HEAD_DIM
ROPE_THETA = 10000.0
ATTN_SCALE = HEAD_DIM ** -0.5
LOG2_E = 1.4426950408889634
LRU_WIDTH = BRANCH_WIDTH
LRU_BLOCKS = 8
CONV_WIDTH = 4
RG_C = 8.0
N_BRANCH = 3
N_EXPERTS = 8
TOP_K = 2
LN_EPS = 1e-5
RMS_EPS = 1e-6
OFF_GATE = BRANCH_WIDTH + Q_WIDTH + 2 * KV_WIDTH + 2 * LRU_WIDTH

LANES = 128
SUBLANES = 8
ROW_TILE = 256
Q_TILE = 256
ATTN_CHAIN = 256
FFN_ROWS = 512
HALO = SUBLANES
VMEM_LIMIT = 56 * 1024 * 1024


def _cparams(*sem):
    return pltpu.CompilerParams(dimension_semantics=sem, vmem_limit_bytes=VMEM_LIMIT)


def _sigmoid(x):
    return 0.5 * jnp.tanh(0.5 * x) + 0.5


def _store_row_tiles(ref, val):
    t, d = val.shape
    for j in range(d // LANES):
        ref[pl.ds(j, t, stride=d // LANES), :] = val[:, j * LANES:(j + 1) * LANES]


def _load_row_tiles(ref, t):
    n = ref.shape[0] // t
    return jnp.concatenate([ref[pl.ds(j, t, stride=n), :] for j in range(n)], axis=1)


def _silu(x):
    return x * _sigmoid(x)


def _layer_norm(v, g, b):
    mu = jnp.mean(v, axis=-1, keepdims=True)
    d = v - mu
    var = jnp.mean(d * d, axis=-1, keepdims=True)
    return d * lax.rsqrt(var + LN_EPS) * g + b


def _mod_kernel(c_ref, w_ref, b_ref, o_ref):
    s = _silu(c_ref[...])
    o_ref[0, 0] = jnp.dot(s, w_ref[0], precision=lax.Precision.HIGHEST,
                          preferred_element_type=F32) + b_ref[0, 0]


def _modulation(cc, w_mod, b_mod):
    depth, d, _ = w_mod.shape
    r = cc.shape[0]
    b6 = b_mod.reshape(depth, 6, 1, d)
    return pl.pallas_call(
        _mod_kernel,
        grid=(depth, 6),
        in_specs=[pl.BlockSpec((r, d), lambda l, j: (0, 0)),
                  pl.BlockSpec((1, d, d), lambda l, j: (l, 0, j)),
                  pl.BlockSpec((1, 1, 1, d), lambda l, j: (l, j, 0, 0))],
        out_specs=pl.BlockSpec((1, 1, r, d), lambda l, j: (l, j, 0, 0)),
        out_shape=jax.ShapeDtypeStruct((depth, 6, r, d), F32),
        compiler_params=_cparams("arbitrary", "arbitrary"),
        name="modulation",
    )(cc, w_mod, b6)


def _norm_rope(zc, bd, g, cos, sin):
    zz = zc * zc
    hi = zz.astype(BF16)
    lo = (zz - hi.astype(F32)).astype(BF16)
    ms = jnp.dot(hi, bd, preferred_element_type=F32) + jnp.dot(lo, bd, preferred_element_type=F32)
    y = zc * lax.rsqrt(ms + RMS_EPS) * g
    nf = HEAD_DIM // 4
    up = pltpu.roll(y, LANES - nf, 1)
    dn = pltpu.roll(y, nf, 1)
    lane = lax.broadcasted_iota(jnp.int32, y.shape, 1)
    partner = jnp.where((lane % (2 * nf)) < nf, up, dn)
    return y * cos + partner * sin


def _proj_kernel(x_ref, mod_ref, w_ref, bd_ref, qg_ref, kg_ref, cos_ref, sin_ref,
                 p_ref, q_ref, k_ref, v_ref, lx_ref, lg_ref):
    x = x_ref[0]
    m = mod_ref[0, 0]
    h = (x * (1.0 + m[1:2]) + m[0:1]).astype(BF16)
    z = jnp.dot(h, w_ref[...], preferred_element_type=F32)
    o = 0
    p_ref[0] = z[:, o:o + BRANCH_WIDTH]
    o += BRANCH_WIDTH
    bd = bd_ref[...]
    cos = cos_ref[...]
    sin = sin_ref[...]
    for c in range(Q_WIDTH // LANES):
        y = _norm_rope(z[:, o:o + LANES], bd, qg_ref[...], cos, sin) * (ATTN_SCALE * LOG2_E)
        q_ref[0, 2 * c] = y[:, :HEAD_DIM].astype(BF16)
        q_ref[0, 2 * c + 1] = y[:, HEAD_DIM:].astype(BF16)
        o += LANES
    for c in range(KV_WIDTH // LANES):
        y = _norm_rope(z[:, o:o + LANES], bd, kg_ref[...], cos, sin)
        k_ref[0, 2 * c] = y[:, :HEAD_DIM].astype(BF16)
        k_ref[0, 2 * c + 1] = y[:, HEAD_DIM:].astype(BF16)
        o += LANES
    for c in range(KV_WIDTH // LANES):
        y = z[:, o:o + LANES]
        ones = jnp.ones((y.shape[0], LANES - HEAD_DIM), F32)
        v_ref[0, 2 * c] = jnp.concatenate([y[:, :HEAD_DIM], ones], axis=1).astype(BF16)
        v_ref[0, 2 * c + 1] = jnp.concatenate([y[:, HEAD_DIM:], ones], axis=1).astype(BF16)
        o += LANES
    lx_ref[0] = z[:, o:o + LRU_WIDTH]
    o += LRU_WIDTH
    lg_ref[0] = z[:, o:o + LRU_WIDTH]


def _project(x, mods, w_a, bd, qg, kg, cos_t, sin_t, nct):
    b, l, d = x.shape
    nt = l // ROW_TILE
    wa = w_a.shape[1]
    tok = lambda w: pl.BlockSpec((1, ROW_TILE, w), lambda bi, i: (bi, i, 0))
    head = lambda n, w=HEAD_DIM: pl.BlockSpec((1, n, ROW_TILE, w), lambda bi, i: (bi, 0, i, 0))
    const = lambda shp: pl.BlockSpec(shp, lambda bi, i: (0,) * len(shp))
    return pl.pallas_call(
        _proj_kernel,
        grid=(b, nt),
        in_specs=[tok(d),
                  pl.BlockSpec((1, 1, SUBLANES, d), lambda bi, i: (bi, (i >= nct).astype(jnp.int32), 0, 0)),
                  const((d, wa)), const((LANES, LANES)), const((1, LANES)), const((1, LANES)),
                  pl.BlockSpec((ROW_TILE, LANES), lambda bi, i: (i, 0)),
                  pl.BlockSpec((ROW_TILE, LANES), lambda bi, i: (i, 0))],
        out_specs=[tok(BRANCH_WIDTH), head(N_Q_HEADS), head(N_KV_HEADS), head(N_KV_HEADS, LANES),
                   tok(LRU_WIDTH), tok(LRU_WIDTH)],
        out_shape=[jax.ShapeDtypeStruct((b, l, BRANCH_WIDTH), F32),
                   jax.ShapeDtypeStruct((b, N_Q_HEADS, l, HEAD_DIM), BF16),
                   jax.ShapeDtypeStruct((b, N_KV_HEADS, l, HEAD_DIM), BF16),
                   jax.ShapeDtypeStruct((b, N_KV_HEADS, l, LANES), BF16),
                   jax.ShapeDtypeStruct((b, l, LRU_WIDTH), F32),
                   jax.ShapeDtypeStruct((b, l, LRU_WIDTH), F32)],
        compiler_params=_cparams("parallel", "parallel"),
        name="in_proj",
    )(x, mods, w_a, bd, qg, kg, cos_t, sin_t)


def _softmax_pv(q, k, v):
    hd = q.shape[1]
    s = lax.dot_general(q, k, (((1,), (1,)), ((), ())), preferred_element_type=F32)
    m = jnp.max(s, axis=1, keepdims=True)
    p = jnp.exp2(s - m).astype(BF16)
    r = jnp.dot(p, v, preferred_element_type=F32)
    return r[:, :hd] / r[:, hd:2 * hd]


def _attn_kernel(q_ref, k_ref, v_ref, o_ref, *, n_ctx_tiles, ctx_len):
    g, tq, hd = q_ref.shape[1:]

    def run(lk):
        for j in range(g):
            for r0 in range(0, tq, ATTN_CHAIN):
                o = _softmax_pv(q_ref[0, j, r0:r0 + ATTN_CHAIN, :], k_ref[0, 0, :lk], v_ref[0, 0, :lk])
                o_ref[0, r0:r0 + ATTN_CHAIN, j * hd:(j + 1) * hd] = o.astype(o_ref.dtype)

    if n_ctx_tiles:
        @pl.when(pl.program_id(2) < n_ctx_tiles)
        def _():
            run(ctx_len)

        @pl.when(pl.program_id(2) >= n_ctx_tiles)
        def _():
            run(k_ref.shape[2])
    else:
        run(k_ref.shape[2])


def _attention(q, k, v, ctx_len, with_ctx):
    b, _, l, hd = q.shape
    nct = ctx_len // Q_TILE
    rows = l if with_ctx else l - ctx_len
    q_off = 0 if with_ctx else nct
    return pl.pallas_call(
        functools.partial(_attn_kernel, n_ctx_tiles=nct if with_ctx else 0, ctx_len=ctx_len),
        grid=(b, N_KV_HEADS, rows // Q_TILE),
        in_specs=[pl.BlockSpec((1, GQA_GROUP, Q_TILE, hd), lambda bi, h, i: (bi, h, i + q_off, 0)),
                  pl.BlockSpec((1, 1, l, hd), lambda bi, h, i: (bi, h, 0, 0)),
                  pl.BlockSpec((1, 1, l, v.shape[3]), lambda bi, h, i: (bi, h, 0, 0))],
        out_specs=pl.BlockSpec((1, Q_TILE, GQA_GROUP * hd), lambda bi, h, i: (bi, i, h)),
        out_shape=jax.ShapeDtypeStruct((b, rows, Q_WIDTH), BF16),
        compiler_params=_cparams("parallel", "parallel", "parallel"),
        name="attention",
    )(q, k, v)


def _fill_ext(ext_ref, main, prev, nxt, first, last):
    t = main.shape[0]
    ext_ref[0:HALO] = jnp.where(first, 0.0, prev)
    ext_ref[HALO:HALO + t] = main
    ext_ref[HALO + t:2 * HALO + t] = jnp.where(last, 0.0, nxt)


def _seg_flags(tile, nct, nt):
    first = jnp.logical_or(tile == 0, tile == nct)
    last = jnp.logical_or(tile == nct - 1, tile == nt - 1)
    return first, last


def _halo_specs(width, tile_of, l):
    rb = ROW_TILE // HALO
    last_blk = l // HALO - 1
    main = pl.BlockSpec((1, ROW_TILE, width), lambda bi, i: (bi, tile_of(i), 0))
    prev = pl.BlockSpec((1, HALO, width), lambda bi, i: (bi, jnp.maximum(tile_of(i) * rb - 1, 0), 0))
    nxt = pl.BlockSpec((1, HALO, width), lambda bi, i: (bi, jnp.minimum((tile_of(i) + 1) * rb, last_blk), 0))
    return main, prev, nxt


def _softplus(z):
    return jnp.maximum(z, 0.0) + jnp.log1p(jnp.exp(-jnp.abs(z)))


def _gelu_tanh(x):
    return 0.5 * x * (1.0 + jnp.tanh(0.7978845608028654 * (x + 0.044715 * (x * x * x))))


def _lru_kernel(*refs, reverse, nct, nt):
    if reverse:
        (lx_ref, lxp_ref, lxn_ref, w_ref, ba_ref, bx_ref, lam_ref, cw_ref, cb_ref, hf_ref, lg_ref,
         o_ref, ext_ref, a_ref, u_ref, h_ref) = refs
    else:
        (lx_ref, lxp_ref, lxn_ref, w_ref, ba_ref, bx_ref, lam_ref, cw_ref, cb_ref,
         o_ref, ext_ref, a_ref, u_ref, h_ref) = refs
    j = pl.program_id(1)
    tile = _lru_tile(j, reverse, nct, nt)
    t = lx_ref.shape[1]

    @pl.when(j == 0)
    def _():
        h_ref[...] = jnp.zeros_like(h_ref)

    first, last = _seg_flags(tile, nct, nt)
    _fill_ext(ext_ref, lx_ref[0], lxp_ref[0], lxn_ref[0], first, last)
    lo = CONV_WIDTH // 2
    xc = cb_ref[...]
    for kk in range(CONV_WIDTH):
        xc = xc + ext_ref[pl.ds(HALO + kk - lo, t), :] * cw_ref[kk:kk + 1, :]
    zz = jnp.dot(xc.astype(BF16), w_ref[...], preferred_element_type=F32)
    r = _sigmoid(zz[:, :LRU_WIDTH] + ba_ref[...])
    gi = _sigmoid(zz[:, LRU_WIDTH:] + bx_ref[...])
    log_a = (-RG_C * _softplus(-lam_ref[...])) * r
    a = jnp.exp(log_a)
    a_ref[...] = a
    u_ref[...] = jnp.sqrt(1.0 - a * a) * gi * xc

    row = lax.broadcasted_iota(jnp.int32, (SUBLANES, LRU_WIDTH), 0)
    n_sub = t // SUBLANES

    def sub(s, h):
        blk = (n_sub - 1 - s) if reverse else s
        r0 = pl.multiple_of(blk * SUBLANES, SUBLANES)
        aa = a_ref[pl.ds(r0, SUBLANES), :]
        uu = u_ref[pl.ds(r0, SUBLANES), :]
        for dd in (1, 2, 4):
            if reverse:
                keep = row < SUBLANES - dd
                sh = SUBLANES - dd
            else:
                keep = row >= dd
                sh = dd
            a_sh = jnp.where(keep, pltpu.roll(aa, sh, 0), 1.0)
            u_sh = jnp.where(keep, pltpu.roll(uu, sh, 0), 0.0)
            uu = aa * u_sh + uu
            aa = aa * a_sh
        hh = aa * h + uu
        if reverse:
            o_ref[0, pl.ds(r0, SUBLANES), :] = (
                (hf_ref[0, pl.ds(r0, SUBLANES), :] + hh)
                * _gelu_tanh(lg_ref[0, pl.ds(r0, SUBLANES), :])).astype(o_ref.dtype)
            edge = hh[0:1]
        else:
            o_ref[0, pl.ds(r0, SUBLANES), :] = hh
            edge = hh[SUBLANES - 1:SUBLANES]
        return jnp.broadcast_to(edge, (SUBLANES, LRU_WIDTH))

    h_ref[...] = lax.fori_loop(0, n_sub, sub, h_ref[...], unroll=4)


def _lru_tile(j, reverse, nct, nt):
    if not reverse:
        return j
    return jnp.where(j < nct, nct - 1 - j, nt - 1 - (j - nct))


def _lru(lx, w, ba, bx, lam, cw, cb, nct, reverse, hf=None, lg=None):
    b, l, wd = lx.shape
    nt = l // ROW_TILE
    tile_of = lambda i: _lru_tile(i, reverse, nct, nt)
    main, prev, nxt = _halo_specs(wd, tile_of, l)
    const = lambda shp: pl.BlockSpec(shp, lambda bi, i: (0,) * len(shp))
    in_specs = [main, prev, nxt, const(w.shape), const((1, wd)), const((1, wd)), const((1, wd)),
                const((SUBLANES, wd)), const((1, wd))]
    args = [lx, lx, lx, w, ba, bx, lam, cw, cb]
    if reverse:
        in_specs += [main, main]
        args += [hf, lg]
    return pl.pallas_call(
        functools.partial(_lru_kernel, reverse=reverse, nct=nct, nt=nt),
        grid=(b, nt),
        in_specs=in_specs,
        out_specs=main,
        out_shape=jax.ShapeDtypeStruct((b, l, wd), BF16 if reverse else F32),
        scratch_shapes=[pltpu.VMEM((ROW_TILE + 2 * HALO, wd), F32), pltpu.VMEM((ROW_TILE, wd), F32),
                        pltpu.VMEM((ROW_TILE, wd), F32), pltpu.VMEM((SUBLANES, wd), F32)],
        compiler_params=_cparams("parallel", "arbitrary"),
        name="lru_bwd" if reverse else "lru_fwd",
    )(*args)


def _merge_kernel(x_ref, mod_ref, p_ref, pp_ref, pn_ref, at_ref, lr_ref, wg_ref, bm_ref, pw_ref, ps_ref,
                  wb_ref, wo_ref, g_ref, b_ref, *rest, nct, nt, tile_off, ctx_len, alpha, route):
    if route:
        rw_ref, rb_ref, x1_ref, h2_ref, logit_ref, ext_ref = rest
    else:
        x1_ref, h2_ref, ext_ref = rest
    tile = pl.program_id(1) + tile_off
    t = x_ref.shape[1]
    x = x_ref[0]
    m = mod_ref[0, 0]
    h = (x * (1.0 + m[1:2]) + m[0:1]).astype(BF16)

    first, last = _seg_flags(tile, nct, nt)
    _fill_ext(ext_ref, p_ref[0], pp_ref[0], pn_ref[0], first, last)
    in_ctx = tile < nct
    seg_len = jnp.where(in_ctx, ctx_len, nt * t - ctx_len)
    pos = (tile - jnp.where(in_ctx, 0, nct)) * t + lax.broadcasted_iota(jnp.int32, (t, 1), 0)
    pooled = []
    for gi, w in enumerate(POOL_WINDOWS):
        lo = w // 2
        cols = slice(gi * POOL_GROUP, (gi + 1) * POOL_GROUP)
        acc = ext_ref[pl.ds(HALO - lo, t), cols]
        for kk in range(1, w):
            acc = acc + ext_ref[pl.ds(HALO - lo + kk, t), cols]
        cnt = jnp.clip(pos - lo + w, 0, seg_len) - jnp.clip(pos - lo, 0, seg_len)
        mean = acc / cnt.astype(F32)
        dlt = (mean - ext_ref[pl.ds(HALO, t), cols]).astype(BF16)
        pooled.append(jnp.dot(dlt, pw_ref[gi], preferred_element_type=F32))
    pool_y = (jnp.concatenate(pooled, axis=1) * ps_ref[...]).astype(BF16)

    d = x.shape[1]
    mix = None
    for n, ys in enumerate((pool_y, at_ref[0], lr_ref[0])):
        gz = jnp.dot(h, wg_ref[:, n * d:(n + 1) * d], preferred_element_type=F32) + bm_ref[n:n + 1, :]
        br = jnp.dot(ys, wb_ref[n], preferred_element_type=F32)
        term = _sigmoid(gz) * br
        mix = term if mix is None else mix + term
    y = jnp.dot(mix.astype(BF16), wo_ref[...], preferred_element_type=F32)
    x1 = _layer_norm(alpha * x + m[2:3] * y, g_ref[...], b_ref[...])
    x1_ref[0] = x1
    h2 = x1 * (1.0 + m[4:5]) + m[3:4]
    if route:
        hi = h2.astype(BF16)
        mid = (h2 - hi.astype(F32)).astype(BF16)
        logit_ref[0] = (jnp.dot(hi, rw_ref[0], preferred_element_type=F32)
                        + jnp.dot(mid, rw_ref[0], preferred_element_type=F32)
                        + jnp.dot(hi, rw_ref[1], preferred_element_type=F32) + rb_ref[...])
        _store_row_tiles(h2_ref.at[0], h2)
    else:
        h2_ref[0] = h2.astype(h2_ref.dtype)


def _merge(x, mods, p, attn, lru, w_gate, b_merge, pool_w, pool_scale, w_branch, w_out, ln_g, ln_b,
           nct, lat_only, alpha, router=None):
    b, l, d = x.shape
    nt = l // ROW_TILE
    off = nct if lat_only else 0
    rows = l - off * ROW_TILE
    tile_of = lambda i: i + off
    tok_l = lambda w: pl.BlockSpec((1, ROW_TILE, w), lambda bi, i: (bi, i + off, 0))
    tok_o = lambda w: pl.BlockSpec((1, ROW_TILE, w), lambda bi, i: (bi, i, 0))
    const = lambda shp: pl.BlockSpec(shp, lambda bi, i: (0,) * len(shp))
    pm, pp, pn = _halo_specs(BRANCH_WIDTH, tile_of, l)
    in_specs = [tok_l(d),
                pl.BlockSpec((1, 1, SUBLANES, d), lambda bi, i: (bi, (i + off >= nct).astype(jnp.int32), 0, 0)),
                pm, pp, pn,
                tok_o(Q_WIDTH) if lat_only else tok_l(Q_WIDTH),
                tok_l(LRU_WIDTH),
                const(w_gate.shape), const(b_merge.shape), const(pool_w.shape), const(pool_scale.shape),
                const(w_branch.shape), const(w_out.shape), const((1, d)), const((1, d))]
    args = [x, mods, p, p, p, attn, lru, w_gate, b_merge, pool_w, pool_scale, w_branch, w_out, ln_g, ln_b]
    out_specs = [tok_o(d)]
    out_shape = [jax.ShapeDtypeStruct((b, rows, d), F32)]
    if router is None:
        out_specs.append(tok_o(d))
        out_shape.append(jax.ShapeDtypeStruct((b, rows, d), BF16))
    else:
        n_sub = d // LANES
        in_specs += [const(router[0].shape), const(router[1].shape)]
        args += list(router)
        out_specs += [pl.BlockSpec((1, ROW_TILE * n_sub, LANES), lambda bi, i: (bi, i, 0)), tok_o(LANES)]
        out_shape += [jax.ShapeDtypeStruct((b, rows * n_sub, LANES), F32),
                      jax.ShapeDtypeStruct((b, rows, LANES), F32)]
    return pl.pallas_call(
        functools.partial(_merge_kernel, nct=nct, nt=nt, tile_off=off, ctx_len=nct * ROW_TILE, alpha=alpha,
                          route=router is not None),
        grid=(b, rows // ROW_TILE),
        in_specs=in_specs,
        out_specs=out_specs,
        out_shape=out_shape,
        scratch_shapes=[pltpu.VMEM((ROW_TILE + 2 * HALO, BRANCH_WIDTH), F32)],
        compiler_params=_cparams("parallel", "parallel"),
        name="merge_ln1",
    )(*args)


def _swiglu_kernel(x_ref, wg_ref, wu_ref, wd_ref, x1_ref, *rest, alpha):
    mod_refs, (g_ref, b_ref, o_ref, acc_ref) = rest[:-4], rest[-4:]
    f = pl.program_id(1)
    nf = pl.num_programs(1)
    x = x_ref[...]
    g = jnp.dot(x, wg_ref[...], preferred_element_type=F32)
    u = jnp.dot(x, wu_ref[...], preferred_element_type=F32)
    part = jnp.dot((_silu(g) * u).astype(BF16), wd_ref[...], preferred_element_type=F32)

    @pl.when(f == 0)
    def _():
        acc_ref[...] = part

    @pl.when(jnp.logical_and(f > 0, f < nf - 1))
    def _():
        acc_ref[...] += part

    @pl.when(f == nf - 1)
    def _():
        for hh, mod_ref in enumerate(mod_refs):
            rs = slice(hh * ROW_TILE, (hh + 1) * ROW_TILE)
            ff = acc_ref[rs, :] + part[rs]
            o_ref[rs, :] = _layer_norm(alpha * x1_ref[rs, :] + mod_ref[0, 0, 5:6, :] * ff, g_ref[...], b_ref[...])


def _swiglu_dense_ln(h2, x1, mods, ln_g, ln_b, w_gate, w_up, w_down, tf, nct, alpha):
    b, l, d = x1.shape
    r = b * l
    nf = w_gate.shape[1] // tf
    assert nf >= 2
    ntl = l // ROW_TILE
    halves = FFN_ROWS // ROW_TILE

    def mod_spec(hh):
        def idx(i, f):
            tile = i * halves + hh
            return (tile // ntl, (tile % ntl >= nct).astype(jnp.int32), 0, 0)
        return pl.BlockSpec((1, 1, SUBLANES, d), idx)

    out = pl.pallas_call(
        functools.partial(_swiglu_kernel, alpha=alpha),
        grid=(r // FFN_ROWS, nf),
        in_specs=[pl.BlockSpec((FFN_ROWS, d), lambda i, f: (i, 0)),
                  pl.BlockSpec((d, tf), lambda i, f: (0, f)),
                  pl.BlockSpec((d, tf), lambda i, f: (0, f)),
                  pl.BlockSpec((tf, d), lambda i, f: (f, 0)),
                  pl.BlockSpec((FFN_ROWS, d), lambda i, f: (i, 0))]
                 + [mod_spec(hh) for hh in range(halves)]
                 + [pl.BlockSpec((1, d), lambda i, f: (0, 0)), pl.BlockSpec((1, d), lambda i, f: (0, 0))],
        out_specs=pl.BlockSpec((FFN_ROWS, d), lambda i, f: (i, 0)),
        out_shape=jax.ShapeDtypeStruct((r, d), F32),
        scratch_shapes=[pltpu.VMEM((FFN_ROWS, d), F32)],
        compiler_params=_cparams("parallel", "arbitrary"),
        name="swiglu_dense_ln2",
    )(h2.reshape(r, d), w_gate, w_up, w_down, x1.reshape(r, d), *([mods] * halves), ln_g, ln_b)
    return out.reshape(b, l, d)


def _moe_kernel(be_ref, nu_ref, tok_ref, dst_ref, t_hbm, wg_ref, wu_ref, wd_ref, y_hbm,
                xbuf, acc_ref, obuf, sem_in, sem_out, sem_fill):
    i = pl.program_id(0)
    f = pl.program_id(1)
    nb = pl.num_programs(0)
    nu = nu_ref[0]
    rows = acc_ref.shape[0]
    sub = xbuf.shape[1] // rows
    slot = i % 2

    def tile_rows(start):
        return pl.ds(pl.multiple_of(start, sub), sub)

    def gather_copy(blk, sl, r):
        return pltpu.make_async_copy(t_hbm.at[tile_rows(tok_ref[blk * rows + r])], xbuf.at[sl, tile_rows(r * sub)],
                                     sem_in.at[sl])

    def scatter_copy(blk, sl, r):
        return pltpu.make_async_copy(obuf.at[sl, tile_rows(r * sub)], y_hbm.at[tile_rows(dst_ref[blk * rows + r])],
                                     sem_out.at[sl])

    def gathers_done(sl):
        pltpu.make_async_copy(t_hbm.at[pl.ds(0, rows * sub)], xbuf.at[sl], sem_in.at[sl]).wait()

    def scatters_done(sl):
        pltpu.make_async_copy(obuf.at[sl], y_hbm.at[pl.ds(0, rows * sub)], sem_out.at[sl]).wait()

    def for_rows(fn):
        def body(r, c):
            fn(r)
            return c
        lax.fori_loop(0, rows, body, 0, unroll=8)

    def load_x():
        return _load_row_tiles(xbuf.at[slot], rows).astype(BF16)

    def swiglu_part(x):
        g = jnp.dot(x, wg_ref[0], preferred_element_type=F32)
        u = jnp.dot(x, wu_ref[0], preferred_element_type=F32)
        return jnp.dot((_silu(g) * u).astype(BF16), wd_ref[0], preferred_element_type=F32)

    @pl.when(jnp.logical_and(i == 0, f == 0))
    def _():
        obuf[1] = jnp.zeros(obuf.shape[1:], obuf.dtype)
        for_rows(lambda r: gather_copy(0, 0, r).start())

    @pl.when(jnp.logical_and(f == 0, i <= nu))
    def _():
        gathers_done(slot)

    @pl.when(jnp.logical_and(i < nu, f == 0))
    def _():
        x = load_x()
        for r in range(rows):
            gather_copy(i + 1, 1 - slot, r).start()
        acc_ref[...] = swiglu_part(x)

    @pl.when(jnp.logical_and(i < nu, f == 1))
    def _():
        @pl.when(i >= 1)
        def _():
            scatters_done(slot)

        x = load_x()
        prev = jnp.where(i >= 1, i - 1, nb)
        for r in range(rows):
            scatter_copy(prev, 1 - slot, r).start()
        _store_row_tiles(obuf.at[slot], acc_ref[...] + swiglu_part(x))

        @pl.when(i == nu - 1)
        def _():
            scatters_done(1 - slot)
            for_rows(lambda r: scatter_copy(i, slot, r).start())
            scatters_done(slot)

    @pl.when(jnp.logical_and(i >= nu, f == 1))
    def _():
        xbuf[0] = jnp.zeros(xbuf.shape[1:], xbuf.dtype)
        fill = pltpu.make_async_copy(xbuf.at[0], y_hbm.at[pl.ds(i * rows * sub, rows * sub)], sem_fill)
        fill.start()
        fill.wait()

    @pl.when(jnp.logical_and(jnp.logical_and(i == nb - 1, f == 1), nu == nb))
    def _():
        gathers_done(nb % 2)


def _swiglu_experts(t, block_e, n_used, slot_tok, slot_dst, w_gate, w_up, w_down, tf):
    d = w_gate.shape[1]
    sub = d // LANES
    p = slot_tok.shape[0]
    nf = w_gate.shape[2] // tf
    assert nf == 2

    def fcol(i, f, nu):
        return jnp.where(i < nu[0], f, nf - 1)

    return pl.pallas_call(
        _moe_kernel,
        grid_spec=pltpu.PrefetchScalarGridSpec(
            num_scalar_prefetch=4,
            grid=(p // FFN_ROWS - 1, nf),
            in_specs=[pl.BlockSpec(memory_space=pl.ANY),
                      pl.BlockSpec((1, d, tf), lambda i, f, be, nu, st, sd: (be[i], 0, fcol(i, f, nu))),
                      pl.BlockSpec((1, d, tf), lambda i, f, be, nu, st, sd: (be[i], 0, fcol(i, f, nu))),
                      pl.BlockSpec((1, tf, d), lambda i, f, be, nu, st, sd: (be[i], fcol(i, f, nu), 0))],
            out_specs=pl.BlockSpec(memory_space=pl.ANY),
            scratch_shapes=[pltpu.VMEM((2, FFN_ROWS * sub, LANES), F32), pltpu.VMEM((FFN_ROWS, d), F32),
                            pltpu.VMEM((2, FFN_ROWS * sub, LANES), F32),
                            pltpu.SemaphoreType.DMA((2,)), pltpu.SemaphoreType.DMA((2,)),
                            pltpu.SemaphoreType.DMA]),
        out_shape=jax.ShapeDtypeStruct((p * sub, LANES), F32),
        compiler_params=_cparams("arbitrary", "arbitrary"),
        name="swiglu_experts",
    )(block_e, n_used, slot_tok, slot_dst, t, w_gate, w_up, w_down)


def _combine_ln_kernel(x_ref, mod_ref, g_ref, b_ref, *rest, alpha):
    y_refs, p_ref, o_ref = rest[:-2], rest[-2], rest[-1]
    t = x_ref.shape[1]
    f = None
    for kk, y_ref in enumerate(y_refs):
        term = _load_row_tiles(y_ref, t) * p_ref[0, :, kk:kk + 1]
        f = term if f is None else f + term
    m = mod_ref[0, 0]
    o_ref[0] = _layer_norm(alpha * x_ref[0] + m[5:6] * f, g_ref[...], b_ref[...])


def _combine_ln(x1, mods, ln_g, ln_b, y, probs, alpha):
    b, rows, d = x1.shape
    nt = rows // ROW_TILE
    sub = d // LANES
    top_k = probs.shape[2]
    tok = lambda w: pl.BlockSpec((1, ROW_TILE, w), lambda bi, i: (bi, i, 0))
    const = lambda shp: pl.BlockSpec(shp, lambda bi, i: (0,) * len(shp))
    y_specs = [pl.BlockSpec((ROW_TILE * sub, LANES), lambda bi, i, kk=kk: ((kk * b + bi) * nt + i, 0))
               for kk in range(top_k)]
    return pl.pallas_call(
        functools.partial(_combine_ln_kernel, alpha=alpha),
        grid=(b, nt),
        in_specs=[tok(d), pl.BlockSpec((1, 1, SUBLANES, d), lambda bi, i: (bi, 1, 0, 0)),
                  const((1, d)), const((1, d))] + y_specs + [tok(top_k)],
        out_specs=tok(d),
        out_shape=jax.ShapeDtypeStruct((b, rows, d), F32),
        compiler_params=_cparams("parallel", "parallel"),
        name="combine_ln2",
    )(x1, mods, ln_g, ln_b, *([y] * top_k), probs)


def _moe(t, logits, w_gate, w_up, w_down, tf):
    n, n_e = logits.shape
    sub = t.shape[0] // n
    top_v, top_i = lax.top_k(logits, TOP_K)
    probs = jax.nn.softmax(top_v, axis=-1)
    m = n * TOP_K
    flat_e = top_i.reshape(m)
    onehot = (flat_e[:, None] == jnp.arange(n_e, dtype=flat_e.dtype)[None, :]).astype(jnp.int32)
    csum = jnp.cumsum(onehot, axis=0)
    rank = jnp.take_along_axis(csum, flat_e[:, None], axis=1)[:, 0] - 1
    counts = csum[-1]
    padded = (counts + FFN_ROWS - 1) // FFN_ROWS * FFN_ROWS
    ends_p = jnp.cumsum(padded)
    start_p = ends_p - padded
    slot = (start_p[flat_e] + rank).astype(jnp.int32)
    nb = -(-(m + n_e * (FFN_ROWS - 1)) // FFN_ROWS)
    p = (nb + 1) * FFN_ROWS
    pair = jnp.arange(m, dtype=jnp.int32)
    slot_pair = jnp.full((p,), -1, jnp.int32).at[slot].set(pair)
    is_pad = slot_pair < 0
    slot_tok = jnp.where(is_pad, 0, slot_pair // TOP_K) * sub
    slot_dst = jnp.where(is_pad, m + jnp.cumsum(is_pad.astype(jnp.int32)) - 1,
                         (slot_pair % TOP_K) * n + slot_pair // TOP_K) * sub
    block_e = jnp.minimum(jnp.searchsorted(ends_p, jnp.arange(nb + 1) * FFN_ROWS, side='right'),
                          n_e - 1).astype(jnp.int32)
    n_used = (ends_p[-1:] // FFN_ROWS).astype(jnp.int32)
    y = _swiglu_experts(t, block_e, n_used, slot_tok, slot_dst, w_gate, w_up, w_down, tf)
    return y, probs


def _block_diag(w):
    nb, bi, bj = w.shape
    eye = jnp.eye(nb, dtype=w.dtype)
    return (eye[:, None, :, None] * w[:, :, None, :]).reshape(nb * bi, nb * bj)


def _rope_tables(ctx_len, s):
    nf = HEAD_DIM // 4
    inv = ROPE_THETA ** (-jnp.arange(nf, dtype=F32) / nf)
    t = jnp.arange(s)
    row = (t // GRID_W).astype(F32)[:, None] * inv
    col = (t % GRID_W).astype(F32)[:, None] * inv
    ang = jnp.concatenate([row, row, col, col], axis=1)
    sign = jnp.tile(jnp.concatenate([-jnp.ones((nf,), F32), jnp.ones((nf,), F32)]), 2)
    cos = jnp.concatenate([jnp.ones((ctx_len, HEAD_DIM), F32), jnp.cos(ang)], axis=0)
    sin = jnp.concatenate([jnp.zeros((ctx_len, HEAD_DIM), F32), jnp.sin(ang) * sign], axis=0)
    reps = LANES // HEAD_DIM
    return jnp.tile(cos, (1, reps)), jnp.tile(sin, (1, reps))


def _pad_rows(a, rows):
    return jnp.zeros((rows,) + a.shape[1:], a.dtype).at[:a.shape[0]].set(a)


def kernel(x, c, ctx, c_ctx, w_mod, b_mod, w_in, b_merge, pool_w, pool_scale, q_norm, k_norm, conv_w, conv_b, lru_wa, lru_ba, lru_wx, lru_bx, lru_lambda, w_branch, w_out, ln1_g, ln1_b, ffn_w_gate, ffn_w_up, ffn_w_down, moe_router, moe_router_b, moe_w_gate, moe_w_up, moe_w_down, ln2_g, ln2_b):
    b, s, d = x.shape
    ctx_len = ctx.shape[1]
    depth = w_in.shape[0]
    l = ctx_len + s
    assert ctx_len % ROW_TILE == 0 and s % ROW_TILE == 0 and (b * s) % FFN_ROWS == 0
    assert (b * l) % FFN_ROWS == 0 and ctx_len % Q_TILE == 0 and s % Q_TILE == 0
    nct = ctx_len // ROW_TILE
    alpha = (2 * depth) ** 0.25

    cc = _pad_rows(jnp.concatenate([c, c_ctx[None, :]], axis=0), -(-(b + 1) // SUBLANES) * SUBLANES)
    mod_all = _modulation(cc, w_mod, b_mod)

    cos_t, sin_t = _rope_tables(ctx_len, s)
    bd = _block_diag(jnp.full((LANES // HEAD_DIM, HEAD_DIM, HEAD_DIM), 1.0 / HEAD_DIM, F32)).astype(BF16)
    reps = LANES // HEAD_DIM

    xs = jnp.concatenate([ctx, x], axis=1)
    for li in range(depth):
        last = li == depth - 1
        ml = jnp.transpose(mod_all[li, :, :b], (1, 0, 2))
        mc = jnp.broadcast_to(mod_all[li, :, b][None], (b, 6, d))
        mods = jnp.stack([mc, ml], axis=1)
        mods = jnp.concatenate([mods, jnp.zeros((b, 2, SUBLANES - 6, d), F32)], axis=2)

        w_l = w_in[li].astype(BF16)
        p, q, k, v, lx, lg = _project(
            xs, mods, w_l[:, :OFF_GATE], bd,
            jnp.tile(q_norm[li], reps)[None], jnp.tile(k_norm[li], reps)[None], cos_t, sin_t, nct)
        attn = _attention(q, k, v, ctx_len, with_ctx=not last)

        cw = _pad_rows(conv_w[li], SUBLANES)
        lru_w = [jnp.concatenate([_block_diag(lru_wa[li, dr]), _block_diag(lru_wx[li, dr])], axis=1).astype(BF16)
                 for dr in range(2)]
        hf = _lru(lx, lru_w[0], lru_ba[li, 0][None], lru_bx[li, 0][None], lru_lambda[li, 0][None],
                  cw, conv_b[li][None], nct, reverse=False)
        lru_y = _lru(lx, lru_w[1], lru_ba[li, 1][None], lru_bx[li, 1][None], lru_lambda[li, 1][None],
                     cw, conv_b[li][None], nct, reverse=True, hf=hf, lg=lg)

        jf = li // 2
        merge_args = (xs, mods, p, attn, lru_y, w_l[:, OFF_GATE:], _pad_rows(b_merge[li], SUBLANES),
                      pool_w[li].astype(BF16), pool_scale[li][None], w_branch[li].astype(BF16),
                      w_out[li].astype(BF16), ln1_g[li][None], ln1_b[li][None])
        if li % 2 == 0:
            assert not last, "the dense layer is expected to carry the context tokens along"
            x1, h2 = _merge(*merge_args, nct, lat_only=False, alpha=alpha)
            xs = _swiglu_dense_ln(h2, x1, mods, ln2_g[li][None], ln2_b[li][None], ffn_w_gate[jf].astype(BF16),
                                  ffn_w_up[jf].astype(BF16), ffn_w_down[jf].astype(BF16),
                                  tf=ffn_w_gate.shape[2] // 2, nct=nct, alpha=alpha)
        else:
            assert last, "the expert layer is expected to be the last layer (latent tokens only)"
            n_e = moe_router.shape[2]
            rw = jnp.zeros((d, LANES), F32).at[:, :n_e].set(moe_router[jf])
            rb = jnp.zeros((1, LANES), F32).at[0, :n_e].set(moe_router_b[jf])
            rw_hi = rw.astype(BF16)
            rw_split = jnp.stack([rw_hi, (rw - rw_hi.astype(F32)).astype(BF16)])
            x1, h2, logits = _merge(*merge_args, nct, lat_only=True, alpha=alpha, router=(rw_split, rb))
            y, probs = _moe(h2.reshape(-1, LANES), logits.reshape(b * s, LANES)[:, :n_e],
                            moe_w_gate[jf].astype(BF16), moe_w_up[jf].astype(BF16), moe_w_down[jf].astype(BF16),
                            tf=moe_w_gate.shape[3] // 2)
            xs = _combine_ln(x1, mods, ln2_g[li][None], ln2_b[li][None], y, probs.reshape(b, s, -1), alpha)
    return xs
```

```python
import functools

import jax
import jax.numpy as jnp
from jax import lax
from jax.experimental import pallas as pl
from jax.experimental.pallas import tpu as pltpu

F32 = jnp.float32
BF16 = jnp.bfloat16

GRID_W = 64
POOL_WINDOWS = (2, 4, 8, 16)
BRANCH_WIDTH = 512
POOL_GROUP = BRANCH_WIDTH // len(POOL_WINDOWS)
N_Q_HEADS = 8
N_KV_HEADS = 2
HEAD_DIM = 64
GQA_GROUP = N_Q_HEADS // N_KV_HEADS
Q_WIDTH = N_Q_HEADS * HEAD_DIM
KV_WIDTH = N_KV_HEADS * HEAD_DIM
ROPE_THETA = 10000.0
ATTN_SCALE = HEAD_DIM ** -0.5
LOG2_E = 1.4426950408889634
LRU_WIDTH = BRANCH_WIDTH
LRU_BLOCKS = 8
CONV_WIDTH = 4
RG_C = 8.0
N_BRANCH = 3
N_EXPERTS = 8
TOP_K = 2
LN_EPS = 1e-5
RMS_EPS = 1e-6
OFF_GATE = BRANCH_WIDTH + Q_WIDTH + 2 * KV_WIDTH + 2 * LRU_WIDTH

LANES = 128
SUBLANES = 8
ROW_TILE = 256
BATCH_BLOCK = 2
Q_TILE = 256
ATTN_CHAIN = 256
FFN_ROWS = 512
HALO = SUBLANES
VMEM_LIMIT = 56 * 1024 * 1024


def _cparams(*sem):
    return pltpu.CompilerParams(dimension_semantics=sem, vmem_limit_bytes=VMEM_LIMIT)


def _sigmoid(x):
    return 0.5 * jnp.tanh(0.5 * x) + 0.5


def _store_row_tiles(ref, val):
    t, d = val.shape
    for j in range(d // LANES):
        ref[pl.ds(j, t, stride=d // LANES), :] = val[:, j * LANES:(j + 1) * LANES]


def _load_row_tiles(ref, t):
    n = ref.shape[0] // t
    return jnp.concatenate([ref[pl.ds(j, t, stride=n), :] for j in range(n)], axis=1)


def _silu(x):
    return x * _sigmoid(x)


def _layer_norm(v, g, b):
    mu = jnp.mean(v, axis=-1, keepdims=True)
    d = v - mu
    var = jnp.mean(d * d, axis=-1, keepdims=True)
    return d * lax.rsqrt(var + LN_EPS) * g + b


def _mod_kernel(c_ref, w_ref, b_ref, o_ref):
    s = _silu(c_ref[...])
    o_ref[0, 0] = jnp.dot(s, w_ref[0], precision=lax.Precision.HIGHEST,
                          preferred_element_type=F32) + b_ref[0, 0]


def _modulation(cc, w_mod, b_mod):
    depth, d, _ = w_mod.shape
    r = cc.shape[0]
    b6 = b_mod.reshape(depth, 6, 1, d)
    return pl.pallas_call(
        _mod_kernel,
        grid=(depth, 6),
        in_specs=[pl.BlockSpec((r, d), lambda l, j: (0, 0)),
                  pl.BlockSpec((1, d, d), lambda l, j: (l, 0, j)),
                  pl.BlockSpec((1, 1, 1, d), lambda l, j: (l, j, 0, 0))],
        out_specs=pl.BlockSpec((1, 1, r, d), lambda l, j: (l, j, 0, 0)),
        out_shape=jax.ShapeDtypeStruct((depth, 6, r, d), F32),
        compiler_params=_cparams("arbitrary", "arbitrary"),
        name="modulation",
    )(cc, w_mod, b6)


def _norm_rope(zc, bd, g, cos, sin):
    zz = zc * zc
    hi = zz.astype(BF16)
    lo = (zz - hi.astype(F32)).astype(BF16)
    ms = jnp.dot(hi, bd, preferred_element_type=F32) + jnp.dot(lo, bd, preferred_element_type=F32)
    y = zc * lax.rsqrt(ms + RMS_EPS) * g
    nf = HEAD_DIM // 4
    up = pltpu.roll(y, LANES - nf, 1)
    dn = pltpu.roll(y, nf, 1)
    lane = lax.broadcasted_iota(jnp.int32, y.shape, 1)
    partner = jnp.where((lane % (2 * nf)) < nf, up, dn)
    return y * cos + partner * sin


def _proj_kernel(x_ref, mod_ref, w_ref, bd_ref, qg_ref, kg_ref, cos_ref, sin_ref,
                 p_ref, q_ref, k_ref, v_ref, lx_ref, lg_ref):
    bd = bd_ref[...]
    cos = cos_ref[...]
    sin = sin_ref[...]
    for bb in range(x_ref.shape[0]):
        x = x_ref[bb]
        m = mod_ref[bb, 0]
        h = (x * (1.0 + m[1:2]) + m[0:1]).astype(BF16)
        z = jnp.dot(h, w_ref[...], preferred_element_type=F32)
        o = 0
        p_ref[bb] = z[:, o:o + BRANCH_WIDTH]
        o += BRANCH_WIDTH
        for c in range(Q_WIDTH // LANES):
            y = _norm_rope(z[:, o:o + LANES], bd, qg_ref[...], cos, sin) * (ATTN_SCALE * LOG2_E)
            q_ref[bb, 2 * c] = y[:, :HEAD_DIM].astype(BF16)
            q_ref[bb, 2 * c + 1] = y[:, HEAD_DIM:].astype(BF16)
            o += LANES
        for c in range(KV_WIDTH // LANES):
            y = _norm_rope(z[:, o:o + LANES], bd, kg_ref[...], cos, sin)
            k_ref[bb, 2 * c] = y[:, :HEAD_DIM].astype(BF16)
            k_ref[bb, 2 * c + 1] = y[:, HEAD_DIM:].astype(BF16)
            o += LANES
        for c in range(KV_WIDTH // LANES):
            y = z[:, o:o + LANES]
            ones = jnp.ones((y.shape[0], LANES - HEAD_DIM), F32)
            v_ref[bb, 2 * c] = jnp.concatenate([y[:, :HEAD_DIM], ones], axis=1).astype(BF16)
            v_ref[bb, 2 * c + 1] = jnp.concatenate([y[:, HEAD_DIM:], ones], axis=1).astype(BF16)
            o += LANES
        lx_ref[bb] = z[:, o:o + LRU_WIDTH]
        o += LRU_WIDTH
        lg_ref[bb] = z[:, o:o + LRU_WIDTH]


def _project(x, mods, w_a, bd, qg, kg, cos_t, sin_t, nct):
    b, l, d = x.shape
    nt = l // ROW_TILE
    wa = w_a.shape[1]
    bb = BATCH_BLOCK
    tok = lambda w: pl.BlockSpec((bb, ROW_TILE, w), lambda bi, i: (bi, i, 0))
    head = lambda n, w=HEAD_DIM: pl.BlockSpec((bb, n, ROW_TILE, w), lambda bi, i: (bi, 0, i, 0))
    const = lambda shp: pl.BlockSpec(shp, lambda bi, i: (0,) * len(shp))
    return pl.pallas_call(
        _proj_kernel,
        grid=(b // bb, nt),
        in_specs=[tok(d),
                  pl.BlockSpec((bb, 1, SUBLANES, d), lambda bi, i: (bi, (i >= nct).astype(jnp.int32), 0, 0)),
                  const((d, wa)), const((LANES, LANES)), const((1, LANES)), const((1, LANES)),
                  pl.BlockSpec((ROW_TILE, LANES), lambda bi, i: (i, 0)),
                  pl.BlockSpec((ROW_TILE, LANES), lambda bi, i: (i, 0))],
        out_specs=[tok(BRANCH_WIDTH), head(N_Q_HEADS), head(N_KV_HEADS), head(N_KV_HEADS, LANES),
                   tok(LRU_WIDTH), tok(LRU_WIDTH)],
        out_shape=[jax.ShapeDtypeStruct((b, l, BRANCH_WIDTH), F32),
                   jax.ShapeDtypeStruct((b, N_Q_HEADS, l, HEAD_DIM), BF16),
                   jax.ShapeDtypeStruct((b, N_KV_HEADS, l, HEAD_DIM), BF16),
                   jax.ShapeDtypeStruct((b, N_KV_HEADS, l, LANES), BF16),
                   jax.ShapeDtypeStruct((b, l, LRU_WIDTH), F32),
                   jax.ShapeDtypeStruct((b, l, LRU_WIDTH), F32)],
        compiler_params=_cparams("parallel", "parallel"),
        name="in_proj",
    )(x, mods, w_a, bd, qg, kg, cos_t, sin_t)


def _softmax_pv(q, k, v):
    hd = q.shape[1]
    s = lax.dot_general(q, k, (((1,), (1,)), ((), ())), preferred_element_type=F32)
    m = jnp.max(s, axis=1, keepdims=True)
    p = jnp.exp2(s - m).astype(BF16)
    r = jnp.dot(p, v, preferred_element_type=F32)
    return r[:, :hd] / r[:, hd:2 * hd]


def _attn_kernel(q_ref, k_ref, v_ref, o_ref, *, n_ctx_tiles, ctx_len):
    g, tq, hd = q_ref.shape[1:]

    def run(lk):
        for j in range(g):
            for r0 in range(0, tq, ATTN_CHAIN):
                o = _softmax_pv(q_ref[0, j, r0:r0 + ATTN_CHAIN, :], k_ref[0, 0, :lk], v_ref[0, 0, :lk])
                o_ref[0, r0:r0 + ATTN_CHAIN, j * hd:(j + 1) * hd] = o.astype(o_ref.dtype)

    if n_ctx_tiles:
        @pl.when(pl.program_id(2) < n_ctx_tiles)
        def _():
            run(ctx_len)

        @pl.when(pl.program_id(2) >= n_ctx_tiles)
        def _():
            run(k_ref.shape[2])
    else:
        run(k_ref.shape[2])


def _attention(q, k, v, ctx_len, with_ctx):
    b, _, l, hd = q.shape
    nct = ctx_len // Q_TILE
    rows = l if with_ctx else l - ctx_len
    q_off = 0 if with_ctx else nct
    return pl.pallas_call(
        functools.partial(_attn_kernel, n_ctx_tiles=nct if with_ctx else 0, ctx_len=ctx_len),
        grid=(b, N_KV_HEADS, rows // Q_TILE),
        in_specs=[pl.BlockSpec((1, GQA_GROUP, Q_TILE, hd), lambda bi, h, i: (bi, h, i + q_off, 0)),
                  pl.BlockSpec((1, 1, l, hd), lambda bi, h, i: (bi, h, 0, 0)),
                  pl.BlockSpec((1, 1, l, v.shape[3]), lambda bi, h, i: (bi, h, 0, 0))],
        out_specs=pl.BlockSpec((1, Q_TILE, GQA_GROUP * hd), lambda bi, h, i: (bi, i, h)),
        out_shape=jax.ShapeDtypeStruct((b, rows, Q_WIDTH), BF16),
        compiler_params=_cparams("parallel", "parallel", "parallel"),
        name="attention",
    )(q, k, v)


def _fill_ext(ext_ref, main, prev, nxt, first, last):
    t = main.shape[0]
    ext_ref[0:HALO] = jnp.where(first, 0.0, prev)
    ext_ref[HALO:HALO + t] = main
    ext_ref[HALO + t:2 * HALO + t] = jnp.where(last, 0.0, nxt)


def _seg_flags(tile, nct, nt):
    first = jnp.logical_or(tile == 0, tile == nct)
    last = jnp.logical_or(tile == nct - 1, tile == nt - 1)
    return first, last


def _halo_specs(width, tile_of, l):
    rb = ROW_TILE // HALO
    last_blk = l // HALO - 1
    bb = BATCH_BLOCK
    main = pl.BlockSpec((bb, ROW_TILE, width), lambda bi, i: (bi, tile_of(i), 0))
    prev = pl.BlockSpec((bb, HALO, width), lambda bi, i: (bi, jnp.maximum(tile_of(i) * rb - 1, 0), 0))
    nxt = pl.BlockSpec((bb, HALO, width), lambda bi, i: (bi, jnp.minimum((tile_of(i) + 1) * rb, last_blk), 0))
    return main, prev, nxt


def _softplus(z):
    return jnp.maximum(z, 0.0) + jnp.log1p(jnp.exp(-jnp.abs(z)))


def _gelu_tanh(x):
    return 0.5 * x * (1.0 + jnp.tanh(0.7978845608028654 * (x + 0.044715 * (x * x * x))))


def _lru_kernel(*refs, reverse, nct, nt):
    if reverse:
        (lx_ref, lxp_ref, lxn_ref, w_ref, ba_ref, bx_ref, lam_ref, cw_ref, cb_ref, hf_ref, lg_ref,
         o_ref, ext_ref, a_ref, u_ref, h_ref) = refs
    else:
        (lx_ref, lxp_ref, lxn_ref, w_ref, ba_ref, bx_ref, lam_ref, cw_ref, cb_ref,
         o_ref, ext_ref, a_ref, u_ref, h_ref) = refs
    j = pl.program_id(1)
    tile = _lru_tile(j, reverse, nct, nt)
    t = lx_ref.shape[1]
    nbb = lx_ref.shape[0]

    @pl.when(j == 0)
    def _():
        h_ref[...] = jnp.zeros_like(h_ref)

    first, last = _seg_flags(tile, nct, nt)
    lo = CONV_WIDTH // 2
    log_a_scale = -RG_C * _softplus(-lam_ref[...])
    for bb in range(nbb):
        ext = ext_ref.at[bb]
        _fill_ext(ext, lx_ref[bb], lxp_ref[bb], lxn_ref[bb], first, last)
        xc = cb_ref[...]
        for kk in range(CONV_WIDTH):
            xc = xc + ext[pl.ds(HALO + kk - lo, t), :] * cw_ref[kk:kk + 1, :]
        zz = jnp.dot(xc.astype(BF16), w_ref[...], preferred_element_type=F32)
        r = _sigmoid(zz[:, :LRU_WIDTH] + ba_ref[...])
        gi = _sigmoid(zz[:, LRU_WIDTH:] + bx_ref[...])
        a = jnp.exp(log_a_scale * r)
        a_ref[bb] = a
        u_ref[bb] = jnp.sqrt(1.0 - a * a) * gi * xc

    row = lax.broadcasted_iota(jnp.int32, (SUBLANES, LRU_WIDTH), 0)
    n_sub = t // SUBLANES

    def sub(s, hs):
        blk = (n_sub - 1 - s) if reverse else s
        r0 = pl.multiple_of(blk * SUBLANES, SUBLANES)
        out = []
        for bb in range(nbb):
            aa = a_ref[bb, pl.ds(r0, SUBLANES), :]
            uu = u_ref[bb, pl.ds(r0, SUBLANES), :]
            for dd in (1, 2, 4):
                if reverse:
                    keep = row < SUBLANES - dd
                    sh = SUBLANES - dd
                else:
                    keep = row >= dd
                    sh = dd
                a_sh = jnp.where(keep, pltpu.roll(aa, sh, 0), 1.0)
                u_sh = jnp.where(keep, pltpu.roll(uu, sh, 0), 0.0)
                uu = aa * u_sh + uu
                aa = aa * a_sh
            hh = aa * hs[bb] + uu
            if reverse:
                o_ref[bb, pl.ds(r0, SUBLANES), :] = (
                    (hf_ref[bb, pl.ds(r0, SUBLANES), :] + hh)
                    * _gelu_tanh(lg_ref[bb, pl.ds(r0, SUBLANES), :])).astype(o_ref.dtype)
                edge = hh[0:1]
            else:
                o_ref[bb, pl.ds(r0, SUBLANES), :] = hh
                edge = hh[SUBLANES - 1:SUBLANES]
            out.append(jnp.broadcast_to(edge, (SUBLANES, LRU_WIDTH)))
        return tuple(out)

    hs = lax.fori_loop(0, n_sub, sub, tuple(h_ref[bb] for bb in range(nbb)), unroll=4)
    for bb in range(nbb):
        h_ref[bb] = hs[bb]


def _lru_tile(j, reverse, nct, nt):
    if not reverse:
        return j
    return jnp.where(j < nct, nct - 1 - j, nt - 1 - (j - nct))


def _lru(lx, w, ba, bx, lam, cw, cb, nct, reverse, hf=None, lg=None):
    b, l, wd = lx.shape
    nt = l // ROW_TILE
    tile_of = lambda i: _lru_tile(i, reverse, nct, nt)
    main, prev, nxt = _halo_specs(wd, tile_of, l)
    const = lambda shp: pl.BlockSpec(shp, lambda bi, i: (0,) * len(shp))
    in_specs = [main, prev, nxt, const(w.shape), const((1, wd)), const((1, wd)), const((1, wd)),
                const((SUBLANES, wd)), const((1, wd))]
    args = [lx, lx, lx, w, ba, bx, lam, cw, cb]
    if reverse:
        in_specs += [main, main]
        args += [hf, lg]
    bb = BATCH_BLOCK
    return pl.pallas_call(
        functools.partial(_lru_kernel, reverse=reverse, nct=nct, nt=nt),
        grid=(b // bb, nt),
        in_specs=in_specs,
        out_specs=main,
        out_shape=jax.ShapeDtypeStruct((b, l, wd), BF16 if reverse else F32),
        scratch_shapes=[pltpu.VMEM((bb, ROW_TILE + 2 * HALO, wd), F32), pltpu.VMEM((bb, ROW_TILE, wd), F32),
                        pltpu.VMEM((bb, ROW_TILE, wd), F32), pltpu.VMEM((bb, SUBLANES, wd), F32)],
        compiler_params=_cparams("parallel", "arbitrary"),
        name="lru_bwd" if reverse else "lru_fwd",
    )(*args)


def _merge_kernel(x_ref, mod_ref, p_ref, pp_ref, pn_ref, at_ref, lr_ref, wg_ref, bm_ref, pw_ref, ps_ref,
                  wb_ref, wo_ref, g_ref, b_ref, *rest, nct, nt, tile_off, ctx_len, alpha, route):
    if route:
        rw_ref, rb_ref, x1_ref, h2_ref, logit_ref, ext_ref = rest
    else:
        x1_ref, h2_ref, ext_ref = rest
    tile = pl.program_id(1) + tile_off
    t = x_ref.shape[1]
    d = x_ref.shape[2]
    first, last = _seg_flags(tile, nct, nt)
    in_ctx = tile < nct
    seg_len = jnp.where(in_ctx, ctx_len, nt * t - ctx_len)
    pos = (tile - jnp.where(in_ctx, 0, nct)) * t + lax.broadcasted_iota(jnp.int32, (t, 1), 0)

    for bb in range(x_ref.shape[0]):
        x = x_ref[bb]
        m = mod_ref[bb, 0]
        h = (x * (1.0 + m[1:2]) + m[0:1]).astype(BF16)

        def branch(n, ys):
            gz = jnp.dot(h, wg_ref[:, n * d:(n + 1) * d], preferred_element_type=F32) + bm_ref[n:n + 1, :]
            return _sigmoid(gz) * jnp.dot(ys, wb_ref[n], preferred_element_type=F32)

        mix = branch(1, at_ref[bb]) + branch(2, lr_ref[bb])

        ext = ext_ref.at[bb]
        _fill_ext(ext, p_ref[bb], pp_ref[bb], pn_ref[bb], first, last)
        pooled = []
        for gi, w in enumerate(POOL_WINDOWS):
            lo = w // 2
            cols = slice(gi * POOL_GROUP, (gi + 1) * POOL_GROUP)
            acc = ext[pl.ds(HALO - lo, t), cols]
            for kk in range(1, w):
                acc = acc + ext[pl.ds(HALO - lo + kk, t), cols]
            cnt = jnp.clip(pos - lo + w, 0, seg_len) - jnp.clip(pos - lo, 0, seg_len)
            mean = acc / cnt.astype(F32)
            dlt = (mean - ext[pl.ds(HALO, t), cols]).astype(BF16)
            pooled.append(jnp.dot(dlt, pw_ref[gi], preferred_element_type=F32))
        pool_y = (jnp.concatenate(pooled, axis=1) * ps_ref[...]).astype(BF16)
        mix = branch(0, pool_y) + mix
        y = jnp.dot(mix.astype(BF16), wo_ref[...], preferred_element_type=F32)
        x1 = _layer_norm(alpha * x + m[2:3] * y, g_ref[...], b_ref[...])
        x1_ref[bb] = x1
        h2 = x1 * (1.0 + m[4:5]) + m[3:4]
        if route:
            hi = h2.astype(BF16)
            mid = (h2 - hi.astype(F32)).astype(BF16)
            logit_ref[bb] = (jnp.dot(hi, rw_ref[0], preferred_element_type=F32)
                             + jnp.dot(mid, rw_ref[0], preferred_element_type=F32)
                             + jnp.dot(hi, rw_ref[1], preferred_element_type=F32) + rb_ref[...])
            _store_row_tiles(h2_ref.at[bb], h2)
        else:
            h2_ref[bb] = h2.astype(h2_ref.dtype)


def _merge(x, mods, p, attn, lru, w_gate, b_merge, pool_w, pool_scale, w_branch, w_out, ln_g, ln_b,
           nct, lat_only, alpha, router=None):
    b, l, d = x.shape
    nt = l // ROW_TILE
    off = nct if lat_only else 0
    rows = l - off * ROW_TILE
    tile_of = lambda i: i + off
    bb = BATCH_BLOCK
    tok_l = lambda w: pl.BlockSpec((bb, ROW_TILE, w), lambda bi, i: (bi, i + off, 0))
    tok_o = lambda w: pl.BlockSpec((bb, ROW_TILE, w), lambda bi, i: (bi, i, 0))
    const = lambda shp: pl.BlockSpec(shp, lambda bi, i: (0,) * len(shp))
    pm, pp, pn = _halo_specs(BRANCH_WIDTH, tile_of, l)
    in_specs = [tok_l(d),
                pl.BlockSpec((bb, 1, SUBLANES, d), lambda bi, i: (bi, (i + off >= nct).astype(jnp.int32), 0, 0)),
                pm, pp, pn,
                tok_o(Q_WIDTH) if lat_only else tok_l(Q_WIDTH),
                tok_l(LRU_WIDTH),
                const(w_gate.shape), const(b_merge.shape), const(pool_w.shape), const(pool_scale.shape),
                const(w_branch.shape), const(w_out.shape), const((1, d)), const((1, d))]
    args = [x, mods, p, p, p, attn, lru, w_gate, b_merge, pool_w, pool_scale, w_branch, w_out, ln_g, ln_b]
    out_specs = [tok_o(d)]
    out_shape = [jax.ShapeDtypeStruct((b, rows, d), F32)]
    if router is None:
        out_specs.append(tok_o(d))
        out_shape.append(jax.ShapeDtypeStruct((b, rows, d), BF16))
    else:
        n_sub = d // LANES
        in_specs += [const(router[0].shape), const(router[1].shape)]
        args += list(router)
        out_specs += [pl.BlockSpec((bb, ROW_TILE * n_sub, LANES), lambda bi, i: (bi, i, 0)), tok_o(LANES)]
        out_shape += [jax.ShapeDtypeStruct((b, rows * n_sub, LANES), F32),
                      jax.ShapeDtypeStruct((b, rows, LANES), F32)]
    return pl.pallas_call(
        functools.partial(_merge_kernel, nct=nct, nt=nt, tile_off=off, ctx_len=nct * ROW_TILE, alpha=alpha,
                          route=router is not None),
        grid=(b // bb, rows // ROW_TILE),
        in_specs=in_specs,
        out_specs=out_specs,
        out_shape=out_shape,
        scratch_shapes=[pltpu.VMEM((bb, ROW_TILE + 2 * HALO, BRANCH_WIDTH), F32)],
        compiler_params=_cparams("parallel", "parallel"),
        name="merge_ln1",
    )(*args)


def _swiglu_kernel(x_ref, wg_ref, wu_ref, wd_ref, x1_ref, *rest, alpha):
    mod_refs, (g_ref, b_ref, o_ref, acc_ref) = rest[:-4], rest[-4:]
    f = pl.program_id(1)
    nf = pl.num_programs(1)
    x = x_ref[...]
    g = jnp.dot(x, wg_ref[...], preferred_element_type=F32)
    u = jnp.dot(x, wu_ref[...], preferred_element_type=F32)
    part = jnp.dot((_silu(g) * u).astype(BF16), wd_ref[...], preferred_element_type=F32)

    @pl.when(f == 0)
    def _():
        acc_ref[...] = part

    @pl.when(jnp.logical_and(f > 0, f < nf - 1))
    def _():
        acc_ref[...] += part

    @pl.when(f == nf - 1)
    def _():
        for hh, mod_ref in enumerate(mod_refs):
            rs = slice(hh * ROW_TILE, (hh + 1) * ROW_TILE)
            ff = acc_ref[rs, :] + part[rs]
            o_ref[rs, :] = _layer_norm(alpha * x1_ref[rs, :] + mod_ref[0, 0, 5:6, :] * ff, g_ref[...], b_ref[...])


def _swiglu_dense_ln(h2, x1, mods, ln_g, ln_b, w_gate, w_up, w_down, tf, nct, alpha):
    b, l, d = x1.shape
    r = b * l
    nf = w_gate.shape[1] // tf
    assert nf >= 2
    ntl = l // ROW_TILE
    halves = FFN_ROWS // ROW_TILE

    def mod_spec(hh):
        def idx(i, f):
            tile = i * halves + hh
            return (tile // ntl, (tile % ntl >= nct).astype(jnp.int32), 0, 0)
        return pl.BlockSpec((1, 1, SUBLANES, d), idx)

    out = pl.pallas_call(
        functools.partial(_swiglu_kernel, alpha=alpha),
        grid=(r // FFN_ROWS, nf),
        in_specs=[pl.BlockSpec((FFN_ROWS, d), lambda i, f: (i, 0)),
                  pl.BlockSpec((d, tf), lambda i, f: (0, f)),
                  pl.BlockSpec((d, tf), lambda i, f: (0, f)),
                  pl.BlockSpec((tf, d), lambda i, f: (f, 0)),
                  pl.BlockSpec((FFN_ROWS, d), lambda i, f: (i, 0))]
                 + [mod_spec(hh) for hh in range(halves)]
                 + [pl.BlockSpec((1, d), lambda i, f: (0, 0)), pl.BlockSpec((1, d), lambda i, f: (0, 0))],
        out_specs=pl.BlockSpec((FFN_ROWS, d), lambda i, f: (i, 0)),
        out_shape=jax.ShapeDtypeStruct((r, d), F32),
        scratch_shapes=[pltpu.VMEM((FFN_ROWS, d), F32)],
        compiler_params=_cparams("parallel", "arbitrary"),
        name="swiglu_dense_ln2",
    )(h2.reshape(r, d), w_gate, w_up, w_down, x1.reshape(r, d), *([mods] * halves), ln_g, ln_b)
    return out.reshape(b, l, d)


def _moe_kernel(be_ref, nu_ref, tok_ref, dst_ref, t_hbm, wg_ref, wu_ref, wd_ref, y_hbm,
                xbuf, acc_ref, obuf, sem_in, sem_out, sem_fill):
    i = pl.program_id(0)
    f = pl.program_id(1)
    nb = pl.num_programs(0)
    nu = nu_ref[0]
    rows = acc_ref.shape[0]
    sub = xbuf.shape[1] // rows
    slot = i % 2

    def tile_rows(start):
        return pl.ds(pl.multiple_of(start, sub), sub)

    def gather_copy(blk, sl, r):
        return pltpu.make_async_copy(t_hbm.at[tile_rows(tok_ref[blk * rows + r])], xbuf.at[sl, tile_rows(r * sub)],
                                     sem_in.at[sl])

    def scatter_copy(blk, sl, r):
        return pltpu.make_async_copy(obuf.at[sl, tile_rows(r * sub)], y_hbm.at[tile_rows(dst_ref[blk * rows + r])],
                                     sem_out.at[sl])

    def gathers_done(sl):
        pltpu.make_async_copy(t_hbm.at[pl.ds(0, rows * sub)], xbuf.at[sl], sem_in.at[sl]).wait()

    def scatters_done(sl):
        pltpu.make_async_copy(obuf.at[sl], y_hbm.at[pl.ds(0, rows * sub)], sem_out.at[sl]).wait()

    def for_rows(fn):
        def body(r, c):
            fn(r)
            return c
        lax.fori_loop(0, rows, body, 0, unroll=8)

    def load_x():
        return _load_row_tiles(xbuf.at[slot], rows).astype(BF16)

    def swiglu_part(x):
        g = jnp.dot(x, wg_ref[0], preferred_element_type=F32)
        u = jnp.dot(x, wu_ref[0], preferred_element_type=F32)
        return jnp.dot((_silu(g) * u).astype(BF16), wd_ref[0], preferred_element_type=F32)

    @pl.when(jnp.logical_and(i == 0, f == 0))
    def _():
        obuf[1] = jnp.zeros(obuf.shape[1:], obuf.dtype)
        for_rows(lambda r: gather_copy(0, 0, r).start())

    @pl.when(jnp.logical_and(f == 0, i <= nu))
    def _():
        gathers_done(slot)

    @pl.when(jnp.logical_and(i < nu, f == 0))
    def _():
        x = load_x()
        for r in range(rows):
            gather_copy(i + 1, 1 - slot, r).start()
        acc_ref[...] = swiglu_part(x)

    @pl.when(jnp.logical_and(i < nu, f == 1))
    def _():
        @pl.when(i >= 1)
        def _():
            scatters_done(slot)

        x = load_x()
        prev = jnp.where(i >= 1, i - 1, nb)
        for r in range(rows):
            scatter_copy(prev, 1 - slot, r).start()
        _store_row_tiles(obuf.at[slot], acc_ref[...] + swiglu_part(x))

        @pl.when(i == nu - 1)
        def _():
            scatters_done(1 - slot)
            for_rows(lambda r: scatter_copy(i, slot, r).start())
            scatters_done(slot)

    @pl.when(jnp.logical_and(i >= nu, f == 1))
    def _():
        xbuf[0] = jnp.zeros(xbuf.shape[1:], xbuf.dtype)
        fill = pltpu.make_async_copy(xbuf.at[0], y_hbm.at[pl.ds(i * rows * sub, rows * sub)], sem_fill)
        fill.start()
        fill.wait()

    @pl.when(jnp.logical_and(jnp.logical_and(i == nb - 1, f == 1), nu == nb))
    def _():
        gathers_done(nb % 2)


def _swiglu_experts(t, block_e, n_used, slot_tok, slot_dst, w_gate, w_up, w_down, tf):
    d = w_gate.shape[1]
    sub = d // LANES
    p = slot_tok.shape[0]
    nf = w_gate.shape[2] // tf
    assert nf == 2

    def fcol(i, f, nu):
        return jnp.where(i < nu[0], f, nf - 1)

    return pl.pallas_call(
        _moe_kernel,
        grid_spec=pltpu.PrefetchScalarGridSpec(
            num_scalar_prefetch=4,
            grid=(p // FFN_ROWS - 1, nf),
            in_specs=[pl.BlockSpec(memory_space=pl.ANY),
                      pl.BlockSpec((1, d, tf), lambda i, f, be, nu, st, sd: (be[i], 0, fcol(i, f, nu))),
                      pl.BlockSpec((1, d, tf), lambda i, f, be, nu, st, sd: (be[i], 0, fcol(i, f, nu))),
                      pl.BlockSpec((1, tf, d), lambda i, f, be, nu, st, sd: (be[i], fcol(i, f, nu), 0))],
            out_specs=pl.BlockSpec(memory_space=pl.ANY),
            scratch_shapes=[pltpu.VMEM((2, FFN_ROWS * sub, LANES), F32), pltpu.VMEM((FFN_ROWS, d), F32),
                            pltpu.VMEM((2, FFN_ROWS * sub, LANES), F32),
                            pltpu.SemaphoreType.DMA((2,)), pltpu.SemaphoreType.DMA((2,)),
                            pltpu.SemaphoreType.DMA]),
        out_shape=jax.ShapeDtypeStruct((p * sub, LANES), F32),
        compiler_params=_cparams("arbitrary", "arbitrary"),
        name="swiglu_experts",
    )(block_e, n_used, slot_tok, slot_dst, t, w_gate, w_up, w_down)


def _combine_ln_kernel(x_ref, mod_ref, g_ref, b_ref, *rest, alpha):
    y_refs, p_ref, o_ref = rest[:-2], rest[-2], rest[-1]
    t = x_ref.shape[1]
    top_k = p_ref.shape[2]
    for bb in range(x_ref.shape[0]):
        f = None
        for kk in range(top_k):
            term = _load_row_tiles(y_refs[bb * top_k + kk], t) * p_ref[bb, :, kk:kk + 1]
            f = term if f is None else f + term
        m = mod_ref[bb, 0]
        o_ref[bb] = _layer_norm(alpha * x_ref[bb] + m[5:6] * f, g_ref[...], b_ref[...])


def _combine_ln(x1, mods, ln_g, ln_b, y, probs, alpha):
    b, rows, d = x1.shape
    nt = rows // ROW_TILE
    sub = d // LANES
    top_k = probs.shape[2]
    nbb = BATCH_BLOCK
    tok = lambda w: pl.BlockSpec((nbb, ROW_TILE, w), lambda bi, i: (bi, i, 0))
    const = lambda shp: pl.BlockSpec(shp, lambda bi, i: (0,) * len(shp))
    y_specs = [pl.BlockSpec((ROW_TILE * sub, LANES),
                            lambda bi, i, kk=kk, bb=bb: ((kk * b + bi * nbb + bb) * nt + i, 0))
               for bb in range(nbb) for kk in range(top_k)]
    return pl.pallas_call(
        functools.partial(_combine_ln_kernel, alpha=alpha),
        grid=(b // nbb, nt),
        in_specs=[tok(d), pl.BlockSpec((nbb, 1, SUBLANES, d), lambda bi, i: (bi, 1, 0, 0)),
                  const((1, d)), const((1, d))] + y_specs + [tok(top_k)],
        out_specs=tok(d),
        out_shape=jax.ShapeDtypeStruct((b, rows, d), F32),
        compiler_params=_cparams("parallel", "parallel"),
        name="combine_ln2",
    )(x1, mods, ln_g, ln_b, *([y] * (top_k * nbb)), probs)


def _moe(t, logits, w_gate, w_up, w_down, tf):
    n, n_e = logits.shape
    sub = t.shape[0] // n
    top_v, top_i = lax.top_k(logits, TOP_K)
    probs = jax.nn.softmax(top_v, axis=-1)
    m = n * TOP_K
    flat_e = top_i.reshape(m)
    onehot = (flat_e[:, None] == jnp.arange(n_e, dtype=flat_e.dtype)[None, :]).astype(jnp.int32)
    csum = jnp.cumsum(onehot, axis=0)
    rank = jnp.take_along_axis(csum, flat_e[:, None], axis=1)[:, 0] - 1
    counts = csum[-1]
    padded = (counts + FFN_ROWS - 1) // FFN_ROWS * FFN_ROWS
    ends_p = jnp.cumsum(padded)
    start_p = ends_p - padded
    slot = (start_p[flat_e] + rank).astype(jnp.int32)
    nb = -(-(m + n_e * (FFN_ROWS - 1)) // FFN_ROWS)
    p = (nb + 1) * FFN_ROWS
    pair = jnp.arange(m, dtype=jnp.int32)
    slot_pair = jnp.full((p,), -1, jnp.int32).at[slot].set(pair)
    is_pad = slot_pair < 0
    slot_tok = jnp.where(is_pad, 0, slot_pair // TOP_K) * sub
    slot_dst = jnp.where(is_pad, m + jnp.cumsum(is_pad.astype(jnp.int32)) - 1,
                         (slot_pair % TOP_K) * n + slot_pair // TOP_K) * sub
    block_e = jnp.minimum(jnp.searchsorted(ends_p, jnp.arange(nb + 1) * FFN_ROWS, side='right'),
                          n_e - 1).astype(jnp.int32)
    n_used = (ends_p[-1:] // FFN_ROWS).astype(jnp.int32)
    y = _swiglu_experts(t, block_e, n_used, slot_tok, slot_dst, w_gate, w_up, w_down, tf)
    return y, probs


def _block_diag(w):
    nb, bi, bj = w.shape
    eye = jnp.eye(nb, dtype=w.dtype)
    return (eye[:, None, :, None] * w[:, :, None, :]).reshape(nb * bi, nb * bj)


def _rope_tables(ctx_len, s):
    nf = HEAD_DIM // 4
    inv = ROPE_THETA ** (-jnp.arange(nf, dtype=F32) / nf)
    t = jnp.arange(s)
    row = (t // GRID_W).astype(F32)[:, None] * inv
    col = (t % GRID_W).astype(F32)[:, None] * inv
    ang = jnp.concatenate([row, row, col, col], axis=1)
    sign = jnp.tile(jnp.concatenate([-jnp.ones((nf,), F32), jnp.ones((nf,), F32)]), 2)
    cos = jnp.concatenate([jnp.ones((ctx_len, HEAD_DIM), F32), jnp.cos(ang)], axis=0)
    sin = jnp.concatenate([jnp.zeros((ctx_len, HEAD_DIM), F32), jnp.sin(ang) * sign], axis=0)
    reps = LANES // HEAD_DIM
    return jnp.tile(cos, (1, reps)), jnp.tile(sin, (1, reps))


def _pad_rows(a, rows):
    return jnp.zeros((rows,) + a.shape[1:], a.dtype).at[:a.shape[0]].set(a)


def kernel(x, c, ctx, c_ctx, w_mod, b_mod, w_in, b_merge, pool_w, pool_scale, q_norm, k_norm, conv_w, conv_b, lru_wa, lru_ba, lru_wx, lru_bx, lru_lambda, w_branch, w_out, ln1_g, ln1_b, ffn_w_gate, ffn_w_up, ffn_w_down, moe_router, moe_router_b, moe_w_gate, moe_w_up, moe_w_down, ln2_g, ln2_b):
    b, s, d = x.shape
    ctx_len = ctx.shape[1]
    depth = w_in.shape[0]
    l = ctx_len + s
    assert ctx_len % ROW_TILE == 0 and s % ROW_TILE == 0 and (b * s) % FFN_ROWS == 0 and b % BATCH_BLOCK == 0
    assert (b * l) % FFN_ROWS == 0 and ctx_len % Q_TILE == 0 and s % Q_TILE == 0
    nct = ctx_len // ROW_TILE
    alpha = (2 * depth) ** 0.25

    cc = _pad_rows(jnp.concatenate([c, c_ctx[None, :]], axis=0), -(-(b + 1) // SUBLANES) * SUBLANES)
    mod_all = _modulation(cc, w_mod, b_mod)

    cos_t, sin_t = _rope_tables(ctx_len, s)
    bd = _block_diag(jnp.full((LANES // HEAD_DIM, HEAD_DIM, HEAD_DIM), 1.0 / HEAD_DIM, F32)).astype(BF16)
    reps = LANES // HEAD_DIM

    xs = jnp.concatenate([ctx, x], axis=1)
    for li in range(depth):
        last = li == depth - 1
        ml = jnp.transpose(mod_all[li, :, :b], (1, 0, 2))
        mc = jnp.broadcast_to(mod_all[li, :, b][None], (b, 6, d))
        mods = jnp.stack([mc, ml], axis=1)
        mods = jnp.concatenate([mods, jnp.zeros((b, 2, SUBLANES - 6, d), F32)], axis=2)

        w_l = w_in[li].astype(BF16)
        p, q, k, v, lx, lg = _project(
            xs, mods, w_l[:, :OFF_GATE], bd,
            jnp.tile(q_norm[li], reps)[None], jnp.tile(k_norm[li], reps)[None], cos_t, sin_t, nct)
        attn = _attention(q, k, v, ctx_len, with_ctx=not last)

        cw = _pad_rows(conv_w[li], SUBLANES)
        lru_w = [jnp.concatenate([_block_diag(lru_wa[li, dr]), _block_diag(lru_wx[li, dr])], axis=1).astype(BF16)
                 for dr in range(2)]
        hf = _lru(lx, lru_w[0], lru_ba[li, 0][None], lru_bx[li, 0][None], lru_lambda[li, 0][None],
                  cw, conv_b[li][None], nct, reverse=False)
        lru_y = _lru(lx, lru_w[1], lru_ba[li, 1][None], lru_bx[li, 1][None], lru_lambda[li, 1][None],
                     cw, conv_b[li][None], nct, reverse=True, hf=hf, lg=lg)

        jf = li // 2
        merge_args = (xs, mods, p, attn, lru_y, w_l[:, OFF_GATE:], _pad_rows(b_merge[li], SUBLANES),
                      pool_w[li].astype(BF16), pool_scale[li][None], w_branch[li].astype(BF16),
                      w_out[li].astype(BF16), ln1_g[li][None], ln1_b[li][None])
        if li % 2 == 0:
            assert not last, "the dense layer is expected to carry the context tokens along"
            x1, h2 = _merge(*merge_args, nct, lat_only=False, alpha=alpha)
            xs = _swiglu_dense_ln(h2, x1, mods, ln2_g[li][None], ln2_b[li][None], ffn_w_gate[jf].astype(BF16),
                                  ffn_w_up[jf].astype(BF16), ffn_w_down[jf].astype(BF16),
                                  tf=ffn_w_gate.shape[2] // 2, nct=nct, alpha=alpha)
        else:
            assert last, "the expert layer is expected to be the last layer (latent tokens only)"
            n_e = moe_router.shape[2]
            rw = jnp.zeros((d, LANES), F32).at[:, :n_e].set(moe_router[jf])
            rb = jnp.zeros((1, LANES), F32).at[0, :n_e].set(moe_router_b[jf])
            rw_hi = rw.astype(BF16)
            rw_split = jnp.stack([rw_hi, (rw - rw_hi.astype(F32)).astype(BF16)])
            x1, h2, logits = _merge(*merge_args, nct, lat_only=True, alpha=alpha, router=(rw_split, rb))
            y, probs = _moe(h2.reshape(-1, LANES), logits.reshape(b * s, LANES)[:, :n_e],
                            moe_w_gate[jf].astype(BF16), moe_w_up[jf].astype(BF16), moe_w_down[jf].astype(BF16),
                            tf=moe_w_gate.shape[3] // 2)
            xs = _combine_ln(x1, mods, ln2_g[li][None], ln2_b[li][None], y, probs.reshape(b, s, -1), alpha)
    return xs
```

```python
import functools

import jax
import jax.numpy as jnp
from jax import lax
from jax.experimental import pallas as pl
from jax.experimental.pallas import tpu as pltpu

F32 = jnp.float32
BF16 = jnp.bfloat16

GRID_W = 64
POOL_WINDOWS = (2, 4, 8, 16)
BRANCH_WIDTH = 512
POOL_GROUP = BRANCH_WIDTH // len(POOL_WINDOWS)
N_Q_HEADS = 8
N_KV_HEADS = 2
HEAD_DIM = 64
GQA_GROUP = N_Q_HEADS // N_KV_HEADS
Q_WIDTH = N_Q_HEADS * HEAD_DIM
KV_WIDTH = N_KV_HEADS * HEAD_DIM
ROPE_THETA = 10000.0
ATTN_SCALE = HEAD_DIM ** -0.5
LOG2_E = 1.4426950408889634
LRU_WIDTH = BRANCH_WIDTH
LRU_BLOCKS = 8
CONV_WIDTH = 4
RG_C = 8.0
N_BRANCH = 3
N_EXPERTS = 8
TOP_K = 2
LN_EPS = 1e-5
RMS_EPS = 1e-6
OFF_GATE = BRANCH_WIDTH + Q_WIDTH + 2 * KV_WIDTH + 2 * LRU_WIDTH

LANES = 128
SUBLANES = 8
ROW_TILE = 256
BATCH_BLOCK = 2
Q_TILE = 256
ATTN_CHAIN = 256
FFN_ROWS = 512
HALO = SUBLANES
VMEM_LIMIT = 56 * 1024 * 1024


def _cparams(*sem):
    return pltpu.CompilerParams(dimension_semantics=sem, vmem_limit_bytes=VMEM_LIMIT)


def _sigmoid(x):
    return 0.5 * jnp.tanh(0.5 * x) + 0.5


def _store_row_tiles(ref, val):
    t, d = val.shape
    for j in range(d // LANES):
        ref[pl.ds(j, t, stride=d // LANES), :] = val[:, j * LANES:(j + 1) * LANES]


def _load_row_tiles(ref, t):
    n = ref.shape[0] // t
    return jnp.concatenate([ref[pl.ds(j, t, stride=n), :] for j in range(n)], axis=1)


def _silu(x):
    return x * _sigmoid(x)


def _layer_norm(v, g, b):
    mu = jnp.mean(v, axis=-1, keepdims=True)
    d = v - mu
    var = jnp.mean(d * d, axis=-1, keepdims=True)
    return d * lax.rsqrt(var + LN_EPS) * g + b


def _mod_kernel(c_ref, w_ref, b_ref, o_ref):
    s = _silu(c_ref[...])
    o_ref[0, 0] = jnp.dot(s, w_ref[0], precision=lax.Precision.HIGHEST,
                          preferred_element_type=F32) + b_ref[0, 0]


def _modulation(cc, w_mod, b_mod):
    depth, d, _ = w_mod.shape
    r = cc.shape[0]
    b6 = b_mod.reshape(depth, 6, 1, d)
    return pl.pallas_call(
        _mod_kernel,
        grid=(depth, 6),
        in_specs=[pl.BlockSpec((r, d), lambda l, j: (0, 0)),
                  pl.BlockSpec((1, d, d), lambda l, j: (l, 0, j)),
                  pl.BlockSpec((1, 1, 1, d), lambda l, j: (l, j, 0, 0))],
        out_specs=pl.BlockSpec((1, 1, r, d), lambda l, j: (l, j, 0, 0)),
        out_shape=jax.ShapeDtypeStruct((depth, 6, r, d), F32),
        compiler_params=_cparams("arbitrary", "arbitrary"),
        name="modulation",
    )(cc, w_mod, b6)


def _norm_rope(zc, bd, g, cos, sin):
    zz = zc * zc
    hi = zz.astype(BF16)
    lo = (zz - hi.astype(F32)).astype(BF16)
    ms = jnp.dot(hi, bd, preferred_element_type=F32) + jnp.dot(lo, bd, preferred_element_type=F32)
    y = zc * lax.rsqrt(ms + RMS_EPS) * g
    nf = HEAD_DIM // 4
    up = pltpu.roll(y, LANES - nf, 1)
    dn = pltpu.roll(y, nf, 1)
    lane = lax.broadcasted_iota(jnp.int32, y.shape, 1)
    partner = jnp.where((lane % (2 * nf)) < nf, up, dn)
    return y * cos + partner * sin


def _proj_kernel(x_ref, mod_ref, w_ref, bd_ref, qg_ref, kg_ref, cos_ref, sin_ref,
                 p_ref, q_ref, k_ref, v_ref, lx_ref, lg_ref):
    bd = bd_ref[...]
    cos = cos_ref[...]
    sin = sin_ref[...]
    for bb in range(x_ref.shape[0]):
        x = x_ref[bb]
        m = mod_ref[bb, 0]
        h = (x * (1.0 + m[1:2]) + m[0:1]).astype(BF16)
        z = jnp.dot(h, w_ref[...], preferred_element_type=F32)
        o = 0
        p_ref[bb] = z[:, o:o + BRANCH_WIDTH]
        o += BRANCH_WIDTH
        for c in range(Q_WIDTH // LANES):
            y = _norm_rope(z[:, o:o + LANES], bd, qg_ref[...], cos, sin) * (ATTN_SCALE * LOG2_E)
            q_ref[bb, 2 * c] = y[:, :HEAD_DIM].astype(BF16)
            q_ref[bb, 2 * c + 1] = y[:, HEAD_DIM:].astype(BF16)
            o += LANES
        for c in range(KV_WIDTH // LANES):
            y = _norm_rope(z[:, o:o + LANES], bd, kg_ref[...], cos, sin)
            k_ref[bb, 2 * c] = y[:, :HEAD_DIM].astype(BF16)
            k_ref[bb, 2 * c + 1] = y[:, HEAD_DIM:].astype(BF16)
            o += LANES
        for c in range(KV_WIDTH // LANES):
            y = z[:, o:o + LANES]
            ones = jnp.ones((y.shape[0], LANES - HEAD_DIM), F32)
            v_ref[bb, 2 * c] = jnp.concatenate([y[:, :HEAD_DIM], ones], axis=1).astype(BF16)
            v_ref[bb, 2 * c + 1] = jnp.concatenate([y[:, HEAD_DIM:], ones], axis=1).astype(BF16)
            o += LANES
        lx_ref[bb] = z[:, o:o + LRU_WIDTH]
        o += LRU_WIDTH
        lg_ref[bb] = z[:, o:o + LRU_WIDTH]


def _project(x, mods, w_a, bd, qg, kg, cos_t, sin_t, nct):
    b, l, d = x.shape
    nt = l // ROW_TILE
    wa = w_a.shape[1]
    bb = BATCH_BLOCK
    tok = lambda w: pl.BlockSpec((bb, ROW_TILE, w), lambda bi, i: (bi, i, 0))
    head = lambda n, w=HEAD_DIM: pl.BlockSpec((bb, n, ROW_TILE, w), lambda bi, i: (bi, 0, i, 0))
    const = lambda shp: pl.BlockSpec(shp, lambda bi, i: (0,) * len(shp))
    return pl.pallas_call(
        _proj_kernel,
        grid=(b // bb, nt),
        in_specs=[tok(d),
                  pl.BlockSpec((bb, 1, SUBLANES, d), lambda bi, i: (bi, (i >= nct).astype(jnp.int32), 0, 0)),
                  const((d, wa)), const((LANES, LANES)), const((1, LANES)), const((1, LANES)),
                  pl.BlockSpec((ROW_TILE, LANES), lambda bi, i: (i, 0)),
                  pl.BlockSpec((ROW_TILE, LANES), lambda bi, i: (i, 0))],
        out_specs=[tok(BRANCH_WIDTH), head(N_Q_HEADS), head(N_KV_HEADS), head(N_KV_HEADS, LANES),
                   tok(LRU_WIDTH), tok(LRU_WIDTH)],
        out_shape=[jax.ShapeDtypeStruct((b, l, BRANCH_WIDTH), F32),
                   jax.ShapeDtypeStruct((b, N_Q_HEADS, l, HEAD_DIM), BF16),
                   jax.ShapeDtypeStruct((b, N_KV_HEADS, l, HEAD_DIM), BF16),
                   jax.ShapeDtypeStruct((b, N_KV_HEADS, l, LANES), BF16),
                   jax.ShapeDtypeStruct((b, l, LRU_WIDTH), F32),
                   jax.ShapeDtypeStruct((b, l, LRU_WIDTH), F32)],
        compiler_params=_cparams("parallel", "parallel"),
        name="in_proj",
    )(x, mods, w_a, bd, qg, kg, cos_t, sin_t)


def _softmax_pv(q, k, v):
    hd = q.shape[1]
    s = lax.dot_general(q, k, (((1,), (1,)), ((), ())), preferred_element_type=F32)
    m = jnp.max(s, axis=1, keepdims=True)
    p = jnp.exp2(s - m).astype(BF16)
    r = jnp.dot(p, v, preferred_element_type=F32)
    return r[:, :hd] / r[:, hd:2 * hd]


def _attn_kernel(q_ref, k_ref, v_ref, o_ref, *, n_ctx_tiles, ctx_len):
    nh, tq, hd = q_ref.shape[1:]
    group = nh // k_ref.shape[1]

    def run(lk):
        for j in range(nh):
            for r0 in range(0, tq, ATTN_CHAIN):
                o = _softmax_pv(q_ref[0, j, r0:r0 + ATTN_CHAIN, :], k_ref[0, j // group, :lk],
                                v_ref[0, j // group, :lk])
                o_ref[0, r0:r0 + ATTN_CHAIN, j * hd:(j + 1) * hd] = o.astype(o_ref.dtype)

    if n_ctx_tiles:
        @pl.when(pl.program_id(1) < n_ctx_tiles)
        def _():
            run(ctx_len)

        @pl.when(pl.program_id(1) >= n_ctx_tiles)
        def _():
            run(k_ref.shape[2])
    else:
        run(k_ref.shape[2])


def _attention(q, k, v, ctx_len, with_ctx):
    b, nh, l, hd = q.shape
    nct = ctx_len // Q_TILE
    rows = l if with_ctx else l - ctx_len
    q_off = 0 if with_ctx else nct
    return pl.pallas_call(
        functools.partial(_attn_kernel, n_ctx_tiles=nct if with_ctx else 0, ctx_len=ctx_len),
        grid=(b, rows // Q_TILE),
        in_specs=[pl.BlockSpec((1, nh, Q_TILE, hd), lambda bi, i: (bi, 0, i + q_off, 0)),
                  pl.BlockSpec((1,) + k.shape[1:], lambda bi, i: (bi, 0, 0, 0)),
                  pl.BlockSpec((1,) + v.shape[1:], lambda bi, i: (bi, 0, 0, 0))],
        out_specs=pl.BlockSpec((1, Q_TILE, nh * hd), lambda bi, i: (bi, i, 0)),
        out_shape=jax.ShapeDtypeStruct((b, rows, nh * hd), BF16),
        compiler_params=_cparams("parallel", "parallel"),
        name="attention",
    )(q, k, v)


def _fill_ext(ext_ref, main, prev, nxt, first, last):
    t = main.shape[0]
    ext_ref[0:HALO] = jnp.where(first, 0.0, prev)
    ext_ref[HALO:HALO + t] = main
    ext_ref[HALO + t:2 * HALO + t] = jnp.where(last, 0.0, nxt)


def _seg_flags(tile, nct, nt):
    first = jnp.logical_or(tile == 0, tile == nct)
    last = jnp.logical_or(tile == nct - 1, tile == nt - 1)
    return first, last


def _halo_specs(width, tile_of, l):
    rb = ROW_TILE // HALO
    last_blk = l // HALO - 1
    bb = BATCH_BLOCK
    main = pl.BlockSpec((bb, ROW_TILE, width), lambda bi, i: (bi, tile_of(i), 0))
    prev = pl.BlockSpec((bb, HALO, width), lambda bi, i: (bi, jnp.maximum(tile_of(i) * rb - 1, 0), 0))
    nxt = pl.BlockSpec((bb, HALO, width), lambda bi, i: (bi, jnp.minimum((tile_of(i) + 1) * rb, last_blk), 0))
    return main, prev, nxt


def _softplus(z):
    return jnp.maximum(z, 0.0) + jnp.log1p(jnp.exp(-jnp.abs(z)))


def _gelu_tanh(x):
    return 0.5 * x * (1.0 + jnp.tanh(0.7978845608028654 * (x + 0.044715 * (x * x * x))))


def _lru_kernel(*refs, reverse, nct, nt):
    if reverse:
        (lx_ref, lxp_ref, lxn_ref, w_ref, ba_ref, bx_ref, lam_ref, cw_ref, cb_ref, hf_ref, lg_ref,
         o_ref, ext_ref, a_ref, u_ref, h_ref) = refs
    else:
        (lx_ref, lxp_ref, lxn_ref, w_ref, ba_ref, bx_ref, lam_ref, cw_ref, cb_ref,
         o_ref, ext_ref, a_ref, u_ref, h_ref) = refs
    j = pl.program_id(1)
    tile = _lru_tile(j, reverse, nct, nt)
    t = lx_ref.shape[1]
    nbb = lx_ref.shape[0]

    @pl.when(j == 0)
    def _():
        h_ref[...] = jnp.zeros_like(h_ref)

    first, last = _seg_flags(tile, nct, nt)
    lo = CONV_WIDTH // 2
    half_log2_a = (-0.5 * RG_C * LOG2_E) * _softplus(-lam_ref[...])
    for bb in range(nbb):
        ext = ext_ref.at[bb]
        _fill_ext(ext, lx_ref[bb], lxp_ref[bb], lxn_ref[bb], first, last)
        xh = cb_ref[...]
        for kk in range(CONV_WIDTH):
            xh = xh + ext[pl.ds(HALO + kk - lo, t), :] * cw_ref[kk:kk + 1, :]
        zz = jnp.dot(xh.astype(BF16), w_ref[...], preferred_element_type=F32)
        tr = jnp.tanh(zz[:, :LRU_WIDTH] + ba_ref[...])
        ti = jnp.tanh(zz[:, LRU_WIDTH:] + bx_ref[...])
        a = jnp.exp2(half_log2_a * (tr + 1.0))
        a_ref[bb] = a
        u_ref[bb] = jnp.sqrt(1.0 - a * a) * (ti + 1.0) * xh

    row = lax.broadcasted_iota(jnp.int32, (SUBLANES, LRU_WIDTH), 0)
    n_sub = t // SUBLANES

    def sub(s, hs):
        blk = (n_sub - 1 - s) if reverse else s
        r0 = pl.multiple_of(blk * SUBLANES, SUBLANES)
        out = []
        for bb in range(nbb):
            aa = a_ref[bb, pl.ds(r0, SUBLANES), :]
            uu = u_ref[bb, pl.ds(r0, SUBLANES), :]
            for dd in (1, 2, 4):
                if reverse:
                    keep = row < SUBLANES - dd
                    sh = SUBLANES - dd
                else:
                    keep = row >= dd
                    sh = dd
                a_sh = jnp.where(keep, pltpu.roll(aa, sh, 0), 1.0)
                u_sh = jnp.where(keep, pltpu.roll(uu, sh, 0), 0.0)
                uu = aa * u_sh + uu
                aa = aa * a_sh
            hh = aa * hs[bb] + uu
            if reverse:
                o_ref[bb, pl.ds(r0, SUBLANES), :] = (
                    (hf_ref[bb, pl.ds(r0, SUBLANES), :] + hh)
                    * _gelu_tanh(lg_ref[bb, pl.ds(r0, SUBLANES), :])).astype(o_ref.dtype)
                edge = hh[0:1]
            else:
                o_ref[bb, pl.ds(r0, SUBLANES), :] = hh
                edge = hh[SUBLANES - 1:SUBLANES]
            out.append(jnp.broadcast_to(edge, (SUBLANES, LRU_WIDTH)))
        return tuple(out)

    hs = lax.fori_loop(0, n_sub, sub, tuple(h_ref[bb] for bb in range(nbb)), unroll=4)
    for bb in range(nbb):
        h_ref[bb] = hs[bb]


def _lru_tile(j, reverse, nct, nt):
    if not reverse:
        return j
    return jnp.where(j < nct, nct - 1 - j, nt - 1 - (j - nct))


def _lru(lx, w, ba, bx, lam, cw, cb, nct, reverse, hf=None, lg=None):
    b, l, wd = lx.shape
    nt = l // ROW_TILE
    tile_of = lambda i: _lru_tile(i, reverse, nct, nt)
    main, prev, nxt = _halo_specs(wd, tile_of, l)
    const = lambda shp: pl.BlockSpec(shp, lambda bi, i: (0,) * len(shp))
    in_specs = [main, prev, nxt, const(w.shape), const((1, wd)), const((1, wd)), const((1, wd)),
                const((SUBLANES, wd)), const((1, wd))]
    args = [lx, lx, lx, w, ba, bx, lam, cw, cb]
    if reverse:
        in_specs += [main, main]
        args += [hf, lg]
    bb = BATCH_BLOCK
    return pl.pallas_call(
        functools.partial(_lru_kernel, reverse=reverse, nct=nct, nt=nt),
        grid=(b // bb, nt),
        in_specs=in_specs,
        out_specs=main,
        out_shape=jax.ShapeDtypeStruct((b, l, wd), BF16 if reverse else F32),
        scratch_shapes=[pltpu.VMEM((bb, ROW_TILE + 2 * HALO, wd), F32), pltpu.VMEM((bb, ROW_TILE, wd), F32),
                        pltpu.VMEM((bb, ROW_TILE, wd), F32), pltpu.VMEM((bb, SUBLANES, wd), F32)],
        compiler_params=_cparams("parallel", "arbitrary"),
        name="lru_bwd" if reverse else "lru_fwd",
    )(*args)


def _merge_kernel(x_ref, mod_ref, p_ref, pp_ref, pn_ref, at_ref, lr_ref, wg_ref, bm_ref, pw_ref, ps_ref,
                  wb_ref, wo_ref, g_ref, b_ref, *rest, nct, nt, tile_off, ctx_len, alpha, route):
    if route:
        rw_ref, rb_ref, x1_ref, h2_ref, logit_ref, ext_ref = rest
    else:
        x1_ref, h2_ref, ext_ref = rest
    tile = pl.program_id(1) + tile_off
    t = x_ref.shape[1]
    d = x_ref.shape[2]
    first, last = _seg_flags(tile, nct, nt)
    in_ctx = tile < nct
    seg_len = jnp.where(in_ctx, ctx_len, nt * t - ctx_len)
    pos = (tile - jnp.where(in_ctx, 0, nct)) * t + lax.broadcasted_iota(jnp.int32, (t, 1), 0)

    for bb in range(x_ref.shape[0]):
        x = x_ref[bb]
        m = mod_ref[bb, 0]
        h = (x * (1.0 + m[1:2]) + m[0:1]).astype(BF16)

        def branch(n, ys):
            tg = jnp.tanh(jnp.dot(h, wg_ref[:, n * d:(n + 1) * d], preferred_element_type=F32) + bm_ref[n:n + 1, :])
            return (tg + 1.0) * jnp.dot(ys, wb_ref[n], preferred_element_type=F32)

        mix = branch(1, at_ref[bb]) + branch(2, lr_ref[bb])

        ext = ext_ref.at[bb]
        _fill_ext(ext, p_ref[bb], pp_ref[bb], pn_ref[bb], first, last)
        pooled = []
        for gi, w in enumerate(POOL_WINDOWS):
            lo = w // 2
            cols = slice(gi * POOL_GROUP, (gi + 1) * POOL_GROUP)
            acc = ext[pl.ds(HALO - lo, t), cols]
            for kk in range(1, w):
                acc = acc + ext[pl.ds(HALO - lo + kk, t), cols]
            cnt = jnp.clip(pos - lo + w, 0, seg_len) - jnp.clip(pos - lo, 0, seg_len)
            mean = acc / cnt.astype(F32)
            dlt = (mean - ext[pl.ds(HALO, t), cols]).astype(BF16)
            pooled.append(jnp.dot(dlt, pw_ref[gi], preferred_element_type=F32))
        pool_y = (jnp.concatenate(pooled, axis=1) * ps_ref[...]).astype(BF16)
        mix = branch(0, pool_y) + mix
        y = jnp.dot(mix.astype(BF16), wo_ref[...], preferred_element_type=F32)
        x1 = _layer_norm(alpha * x + m[2:3] * y, g_ref[...], b_ref[...])
        x1_ref[bb] = x1
        h2 = x1 * (1.0 + m[4:5]) + m[3:4]
        if route:
            hi = h2.astype(BF16)
            mid = (h2 - hi.astype(F32)).astype(BF16)
            logit_ref[bb] = (jnp.dot(hi, rw_ref[0], preferred_element_type=F32)
                             + jnp.dot(mid, rw_ref[0], preferred_element_type=F32)
                             + jnp.dot(hi, rw_ref[1], preferred_element_type=F32) + rb_ref[...])
            _store_row_tiles(h2_ref.at[bb], h2)
        else:
            h2_ref[bb] = h2.astype(h2_ref.dtype)


def _merge(x, mods, p, attn, lru, w_gate, b_merge, pool_w, pool_scale, w_branch, w_out, ln_g, ln_b,
           nct, lat_only, alpha, router=None):
    b, l, d = x.shape
    nt = l // ROW_TILE
    off = nct if lat_only else 0
    rows = l - off * ROW_TILE
    tile_of = lambda i: i + off
    bb = BATCH_BLOCK
    tok_l = lambda w: pl.BlockSpec((bb, ROW_TILE, w), lambda bi, i: (bi, i + off, 0))
    tok_o = lambda w: pl.BlockSpec((bb, ROW_TILE, w), lambda bi, i: (bi, i, 0))
    const = lambda shp: pl.BlockSpec(shp, lambda bi, i: (0,) * len(shp))
    pm, pp, pn = _halo_specs(BRANCH_WIDTH, tile_of, l)
    in_specs = [tok_l(d),
                pl.BlockSpec((bb, 1, SUBLANES, d), lambda bi, i: (bi, (i + off >= nct).astype(jnp.int32), 0, 0)),
                pm, pp, pn,
                tok_o(Q_WIDTH) if lat_only else tok_l(Q_WIDTH),
                tok_l(LRU_WIDTH),
                const(w_gate.shape), const(b_merge.shape), const(pool_w.shape), const(pool_scale.shape),
                const(w_branch.shape), const(w_out.shape), const((1, d)), const((1, d))]
    args = [x, mods, p, p, p, attn, lru, w_gate, b_merge, pool_w, pool_scale, w_branch, w_out, ln_g, ln_b]
    out_specs = [tok_o(d)]
    out_shape = [jax.ShapeDtypeStruct((b, rows, d), F32)]
    if router is None:
        out_specs.append(tok_o(d))
        out_shape.append(jax.ShapeDtypeStruct((b, rows, d), BF16))
    else:
        n_sub = d // LANES
        in_specs += [const(router[0].shape), const(router[1].shape)]
        args += list(router)
        out_specs += [pl.BlockSpec((bb, ROW_TILE * n_sub, LANES), lambda bi, i: (bi, i, 0)), tok_o(LANES)]
        out_shape += [jax.ShapeDtypeStruct((b, rows * n_sub, LANES), F32),
                      jax.ShapeDtypeStruct((b, rows, LANES), F32)]
    return pl.pallas_call(
        functools.partial(_merge_kernel, nct=nct, nt=nt, tile_off=off, ctx_len=nct * ROW_TILE, alpha=alpha,
                          route=router is not None),
        grid=(b // bb, rows // ROW_TILE),
        in_specs=in_specs,
        out_specs=out_specs,
        out_shape=out_shape,
        scratch_shapes=[pltpu.VMEM((bb, ROW_TILE + 2 * HALO, BRANCH_WIDTH), F32)],
        compiler_params=_cparams("parallel", "parallel"),
        name="merge_ln1",
    )(*args)


def _swiglu_kernel(x_ref, wg_ref, wu_ref, wd_ref, x1_ref, *rest, alpha):
    mod_refs, (g_ref, b_ref, o_ref) = rest[:-3], rest[-3:]
    x = x_ref[...]
    g = jnp.dot(x, wg_ref[...], preferred_element_type=F32)
    u = jnp.dot(x, wu_ref[...], preferred_element_type=F32)
    ff = jnp.dot((_silu(g) * u).astype(BF16), wd_ref[...], preferred_element_type=F32)
    for hh, mod_ref in enumerate(mod_refs):
        rs = slice(hh * ROW_TILE, (hh + 1) * ROW_TILE)
        o_ref[rs, :] = _layer_norm(alpha * x1_ref[rs, :] + mod_ref[0, 0, 5:6, :] * ff[rs], g_ref[...], b_ref[...])


def _swiglu_dense_ln(h2, x1, mods, ln_g, ln_b, w_gate, w_up, w_down, nct, alpha):
    b, l, d = x1.shape
    r = b * l
    ntl = l // ROW_TILE
    halves = FFN_ROWS // ROW_TILE

    def mod_spec(hh):
        def idx(i):
            tile = i * halves + hh
            return (tile // ntl, (tile % ntl >= nct).astype(jnp.int32), 0, 0)
        return pl.BlockSpec((1, 1, SUBLANES, d), idx)

    resident = lambda shp: pl.BlockSpec(shp, lambda i: (0, 0), pipeline_mode=pl.Buffered(1))
    out = pl.pallas_call(
        functools.partial(_swiglu_kernel, alpha=alpha),
        grid=(r // FFN_ROWS,),
        in_specs=[pl.BlockSpec((FFN_ROWS, d), lambda i: (i, 0)),
                  resident(w_gate.shape), resident(w_up.shape), resident(w_down.shape),
                  pl.BlockSpec((FFN_ROWS, d), lambda i: (i, 0))]
                 + [mod_spec(hh) for hh in range(halves)]
                 + [pl.BlockSpec((1, d), lambda i: (0, 0)), pl.BlockSpec((1, d), lambda i: (0, 0))],
        out_specs=pl.BlockSpec((FFN_ROWS, d), lambda i: (i, 0)),
        out_shape=jax.ShapeDtypeStruct((r, d), F32),
        compiler_params=_cparams("parallel"),
        name="swiglu_dense_ln2",
    )(h2.reshape(r, d), w_gate, w_up, w_down, x1.reshape(r, d), *([mods] * halves), ln_g, ln_b)
    return out.reshape(b, l, d)


def _moe_kernel(be_ref, nu_ref, tok_ref, dst_ref, t_hbm, wg_ref, wu_ref, wd_ref, y_hbm,
                xbuf, acc_ref, obuf, sem_in, sem_out, sem_fill):
    i = pl.program_id(0)
    f = pl.program_id(1)
    nb = pl.num_programs(0)
    nu = nu_ref[0]
    rows = acc_ref.shape[0]
    sub = xbuf.shape[1] // rows
    slot = i % 2

    def tile_rows(start):
        return pl.ds(pl.multiple_of(start, sub), sub)

    def gather_copy(blk, sl, r):
        return pltpu.make_async_copy(t_hbm.at[tile_rows(tok_ref[blk * rows + r])], xbuf.at[sl, tile_rows(r * sub)],
                                     sem_in.at[sl])

    def scatter_copy(blk, sl, r):
        return pltpu.make_async_copy(obuf.at[sl, tile_rows(r * sub)], y_hbm.at[tile_rows(dst_ref[blk * rows + r])],
                                     sem_out.at[sl])

    def gathers_done(sl):
        pltpu.make_async_copy(t_hbm.at[pl.ds(0, rows * sub)], xbuf.at[sl], sem_in.at[sl]).wait()

    def scatters_done(sl):
        pltpu.make_async_copy(obuf.at[sl], y_hbm.at[pl.ds(0, rows * sub)], sem_out.at[sl]).wait()

    def for_rows(fn):
        def body(r, c):
            fn(r)
            return c
        lax.fori_loop(0, rows, body, 0, unroll=8)

    def load_x():
        return _load_row_tiles(xbuf.at[slot], rows).astype(BF16)

    def swiglu_part(x):
        g = jnp.dot(x, wg_ref[0], preferred_element_type=F32)
        u = jnp.dot(x, wu_ref[0], preferred_element_type=F32)
        return jnp.dot((_silu(g) * u).astype(BF16), wd_ref[0], preferred_element_type=F32)

    @pl.when(jnp.logical_and(i == 0, f == 0))
    def _():
        obuf[1] = jnp.zeros(obuf.shape[1:], obuf.dtype)
        for_rows(lambda r: gather_copy(0, 0, r).start())

    @pl.when(jnp.logical_and(f == 0, i <= nu))
    def _():
        gathers_done(slot)

    @pl.when(jnp.logical_and(i < nu, f == 0))
    def _():
        x = load_x()
        for r in range(rows):
            gather_copy(i + 1, 1 - slot, r).start()
        acc_ref[...] = swiglu_part(x)

    @pl.when(jnp.logical_and(i < nu, f == 1))
    def _():
        @pl.when(i >= 1)
        def _():
            scatters_done(slot)

        x = load_x()
        prev = jnp.where(i >= 1, i - 1, nb)
        for r in range(rows):
            scatter_copy(prev, 1 - slot, r).start()
        _store_row_tiles(obuf.at[slot], acc_ref[...] + swiglu_part(x))

        @pl.when(i == nu - 1)
        def _():
            scatters_done(1 - slot)
            for_rows(lambda r: scatter_copy(i, slot, r).start())
            scatters_done(slot)

    @pl.when(jnp.logical_and(i >= nu, f == 1))
    def _():
        xbuf[0] = jnp.zeros(xbuf.shape[1:], xbuf.dtype)
        fill = pltpu.make_async_copy(xbuf.at[0], y_hbm.at[pl.ds(i * rows * sub, rows * sub)], sem_fill)
        fill.start()
        fill.wait()

    @pl.when(jnp.logical_and(jnp.logical_and(i == nb - 1, f == 1), nu == nb))
    def _():
        gathers_done(nb % 2)


def _swiglu_experts(t, block_e, n_used, slot_tok, slot_dst, w_gate, w_up, w_down, tf):
    d = w_gate.shape[1]
    sub = d // LANES
    p = slot_tok.shape[0]
    nf = w_gate.shape[2] // tf
    assert nf == 2

    def fcol(i, f, nu):
        return jnp.where(i < nu[0], f, nf - 1)

    return pl.pallas_call(
        _moe_kernel,
        grid_spec=pltpu.PrefetchScalarGridSpec(
            num_scalar_prefetch=4,
            grid=(p // FFN_ROWS - 1, nf),
            in_specs=[pl.BlockSpec(memory_space=pl.ANY),
                      pl.BlockSpec((1, d, tf), lambda i, f, be, nu, st, sd: (be[i], 0, fcol(i, f, nu))),
                      pl.BlockSpec((1, d, tf), lambda i, f, be, nu, st, sd: (be[i], 0, fcol(i, f, nu))),
                      pl.BlockSpec((1, tf, d), lambda i, f, be, nu, st, sd: (be[i], fcol(i, f, nu), 0))],
            out_specs=pl.BlockSpec(memory_space=pl.ANY),
            scratch_shapes=[pltpu.VMEM((2, FFN_ROWS * sub, LANES), F32), pltpu.VMEM((FFN_ROWS, d), F32),
                            pltpu.VMEM((2, FFN_ROWS * sub, LANES), F32),
                            pltpu.SemaphoreType.DMA((2,)), pltpu.SemaphoreType.DMA((2,)),
                            pltpu.SemaphoreType.DMA]),
        out_shape=jax.ShapeDtypeStruct((p * sub, LANES), F32),
        compiler_params=_cparams("arbitrary", "arbitrary"),
        name="swiglu_experts",
    )(block_e, n_used, slot_tok, slot_dst, t, w_gate, w_up, w_down)


def _combine_ln_kernel(x_ref, mod_ref, g_ref, b_ref, *rest, alpha):
    y_refs, p_ref, o_ref = rest[:-2], rest[-2], rest[-1]
    t = x_ref.shape[1]
    top_k = p_ref.shape[2]
    for bb in range(x_ref.shape[0]):
        f = None
        for kk in range(top_k):
            term = _load_row_tiles(y_refs[bb * top_k + kk], t) * p_ref[bb, :, kk:kk + 1]
            f = term if f is None else f + term
        m = mod_ref[bb, 0]
        o_ref[bb] = _layer_norm(alpha * x_ref[bb] + m[5:6] * f, g_ref[...], b_ref[...])


def _combine_ln(x1, mods, ln_g, ln_b, y, probs, alpha):
    b, rows, d = x1.shape
    nt = rows // ROW_TILE
    sub = d // LANES
    top_k = probs.shape[2]
    nbb = BATCH_BLOCK
    tok = lambda w: pl.BlockSpec((nbb, ROW_TILE, w), lambda bi, i: (bi, i, 0))
    const = lambda shp: pl.BlockSpec(shp, lambda bi, i: (0,) * len(shp))
    y_specs = [pl.BlockSpec((ROW_TILE * sub, LANES),
                            lambda bi, i, kk=kk, bb=bb: ((kk * b + bi * nbb + bb) * nt + i, 0))
               for bb in range(nbb) for kk in range(top_k)]
    return pl.pallas_call(
        functools.partial(_combine_ln_kernel, alpha=alpha),
        grid=(b // nbb, nt),
        in_specs=[tok(d), pl.BlockSpec((nbb, 1, SUBLANES, d), lambda bi, i: (bi, 1, 0, 0)),
                  const((1, d)), const((1, d))] + y_specs + [tok(top_k)],
        out_specs=tok(d),
        out_shape=jax.ShapeDtypeStruct((b, rows, d), F32),
        compiler_params=_cparams("parallel", "parallel"),
        name="combine_ln2",
    )(x1, mods, ln_g, ln_b, *([y] * (top_k * nbb)), probs)


def _moe(t, logits, w_gate, w_up, w_down, tf):
    n, n_e = logits.shape
    sub = t.shape[0] // n
    top_v, top_i = lax.top_k(logits, TOP_K)
    probs = jax.nn.softmax(top_v, axis=-1)
    m = n * TOP_K
    flat_e = top_i.reshape(m)
    onehot = (flat_e[:, None] == jnp.arange(n_e, dtype=flat_e.dtype)[None, :]).astype(jnp.int32)
    csum = jnp.cumsum(onehot, axis=0)
    rank = jnp.take_along_axis(csum, flat_e[:, None], axis=1)[:, 0] - 1
    counts = csum[-1]
    padded = (counts + FFN_ROWS - 1) // FFN_ROWS * FFN_ROWS
    ends_p = jnp.cumsum(padded)
    start_p = ends_p - padded
    slot = (start_p[flat_e] + rank).astype(jnp.int32)
    nb = -(-(m + n_e * (FFN_ROWS - 1)) // FFN_ROWS)
    p = (nb + 1) * FFN_ROWS
    pair = jnp.arange(m, dtype=jnp.int32)
    slot_pair = jnp.full((p,), -1, jnp.int32).at[slot].set(pair)
    is_pad = slot_pair < 0
    slot_tok = jnp.where(is_pad, 0, slot_pair // TOP_K) * sub
    slot_dst = jnp.where(is_pad, m + jnp.cumsum(is_pad.astype(jnp.int32)) - 1,
                         (slot_pair % TOP_K) * n + slot_pair // TOP_K) * sub
    block_e = jnp.minimum(jnp.searchsorted(ends_p, jnp.arange(nb + 1) * FFN_ROWS, side='right'),
                          n_e - 1).astype(jnp.int32)
    n_used = (ends_p[-1:] // FFN_ROWS).astype(jnp.int32)
    y = _swiglu_experts(t, block_e, n_used, slot_tok, slot_dst, w_gate, w_up, w_down, tf)
    return y, probs


def _block_diag(w):
    nb, bi, bj = w.shape
    eye = jnp.eye(nb, dtype=w.dtype)
    return (eye[:, None, :, None] * w[:, :, None, :]).reshape(nb * bi, nb * bj)


def _rope_tables(ctx_len, s):
    nf = HEAD_DIM // 4
    inv = ROPE_THETA ** (-jnp.arange(nf, dtype=F32) / nf)
    t = jnp.arange(s)
    row = (t // GRID_W).astype(F32)[:, None] * inv
    col = (t % GRID_W).astype(F32)[:, None] * inv
    ang = jnp.concatenate([row, row, col, col], axis=1)
    sign = jnp.tile(jnp.concatenate([-jnp.ones((nf,), F32), jnp.ones((nf,), F32)]), 2)
    cos = jnp.concatenate([jnp.ones((ctx_len, HEAD_DIM), F32), jnp.cos(ang)], axis=0)
    sin = jnp.concatenate([jnp.zeros((ctx_len, HEAD_DIM), F32), jnp.sin(ang) * sign], axis=0)
    reps = LANES // HEAD_DIM
    return jnp.tile(cos, (1, reps)), jnp.tile(sin, (1, reps))


def _pad_rows(a, rows):
    return jnp.zeros((rows,) + a.shape[1:], a.dtype).at[:a.shape[0]].set(a)


def kernel(x, c, ctx, c_ctx, w_mod, b_mod, w_in, b_merge, pool_w, pool_scale, q_norm, k_norm, conv_w, conv_b, lru_wa, lru_ba, lru_wx, lru_bx, lru_lambda, w_branch, w_out, ln1_g, ln1_b, ffn_w_gate, ffn_w_up, ffn_w_down, moe_router, moe_router_b, moe_w_gate, moe_w_up, moe_w_down, ln2_g, ln2_b):
    b, s, d = x.shape
    ctx_len = ctx.shape[1]
    depth = w_in.shape[0]
    l = ctx_len + s
    assert ctx_len % ROW_TILE == 0 and s % ROW_TILE == 0 and (b * s) % FFN_ROWS == 0 and b % BATCH_BLOCK == 0
    assert (b * l) % FFN_ROWS == 0 and ctx_len % Q_TILE == 0 and s % Q_TILE == 0
    nct = ctx_len // ROW_TILE
    alpha = (2 * depth) ** 0.25

    cc = _pad_rows(jnp.concatenate([c, c_ctx[None, :]], axis=0), -(-(b + 1) // SUBLANES) * SUBLANES)
    mod_all = _modulation(cc, w_mod, b_mod)

    cos_t, sin_t = _rope_tables(ctx_len, s)
    bd = _block_diag(jnp.full((LANES // HEAD_DIM, HEAD_DIM, HEAD_DIM), 1.0 / HEAD_DIM, F32)).astype(BF16)
    reps = LANES // HEAD_DIM

    xs = jnp.concatenate([ctx, x], axis=1)
    for li in range(depth):
        last = li == depth - 1
        ml = jnp.transpose(mod_all[li, :, :b], (1, 0, 2))
        mc = jnp.broadcast_to(mod_all[li, :, b][None], (b, 6, d))
        mods = jnp.stack([mc, ml], axis=1)
        mods = jnp.concatenate([mods, jnp.zeros((b, 2, SUBLANES - 6, d), F32)], axis=2)

        w_l = w_in[li].astype(BF16)
        p, q, k, v, lx, lg = _project(
            xs, mods, w_l[:, :OFF_GATE], bd,
            jnp.tile(q_norm[li], reps)[None], jnp.tile(k_norm[li], reps)[None], cos_t, sin_t, nct)
        attn = _attention(q, k, v, ctx_len, with_ctx=not last)

        cw = _pad_rows(0.5 * conv_w[li], SUBLANES)
        cb = 0.5 * conv_b[li][None]
        lru_w = [jnp.concatenate([_block_diag(lru_wa[li, dr]), _block_diag(lru_wx[li, dr])], axis=1).astype(BF16)
                 for dr in range(2)]
        hf = _lru(lx, lru_w[0], 0.5 * lru_ba[li, 0][None], 0.5 * lru_bx[li, 0][None], lru_lambda[li, 0][None],
                  cw, cb, nct, reverse=False)
        lru_y = _lru(lx, lru_w[1], 0.5 * lru_ba[li, 1][None], 0.5 * lru_bx[li, 1][None], lru_lambda[li, 1][None],
                     cw, cb, nct, reverse=True, hf=hf, lg=lg)

        jf = li // 2
        merge_args = (xs, mods, p, attn, lru_y, 0.5 * w_l[:, OFF_GATE:], _pad_rows(0.5 * b_merge[li], SUBLANES),
                      pool_w[li].astype(BF16), pool_scale[li][None], (0.5 * w_branch[li]).astype(BF16),
                      w_out[li].astype(BF16), ln1_g[li][None], ln1_b[li][None])
        if li % 2 == 0:
            assert not last, "the dense layer is expected to carry the context tokens along"
            x1, h2 = _merge(*merge_args, nct, lat_only=False, alpha=alpha)
            xs = _swiglu_dense_ln(h2, x1, mods, ln2_g[li][None], ln2_b[li][None], ffn_w_gate[jf].astype(BF16),
                                  ffn_w_up[jf].astype(BF16), ffn_w_down[jf].astype(BF16), nct=nct, alpha=alpha)
        else:
            assert last, "the expert layer is expected to be the last layer (latent tokens only)"
            n_e = moe_router.shape[2]
            rw = jnp.zeros((d, LANES), F32).at[:, :n_e].set(moe_router[jf])
            rb = jnp.zeros((1, LANES), F32).at[0, :n_e].set(moe_router_b[jf])
            rw_hi = rw.astype(BF16)
            rw_split = jnp.stack([rw_hi, (rw - rw_hi.astype(F32)).astype(BF16)])
            x1, h2, logits = _merge(*merge_args, nct, lat_only=True, alpha=alpha, router=(rw_split, rb))
            y, probs = _moe(h2.reshape(-1, LANES), logits.reshape(b * s, LANES)[:, :n_e],
                            moe_w_gate[jf].astype(BF16), moe_w_up[jf].astype(BF16), moe_w_down[jf].astype(BF16),
                            tf=moe_w_gate.shape[3] // 2)
            xs = _combine_ln(x1, mods, ln2_g[li][None], ln2_b[li][None], y, probs.reshape(b, s, -1), alpha)
    return xs
```

```python
import functools

import jax
import jax.numpy as jnp
from jax import lax
from jax.experimental import pallas as pl
from jax.experimental.pallas import tpu as pltpu

F32 = jnp.float32
BF16 = jnp.bfloat16

GRID_W = 64
POOL_WINDOWS = (2, 4, 8, 16)
BRANCH_WIDTH = 512
POOL_GROUP = BRANCH_WIDTH // len(POOL_WINDOWS)
N_Q_HEADS = 8
N_KV_HEADS = 2
HEAD_DIM = 64
GQA_GROUP = N_Q_HEADS // N_KV_HEADS
Q_WIDTH = N_Q_HEADS * HEAD_DIM
KV_WIDTH = N_KV_HEADS * HEAD_DIM
ROPE_THETA = 10000.0
ATTN_SCALE = HEAD_DIM ** -0.5
LOG2_E = 1.4426950408889634
LRU_WIDTH = BRANCH_WIDTH
LRU_BLOCKS = 8
CONV_WIDTH = 4
RG_C = 8.0
N_BRANCH = 3
N_EXPERTS = 8
TOP_K = 2
LN_EPS = 1e-5
RMS_EPS = 1e-6
OFF_GATE = BRANCH_WIDTH + Q_WIDTH + 2 * KV_WIDTH + 2 * LRU_WIDTH

LANES = 128
SUBLANES = 8
ROW_TILE = 256
BATCH_BLOCK = 2
Q_TILE = 256
ATTN_CHAIN = 256
FFN_ROWS = 512
HALO = SUBLANES
VMEM_LIMIT = 56 * 1024 * 1024


def _cparams(*sem):
    return pltpu.CompilerParams(dimension_semantics=sem, vmem_limit_bytes=VMEM_LIMIT)


def _sigmoid(x):
    return 0.5 * jnp.tanh(0.5 * x) + 0.5


def _store_row_tiles(ref, val):
    t, d = val.shape
    for j in range(d // LANES):
        ref[pl.ds(j, t, stride=d // LANES), :] = val[:, j * LANES:(j + 1) * LANES]


def _load_row_tiles(ref, t):
    n = ref.shape[0] // t
    return jnp.concatenate([ref[pl.ds(j, t, stride=n), :] for j in range(n)], axis=1)


def _silu(x):
    return x * _sigmoid(x)


def _layer_norm(v, g, b):
    mu = jnp.mean(v, axis=-1, keepdims=True)
    d = v - mu
    var = jnp.mean(d * d, axis=-1, keepdims=True)
    return d * lax.rsqrt(var + LN_EPS) * g + b


def _mod_kernel(c_ref, w_ref, b_ref, o_ref):
    s = _silu(c_ref[...])
    o_ref[0, 0] = jnp.dot(s, w_ref[0], precision=lax.Precision.HIGHEST,
                          preferred_element_type=F32) + b_ref[0, 0]


def _modulation(cc, w_mod, b_mod):
    depth, d, _ = w_mod.shape
    r = cc.shape[0]
    b6 = b_mod.reshape(depth, 6, 1, d)
    return pl.pallas_call(
        _mod_kernel,
        grid=(depth, 6),
        in_specs=[pl.BlockSpec((r, d), lambda l, j: (0, 0)),
                  pl.BlockSpec((1, d, d), lambda l, j: (l, 0, j)),
                  pl.BlockSpec((1, 1, 1, d), lambda l, j: (l, j, 0, 0))],
        out_specs=pl.BlockSpec((1, 1, r, d), lambda l, j: (l, j, 0, 0)),
        out_shape=jax.ShapeDtypeStruct((depth, 6, r, d), F32),
        compiler_params=_cparams("arbitrary", "arbitrary"),
        name="modulation",
    )(cc, w_mod, b6)


def _norm_rope(zc, bd, g, cos, sin):
    zz = zc * zc
    hi = zz.astype(BF16)
    lo = (zz - hi.astype(F32)).astype(BF16)
    ms = jnp.dot(hi, bd, preferred_element_type=F32) + jnp.dot(lo, bd, preferred_element_type=F32)
    y = zc * lax.rsqrt(ms + RMS_EPS) * g
    nf = HEAD_DIM // 4
    up = pltpu.roll(y, LANES - nf, 1)
    dn = pltpu.roll(y, nf, 1)
    lane = lax.broadcasted_iota(jnp.int32, y.shape, 1)
    partner = jnp.where((lane % (2 * nf)) < nf, up, dn)
    return y * cos + partner * sin


def _proj_kernel(x_ref, mod_ref, w_ref, bd_ref, qg_ref, kg_ref, cos_ref, sin_ref,
                 p_ref, q_ref, k_ref, v_ref, lx_ref, lg_ref):
    bd = bd_ref[...]
    cos = cos_ref[...]
    sin = sin_ref[...]
    for bb in range(x_ref.shape[0]):
        x = x_ref[bb]
        m = mod_ref[bb, 0]
        h = (x * (1.0 + m[1:2]) + m[0:1]).astype(BF16)
        z = jnp.dot(h, w_ref[...], preferred_element_type=F32)
        o = 0
        p_ref[bb] = z[:, o:o + BRANCH_WIDTH]
        o += BRANCH_WIDTH
        for c in range(Q_WIDTH // LANES):
            y = _norm_rope(z[:, o:o + LANES], bd, qg_ref[...], cos, sin) * (ATTN_SCALE * LOG2_E)
            q_ref[bb, 2 * c] = y[:, :HEAD_DIM].astype(BF16)
            q_ref[bb, 2 * c + 1] = y[:, HEAD_DIM:].astype(BF16)
            o += LANES
        for c in range(KV_WIDTH // LANES):
            y = _norm_rope(z[:, o:o + LANES], bd, kg_ref[...], cos, sin)
            k_ref[bb, 2 * c] = y[:, :HEAD_DIM].astype(BF16)
            k_ref[bb, 2 * c + 1] = y[:, HEAD_DIM:].astype(BF16)
            o += LANES
        for c in range(KV_WIDTH // LANES):
            y = z[:, o:o + LANES]
            ones = jnp.ones((y.shape[0], LANES - HEAD_DIM), F32)
            v_ref[bb, 2 * c] = jnp.concatenate([y[:, :HEAD_DIM], ones], axis=1).astype(BF16)
            v_ref[bb, 2 * c + 1] = jnp.concatenate([y[:, HEAD_DIM:], ones], axis=1).astype(BF16)
            o += LANES
        lx_ref[bb] = z[:, o:o + LRU_WIDTH]
        o += LRU_WIDTH
        lg_ref[bb] = z[:, o:o + LRU_WIDTH]


def _project(x, mods, w_a, bd, qg, kg, cos_t, sin_t, nct):
    b, l, d = x.shape
    nt = l // ROW_TILE
    wa = w_a.shape[1]
    bb = BATCH_BLOCK
    tok = lambda w: pl.BlockSpec((bb, ROW_TILE, w), lambda bi, i: (bi, i, 0))
    head = lambda n, w=HEAD_DIM: pl.BlockSpec((bb, n, ROW_TILE, w), lambda bi, i: (bi, 0, i, 0))
    const = lambda shp: pl.BlockSpec(shp, lambda bi, i: (0,) * len(shp))
    return pl.pallas_call(
        _proj_kernel,
        grid=(b // bb, nt),
        in_specs=[tok(d),
                  pl.BlockSpec((bb, 1, SUBLANES, d), lambda bi, i: (bi, (i >= nct).astype(jnp.int32), 0, 0)),
                  const((d, wa)), const((LANES, LANES)), const((1, LANES)), const((1, LANES)),
                  pl.BlockSpec((ROW_TILE, LANES), lambda bi, i: (i, 0)),
                  pl.BlockSpec((ROW_TILE, LANES), lambda bi, i: (i, 0))],
        out_specs=[tok(BRANCH_WIDTH), head(N_Q_HEADS), head(N_KV_HEADS), head(N_KV_HEADS, LANES),
                   tok(LRU_WIDTH), tok(LRU_WIDTH)],
        out_shape=[jax.ShapeDtypeStruct((b, l, BRANCH_WIDTH), F32),
                   jax.ShapeDtypeStruct((b, N_Q_HEADS, l, HEAD_DIM), BF16),
                   jax.ShapeDtypeStruct((b, N_KV_HEADS, l, HEAD_DIM), BF16),
                   jax.ShapeDtypeStruct((b, N_KV_HEADS, l, LANES), BF16),
                   jax.ShapeDtypeStruct((b, l, LRU_WIDTH), F32),
                   jax.ShapeDtypeStruct((b, l, LRU_WIDTH), F32)],
        compiler_params=_cparams("parallel", "parallel"),
        name="in_proj",
    )(x, mods, w_a, bd, qg, kg, cos_t, sin_t)


def _softmax_pv(q, k, v):
    hd = q.shape[1]
    s = lax.dot_general(q, k, (((1,), (1,)), ((), ())), preferred_element_type=F32)
    m = jnp.max(s, axis=1, keepdims=True)
    p = jnp.exp2(s - m).astype(BF16)
    r = jnp.dot(p, v, preferred_element_type=F32)
    return r[:, :hd] / r[:, hd:2 * hd]


def _attn_kernel(q_ref, k_ref, v_ref, o_ref, *, n_ctx_tiles, ctx_len):
    nh, tq, hd = q_ref.shape[1:]
    group = nh // k_ref.shape[1]

    def run(lk):
        for j in range(nh):
            for r0 in range(0, tq, ATTN_CHAIN):
                o = _softmax_pv(q_ref[0, j, r0:r0 + ATTN_CHAIN, :], k_ref[0, j // group, :lk],
                                v_ref[0, j // group, :lk])
                o_ref[0, r0:r0 + ATTN_CHAIN, j * hd:(j + 1) * hd] = o.astype(o_ref.dtype)

    if n_ctx_tiles:
        @pl.when(pl.program_id(1) < n_ctx_tiles)
        def _():
            run(ctx_len)

        @pl.when(pl.program_id(1) >= n_ctx_tiles)
        def _():
            run(k_ref.shape[2])
    else:
        run(k_ref.shape[2])


def _attention(q, k, v, ctx_len, with_ctx):
    b, nh, l, hd = q.shape
    nct = ctx_len // Q_TILE
    rows = l if with_ctx else l - ctx_len
    q_off = 0 if with_ctx else nct
    return pl.pallas_call(
        functools.partial(_attn_kernel, n_ctx_tiles=nct if with_ctx else 0, ctx_len=ctx_len),
        grid=(b, rows // Q_TILE),
        in_specs=[pl.BlockSpec((1, nh, Q_TILE, hd), lambda bi, i: (bi, 0, i + q_off, 0)),
                  pl.BlockSpec((1,) + k.shape[1:], lambda bi, i: (bi, 0, 0, 0)),
                  pl.BlockSpec((1,) + v.shape[1:], lambda bi, i: (bi, 0, 0, 0))],
        out_specs=pl.BlockSpec((1, Q_TILE, nh * hd), lambda bi, i: (bi, i, 0)),
        out_shape=jax.ShapeDtypeStruct((b, rows, nh * hd), BF16),
        compiler_params=_cparams("parallel", "parallel"),
        name="attention",
    )(q, k, v)


def _fill_ext(ext_ref, main, prev, nxt, first, last):
    t = main.shape[0]
    ext_ref[0:HALO] = jnp.where(first, 0.0, prev)
    ext_ref[HALO:HALO + t] = main
    ext_ref[HALO + t:2 * HALO + t] = jnp.where(last, 0.0, nxt)


def _seg_flags(tile, nct, nt):
    first = jnp.logical_or(tile == 0, tile == nct)
    last = jnp.logical_or(tile == nct - 1, tile == nt - 1)
    return first, last


def _halo_specs(width, tile_of, l):
    rb = ROW_TILE // HALO
    last_blk = l // HALO - 1
    bb = BATCH_BLOCK
    main = pl.BlockSpec((bb, ROW_TILE, width), lambda bi, i: (bi, tile_of(i), 0))
    prev = pl.BlockSpec((bb, HALO, width), lambda bi, i: (bi, jnp.maximum(tile_of(i) * rb - 1, 0), 0))
    nxt = pl.BlockSpec((bb, HALO, width), lambda bi, i: (bi, jnp.minimum((tile_of(i) + 1) * rb, last_blk), 0))
    return main, prev, nxt


def _softplus(z):
    return jnp.maximum(z, 0.0) + jnp.log1p(jnp.exp(-jnp.abs(z)))


def _gelu_tanh(x):
    return 0.5 * x * (1.0 + jnp.tanh(0.7978845608028654 * (x + 0.044715 * (x * x * x))))


def _lru_kernel(*refs, reverse, nct, nt):
    if reverse:
        (lx_ref, lxp_ref, lxn_ref, w_ref, ba_ref, bx_ref, lam_ref, cw_ref, cb_ref, hf_ref, lg_ref,
         o_ref, ext_ref, a_ref, u_ref, h_ref) = refs
    else:
        (lx_ref, lxp_ref, lxn_ref, w_ref, ba_ref, bx_ref, lam_ref, cw_ref, cb_ref,
         o_ref, ext_ref, a_ref, u_ref, h_ref) = refs
    j = pl.program_id(1)
    tile = _lru_tile(j, reverse, nct, nt)
    t = lx_ref.shape[1]
    nbb = lx_ref.shape[0]

    @pl.when(j == 0)
    def _():
        h_ref[...] = jnp.zeros_like(h_ref)

    first, last = _seg_flags(tile, nct, nt)
    lo = CONV_WIDTH // 2
    half_log2_a = (-0.5 * RG_C * LOG2_E) * _softplus(-lam_ref[...])
    for bb in range(nbb):
        ext = ext_ref.at[bb]
        _fill_ext(ext, lx_ref[bb], lxp_ref[bb], lxn_ref[bb], first, last)
        xh = cb_ref[...]
        for kk in range(CONV_WIDTH):
            xh = xh + ext[pl.ds(HALO + kk - lo, t), :] * cw_ref[kk:kk + 1, :]
        zz = jnp.dot(xh.astype(BF16), w_ref[...], preferred_element_type=F32)
        tr = jnp.tanh(zz[:, :LRU_WIDTH] + ba_ref[...])
        ti = jnp.tanh(zz[:, LRU_WIDTH:] + bx_ref[...])
        a = jnp.exp2(half_log2_a * (tr + 1.0))
        a_ref[bb] = a
        u_ref[bb] = jnp.sqrt(1.0 - a * a) * (ti + 1.0) * xh

    row = lax.broadcasted_iota(jnp.int32, (SUBLANES, LRU_WIDTH), 0)
    n_sub = t // SUBLANES

    def sub(s, hs):
        blk = (n_sub - 1 - s) if reverse else s
        r0 = pl.multiple_of(blk * SUBLANES, SUBLANES)
        out = []
        for bb in range(nbb):
            aa = a_ref[bb, pl.ds(r0, SUBLANES), :]
            uu = u_ref[bb, pl.ds(r0, SUBLANES), :]
            for dd in (1, 2, 4):
                if reverse:
                    keep = row < SUBLANES - dd
                    sh = SUBLANES - dd
                else:
                    keep = row >= dd
                    sh = dd
                a_sh = jnp.where(keep, pltpu.roll(aa, sh, 0), 1.0)
                u_sh = jnp.where(keep, pltpu.roll(uu, sh, 0), 0.0)
                uu = aa * u_sh + uu
                aa = aa * a_sh
            hh = aa * hs[bb] + uu
            if reverse:
                o_ref[bb, pl.ds(r0, SUBLANES), :] = (
                    (hf_ref[bb, pl.ds(r0, SUBLANES), :] + hh)
                    * _gelu_tanh(lg_ref[bb, pl.ds(r0, SUBLANES), :])).astype(o_ref.dtype)
                edge = hh[0:1]
            else:
                o_ref[bb, pl.ds(r0, SUBLANES), :] = hh
                edge = hh[SUBLANES - 1:SUBLANES]
            out.append(jnp.broadcast_to(edge, (SUBLANES, LRU_WIDTH)))
        return tuple(out)

    hs = lax.fori_loop(0, n_sub, sub, tuple(h_ref[bb] for bb in range(nbb)), unroll=4)
    for bb in range(nbb):
        h_ref[bb] = hs[bb]


def _lru_tile(j, reverse, nct, nt):
    if not reverse:
        return j
    return jnp.where(j < nct, nct - 1 - j, nt - 1 - (j - nct))


def _lru(lx, w, ba, bx, lam, cw, cb, nct, reverse, hf=None, lg=None):
    b, l, wd = lx.shape
    nt = l // ROW_TILE
    tile_of = lambda i: _lru_tile(i, reverse, nct, nt)
    main, prev, nxt = _halo_specs(wd, tile_of, l)
    const = lambda shp: pl.BlockSpec(shp, lambda bi, i: (0,) * len(shp))
    in_specs = [main, prev, nxt, const(w.shape), const((1, wd)), const((1, wd)), const((1, wd)),
                const((SUBLANES, wd)), const((1, wd))]
    args = [lx, lx, lx, w, ba, bx, lam, cw, cb]
    if reverse:
        in_specs += [main, main]
        args += [hf, lg]
    bb = BATCH_BLOCK
    return pl.pallas_call(
        functools.partial(_lru_kernel, reverse=reverse, nct=nct, nt=nt),
        grid=(b // bb, nt),
        in_specs=in_specs,
        out_specs=main,
        out_shape=jax.ShapeDtypeStruct((b, l, wd), BF16 if reverse else F32),
        scratch_shapes=[pltpu.VMEM((bb, ROW_TILE + 2 * HALO, wd), F32), pltpu.VMEM((bb, ROW_TILE, wd), F32),
                        pltpu.VMEM((bb, ROW_TILE, wd), F32), pltpu.VMEM((bb, SUBLANES, wd), F32)],
        compiler_params=_cparams("parallel", "arbitrary"),
        name="lru_bwd" if reverse else "lru_fwd",
    )(*args)


def _merge_kernel(x_ref, mod_ref, p_ref, pp_ref, pn_ref, at_ref, lr_ref, wg_ref, bm_ref, pw_ref, ps_ref,
                  wb_ref, wo_ref, g_ref, b_ref, *rest, nct, nt, tile_off, ctx_len, alpha, route):
    if route:
        rw_ref, rb_ref, x1_ref, h2_ref, logit_ref, ext_ref = rest
    else:
        x1_ref, h2_ref, ext_ref = rest
    tile = pl.program_id(1) + tile_off
    t = x_ref.shape[1]
    d = x_ref.shape[2]
    first, last = _seg_flags(tile, nct, nt)
    in_ctx = tile < nct
    seg_len = jnp.where(in_ctx, ctx_len, nt * t - ctx_len)
    pos = (tile - jnp.where(in_ctx, 0, nct)) * t + lax.broadcasted_iota(jnp.int32, (t, 1), 0)

    for bb in range(x_ref.shape[0]):
        x = x_ref[bb]
        m = mod_ref[bb, 0]
        h = (x * (1.0 + m[1:2]) + m[0:1]).astype(BF16)

        def branch(n, ys):
            tg = jnp.tanh(jnp.dot(h, wg_ref[:, n * d:(n + 1) * d], preferred_element_type=F32) + bm_ref[n:n + 1, :])
            return (tg + 1.0) * jnp.dot(ys, wb_ref[n], preferred_element_type=F32)

        mix = branch(1, at_ref[bb]) + branch(2, lr_ref[bb])

        ext = ext_ref.at[bb]
        _fill_ext(ext, p_ref[bb], pp_ref[bb], pn_ref[bb], first, last)
        pooled = []
        for gi, w in enumerate(POOL_WINDOWS):
            lo = w // 2
            cols = slice(gi * POOL_GROUP, (gi + 1) * POOL_GROUP)
            acc = ext[pl.ds(HALO - lo, t), cols]
            for kk in range(1, w):
                acc = acc + ext[pl.ds(HALO - lo + kk, t), cols]
            cnt = jnp.clip(pos - lo + w, 0, seg_len) - jnp.clip(pos - lo, 0, seg_len)
            mean = acc / cnt.astype(F32)
            dlt = (mean - ext[pl.ds(HALO, t), cols]).astype(BF16)
            pooled.append(jnp.dot(dlt, pw_ref[gi], preferred_element_type=F32))
        pool_y = (jnp.concatenate(pooled, axis=1) * ps_ref[...]).astype(BF16)
        mix = branch(0, pool_y) + mix
        y = jnp.dot(mix.astype(BF16), wo_ref[...], preferred_element_type=F32)
        x1 = _layer_norm(alpha * x + m[2:3] * y, g_ref[...], b_ref[...])
        x1_ref[bb] = x1
        h2 = x1 * (1.0 + m[4:5]) + m[3:4]
        if route:
            hi = h2.astype(BF16)
            mid = (h2 - hi.astype(F32)).astype(BF16)
            logit_ref[bb] = (jnp.dot(hi, rw_ref[0], preferred_element_type=F32)
                             + jnp.dot(mid, rw_ref[0], preferred_element_type=F32)
                             + jnp.dot(hi, rw_ref[1], preferred_element_type=F32) + rb_ref[...])
            _store_row_tiles(h2_ref.at[bb], h2)
        else:
            h2_ref[bb] = h2.astype(h2_ref.dtype)


def _merge(x, mods, p, attn, lru, w_gate, b_merge, pool_w, pool_scale, w_branch, w_out, ln_g, ln_b,
           nct, lat_only, alpha, router=None):
    b, l, d = x.shape
    nt = l // ROW_TILE
    off = nct if lat_only else 0
    rows = l - off * ROW_TILE
    tile_of = lambda i: i + off
    bb = BATCH_BLOCK
    tok_l = lambda w: pl.BlockSpec((bb, ROW_TILE, w), lambda bi, i: (bi, i + off, 0))
    tok_o = lambda w: pl.BlockSpec((bb, ROW_TILE, w), lambda bi, i: (bi, i, 0))
    const = lambda shp: pl.BlockSpec(shp, lambda bi, i: (0,) * len(shp))
    pm, pp, pn = _halo_specs(BRANCH_WIDTH, tile_of, l)
    in_specs = [tok_l(d),
                pl.BlockSpec((bb, 1, SUBLANES, d), lambda bi, i: (bi, (i + off >= nct).astype(jnp.int32), 0, 0)),
                pm, pp, pn,
                tok_o(Q_WIDTH) if lat_only else tok_l(Q_WIDTH),
                tok_l(LRU_WIDTH),
                const(w_gate.shape), const(b_merge.shape), const(pool_w.shape), const(pool_scale.shape),
                const(w_branch.shape), const(w_out.shape), const((1, d)), const((1, d))]
    args = [x, mods, p, p, p, attn, lru, w_gate, b_merge, pool_w, pool_scale, w_branch, w_out, ln_g, ln_b]
    out_specs = [tok_o(d)]
    out_shape = [jax.ShapeDtypeStruct((b, rows, d), F32)]
    if router is None:
        out_specs.append(tok_o(d))
        out_shape.append(jax.ShapeDtypeStruct((b, rows, d), BF16))
    else:
        n_sub = d // LANES
        in_specs += [const(router[0].shape), const(router[1].shape)]
        args += list(router)
        out_specs += [pl.BlockSpec((bb, ROW_TILE * n_sub, LANES), lambda bi, i: (bi, i, 0)), tok_o(LANES)]
        out_shape += [jax.ShapeDtypeStruct((b, rows * n_sub, LANES), F32),
                      jax.ShapeDtypeStruct((b, rows, LANES), F32)]
    return pl.pallas_call(
        functools.partial(_merge_kernel, nct=nct, nt=nt, tile_off=off, ctx_len=nct * ROW_TILE, alpha=alpha,
                          route=router is not None),
        grid=(b // bb, rows // ROW_TILE),
        in_specs=in_specs,
        out_specs=out_specs,
        out_shape=out_shape,
        scratch_shapes=[pltpu.VMEM((bb, ROW_TILE + 2 * HALO, BRANCH_WIDTH), F32)],
        compiler_params=_cparams("parallel", "parallel"),
        name="merge_ln1",
    )(*args)


def _swiglu_kernel(x_ref, wg_ref, wu_ref, wd_ref, x1_ref, *rest, alpha):
    mod_refs, (g_ref, b_ref, o_ref) = rest[:-3], rest[-3:]
    x = x_ref[...]
    g = jnp.dot(x, wg_ref[...], preferred_element_type=F32)
    u = jnp.dot(x, wu_ref[...], preferred_element_type=F32)
    ff = jnp.dot((_silu(g) * u).astype(BF16), wd_ref[...], preferred_element_type=F32)
    for hh, mod_ref in enumerate(mod_refs):
        rs = slice(hh * ROW_TILE, (hh + 1) * ROW_TILE)
        o_ref[rs, :] = _layer_norm(alpha * x1_ref[rs, :] + mod_ref[0, 0, 5:6, :] * ff[rs], g_ref[...], b_ref[...])


def _swiglu_dense_ln(h2, x1, mods, ln_g, ln_b, w_gate, w_up, w_down, nct, alpha):
    b, l, d = x1.shape
    r = b * l
    ntl = l // ROW_TILE
    halves = FFN_ROWS // ROW_TILE

    def mod_spec(hh):
        def idx(i):
            tile = i * halves + hh
            return (tile // ntl, (tile % ntl >= nct).astype(jnp.int32), 0, 0)
        return pl.BlockSpec((1, 1, SUBLANES, d), idx)

    resident = lambda shp: pl.BlockSpec(shp, lambda i: (0, 0), pipeline_mode=pl.Buffered(1))
    out = pl.pallas_call(
        functools.partial(_swiglu_kernel, alpha=alpha),
        grid=(r // FFN_ROWS,),
        in_specs=[pl.BlockSpec((FFN_ROWS, d), lambda i: (i, 0)),
                  resident(w_gate.shape), resident(w_up.shape), resident(w_down.shape),
                  pl.BlockSpec((FFN_ROWS, d), lambda i: (i, 0))]
                 + [mod_spec(hh) for hh in range(halves)]
                 + [pl.BlockSpec((1, d), lambda i: (0, 0)), pl.BlockSpec((1, d), lambda i: (0, 0))],
        out_specs=pl.BlockSpec((FFN_ROWS, d), lambda i: (i, 0)),
        out_shape=jax.ShapeDtypeStruct((r, d), F32),
        compiler_params=_cparams("parallel"),
        name="swiglu_dense_ln2",
    )(h2.reshape(r, d), w_gate, w_up, w_down, x1.reshape(r, d), *([mods] * halves), ln_g, ln_b)
    return out.reshape(b, l, d)


def _moe_kernel(be_ref, nu_ref, tok_ref, dst_ref, t_hbm, wg_ref, wu_ref, wd_ref, y_hbm,
                xbuf, acc_ref, obuf, sem_in, sem_out, sem_fill):
    i = pl.program_id(0)
    f = pl.program_id(1)
    nb = pl.num_programs(0)
    nu = nu_ref[0]
    rows = acc_ref.shape[0]
    sub = xbuf.shape[1] // rows
    slot = i % 2

    def tile_rows(start):
        return pl.ds(pl.multiple_of(start, sub), sub)

    def gather_copy(blk, sl, r):
        return pltpu.make_async_copy(t_hbm.at[tile_rows(tok_ref[blk * rows + r])], xbuf.at[sl, tile_rows(r * sub)],
                                     sem_in.at[sl])

    def scatter_copy(blk, sl, r):
        return pltpu.make_async_copy(obuf.at[sl, tile_rows(r * sub)], y_hbm.at[tile_rows(dst_ref[blk * rows + r])],
                                     sem_out.at[sl])

    def gathers_done(sl):
        pltpu.make_async_copy(t_hbm.at[pl.ds(0, rows * sub)], xbuf.at[sl], sem_in.at[sl]).wait()

    def scatters_done(sl):
        pltpu.make_async_copy(obuf.at[sl], y_hbm.at[pl.ds(0, rows * sub)], sem_out.at[sl]).wait()

    def for_rows(fn):
        def body(r, c):
            fn(r)
            return c
        lax.fori_loop(0, rows, body, 0, unroll=8)

    def load_x():
        return _load_row_tiles(xbuf.at[slot], rows).astype(BF16)

    def swiglu_part(x):
        g = jnp.dot(x, wg_ref[0], preferred_element_type=F32)
        u = jnp.dot(x, wu_ref[0], preferred_element_type=F32)
        return jnp.dot((_silu(g) * u).astype(BF16), wd_ref[0], preferred_element_type=F32)

    @pl.when(jnp.logical_and(i == 0, f == 0))
    def _():
        obuf[1] = jnp.zeros(obuf.shape[1:], obuf.dtype)
        for_rows(lambda r: gather_copy(0, 0, r).start())

    @pl.when(jnp.logical_and(f == 0, i <= nu))
    def _():
        gathers_done(slot)

    @pl.when(jnp.logical_and(i < nu, f == 0))
    def _():
        x = load_x()
        for r in range(rows):
            gather_copy(i + 1, 1 - slot, r).start()
        acc_ref[...] = swiglu_part(x)

    @pl.when(jnp.logical_and(i < nu, f == 1))
    def _():
        @pl.when(i >= 1)
        def _():
            scatters_done(slot)

        x = load_x()
        prev = jnp.where(i >= 1, i - 1, nb)
        for r in range(rows):
            scatter_copy(prev, 1 - slot, r).start()
        _store_row_tiles(obuf.at[slot], acc_ref[...] + swiglu_part(x))

        @pl.when(i == nu - 1)
        def _():
            scatters_done(1 - slot)
            for_rows(lambda r: scatter_copy(i, slot, r).start())
            scatters_done(slot)

    @pl.when(jnp.logical_and(i >= nu, f == 1))
    def _():
        xbuf[0] = jnp.zeros(xbuf.shape[1:], xbuf.dtype)
        fill = pltpu.make_async_copy(xbuf.at[0], y_hbm.at[pl.ds(i * rows * sub, rows * sub)], sem_fill)
        fill.start()
        fill.wait()

    @pl.when(jnp.logical_and(jnp.logical_and(i == nb - 1, f == 1), nu == nb))
    def _():
        gathers_done(nb % 2)


def _swiglu_experts(t, block_e, n_used, slot_tok, slot_dst, w_gate, w_up, w_down, tf):
    d = w_gate.shape[1]
    sub = d // LANES
    p = slot_tok.shape[0]
    nf = w_gate.shape[2] // tf
    assert nf == 2

    def fcol(i, f, nu):
        return jnp.where(i < nu[0], f, nf - 1)

    return pl.pallas_call(
        _moe_kernel,
        grid_spec=pltpu.PrefetchScalarGridSpec(
            num_scalar_prefetch=4,
            grid=(p // FFN_ROWS - 1, nf),
            in_specs=[pl.BlockSpec(memory_space=pl.ANY),
                      pl.BlockSpec((1, d, tf), lambda i, f, be, nu, st, sd: (be[i], 0, fcol(i, f, nu))),
                      pl.BlockSpec((1, d, tf), lambda i, f, be, nu, st, sd: (be[i], 0, fcol(i, f, nu))),
                      pl.BlockSpec((1, tf, d), lambda i, f, be, nu, st, sd: (be[i], fcol(i, f, nu), 0))],
            out_specs=pl.BlockSpec(memory_space=pl.ANY),
            scratch_shapes=[pltpu.VMEM((2, FFN_ROWS * sub, LANES), F32), pltpu.VMEM((FFN_ROWS, d), F32),
                            pltpu.VMEM((2, FFN_ROWS * sub, LANES), F32),
                            pltpu.SemaphoreType.DMA((2,)), pltpu.SemaphoreType.DMA((2,)),
                            pltpu.SemaphoreType.DMA]),
        out_shape=jax.ShapeDtypeStruct((p * sub, LANES), F32),
        compiler_params=_cparams("arbitrary", "arbitrary"),
        name="swiglu_experts",
    )(block_e, n_used, slot_tok, slot_dst, t, w_gate, w_up, w_down)


def _combine_ln_kernel(x_ref, mod_ref, g_ref, b_ref, *rest, alpha):
    y_refs, p_ref, o_ref = rest[:-2], rest[-2], rest[-1]
    t = x_ref.shape[1]
    top_k = p_ref.shape[2]
    for bb in range(x_ref.shape[0]):
        f = None
        for kk in range(top_k):
            term = _load_row_tiles(y_refs[bb * top_k + kk], t) * p_ref[bb, :, kk:kk + 1]
            f = term if f is None else f + term
        m = mod_ref[bb, 0]
        o_ref[bb] = _layer_norm(alpha * x_ref[bb] + m[5:6] * f, g_ref[...], b_ref[...])


def _combine_ln(x1, mods, ln_g, ln_b, y, probs, alpha):
    b, rows, d = x1.shape
    nt = rows // ROW_TILE
    sub = d // LANES
    top_k = probs.shape[2]
    nbb = BATCH_BLOCK
    tok = lambda w: pl.BlockSpec((nbb, ROW_TILE, w), lambda bi, i: (bi, i, 0))
    const = lambda shp: pl.BlockSpec(shp, lambda bi, i: (0,) * len(shp))
    y_specs = [pl.BlockSpec((ROW_TILE * sub, LANES),
                            lambda bi, i, kk=kk, bb=bb: ((kk * b + bi * nbb + bb) * nt + i, 0))
               for bb in range(nbb) for kk in range(top_k)]
    return pl.pallas_call(
        functools.partial(_combine_ln_kernel, alpha=alpha),
        grid=(b // nbb, nt),
        in_specs=[tok(d), pl.BlockSpec((nbb, 1, SUBLANES, d), lambda bi, i: (bi, 1, 0, 0)),
                  const((1, d)), const((1, d))] + y_specs + [tok(top_k)],
        out_specs=tok(d),
        out_shape=jax.ShapeDtypeStruct((b, rows, d), F32),
        compiler_params=_cparams("parallel", "parallel"),
        name="combine_ln2",
    )(x1, mods, ln_g, ln_b, *([y] * (top_k * nbb)), probs)


def _moe(t, logits, w_gate, w_up, w_down, tf):
    n, n_e = logits.shape
    sub = t.shape[0] // n
    top_v, top_i = lax.top_k(logits, TOP_K)
    probs = jax.nn.softmax(top_v, axis=-1)
    m = n * TOP_K
    flat_e = top_i.reshape(m)
    counts = jnp.sum((flat_e[:, None] == jnp.arange(n_e, dtype=flat_e.dtype)[None, :]).astype(jnp.int32), axis=0)
    order = jnp.argsort(flat_e, stable=True).astype(jnp.int32)
    start = jnp.cumsum(counts) - counts
    padded = (counts + FFN_ROWS - 1) // FFN_ROWS * FFN_ROWS
    ends_p = jnp.cumsum(padded)
    start_p = ends_p - padded
    nb = -(-(m + n_e * (FFN_ROWS - 1)) // FFN_ROWS)
    p = (nb + 1) * FFN_ROWS
    block_e = jnp.minimum(jnp.searchsorted(ends_p, jnp.arange(nb + 1) * FFN_ROWS, side='right'),
                          n_e - 1).astype(jnp.int32)
    n_used = (ends_p[-1:] // FFN_ROWS).astype(jnp.int32)
    slot_e = jnp.repeat(block_e, FFN_ROWS)
    r = jnp.arange(p, dtype=jnp.int32) - start_p[slot_e]
    is_pad = r >= counts[slot_e]
    slot_pair = order[jnp.clip(start[slot_e] + r, 0, m - 1)]
    slot_tok = jnp.where(is_pad, 0, slot_pair // TOP_K) * sub
    slot_dst = jnp.where(is_pad, m + jnp.cumsum(is_pad.astype(jnp.int32)) - 1,
                         (slot_pair % TOP_K) * n + slot_pair // TOP_K) * sub
    y = _swiglu_experts(t, block_e, n_used, slot_tok, slot_dst, w_gate, w_up, w_down, tf)
    return y, probs


def _block_diag(w):
    nb, bi, bj = w.shape
    eye = jnp.eye(nb, dtype=w.dtype)
    return (eye[:, None, :, None] * w[:, :, None, :]).reshape(nb * bi, nb * bj)


def _rope_tables(ctx_len, s):
    nf = HEAD_DIM // 4
    inv = ROPE_THETA ** (-jnp.arange(nf, dtype=F32) / nf)
    t = jnp.arange(s)
    row = (t // GRID_W).astype(F32)[:, None] * inv
    col = (t % GRID_W).astype(F32)[:, None] * inv
    ang = jnp.concatenate([row, row, col, col], axis=1)
    sign = jnp.tile(jnp.concatenate([-jnp.ones((nf,), F32), jnp.ones((nf,), F32)]), 2)
    cos = jnp.concatenate([jnp.ones((ctx_len, HEAD_DIM), F32), jnp.cos(ang)], axis=0)
    sin = jnp.concatenate([jnp.zeros((ctx_len, HEAD_DIM), F32), jnp.sin(ang) * sign], axis=0)
    reps = LANES // HEAD_DIM
    return jnp.tile(cos, (1, reps)), jnp.tile(sin, (1, reps))


def _pad_rows(a, rows):
    return jnp.zeros((rows,) + a.shape[1:], a.dtype).at[:a.shape[0]].set(a)


def kernel(x, c, ctx, c_ctx, w_mod, b_mod, w_in, b_merge, pool_w, pool_scale, q_norm, k_norm, conv_w, conv_b, lru_wa, lru_ba, lru_wx, lru_bx, lru_lambda, w_branch, w_out, ln1_g, ln1_b, ffn_w_gate, ffn_w_up, ffn_w_down, moe_router, moe_router_b, moe_w_gate, moe_w_up, moe_w_down, ln2_g, ln2_b):
    b, s, d = x.shape
    ctx_len = ctx.shape[1]
    depth = w_in.shape[0]
    l = ctx_len + s
    assert ctx_len % ROW_TILE == 0 and s % ROW_TILE == 0 and (b * s) % FFN_ROWS == 0 and b % BATCH_BLOCK == 0
    assert (b * l) % FFN_ROWS == 0 and ctx_len % Q_TILE == 0 and s % Q_TILE == 0
    nct = ctx_len // ROW_TILE
    alpha = (2 * depth) ** 0.25

    cc = _pad_rows(jnp.concatenate([c, c_ctx[None, :]], axis=0), -(-(b + 1) // SUBLANES) * SUBLANES)
    mod_all = _modulation(cc, w_mod, b_mod)

    cos_t, sin_t = _rope_tables(ctx_len, s)
    bd = _block_diag(jnp.full((LANES // HEAD_DIM, HEAD_DIM, HEAD_DIM), 1.0 / HEAD_DIM, F32)).astype(BF16)
    reps = LANES // HEAD_DIM

    xs = jnp.concatenate([ctx, x], axis=1)
    for li in range(depth):
        last = li == depth - 1
        ml = jnp.transpose(mod_all[li, :, :b], (1, 0, 2))
        mc = jnp.broadcast_to(mod_all[li, :, b][None], (b, 6, d))
        mods = jnp.stack([mc, ml], axis=1)
        mods = jnp.concatenate([mods, jnp.zeros((b, 2, SUBLANES - 6, d), F32)], axis=2)

        w_l = w_in[li].astype(BF16)
        p, q, k, v, lx, lg = _project(
            xs, mods, w_l[:, :OFF_GATE], bd,
            jnp.tile(q_norm[li], reps)[None], jnp.tile(k_norm[li], reps)[None], cos_t, sin_t, nct)
        attn = _attention(q, k, v, ctx_len, with_ctx=not last)

        cw = _pad_rows(0.5 * conv_w[li], SUBLANES)
        cb = 0.5 * conv_b[li][None]
        lru_w = [jnp.concatenate([_block_diag(lru_wa[li, dr]), _block_diag(lru_wx[li, dr])], axis=1).astype(BF16)
                 for dr in range(2)]
        hf = _lru(lx, lru_w[0], 0.5 * lru_ba[li, 0][None], 0.5 * lru_bx[li, 0][None], lru_lambda[li, 0][None],
                  cw, cb, nct, reverse=False)
        lru_y = _lru(lx, lru_w[1], 0.5 * lru_ba[li, 1][None], 0.5 * lru_bx[li, 1][None], lru_lambda[li, 1][None],
                     cw, cb, nct, reverse=True, hf=hf, lg=lg)

        jf = li // 2
        merge_args = (xs, mods, p, attn, lru_y, 0.5 * w_l[:, OFF_GATE:], _pad_rows(0.5 * b_merge[li], SUBLANES),
                      pool_w[li].astype(BF16), pool_scale[li][None], (0.5 * w_branch[li]).astype(BF16),
                      w_out[li].astype(BF16), ln1_g[li][None], ln1_b[li][None])
        if li % 2 == 0:
            assert not last, "the dense layer is expected to carry the context tokens along"
            x1, h2 = _merge(*merge_args, nct, lat_only=False, alpha=alpha)
            xs = _swiglu_dense_ln(h2, x1, mods, ln2_g[li][None], ln2_b[li][None], ffn_w_gate[jf].astype(BF16),
                                  ffn_w_up[jf].astype(BF16), ffn_w_down[jf].astype(BF16), nct=nct, alpha=alpha)
        else:
            assert last, "the expert layer is expected to be the last layer (latent tokens only)"
            n_e = moe_router.shape[2]
            rw = jnp.zeros((d, LANES), F32).at[:, :n_e].set(moe_router[jf])
            rb = jnp.zeros((1, LANES), F32).at[0, :n_e].set(moe_router_b[jf])
            rw_hi = rw.astype(BF16)
            rw_split = jnp.stack([rw_hi, (rw - rw_hi.astype(F32)).astype(BF16)])
            x1, h2, logits = _merge(*merge_args, nct, lat_only=True, alpha=alpha, router=(rw_split, rb))
            y, probs = _moe(h2.reshape(-1, LANES), logits.reshape(b * s, LANES)[:, :n_e],
                            moe_w_gate[jf].astype(BF16), moe_w_up[jf].astype(BF16), moe_w_down[jf].astype(BF16),
                            tf=moe_w_gate.shape[3] // 2)
            xs = _combine_ln(x1, mods, ln2_g[li][None], ln2_b[li][None], y, probs.reshape(b, s, -1), alpha)
    return xs
```

```python
import functools

import jax
import jax.numpy as jnp
from jax import lax
from jax.experimental import pallas as pl
from jax.experimental.pallas import tpu as pltpu

F32 = jnp.float32
BF16 = jnp.bfloat16

GRID_W = 64
POOL_WINDOWS = (2, 4, 8, 16)
BRANCH_WIDTH = 512
POOL_GROUP = BRANCH_WIDTH // len(POOL_WINDOWS)
N_Q_HEADS = 8
N_KV_HEADS = 2
HEAD_DIM = 64
GQA_GROUP = N_Q_HEADS // N_KV_HEADS
Q_WIDTH = N_Q_HEADS * HEAD_DIM
KV_WIDTH = N_KV_HEADS * HEAD_DIM
ROPE_THETA = 10000.0
ATTN_SCALE = HEAD_DIM ** -0.5
LOG2_E = 1.4426950408889634
LRU_WIDTH = BRANCH_WIDTH
LRU_BLOCKS = 8
CONV_WIDTH = 4
RG_C = 8.0
N_BRANCH = 3
N_EXPERTS = 8
TOP_K = 2
LN_EPS = 1e-5
RMS_EPS = 1e-6
OFF_GATE = BRANCH_WIDTH + Q_WIDTH + 2 * KV_WIDTH + 2 * LRU_WIDTH

LANES = 128
SUBLANES = 8
ROW_TILE = 256
BATCH_BLOCK = 2
Q_TILE = 256
ATTN_CHAIN = 256
FFN_ROWS = 512
HALO = SUBLANES
VMEM_LIMIT = 56 * 1024 * 1024


def _cparams(*sem):
    return pltpu.CompilerParams(dimension_semantics=sem, vmem_limit_bytes=VMEM_LIMIT)


def _sigmoid(x):
    return 0.5 * jnp.tanh(0.5 * x) + 0.5


def _store_row_tiles(ref, val):
    t, d = val.shape
    for j in range(d // LANES):
        ref[pl.ds(j, t, stride=d // LANES), :] = val[:, j * LANES:(j + 1) * LANES]


def _load_row_tiles(ref, t):
    n = ref.shape[0] // t
    return jnp.concatenate([ref[pl.ds(j, t, stride=n), :] for j in range(n)], axis=1)


def _silu(x):
    return x * _sigmoid(x)


def _layer_norm(v, g, b):
    mu = jnp.mean(v, axis=-1, keepdims=True)
    d = v - mu
    var = jnp.mean(d * d, axis=-1, keepdims=True)
    return d * lax.rsqrt(var + LN_EPS) * g + b


def _mod_kernel(c_ref, w_ref, b_ref, o_ref):
    s = _silu(c_ref[...])
    o_ref[0, 0] = jnp.dot(s, w_ref[0], precision=lax.Precision.HIGHEST,
                          preferred_element_type=F32) + b_ref[0, 0]


def _modulation(cc, w_mod, b_mod):
    depth, d, _ = w_mod.shape
    r = cc.shape[0]
    b6 = b_mod.reshape(depth, 6, 1, d)
    return pl.pallas_call(
        _mod_kernel,
        grid=(depth, 6),
        in_specs=[pl.BlockSpec((r, d), lambda l, j: (0, 0)),
                  pl.BlockSpec((1, d, d), lambda l, j: (l, 0, j)),
                  pl.BlockSpec((1, 1, 1, d), lambda l, j: (l, j, 0, 0))],
        out_specs=pl.BlockSpec((1, 1, r, d), lambda l, j: (l, j, 0, 0)),
        out_shape=jax.ShapeDtypeStruct((depth, 6, r, d), F32),
        compiler_params=_cparams("arbitrary", "arbitrary"),
        name="modulation",
    )(cc, w_mod, b6)


def _norm_rope(zc, bd, g, cos, sin):
    zz = zc * zc
    hi = zz.astype(BF16)
    lo = (zz - hi.astype(F32)).astype(BF16)
    ms = jnp.dot(hi, bd, preferred_element_type=F32) + jnp.dot(lo, bd, preferred_element_type=F32)
    y = zc * lax.rsqrt(ms + RMS_EPS) * g
    nf = HEAD_DIM // 4
    up = pltpu.roll(y, LANES - nf, 1)
    dn = pltpu.roll(y, nf, 1)
    lane = lax.broadcasted_iota(jnp.int32, y.shape, 1)
    partner = jnp.where((lane % (2 * nf)) < nf, up, dn)
    return y * cos + partner * sin


def _proj_kernel(x_ref, mod_ref, w_ref, bd_ref, qg_ref, kg_ref, cos_ref, sin_ref,
                 p_ref, q_ref, k_ref, v_ref, lx_ref, lg_ref):
    bd = bd_ref[...]
    cos = cos_ref[...]
    sin = sin_ref[...]
    for bb in range(x_ref.shape[0]):
        x = x_ref[bb]
        m = mod_ref[bb, 0]
        h = (x * (1.0 + m[1:2]) + m[0:1]).astype(BF16)
        z = jnp.dot(h, w_ref[...], preferred_element_type=F32)
        o = 0
        p_ref[bb] = z[:, o:o + BRANCH_WIDTH]
        o += BRANCH_WIDTH
        for c in range(Q_WIDTH // LANES):
            y = _norm_rope(z[:, o:o + LANES], bd, qg_ref[...], cos, sin) * (ATTN_SCALE * LOG2_E)
            q_ref[bb, 2 * c] = y[:, :HEAD_DIM].astype(BF16)
            q_ref[bb, 2 * c + 1] = y[:, HEAD_DIM:].astype(BF16)
            o += LANES
        for c in range(KV_WIDTH // LANES):
            y = _norm_rope(z[:, o:o + LANES], bd, kg_ref[...], cos, sin)
            k_ref[bb, 2 * c] = y[:, :HEAD_DIM].astype(BF16)
            k_ref[bb, 2 * c + 1] = y[:, HEAD_DIM:].astype(BF16)
            o += LANES
        for c in range(KV_WIDTH // LANES):
            y = z[:, o:o + LANES]
            ones = jnp.ones((y.shape[0], LANES - HEAD_DIM), F32)
            v_ref[bb, 2 * c] = jnp.concatenate([y[:, :HEAD_DIM], ones], axis=1).astype(BF16)
            v_ref[bb, 2 * c + 1] = jnp.concatenate([y[:, HEAD_DIM:], ones], axis=1).astype(BF16)
            o += LANES
        lx_ref[bb] = z[:, o:o + LRU_WIDTH]
        o += LRU_WIDTH
        lg_ref[bb] = z[:, o:o + LRU_WIDTH]


def _project(x, mods, w_a, bd, qg, kg, cos_t, sin_t, nct):
    b, l, d = x.shape
    nt = l // ROW_TILE
    wa = w_a.shape[1]
    bb = BATCH_BLOCK
    tok = lambda w: pl.BlockSpec((bb, ROW_TILE, w), lambda bi, i: (bi, i, 0))
    head = lambda n, w=HEAD_DIM: pl.BlockSpec((bb, n, ROW_TILE, w), lambda bi, i: (bi, 0, i, 0))
    const = lambda shp: pl.BlockSpec(shp, lambda bi, i: (0,) * len(shp))
    return pl.pallas_call(
        _proj_kernel,
        grid=(b // bb, nt),
        in_specs=[tok(d),
                  pl.BlockSpec((bb, 1, SUBLANES, d), lambda bi, i: (bi, (i >= nct).astype(jnp.int32), 0, 0)),
                  const((d, wa)), const((LANES, LANES)), const((1, LANES)), const((1, LANES)),
                  pl.BlockSpec((ROW_TILE, LANES), lambda bi, i: (i, 0)),
                  pl.BlockSpec((ROW_TILE, LANES), lambda bi, i: (i, 0))],
        out_specs=[tok(BRANCH_WIDTH), head(N_Q_HEADS), head(N_KV_HEADS), head(N_KV_HEADS, LANES),
                   tok(LRU_WIDTH), tok(LRU_WIDTH)],
        out_shape=[jax.ShapeDtypeStruct((b, l, BRANCH_WIDTH), F32),
                   jax.ShapeDtypeStruct((b, N_Q_HEADS, l, HEAD_DIM), BF16),
                   jax.ShapeDtypeStruct((b, N_KV_HEADS, l, HEAD_DIM), BF16),
                   jax.ShapeDtypeStruct((b, N_KV_HEADS, l, LANES), BF16),
                   jax.ShapeDtypeStruct((b, l, LRU_WIDTH), F32),
                   jax.ShapeDtypeStruct((b, l, LRU_WIDTH), F32)],
        compiler_params=_cparams("parallel", "parallel"),
        name="in_proj",
    )(x, mods, w_a, bd, qg, kg, cos_t, sin_t)


def _softmax_pv(q, k, v):
    hd = q.shape[1]
    s = lax.dot_general(q, k, (((1,), (1,)), ((), ())), preferred_element_type=F32)
    m = jnp.max(s, axis=1, keepdims=True)
    p = jnp.exp2(s - m).astype(BF16)
    r = jnp.dot(p, v, preferred_element_type=F32)
    return r[:, :hd] / r[:, hd:2 * hd]


def _attn_kernel(q_ref, k_ref, v_ref, *rest, n_ctx_tiles, ctx_len, n_step_casts, n_slab_casts):
    n_casts = n_step_casts + n_slab_casts
    cast_in, o_ref, cast_out = rest[:n_casts], rest[n_casts], rest[n_casts + 1:]
    for src, dst in zip(cast_in[:n_step_casts], cast_out[:n_step_casts]):
        dst[...] = src[...].astype(dst.dtype)

    @pl.when(pl.program_id(1) == 0)
    def _():
        for src, dst in zip(cast_in[n_step_casts:], cast_out[n_step_casts:]):
            dst[...] = src[...].astype(dst.dtype)

    nh, tq, hd = q_ref.shape[1:]
    group = nh // k_ref.shape[1]

    def run(lk):
        for j in range(nh):
            for r0 in range(0, tq, ATTN_CHAIN):
                o = _softmax_pv(q_ref[0, j, r0:r0 + ATTN_CHAIN, :], k_ref[0, j // group, :lk],
                                v_ref[0, j // group, :lk])
                o_ref[0, r0:r0 + ATTN_CHAIN, j * hd:(j + 1) * hd] = o.astype(o_ref.dtype)

    if n_ctx_tiles:
        @pl.when(pl.program_id(1) < n_ctx_tiles)
        def _():
            run(ctx_len)

        @pl.when(pl.program_id(1) >= n_ctx_tiles)
        def _():
            run(k_ref.shape[2])
    else:
        run(k_ref.shape[2])


def _attention(q, k, v, ctx_len, with_ctx, step_casts=(), slab_casts=()):
    b, nh, l, hd = q.shape
    nct = ctx_len // Q_TILE
    rows = l if with_ctx else l - ctx_len
    q_off = 0 if with_ctx else nct
    nq = rows // Q_TILE
    cast_specs, cast_shapes = [], []
    for w in step_casts:
        assert w.shape[0] == b and w.shape[1] % nq == 0
        cast_specs.append(pl.BlockSpec((1, w.shape[1] // nq, w.shape[2]), lambda bi, i: (bi, i, 0)))
        cast_shapes.append(jax.ShapeDtypeStruct(w.shape, BF16))
    for w in slab_casts:
        assert w.shape[0] % b == 0
        cast_specs.append(pl.BlockSpec((w.shape[0] // b, w.shape[1]), lambda bi, i: (bi, 0)))
        cast_shapes.append(jax.ShapeDtypeStruct(w.shape, BF16))
    return pl.pallas_call(
        functools.partial(_attn_kernel, n_ctx_tiles=nct if with_ctx else 0, ctx_len=ctx_len,
                          n_step_casts=len(step_casts), n_slab_casts=len(slab_casts)),
        grid=(b, nq),
        in_specs=[pl.BlockSpec((1, nh, Q_TILE, hd), lambda bi, i: (bi, 0, i + q_off, 0)),
                  pl.BlockSpec((1,) + k.shape[1:], lambda bi, i: (bi, 0, 0, 0)),
                  pl.BlockSpec((1,) + v.shape[1:], lambda bi, i: (bi, 0, 0, 0))] + cast_specs,
        out_specs=[pl.BlockSpec((1, Q_TILE, nh * hd), lambda bi, i: (bi, i, 0))] + cast_specs,
        out_shape=[jax.ShapeDtypeStruct((b, rows, nh * hd), BF16)] + cast_shapes,
        compiler_params=_cparams("parallel", "arbitrary"),
        name="attention",
    )(q, k, v, *step_casts, *slab_casts)


def _fill_ext(ext_ref, main, prev, nxt, first, last):
    t = main.shape[0]
    ext_ref[0:HALO] = jnp.where(first, 0.0, prev)
    ext_ref[HALO:HALO + t] = main
    ext_ref[HALO + t:2 * HALO + t] = jnp.where(last, 0.0, nxt)


def _seg_flags(tile, nct, nt):
    first = jnp.logical_or(tile == 0, tile == nct)
    last = jnp.logical_or(tile == nct - 1, tile == nt - 1)
    return first, last


def _halo_specs(width, tile_of, l):
    rb = ROW_TILE // HALO
    last_blk = l // HALO - 1
    bb = BATCH_BLOCK
    main = pl.BlockSpec((bb, ROW_TILE, width), lambda bi, i: (bi, tile_of(i), 0))
    prev = pl.BlockSpec((bb, HALO, width), lambda bi, i: (bi, jnp.maximum(tile_of(i) * rb - 1, 0), 0))
    nxt = pl.BlockSpec((bb, HALO, width), lambda bi, i: (bi, jnp.minimum((tile_of(i) + 1) * rb, last_blk), 0))
    return main, prev, nxt


def _softplus(z):
    return jnp.maximum(z, 0.0) + jnp.log1p(jnp.exp(-jnp.abs(z)))


def _gelu_tanh(x):
    return 0.5 * x * (1.0 + jnp.tanh(0.7978845608028654 * (x + 0.044715 * (x * x * x))))


def _lru_kernel(*refs, reverse, nct, nt):
    if reverse:
        (lx_ref, lxp_ref, lxn_ref, w_ref, ba_ref, bx_ref, lam_ref, cw_ref, cb_ref, hf_ref, lg_ref,
         o_ref, ext_ref, a_ref, u_ref, h_ref) = refs
    else:
        (lx_ref, lxp_ref, lxn_ref, w_ref, ba_ref, bx_ref, lam_ref, cw_ref, cb_ref,
         o_ref, ext_ref, a_ref, u_ref, h_ref) = refs
    j = pl.program_id(1)
    tile = _lru_tile(j, reverse, nct, nt)
    t = lx_ref.shape[1]
    nbb = lx_ref.shape[0]

    @pl.when(j == 0)
    def _():
        h_ref[...] = jnp.zeros_like(h_ref)

    first, last = _seg_flags(tile, nct, nt)
    lo = CONV_WIDTH // 2
    half_log2_a = (-0.5 * RG_C * LOG2_E) * _softplus(-lam_ref[...])
    for bb in range(nbb):
        ext = ext_ref.at[bb]
        _fill_ext(ext, lx_ref[bb], lxp_ref[bb], lxn_ref[bb], first, last)
        xh = cb_ref[...]
        for kk in range(CONV_WIDTH):
            xh = xh + ext[pl.ds(HALO + kk - lo, t), :] * cw_ref[kk:kk + 1, :]
        zz = jnp.dot(xh.astype(BF16), w_ref[...], preferred_element_type=F32)
        tr = jnp.tanh(zz[:, :LRU_WIDTH] + ba_ref[...])
        ti = jnp.tanh(zz[:, LRU_WIDTH:] + bx_ref[...])
        a = jnp.exp2(half_log2_a * (tr + 1.0))
        a_ref[bb] = a
        u_ref[bb] = jnp.sqrt(1.0 - a * a) * (ti + 1.0) * xh

    row = lax.broadcasted_iota(jnp.int32, (SUBLANES, LRU_WIDTH), 0)
    n_sub = t // SUBLANES

    def sub(s, hs):
        blk = (n_sub - 1 - s) if reverse else s
        r0 = pl.multiple_of(blk * SUBLANES, SUBLANES)
        out = []
        for bb in range(nbb):
            aa = a_ref[bb, pl.ds(r0, SUBLANES), :]
            uu = u_ref[bb, pl.ds(r0, SUBLANES), :]
            for dd in (1, 2, 4):
                if reverse:
                    keep = row < SUBLANES - dd
                    sh = SUBLANES - dd
                else:
                    keep = row >= dd
                    sh = dd
                a_sh = jnp.where(keep, pltpu.roll(aa, sh, 0), 1.0)
                u_sh = jnp.where(keep, pltpu.roll(uu, sh, 0), 0.0)
                uu = aa * u_sh + uu
                aa = aa * a_sh
            hh = aa * hs[bb] + uu
            if reverse:
                o_ref[bb, pl.ds(r0, SUBLANES), :] = (
                    (hf_ref[bb, pl.ds(r0, SUBLANES), :] + hh)
                    * _gelu_tanh(lg_ref[bb, pl.ds(r0, SUBLANES), :])).astype(o_ref.dtype)
                edge = hh[0:1]
            else:
                o_ref[bb, pl.ds(r0, SUBLANES), :] = hh
                edge = hh[SUBLANES - 1:SUBLANES]
            out.append(jnp.broadcast_to(edge, (SUBLANES, LRU_WIDTH)))
        return tuple(out)

    hs = lax.fori_loop(0, n_sub, sub, tuple(h_ref[bb] for bb in range(nbb)), unroll=4)
    for bb in range(nbb):
        h_ref[bb] = hs[bb]


def _lru_tile(j, reverse, nct, nt):
    if not reverse:
        return j
    return jnp.where(j < nct, nct - 1 - j, nt - 1 - (j - nct))


def _lru(lx, w, ba, bx, lam, cw, cb, nct, reverse, hf=None, lg=None):
    b, l, wd = lx.shape
    nt = l // ROW_TILE
    tile_of = lambda i: _lru_tile(i, reverse, nct, nt)
    main, prev, nxt = _halo_specs(wd, tile_of, l)
    const = lambda shp: pl.BlockSpec(shp, lambda bi, i: (0,) * len(shp))
    in_specs = [main, prev, nxt, const(w.shape), const((1, wd)), const((1, wd)), const((1, wd)),
                const((SUBLANES, wd)), const((1, wd))]
    args = [lx, lx, lx, w, ba, bx, lam, cw, cb]
    if reverse:
        in_specs += [main, main]
        args += [hf, lg]
    bb = BATCH_BLOCK
    return pl.pallas_call(
        functools.partial(_lru_kernel, reverse=reverse, nct=nct, nt=nt),
        grid=(b // bb, nt),
        in_specs=in_specs,
        out_specs=main,
        out_shape=jax.ShapeDtypeStruct((b, l, wd), BF16 if reverse else F32),
        scratch_shapes=[pltpu.VMEM((bb, ROW_TILE + 2 * HALO, wd), F32), pltpu.VMEM((bb, ROW_TILE, wd), F32),
                        pltpu.VMEM((bb, ROW_TILE, wd), F32), pltpu.VMEM((bb, SUBLANES, wd), F32)],
        compiler_params=_cparams("parallel", "arbitrary"),
        name="lru_bwd" if reverse else "lru_fwd",
    )(*args)


def _merge_kernel(x_ref, mod_ref, p_ref, pp_ref, pn_ref, at_ref, lr_ref, wg_ref, bm_ref, pw_ref, ps_ref,
                  wb_ref, wo_ref, g_ref, b_ref, *rest, nct, nt, tile_off, ctx_len, alpha, route):
    if route:
        rw_ref, rb_ref, x1_ref, h2_ref, logit_ref, ext_ref = rest
    else:
        x1_ref, h2_ref, ext_ref = rest
    tile = pl.program_id(1) + tile_off
    t = x_ref.shape[1]
    d = x_ref.shape[2]
    first, last = _seg_flags(tile, nct, nt)
    in_ctx = tile < nct
    seg_len = jnp.where(in_ctx, ctx_len, nt * t - ctx_len)
    pos = (tile - jnp.where(in_ctx, 0, nct)) * t + lax.broadcasted_iota(jnp.int32, (t, 1), 0)

    for bb in range(x_ref.shape[0]):
        x = x_ref[bb]
        m = mod_ref[bb, 0]
        h_half = ((x * (1.0 + m[1:2]) + m[0:1]) * 0.5).astype(BF16)

        def branch(n, ys):
            tg = jnp.tanh(jnp.dot(h_half, wg_ref[:, n * d:(n + 1) * d], preferred_element_type=F32)
                          + bm_ref[n:n + 1, :])
            return (tg + 1.0) * jnp.dot(ys, wb_ref[n], preferred_element_type=F32)

        mix = branch(1, at_ref[bb]) + branch(2, lr_ref[bb])

        ext = ext_ref.at[bb]
        _fill_ext(ext, p_ref[bb], pp_ref[bb], pn_ref[bb], first, last)
        pooled = []
        for gi, w in enumerate(POOL_WINDOWS):
            lo = w // 2
            cols = slice(gi * POOL_GROUP, (gi + 1) * POOL_GROUP)
            acc = ext[pl.ds(HALO - lo, t), cols]
            for kk in range(1, w):
                acc = acc + ext[pl.ds(HALO - lo + kk, t), cols]
            cnt = jnp.clip(pos - lo + w, 0, seg_len) - jnp.clip(pos - lo, 0, seg_len)
            mean = acc / cnt.astype(F32)
            dlt = (mean - ext[pl.ds(HALO, t), cols]).astype(BF16)
            pooled.append(jnp.dot(dlt, pw_ref[gi], preferred_element_type=F32))
        pool_y = (jnp.concatenate(pooled, axis=1) * ps_ref[...]).astype(BF16)
        mix = branch(0, pool_y) + mix
        y2 = jnp.dot(mix.astype(BF16), wo_ref[...], preferred_element_type=F32)
        x1 = _layer_norm(alpha * x + m[6:7] * y2, g_ref[...], b_ref[...])
        x1_ref[bb] = x1
        h2 = x1 * (1.0 + m[4:5]) + m[3:4]
        if route:
            hi = h2.astype(BF16)
            mid = (h2 - hi.astype(F32)).astype(BF16)
            logit_ref[bb] = (jnp.dot(hi, rw_ref[0], preferred_element_type=F32)
                             + jnp.dot(mid, rw_ref[0], preferred_element_type=F32)
                             + jnp.dot(hi, rw_ref[1], preferred_element_type=F32) + rb_ref[...])
            _store_row_tiles(h2_ref.at[bb], h2)
        else:
            h2_ref[bb] = h2.astype(h2_ref.dtype)


def _merge(x, mods, p, attn, lru, w_gate, b_merge, pool_w, pool_scale, w_branch, w_out, ln_g, ln_b,
           nct, lat_only, alpha, router=None):
    b, l, d = x.shape
    nt = l // ROW_TILE
    off = nct if lat_only else 0
    rows = l - off * ROW_TILE
    tile_of = lambda i: i + off
    bb = BATCH_BLOCK
    tok_l = lambda w: pl.BlockSpec((bb, ROW_TILE, w), lambda bi, i: (bi, i + off, 0))
    tok_o = lambda w: pl.BlockSpec((bb, ROW_TILE, w), lambda bi, i: (bi, i, 0))
    const = lambda shp: pl.BlockSpec(shp, lambda bi, i: (0,) * len(shp))
    pm, pp, pn = _halo_specs(BRANCH_WIDTH, tile_of, l)
    in_specs = [tok_l(d),
                pl.BlockSpec((bb, 1, SUBLANES, d), lambda bi, i: (bi, (i + off >= nct).astype(jnp.int32), 0, 0)),
                pm, pp, pn,
                tok_o(Q_WIDTH) if lat_only else tok_l(Q_WIDTH),
                tok_l(LRU_WIDTH),
                const(w_gate.shape), const(b_merge.shape), const(pool_w.shape), const(pool_scale.shape),
                const(w_branch.shape), const(w_out.shape), const((1, d)), const((1, d))]
    args = [x, mods, p, p, p, attn, lru, w_gate, b_merge, pool_w, pool_scale, w_branch, w_out, ln_g, ln_b]
    out_specs = [tok_o(d)]
    out_shape = [jax.ShapeDtypeStruct((b, rows, d), F32)]
    if router is None:
        out_specs.append(tok_o(d))
        out_shape.append(jax.ShapeDtypeStruct((b, rows, d), BF16))
    else:
        n_sub = d // LANES
        in_specs += [const(router[0].shape), const(router[1].shape)]
        args += list(router)
        out_specs += [pl.BlockSpec((bb, ROW_TILE * n_sub, LANES), lambda bi, i: (bi, i, 0)), tok_o(LANES)]
        out_shape += [jax.ShapeDtypeStruct((b, rows * n_sub, LANES), F32),
                      jax.ShapeDtypeStruct((b, rows, LANES), F32)]
    return pl.pallas_call(
        functools.partial(_merge_kernel, nct=nct, nt=nt, tile_off=off, ctx_len=nct * ROW_TILE, alpha=alpha,
                          route=router is not None),
        grid=(b // bb, rows // ROW_TILE),
        in_specs=in_specs,
        out_specs=out_specs,
        out_shape=out_shape,
        scratch_shapes=[pltpu.VMEM((bb, ROW_TILE + 2 * HALO, BRANCH_WIDTH), F32)],
        compiler_params=_cparams("parallel", "parallel"),
        name="merge_ln1",
    )(*args)


def _swiglu_kernel(x_ref, wg_ref, wu_ref, wd_ref, x1_ref, *rest, alpha):
    mod_refs, (g_ref, b_ref, o_ref) = rest[:-3], rest[-3:]
    x = x_ref[...]
    g = jnp.dot(x, wg_ref[...], preferred_element_type=F32)
    u = jnp.dot(x, wu_ref[...], preferred_element_type=F32)
    ff = jnp.dot((_silu(g) * u).astype(BF16), wd_ref[...], preferred_element_type=F32)
    for hh, mod_ref in enumerate(mod_refs):
        rs = slice(hh * ROW_TILE, (hh + 1) * ROW_TILE)
        o_ref[rs, :] = _layer_norm(alpha * x1_ref[rs, :] + mod_ref[0, 0, 5:6, :] * ff[rs], g_ref[...], b_ref[...])


def _swiglu_dense_ln(h2, x1, mods, ln_g, ln_b, w_gate, w_up, w_down, nct, alpha):
    b, l, d = x1.shape
    r = b * l
    ntl = l // ROW_TILE
    halves = FFN_ROWS // ROW_TILE

    def mod_spec(hh):
        def idx(i):
            tile = i * halves + hh
            return (tile // ntl, (tile % ntl >= nct).astype(jnp.int32), 0, 0)
        return pl.BlockSpec((1, 1, SUBLANES, d), idx)

    resident = lambda shp: pl.BlockSpec(shp, lambda i: (0, 0), pipeline_mode=pl.Buffered(1))
    out = pl.pallas_call(
        functools.partial(_swiglu_kernel, alpha=alpha),
        grid=(r // FFN_ROWS,),
        in_specs=[pl.BlockSpec((FFN_ROWS, d), lambda i: (i, 0)),
                  resident(w_gate.shape), resident(w_up.shape), resident(w_down.shape),
                  pl.BlockSpec((FFN_ROWS, d), lambda i: (i, 0))]
                 + [mod_spec(hh) for hh in range(halves)]
                 + [pl.BlockSpec((1, d), lambda i: (0, 0)), pl.BlockSpec((1, d), lambda i: (0, 0))],
        out_specs=pl.BlockSpec((FFN_ROWS, d), lambda i: (i, 0)),
        out_shape=jax.ShapeDtypeStruct((r, d), F32),
        compiler_params=_cparams("parallel"),
        name="swiglu_dense_ln2",
    )(h2.reshape(r, d), w_gate, w_up, w_down, x1.reshape(r, d), *([mods] * halves), ln_g, ln_b)
    return out.reshape(b, l, d)


def _moe_kernel(be_ref, nu_ref, tok_ref, dst_ref, t_hbm, wg_ref, wu_ref, wd_ref, y_hbm,
                xbuf, acc_ref, obuf, sem_in, sem_out, sem_fill):
    i = pl.program_id(0)
    f = pl.program_id(1)
    nb = pl.num_programs(0)
    nu = nu_ref[0]
    rows = acc_ref.shape[0]
    sub = xbuf.shape[1] // rows
    slot = i % 2

    def tile_rows(start):
        return pl.ds(pl.multiple_of(start, sub), sub)

    def gather_copy(blk, sl, r):
        return pltpu.make_async_copy(t_hbm.at[tile_rows(tok_ref[blk * rows + r])], xbuf.at[sl, tile_rows(r * sub)],
                                     sem_in.at[sl])

    def scatter_copy(blk, sl, r):
        return pltpu.make_async_copy(obuf.at[sl, tile_rows(r * sub)], y_hbm.at[tile_rows(dst_ref[blk * rows + r])],
                                     sem_out.at[sl])

    def gathers_done(sl):
        pltpu.make_async_copy(t_hbm.at[pl.ds(0, rows * sub)], xbuf.at[sl], sem_in.at[sl]).wait()

    def scatters_done(sl):
        pltpu.make_async_copy(obuf.at[sl], y_hbm.at[pl.ds(0, rows * sub)], sem_out.at[sl]).wait()

    def for_rows(fn):
        def body(r, c):
            fn(r)
            return c
        lax.fori_loop(0, rows, body, 0, unroll=8)

    def load_x():
        return _load_row_tiles(xbuf.at[slot], rows).astype(BF16)

    def swiglu_part(x):
        g = jnp.dot(x, wg_ref[0], preferred_element_type=F32)
        u = jnp.dot(x, wu_ref[0], preferred_element_type=F32)
        return jnp.dot((_silu(g) * u).astype(BF16), wd_ref[0], preferred_element_type=F32)

    @pl.when(jnp.logical_and(i == 0, f == 0))
    def _():
        obuf[1] = jnp.zeros(obuf.shape[1:], obuf.dtype)
        for_rows(lambda r: gather_copy(0, 0, r).start())

    @pl.when(jnp.logical_and(f == 0, i <= nu))
    def _():
        gathers_done(slot)

    @pl.when(jnp.logical_and(i < nu, f == 0))
    def _():
        x = load_x()
        for r in range(rows):
            gather_copy(i + 1, 1 - slot, r).start()
        acc_ref[...] = swiglu_part(x)

    @pl.when(jnp.logical_and(i < nu, f == 1))
    def _():
        @pl.when(i >= 1)
        def _():
            scatters_done(slot)

        x = load_x()
        prev = jnp.where(i >= 1, i - 1, nb)
        for r in range(rows):
            scatter_copy(prev, 1 - slot, r).start()
        _store_row_tiles(obuf.at[slot], acc_ref[...] + swiglu_part(x))

        @pl.when(i == nu - 1)
        def _():
            scatters_done(1 - slot)
            for_rows(lambda r: scatter_copy(i, slot, r).start())
            scatters_done(slot)

    @pl.when(jnp.logical_and(i >= nu, f == 1))
    def _():
        xbuf[0] = jnp.zeros(xbuf.shape[1:], xbuf.dtype)
        fill = pltpu.make_async_copy(xbuf.at[0], y_hbm.at[pl.ds(i * rows * sub, rows * sub)], sem_fill)
        fill.start()
        fill.wait()

    @pl.when(jnp.logical_and(jnp.logical_and(i == nb - 1, f == 1), nu == nb))
    def _():
        gathers_done(nb % 2)


def _swiglu_experts(t, block_e, n_used, slot_tok, slot_dst, w_gate, w_up, w_down, tf):
    d = w_gate.shape[1]
    sub = d // LANES
    p = slot_tok.shape[0]
    nf = w_gate.shape[2] // tf
    assert nf == 2

    def fcol(i, f, nu):
        return jnp.where(i < nu[0], f, nf - 1)

    return pl.pallas_call(
        _moe_kernel,
        grid_spec=pltpu.PrefetchScalarGridSpec(
            num_scalar_prefetch=4,
            grid=(p // FFN_ROWS - 1, nf),
            in_specs=[pl.BlockSpec(memory_space=pl.ANY),
                      pl.BlockSpec((1, d, tf), lambda i, f, be, nu, st, sd: (be[i], 0, fcol(i, f, nu))),
                      pl.BlockSpec((1, d, tf), lambda i, f, be, nu, st, sd: (be[i], 0, fcol(i, f, nu))),
                      pl.BlockSpec((1, tf, d), lambda i, f, be, nu, st, sd: (be[i], fcol(i, f, nu), 0))],
            out_specs=pl.BlockSpec(memory_space=pl.ANY),
            scratch_shapes=[pltpu.VMEM((2, FFN_ROWS * sub, LANES), F32), pltpu.VMEM((FFN_ROWS, d), F32),
                            pltpu.VMEM((2, FFN_ROWS * sub, LANES), F32),
                            pltpu.SemaphoreType.DMA((2,)), pltpu.SemaphoreType.DMA((2,)),
                            pltpu.SemaphoreType.DMA]),
        out_shape=jax.ShapeDtypeStruct((p * sub, LANES), F32),
        compiler_params=_cparams("arbitrary", "arbitrary"),
        name="swiglu_experts",
    )(block_e, n_used, slot_tok, slot_dst, t, w_gate, w_up, w_down)


def _combine_ln_kernel(x_ref, mod_ref, g_ref, b_ref, *rest, alpha):
    y_refs, p_ref, o_ref = rest[:-2], rest[-2], rest[-1]
    t = x_ref.shape[1]
    top_k = p_ref.shape[2]
    for bb in range(x_ref.shape[0]):
        f = None
        for kk in range(top_k):
            term = _load_row_tiles(y_refs[bb * top_k + kk], t) * p_ref[bb, :, kk:kk + 1]
            f = term if f is None else f + term
        m = mod_ref[bb, 0]
        o_ref[bb] = _layer_norm(alpha * x_ref[bb] + m[5:6] * f, g_ref[...], b_ref[...])


def _combine_ln(x1, mods, ln_g, ln_b, y, probs, alpha):
    b, rows, d = x1.shape
    nt = rows // ROW_TILE
    sub = d // LANES
    top_k = probs.shape[2]
    nbb = BATCH_BLOCK
    tok = lambda w: pl.BlockSpec((nbb, ROW_TILE, w), lambda bi, i: (bi, i, 0))
    const = lambda shp: pl.BlockSpec(shp, lambda bi, i: (0,) * len(shp))
    y_specs = [pl.BlockSpec((ROW_TILE * sub, LANES),
                            lambda bi, i, kk=kk, bb=bb: ((kk * b + bi * nbb + bb) * nt + i, 0))
               for bb in range(nbb) for kk in range(top_k)]
    return pl.pallas_call(
        functools.partial(_combine_ln_kernel, alpha=alpha),
        grid=(b // nbb, nt),
        in_specs=[tok(d), pl.BlockSpec((nbb, 1, SUBLANES, d), lambda bi, i: (bi, 1, 0, 0)),
                  const((1, d)), const((1, d))] + y_specs + [tok(top_k)],
        out_specs=tok(d),
        out_shape=jax.ShapeDtypeStruct((b, rows, d), F32),
        compiler_params=_cparams("parallel", "parallel"),
        name="combine_ln2",
    )(x1, mods, ln_g, ln_b, *([y] * (top_k * nbb)), probs)


def _moe(t, logits, w_gate, w_up, w_down, tf):
    n, n_e = logits.shape
    sub = t.shape[0] // n
    top_v, top_i = lax.top_k(logits, TOP_K)
    probs = jax.nn.softmax(top_v, axis=-1)
    m = n * TOP_K
    flat_e = top_i.reshape(m)
    counts = jnp.sum((flat_e[:, None] == jnp.arange(n_e, dtype=flat_e.dtype)[None, :]).astype(jnp.int32), axis=0)
    order = jnp.argsort(flat_e, stable=True).astype(jnp.int32)
    start = jnp.cumsum(counts) - counts
    padded = (counts + FFN_ROWS - 1) // FFN_ROWS * FFN_ROWS
    ends_p = jnp.cumsum(padded)
    start_p = ends_p - padded
    nb = -(-(m + n_e * (FFN_ROWS - 1)) // FFN_ROWS)
    p = (nb + 1) * FFN_ROWS
    block_e = jnp.minimum(jnp.searchsorted(ends_p, jnp.arange(nb + 1) * FFN_ROWS, side='right'),
                          n_e - 1).astype(jnp.int32)
    n_used = (ends_p[-1:] // FFN_ROWS).astype(jnp.int32)
    slot_e = jnp.repeat(block_e, FFN_ROWS)
    r = jnp.arange(p, dtype=jnp.int32) - start_p[slot_e]
    is_pad = r >= counts[slot_e]
    slot_pair = order[jnp.clip(start[slot_e] + r, 0, m - 1)]
    slot_tok = jnp.where(is_pad, 0, slot_pair // TOP_K) * sub
    pads_before = (start_p - start)[slot_e] + r - counts[slot_e]
    slot_dst = jnp.where(is_pad, m + pads_before, (slot_pair % TOP_K) * n + slot_pair // TOP_K) * sub
    y = _swiglu_experts(t, block_e, n_used, slot_tok, slot_dst, w_gate, w_up, w_down, tf)
    return y, probs


def _block_diag(w):
    nb, bi, bj = w.shape
    eye = jnp.eye(nb, dtype=w.dtype)
    return (eye[:, None, :, None] * w[:, :, None, :]).reshape(nb * bi, nb * bj)


def _rope_tables(ctx_len, s):
    nf = HEAD_DIM // 4
    inv = ROPE_THETA ** (-jnp.arange(nf, dtype=F32) / nf)
    t = jnp.arange(s)
    row = (t // GRID_W).astype(F32)[:, None] * inv
    col = (t % GRID_W).astype(F32)[:, None] * inv
    ang = jnp.concatenate([row, row, col, col], axis=1)
    sign = jnp.tile(jnp.concatenate([-jnp.ones((nf,), F32), jnp.ones((nf,), F32)]), 2)
    cos = jnp.concatenate([jnp.ones((ctx_len, HEAD_DIM), F32), jnp.cos(ang)], axis=0)
    sin = jnp.concatenate([jnp.zeros((ctx_len, HEAD_DIM), F32), jnp.sin(ang) * sign], axis=0)
    reps = LANES // HEAD_DIM
    return jnp.tile(cos, (1, reps)), jnp.tile(sin, (1, reps))


def _pad_rows(a, rows):
    return jnp.zeros((rows,) + a.shape[1:], a.dtype).at[:a.shape[0]].set(a)


def kernel(x, c, ctx, c_ctx, w_mod, b_mod, w_in, b_merge, pool_w, pool_scale, q_norm, k_norm, conv_w, conv_b, lru_wa, lru_ba, lru_wx, lru_bx, lru_lambda, w_branch, w_out, ln1_g, ln1_b, ffn_w_gate, ffn_w_up, ffn_w_down, moe_router, moe_router_b, moe_w_gate, moe_w_up, moe_w_down, ln2_g, ln2_b):
    b, s, d = x.shape
    ctx_len = ctx.shape[1]
    depth = w_in.shape[0]
    l = ctx_len + s
    assert ctx_len % ROW_TILE == 0 and s % ROW_TILE == 0 and (b * s) % FFN_ROWS == 0 and b % BATCH_BLOCK == 0
    assert (b * l) % FFN_ROWS == 0 and ctx_len % Q_TILE == 0 and s % Q_TILE == 0
    nct = ctx_len // ROW_TILE
    alpha = (2 * depth) ** 0.25

    cc = _pad_rows(jnp.concatenate([c, c_ctx[None, :]], axis=0), -(-(b + 1) // SUBLANES) * SUBLANES)
    mod_all = _modulation(cc, w_mod, b_mod)

    cos_t, sin_t = _rope_tables(ctx_len, s)
    bd = _block_diag(jnp.full((LANES // HEAD_DIM, HEAD_DIM, HEAD_DIM), 1.0 / HEAD_DIM, F32)).astype(BF16)
    reps = LANES // HEAD_DIM

    xs = jnp.concatenate([ctx, x], axis=1)
    w_in_bf = w_in[0].astype(BF16)
    for li in range(depth):
        last = li == depth - 1
        dense = li % 2 == 0
        jf = li // 2
        ml = jnp.transpose(mod_all[li, :, :b], (1, 0, 2))
        mc = jnp.broadcast_to(mod_all[li, :, b][None], (b, 6, d))
        mods = jnp.stack([mc, ml], axis=1)
        mods = jnp.concatenate([mods, 0.5 * mods[:, :, 2:3], jnp.zeros((b, 2, 1, d), F32)], axis=2)

        p, q, k, v, lx, lg = _project(
            xs, mods, w_in_bf[:, :OFF_GATE], bd,
            jnp.tile(q_norm[li], reps)[None], jnp.tile(k_norm[li], reps)[None], cos_t, sin_t, nct)

        slab = [w_branch[li].reshape(-1, d), w_out[li]]
        if dense:
            slab += [ffn_w_gate[jf], ffn_w_up[jf], ffn_w_down[jf]]
            step = []
        else:
            step = [moe_w_gate[jf], moe_w_up[jf], moe_w_down[jf]]
        if not last:
            slab.append(w_in[li + 1])
        if not all(w.shape[0] == b and w.shape[1] % ((l - (ctx_len if last else 0)) // Q_TILE * 16) == 0
                   for w in step):
            step_bf, step = [w.astype(BF16) for w in step], []
        attn, *made = _attention(q, k, v, ctx_len, with_ctx=not last, step_casts=step, slab_casts=slab)
        if step:
            step_bf = made[:len(step)]
        slab_bf = made[len(step):]
        wb_bf, wo_bf = slab_bf[0].reshape(w_branch[li].shape), slab_bf[1]

        cw = _pad_rows(0.5 * conv_w[li], SUBLANES)
        cb = 0.5 * conv_b[li][None]
        lru_w = [jnp.concatenate([_block_diag(lru_wa[li, dr]), _block_diag(lru_wx[li, dr])], axis=1).astype(BF16)
                 for dr in range(2)]
        hf = _lru(lx, lru_w[0], 0.5 * lru_ba[li, 0][None], 0.5 * lru_bx[li, 0][None], lru_lambda[li, 0][None],
                  cw, cb, nct, reverse=False)
        lru_y = _lru(lx, lru_w[1], 0.5 * lru_ba[li, 1][None], 0.5 * lru_bx[li, 1][None], lru_lambda[li, 1][None],
                     cw, cb, nct, reverse=True, hf=hf, lg=lg)

        merge_args = (xs, mods, p, attn, lru_y, w_in_bf[:, OFF_GATE:], _pad_rows(0.5 * b_merge[li], SUBLANES),
                      pool_w[li].astype(BF16), pool_scale[li][None], wb_bf, wo_bf, ln1_g[li][None], ln1_b[li][None])
        if dense:
            assert not last, "the dense layer is expected to carry the context tokens along"
            x1, h2 = _merge(*merge_args, nct, lat_only=False, alpha=alpha)
            xs = _swiglu_dense_ln(h2, x1, mods, ln2_g[li][None], ln2_b[li][None], *slab_bf[2:5],
                                  nct=nct, alpha=alpha)
        else:
            assert last, "the expert layer is expected to be the last layer (latent tokens only)"
            n_e = moe_router.shape[2]
            rw = jnp.zeros((d, LANES), F32).at[:, :n_e].set(moe_router[jf])
            rb = jnp.zeros((1, LANES), F32).at[0, :n_e].set(moe_router_b[jf])
            rw_hi = rw.astype(BF16)
            rw_split = jnp.stack([rw_hi, (rw - rw_hi.astype(F32)).astype(BF16)])
            x1, h2, logits = _merge(*merge_args, nct, lat_only=True, alpha=alpha, router=(rw_split, rb))
            y, probs = _moe(h2.reshape(-1, LANES), logits.reshape(b * s, LANES)[:, :n_e], *step_bf,
                            tf=moe_w_gate.shape[3] // 2)
            xs = _combine_ln(x1, mods, ln2_g[li][None], ln2_b[li][None], y, probs.reshape(b, s, -1), alpha)
        if not last:
            w_in_bf = slab_bf[-1]
    return xs
```

```python
import functools

import jax
import jax.numpy as jnp
from jax import lax
from jax.experimental import pallas as pl
from jax.experimental.pallas import tpu as pltpu

F32 = jnp.float32
BF16 = jnp.bfloat16

GRID_W = 64
POOL_WINDOWS = (2, 4, 8, 16)
BRANCH_WIDTH = 512
POOL_GROUP = BRANCH_WIDTH // len(POOL_WINDOWS)
N_Q_HEADS = 8
N_KV_HEADS = 2
HEAD_DIM = 64
GQA_GROUP = N_Q_HEADS // N_KV_HEADS
Q_WIDTH = N_Q_HEADS * HEAD_DIM
KV_WIDTH = N_KV_HEADS * HEAD_DIM
ROPE_THETA = 10000.0
ATTN_SCALE = HEAD_DIM ** -0.5
LOG2_E = 1.4426950408889634
LRU_WIDTH = BRANCH_WIDTH
LRU_BLOCKS = 8
CONV_WIDTH = 4
RG_C = 8.0
N_BRANCH = 3
N_EXPERTS = 8
TOP_K = 2
LN_EPS = 1e-5
RMS_EPS = 1e-6
OFF_GATE = BRANCH_WIDTH + Q_WIDTH + 2 * KV_WIDTH + 2 * LRU_WIDTH

LANES = 128
SUBLANES = 8
ROW_TILE = 256
BATCH_BLOCK = 2
Q_TILE = 256
ATTN_CHAIN = 256
FFN_ROWS = 512
HALO = SUBLANES
VMEM_LIMIT = 56 * 1024 * 1024


def _cparams(*sem):
    return pltpu.CompilerParams(dimension_semantics=sem, vmem_limit_bytes=VMEM_LIMIT)


def _sigmoid(x):
    return 0.5 * jnp.tanh(0.5 * x) + 0.5


def _store_row_tiles(ref, val):
    t, d = val.shape
    for j in range(d // LANES):
        ref[pl.ds(j, t, stride=d // LANES), :] = val[:, j * LANES:(j + 1) * LANES]


def _load_row_tiles(ref, t):
    n = ref.shape[0] // t
    return jnp.concatenate([ref[pl.ds(j, t, stride=n), :] for j in range(n)], axis=1)


def _silu(x):
    return x * _sigmoid(x)


def _layer_norm(v, g, b):
    mu = jnp.mean(v, axis=-1, keepdims=True)
    d = v - mu
    var = jnp.mean(d * d, axis=-1, keepdims=True)
    return d * lax.rsqrt(var + LN_EPS) * g + b


def _mod_kernel(c_ref, w_ref, b_ref, o_ref):
    s = _silu(c_ref[...])
    o_ref[0, 0] = jnp.dot(s, w_ref[0], precision=lax.Precision.HIGHEST,
                          preferred_element_type=F32) + b_ref[0, 0]


def _modulation(cc, w_mod, b_mod):
    depth, d, _ = w_mod.shape
    r = cc.shape[0]
    b6 = b_mod.reshape(depth, 6, 1, d)
    return pl.pallas_call(
        _mod_kernel,
        grid=(depth, 6),
        in_specs=[pl.BlockSpec((r, d), lambda l, j: (0, 0)),
                  pl.BlockSpec((1, d, d), lambda l, j: (l, 0, j)),
                  pl.BlockSpec((1, 1, 1, d), lambda l, j: (l, j, 0, 0))],
        out_specs=pl.BlockSpec((1, 1, r, d), lambda l, j: (l, j, 0, 0)),
        out_shape=jax.ShapeDtypeStruct((depth, 6, r, d), F32),
        compiler_params=_cparams("arbitrary", "arbitrary"),
        name="modulation",
    )(cc, w_mod, b6)


def _norm_rope(zc, bd, g, cos, sin):
    zz = zc * zc
    hi = zz.astype(BF16)
    lo = (zz - hi.astype(F32)).astype(BF16)
    ms = jnp.dot(hi, bd, preferred_element_type=F32) + jnp.dot(lo, bd, preferred_element_type=F32)
    y = zc * lax.rsqrt(ms + RMS_EPS) * g
    nf = HEAD_DIM // 4
    up = pltpu.roll(y, LANES - nf, 1)
    dn = pltpu.roll(y, nf, 1)
    lane = lax.broadcasted_iota(jnp.int32, y.shape, 1)
    partner = jnp.where((lane % (2 * nf)) < nf, up, dn)
    return y * cos + partner * sin


def _proj_kernel(*refs, nct, joined):
    if joined:
        x_ref, mod_ref, w_ref, bd_ref, qg_ref, kg_ref, cos_ref, sin_ref, p_ref, q_ref, k_ref, v_ref, lx_ref, lg_ref = refs
    else:
        (c_ref, x_ref, mod_ref, w_ref, bd_ref, qg_ref, kg_ref, cos_ref, sin_ref,
         p_ref, q_ref, k_ref, v_ref, lx_ref, lg_ref, xs_ref) = refs
    bd = bd_ref[...]
    cos = cos_ref[...]
    sin = sin_ref[...]
    for bb in range(x_ref.shape[0]):
        if joined:
            x = x_ref[bb]
        else:
            x = jnp.where(pl.program_id(1) < nct, c_ref[bb], x_ref[bb])
            xs_ref[bb] = x
        m = mod_ref[bb, 0]
        h = (x * (1.0 + m[1:2]) + m[0:1]).astype(BF16)
        z = jnp.dot(h, w_ref[...], preferred_element_type=F32)
        o = 0
        p_ref[bb] = z[:, o:o + BRANCH_WIDTH]
        o += BRANCH_WIDTH
        for c in range(Q_WIDTH // LANES):
            y = _norm_rope(z[:, o:o + LANES], bd, qg_ref[...], cos, sin) * (ATTN_SCALE * LOG2_E)
            q_ref[bb, 2 * c] = y[:, :HEAD_DIM].astype(BF16)
            q_ref[bb, 2 * c + 1] = y[:, HEAD_DIM:].astype(BF16)
            o += LANES
        for c in range(KV_WIDTH // LANES):
            y = _norm_rope(z[:, o:o + LANES], bd, kg_ref[...], cos, sin)
            k_ref[bb, 2 * c] = y[:, :HEAD_DIM].astype(BF16)
            k_ref[bb, 2 * c + 1] = y[:, HEAD_DIM:].astype(BF16)
            o += LANES
        for c in range(KV_WIDTH // LANES):
            y = z[:, o:o + LANES]
            ones = jnp.ones((y.shape[0], LANES - HEAD_DIM), F32)
            v_ref[bb, 2 * c] = jnp.concatenate([y[:, :HEAD_DIM], ones], axis=1).astype(BF16)
            v_ref[bb, 2 * c + 1] = jnp.concatenate([y[:, HEAD_DIM:], ones], axis=1).astype(BF16)
            o += LANES
        lx_ref[bb] = z[:, o:o + LRU_WIDTH]
        o += LRU_WIDTH
        lg_ref[bb] = z[:, o:o + LRU_WIDTH]


def _project(x, mods, w_a, bd, qg, kg, cos_t, sin_t, nct, ctx=None):
    b, _, d = x.shape
    l = x.shape[1] if ctx is None else x.shape[1] + ctx.shape[1]
    nt = l // ROW_TILE
    wa = w_a.shape[1]
    bb = BATCH_BLOCK
    tok = lambda w: pl.BlockSpec((bb, ROW_TILE, w), lambda bi, i: (bi, i, 0))
    head = lambda n, w=HEAD_DIM: pl.BlockSpec((bb, n, ROW_TILE, w), lambda bi, i: (bi, 0, i, 0))
    const = lambda shp: pl.BlockSpec(shp, lambda bi, i: (0,) * len(shp))
    if ctx is None:
        tokens, token_specs = [x], [tok(d)]
    else:
        tokens = [ctx, x]
        token_specs = [pl.BlockSpec((bb, ROW_TILE, d), lambda bi, i: (bi, jnp.minimum(i, nct - 1), 0)),
                       pl.BlockSpec((bb, ROW_TILE, d), lambda bi, i: (bi, jnp.maximum(i - nct, 0), 0))]
    out_specs = [tok(BRANCH_WIDTH), head(N_Q_HEADS), head(N_KV_HEADS), head(N_KV_HEADS, LANES),
                 tok(LRU_WIDTH), tok(LRU_WIDTH)]
    out_shape = [jax.ShapeDtypeStruct((b, l, BRANCH_WIDTH), F32),
                 jax.ShapeDtypeStruct((b, N_Q_HEADS, l, HEAD_DIM), BF16),
                 jax.ShapeDtypeStruct((b, N_KV_HEADS, l, HEAD_DIM), BF16),
                 jax.ShapeDtypeStruct((b, N_KV_HEADS, l, LANES), BF16),
                 jax.ShapeDtypeStruct((b, l, LRU_WIDTH), F32),
                 jax.ShapeDtypeStruct((b, l, LRU_WIDTH), F32)]
    if ctx is not None:
        out_specs.append(tok(d))
        out_shape.append(jax.ShapeDtypeStruct((b, l, d), F32))
    return pl.pallas_call(
        functools.partial(_proj_kernel, nct=nct, joined=ctx is None),
        grid=(b // bb, nt),
        in_specs=token_specs + [
            pl.BlockSpec((bb, 1, SUBLANES, d), lambda bi, i: (bi, (i >= nct).astype(jnp.int32), 0, 0)),
            const((d, wa)), const((LANES, LANES)), const((1, LANES)), const((1, LANES)),
            pl.BlockSpec((ROW_TILE, LANES), lambda bi, i: (i, 0)),
            pl.BlockSpec((ROW_TILE, LANES), lambda bi, i: (i, 0))],
        out_specs=out_specs,
        out_shape=out_shape,
        compiler_params=_cparams("parallel", "parallel"),
        name="in_proj",
    )(*tokens, mods, w_a, bd, qg, kg, cos_t, sin_t)


def _softmax_pv(q, k, v):
    hd = q.shape[1]
    s = lax.dot_general(q, k, (((1,), (1,)), ((), ())), preferred_element_type=F32)
    m = jnp.max(s, axis=1, keepdims=True)
    p = jnp.exp2(s - m).astype(BF16)
    r = jnp.dot(p, v, preferred_element_type=F32)
    return r[:, :hd] / r[:, hd:2 * hd]


def _attn_kernel(q_ref, k_ref, v_ref, *rest, n_ctx_tiles, ctx_len, n_step_casts, n_slab_casts):
    n_casts = n_step_casts + n_slab_casts
    cast_in, o_ref, cast_out = rest[:n_casts], rest[n_casts], rest[n_casts + 1:]
    for src, dst in zip(cast_in[:n_step_casts], cast_out[:n_step_casts]):
        dst[...] = src[...].astype(dst.dtype)

    @pl.when(pl.program_id(1) == 0)
    def _():
        for src, dst in zip(cast_in[n_step_casts:], cast_out[n_step_casts:]):
            dst[...] = src[...].astype(dst.dtype)

    nh, tq, hd = q_ref.shape[1:]
    group = nh // k_ref.shape[1]

    def run(lk):
        for j in range(nh):
            for r0 in range(0, tq, ATTN_CHAIN):
                o = _softmax_pv(q_ref[0, j, r0:r0 + ATTN_CHAIN, :], k_ref[0, j // group, :lk],
                                v_ref[0, j // group, :lk])
                o_ref[0, r0:r0 + ATTN_CHAIN, j * hd:(j + 1) * hd] = o.astype(o_ref.dtype)

    if n_ctx_tiles:
        @pl.when(pl.program_id(1) < n_ctx_tiles)
        def _():
            run(ctx_len)

        @pl.when(pl.program_id(1) >= n_ctx_tiles)
        def _():
            run(k_ref.shape[2])
    else:
        run(k_ref.shape[2])


def _attention(q, k, v, ctx_len, with_ctx, step_casts=(), slab_casts=()):
    b, nh, l, hd = q.shape
    nct = ctx_len // Q_TILE
    rows = l if with_ctx else l - ctx_len
    q_off = 0 if with_ctx else nct
    nq = rows // Q_TILE
    cast_specs, cast_shapes = [], []
    for w in step_casts:
        assert w.shape[0] == b and w.shape[1] % nq == 0
        cast_specs.append(pl.BlockSpec((1, w.shape[1] // nq, w.shape[2]), lambda bi, i: (bi, i, 0)))
        cast_shapes.append(jax.ShapeDtypeStruct(w.shape, BF16))
    for w in slab_casts:
        assert w.shape[0] % b == 0
        cast_specs.append(pl.BlockSpec((w.shape[0] // b, w.shape[1]), lambda bi, i: (bi, 0)))
        cast_shapes.append(jax.ShapeDtypeStruct(w.shape, BF16))
    return pl.pallas_call(
        functools.partial(_attn_kernel, n_ctx_tiles=nct if with_ctx else 0, ctx_len=ctx_len,
                          n_step_casts=len(step_casts), n_slab_casts=len(slab_casts)),
        grid=(b, nq),
        in_specs=[pl.BlockSpec((1, nh, Q_TILE, hd), lambda bi, i: (bi, 0, i + q_off, 0)),
                  pl.BlockSpec((1,) + k.shape[1:], lambda bi, i: (bi, 0, 0, 0)),
                  pl.BlockSpec((1,) + v.shape[1:], lambda bi, i: (bi, 0, 0, 0))] + cast_specs,
        out_specs=[pl.BlockSpec((1, Q_TILE, nh * hd), lambda bi, i: (bi, i, 0))] + cast_specs,
        out_shape=[jax.ShapeDtypeStruct((b, rows, nh * hd), BF16)] + cast_shapes,
        compiler_params=_cparams("parallel", "arbitrary"),
        name="attention",
    )(q, k, v, *step_casts, *slab_casts)


def _fill_ext(ext_ref, main, prev, nxt, first, last):
    t = main.shape[0]
    ext_ref[0:HALO] = jnp.where(first, 0.0, prev)
    ext_ref[HALO:HALO + t] = main
    ext_ref[HALO + t:2 * HALO + t] = jnp.where(last, 0.0, nxt)


def _seg_flags(tile, nct, nt):
    first = jnp.logical_or(tile == 0, tile == nct)
    last = jnp.logical_or(tile == nct - 1, tile == nt - 1)
    return first, last


def _halo_specs(width, tile_of, l):
    rb = ROW_TILE // HALO
    last_blk = l // HALO - 1
    bb = BATCH_BLOCK
    main = pl.BlockSpec((bb, ROW_TILE, width), lambda bi, i: (bi, tile_of(i), 0))
    prev = pl.BlockSpec((bb, HALO, width), lambda bi, i: (bi, jnp.maximum(tile_of(i) * rb - 1, 0), 0))
    nxt = pl.BlockSpec((bb, HALO, width), lambda bi, i: (bi, jnp.minimum((tile_of(i) + 1) * rb, last_blk), 0))
    return main, prev, nxt


def _softplus(z):
    return jnp.maximum(z, 0.0) + jnp.log1p(jnp.exp(-jnp.abs(z)))


def _gelu_tanh(x):
    return 0.5 * x * (1.0 + jnp.tanh(0.7978845608028654 * (x + 0.044715 * (x * x * x))))


def _lru_kernel(*refs, reverse, nct, nt):
    if reverse:
        (lx_ref, lxp_ref, lxn_ref, w_ref, ba_ref, bx_ref, lam_ref, cw_ref, cb_ref, hf_ref, lg_ref,
         o_ref, ext_ref, a_ref, u_ref, h_ref) = refs
    else:
        (lx_ref, lxp_ref, lxn_ref, w_ref, ba_ref, bx_ref, lam_ref, cw_ref, cb_ref,
         o_ref, ext_ref, a_ref, u_ref, h_ref) = refs
    j = pl.program_id(1)
    tile = _lru_tile(j, reverse, nct, nt)
    t = lx_ref.shape[1]
    nbb = lx_ref.shape[0]

    @pl.when(j == 0)
    def _():
        h_ref[...] = jnp.zeros_like(h_ref)

    first, last = _seg_flags(tile, nct, nt)
    lo = CONV_WIDTH // 2
    half_log2_a = (-0.5 * RG_C * LOG2_E) * _softplus(-lam_ref[...])
    for bb in range(nbb):
        ext = ext_ref.at[bb]
        _fill_ext(ext, lx_ref[bb], lxp_ref[bb], lxn_ref[bb], first, last)
        xh = cb_ref[...]
        for kk in range(CONV_WIDTH):
            xh = xh + ext[pl.ds(HALO + kk - lo, t), :] * cw_ref[kk:kk + 1, :]
        zz = jnp.dot(xh.astype(BF16), w_ref[...], preferred_element_type=F32)
        tr = jnp.tanh(zz[:, :LRU_WIDTH] + ba_ref[...])
        ti = jnp.tanh(zz[:, LRU_WIDTH:] + bx_ref[...])
        a = jnp.exp2(half_log2_a * (tr + 1.0))
        a_ref[bb] = a
        u_ref[bb] = jnp.sqrt(1.0 - a * a) * (ti + 1.0) * xh

    row = lax.broadcasted_iota(jnp.int32, (SUBLANES, LRU_WIDTH), 0)
    n_sub = t // SUBLANES

    def sub(s, hs):
        blk = (n_sub - 1 - s) if reverse else s
        r0 = pl.multiple_of(blk * SUBLANES, SUBLANES)
        out = []
        for bb in range(nbb):
            aa = a_ref[bb, pl.ds(r0, SUBLANES), :]
            uu = u_ref[bb, pl.ds(r0, SUBLANES), :]
            for dd in (1, 2, 4):
                if reverse:
                    keep = row < SUBLANES - dd
                    sh = SUBLANES - dd
                else:
                    keep = row >= dd
                    sh = dd
                a_sh = jnp.where(keep, pltpu.roll(aa, sh, 0), 1.0)
                u_sh = jnp.where(keep, pltpu.roll(uu, sh, 0), 0.0)
                uu = aa * u_sh + uu
                aa = aa * a_sh
            hh = aa * hs[bb] + uu
            if reverse:
                o_ref[bb, pl.ds(r0, SUBLANES), :] = (
                    (hf_ref[bb, pl.ds(r0, SUBLANES), :] + hh)
                    * _gelu_tanh(lg_ref[bb, pl.ds(r0, SUBLANES), :])).astype(o_ref.dtype)
                edge = hh[0:1]
            else:
                o_ref[bb, pl.ds(r0, SUBLANES), :] = hh
                edge = hh[SUBLANES - 1:SUBLANES]
            out.append(jnp.broadcast_to(edge, (SUBLANES, LRU_WIDTH)))
        return tuple(out)

    hs = lax.fori_loop(0, n_sub, sub, tuple(h_ref[bb] for bb in range(nbb)), unroll=4)
    for bb in range(nbb):
        h_ref[bb] = hs[bb]


def _lru_tile(j, reverse, nct, nt):
    if not reverse:
        return j
    return jnp.where(j < nct, nct - 1 - j, nt - 1 - (j - nct))


def _lru(lx, w, ba, bx, lam, cw, cb, nct, reverse, hf=None, lg=None):
    b, l, wd = lx.shape
    nt = l // ROW_TILE
    tile_of = lambda i: _lru_tile(i, reverse, nct, nt)
    main, prev, nxt = _halo_specs(wd, tile_of, l)
    const = lambda shp: pl.BlockSpec(shp, lambda bi, i: (0,) * len(shp))
    in_specs = [main, prev, nxt, const(w.shape), const((1, wd)), const((1, wd)), const((1, wd)),
                const((SUBLANES, wd)), const((1, wd))]
    args = [lx, lx, lx, w, ba, bx, lam, cw, cb]
    if reverse:
        in_specs += [main, main]
        args += [hf, lg]
    bb = BATCH_BLOCK
    return pl.pallas_call(
        functools.partial(_lru_kernel, reverse=reverse, nct=nct, nt=nt),
        grid=(b // bb, nt),
        in_specs=in_specs,
        out_specs=main,
        out_shape=jax.ShapeDtypeStruct((b, l, wd), BF16 if reverse else F32),
        scratch_shapes=[pltpu.VMEM((bb, ROW_TILE + 2 * HALO, wd), F32), pltpu.VMEM((bb, ROW_TILE, wd), F32),
                        pltpu.VMEM((bb, ROW_TILE, wd), F32), pltpu.VMEM((bb, SUBLANES, wd), F32)],
        compiler_params=_cparams("parallel", "arbitrary"),
        name="lru_bwd" if reverse else "lru_fwd",
    )(*args)


def _merge_kernel(x_ref, mod_ref, p_ref, pp_ref, pn_ref, at_ref, lr_ref, wg_ref, bm_ref, pw_ref, ps_ref,
                  wb_ref, wo_ref, g_ref, b_ref, *rest, nct, nt, tile_off, ctx_len, alpha, route):
    if route:
        rw_ref, rb_ref, x1_ref, h2_ref, logit_ref, ext_ref = rest
    else:
        x1_ref, h2_ref, ext_ref = rest
    tile = pl.program_id(1) + tile_off
    t = x_ref.shape[1]
    d = x_ref.shape[2]
    first, last = _seg_flags(tile, nct, nt)
    in_ctx = tile < nct
    seg_len = jnp.where(in_ctx, ctx_len, nt * t - ctx_len)
    pos = (tile - jnp.where(in_ctx, 0, nct)) * t + lax.broadcasted_iota(jnp.int32, (t, 1), 0)

    for bb in range(x_ref.shape[0]):
        x = x_ref[bb]
        m = mod_ref[bb, 0]
        h_half = ((x * (1.0 + m[1:2]) + m[0:1]) * 0.5).astype(BF16)

        def branch(n, ys):
            tg = jnp.tanh(jnp.dot(h_half, wg_ref[:, n * d:(n + 1) * d], preferred_element_type=F32)
                          + bm_ref[n:n + 1, :])
            return (tg + 1.0) * jnp.dot(ys, wb_ref[n], preferred_element_type=F32)

        mix = branch(1, at_ref[bb]) + branch(2, lr_ref[bb])

        ext = ext_ref.at[bb]
        _fill_ext(ext, p_ref[bb], pp_ref[bb], pn_ref[bb], first, last)
        pooled = []
        for gi, w in enumerate(POOL_WINDOWS):
            lo = w // 2
            cols = slice(gi * POOL_GROUP, (gi + 1) * POOL_GROUP)
            acc = ext[pl.ds(HALO - lo, t), cols]
            for kk in range(1, w):
                acc = acc + ext[pl.ds(HALO - lo + kk, t), cols]
            cnt = jnp.clip(pos - lo + w, 0, seg_len) - jnp.clip(pos - lo, 0, seg_len)
            mean = acc / cnt.astype(F32)
            dlt = (mean - ext[pl.ds(HALO, t), cols]).astype(BF16)
            pooled.append(jnp.dot(dlt, pw_ref[gi], preferred_element_type=F32))
        pool_y = (jnp.concatenate(pooled, axis=1) * ps_ref[...]).astype(BF16)
        mix = branch(0, pool_y) + mix
        y2 = jnp.dot(mix.astype(BF16), wo_ref[...], preferred_element_type=F32)
        x1 = _layer_norm(alpha * x + m[6:7] * y2, g_ref[...], b_ref[...])
        x1_ref[bb] = x1
        h2 = x1 * (1.0 + m[4:5]) + m[3:4]
        if route:
            hi = h2.astype(BF16)
            mid = (h2 - hi.astype(F32)).astype(BF16)
            logit_ref[bb] = (jnp.dot(hi, rw_ref[0], preferred_element_type=F32)
                             + jnp.dot(mid, rw_ref[0], preferred_element_type=F32)
                             + jnp.dot(hi, rw_ref[1], preferred_element_type=F32) + rb_ref[...])
            _store_row_tiles(h2_ref.at[bb], h2)
        else:
            h2_ref[bb] = h2.astype(h2_ref.dtype)


def _merge(x, mods, p, attn, lru, w_gate, b_merge, pool_w, pool_scale, w_branch, w_out, ln_g, ln_b,
           nct, lat_only, alpha, router=None):
    b, l, d = x.shape
    nt = l // ROW_TILE
    off = nct if lat_only else 0
    rows = l - off * ROW_TILE
    tile_of = lambda i: i + off
    bb = BATCH_BLOCK
    tok_l = lambda w: pl.BlockSpec((bb, ROW_TILE, w), lambda bi, i: (bi, i + off, 0))
    tok_o = lambda w: pl.BlockSpec((bb, ROW_TILE, w), lambda bi, i: (bi, i, 0))
    const = lambda shp: pl.BlockSpec(shp, lambda bi, i: (0,) * len(shp))
    pm, pp, pn = _halo_specs(BRANCH_WIDTH, tile_of, l)
    in_specs = [tok_l(d),
                pl.BlockSpec((bb, 1, SUBLANES, d), lambda bi, i: (bi, (i + off >= nct).astype(jnp.int32), 0, 0)),
                pm, pp, pn,
                tok_o(Q_WIDTH) if lat_only else tok_l(Q_WIDTH),
                tok_l(LRU_WIDTH),
                const(w_gate.shape), const(b_merge.shape), const(pool_w.shape), const(pool_scale.shape),
                const(w_branch.shape), const(w_out.shape), const((1, d)), const((1, d))]
    args = [x, mods, p, p, p, attn, lru, w_gate, b_merge, pool_w, pool_scale, w_branch, w_out, ln_g, ln_b]
    out_specs = [tok_o(d)]
    out_shape = [jax.ShapeDtypeStruct((b, rows, d), F32)]
    if router is None:
        out_specs.append(tok_o(d))
        out_shape.append(jax.ShapeDtypeStruct((b, rows, d), BF16))
    else:
        n_sub = d // LANES
        in_specs += [const(router[0].shape), const(router[1].shape)]
        args += list(router)
        out_specs += [pl.BlockSpec((bb, ROW_TILE * n_sub, LANES), lambda bi, i: (bi, i, 0)), tok_o(LANES)]
        out_shape += [jax.ShapeDtypeStruct((b, rows * n_sub, LANES), F32),
                      jax.ShapeDtypeStruct((b, rows, LANES), F32)]
    return pl.pallas_call(
        functools.partial(_merge_kernel, nct=nct, nt=nt, tile_off=off, ctx_len=nct * ROW_TILE, alpha=alpha,
                          route=router is not None),
        grid=(b // bb, rows // ROW_TILE),
        in_specs=in_specs,
        out_specs=out_specs,
        out_shape=out_shape,
        scratch_shapes=[pltpu.VMEM((bb, ROW_TILE + 2 * HALO, BRANCH_WIDTH), F32)],
        compiler_params=_cparams("parallel", "parallel"),
        name="merge_ln1",
    )(*args)


def _swiglu_kernel(x_ref, wg_ref, wu_ref, wd_ref, x1_ref, *rest, alpha):
    mod_refs, (g_ref, b_ref, o_ref) = rest[:-3], rest[-3:]
    x = x_ref[...]
    g = jnp.dot(x, wg_ref[...], preferred_element_type=F32)
    u = jnp.dot(x, wu_ref[...], preferred_element_type=F32)
    ff = jnp.dot((_silu(g) * u).astype(BF16), wd_ref[...], preferred_element_type=F32)
    for hh, mod_ref in enumerate(mod_refs):
        rs = slice(hh * ROW_TILE, (hh + 1) * ROW_TILE)
        o_ref[rs, :] = _layer_norm(alpha * x1_ref[rs, :] + mod_ref[0, 0, 5:6, :] * ff[rs], g_ref[...], b_ref[...])


def _swiglu_dense_ln(h2, x1, mods, ln_g, ln_b, w_gate, w_up, w_down, nct, alpha):
    b, l, d = x1.shape
    r = b * l
    ntl = l // ROW_TILE
    halves = FFN_ROWS // ROW_TILE

    def mod_spec(hh):
        def idx(i):
            tile = i * halves + hh
            return (tile // ntl, (tile % ntl >= nct).astype(jnp.int32), 0, 0)
        return pl.BlockSpec((1, 1, SUBLANES, d), idx)

    resident = lambda shp: pl.BlockSpec(shp, lambda i: (0, 0), pipeline_mode=pl.Buffered(1))
    out = pl.pallas_call(
        functools.partial(_swiglu_kernel, alpha=alpha),
        grid=(r // FFN_ROWS,),
        in_specs=[pl.BlockSpec((FFN_ROWS, d), lambda i: (i, 0)),
                  resident(w_gate.shape), resident(w_up.shape), resident(w_down.shape),
                  pl.BlockSpec((FFN_ROWS, d), lambda i: (i, 0))]
                 + [mod_spec(hh) for hh in range(halves)]
                 + [pl.BlockSpec((1, d), lambda i: (0, 0)), pl.BlockSpec((1, d), lambda i: (0, 0))],
        out_specs=pl.BlockSpec((FFN_ROWS, d), lambda i: (i, 0)),
        out_shape=jax.ShapeDtypeStruct((r, d), F32),
        compiler_params=_cparams("parallel"),
        name="swiglu_dense_ln2",
    )(h2.reshape(r, d), w_gate, w_up, w_down, x1.reshape(r, d), *([mods] * halves), ln_g, ln_b)
    return out.reshape(b, l, d)


def _moe_kernel(be_ref, nu_ref, tok_ref, dst_ref, t_hbm, wg_ref, wu_ref, wd_ref, y_hbm,
                xbuf, acc_ref, obuf, sem_in, sem_out, sem_fill):
    i = pl.program_id(0)
    f = pl.program_id(1)
    nb = pl.num_programs(0)
    nu = nu_ref[0]
    rows = acc_ref.shape[0]
    sub = xbuf.shape[1] // rows
    slot = i % 2

    def tile_rows(start):
        return pl.ds(pl.multiple_of(start, sub), sub)

    def gather_copy(blk, sl, r):
        return pltpu.make_async_copy(t_hbm.at[tile_rows(tok_ref[blk * rows + r])], xbuf.at[sl, tile_rows(r * sub)],
                                     sem_in.at[sl])

    def scatter_copy(blk, sl, r):
        return pltpu.make_async_copy(obuf.at[sl, tile_rows(r * sub)], y_hbm.at[tile_rows(dst_ref[blk * rows + r])],
                                     sem_out.at[sl])

    def gathers_done(sl):
        pltpu.make_async_copy(t_hbm.at[pl.ds(0, rows * sub)], xbuf.at[sl], sem_in.at[sl]).wait()

    def scatters_done(sl):
        pltpu.make_async_copy(obuf.at[sl], y_hbm.at[pl.ds(0, rows * sub)], sem_out.at[sl]).wait()

    def for_rows(fn):
        def body(r, c):
            fn(r)
            return c
        lax.fori_loop(0, rows, body, 0, unroll=8)

    def load_x():
        return _load_row_tiles(xbuf.at[slot], rows).astype(BF16)

    def swiglu_part(x):
        g = jnp.dot(x, wg_ref[0], preferred_element_type=F32)
        u = jnp.dot(x, wu_ref[0], preferred_element_type=F32)
        return jnp.dot((_silu(g) * u).astype(BF16), wd_ref[0], preferred_element_type=F32)

    @pl.when(jnp.logical_and(i == 0, f == 0))
    def _():
        obuf[1] = jnp.zeros(obuf.shape[1:], obuf.dtype)
        for_rows(lambda r: gather_copy(0, 0, r).start())

    @pl.when(jnp.logical_and(f == 0, i <= nu))
    def _():
        gathers_done(slot)

    @pl.when(jnp.logical_and(i < nu, f == 0))
    def _():
        x = load_x()
        for r in range(rows):
            gather_copy(i + 1, 1 - slot, r).start()
        acc_ref[...] = swiglu_part(x)

    @pl.when(jnp.logical_and(i < nu, f == 1))
    def _():
        @pl.when(i >= 1)
        def _():
            scatters_done(slot)

        x = load_x()
        prev = jnp.where(i >= 1, i - 1, nb)
        for r in range(rows):
            scatter_copy(prev, 1 - slot, r).start()
        _store_row_tiles(obuf.at[slot], acc_ref[...] + swiglu_part(x))

        @pl.when(i == nu - 1)
        def _():
            scatters_done(1 - slot)
            for_rows(lambda r: scatter_copy(i, slot, r).start())
            scatters_done(slot)

    @pl.when(jnp.logical_and(i >= nu, f == 1))
    def _():
        xbuf[0] = jnp.zeros(xbuf.shape[1:], xbuf.dtype)
        fill = pltpu.make_async_copy(xbuf.at[0], y_hbm.at[pl.ds(i * rows * sub, rows * sub)], sem_fill)
        fill.start()
        fill.wait()

    @pl.when(jnp.logical_and(jnp.logical_and(i == nb - 1, f == 1), nu == nb))
    def _():
        gathers_done(nb % 2)


def _swiglu_experts(t, block_e, n_used, slot_tok, slot_dst, w_gate, w_up, w_down, tf):
    d = w_gate.shape[1]
    sub = d // LANES
    p = slot_tok.shape[0]
    nf = w_gate.shape[2] // tf
    assert nf == 2

    def fcol(i, f, nu):
        return jnp.where(i < nu[0], f, nf - 1)

    return pl.pallas_call(
        _moe_kernel,
        grid_spec=pltpu.PrefetchScalarGridSpec(
            num_scalar_prefetch=4,
            grid=(p // FFN_ROWS - 1, nf),
            in_specs=[pl.BlockSpec(memory_space=pl.ANY),
                      pl.BlockSpec((1, d, tf), lambda i, f, be, nu, st, sd: (be[i], 0, fcol(i, f, nu))),
                      pl.BlockSpec((1, d, tf), lambda i, f, be, nu, st, sd: (be[i], 0, fcol(i, f, nu))),
                      pl.BlockSpec((1, tf, d), lambda i, f, be, nu, st, sd: (be[i], fcol(i, f, nu), 0))],
            out_specs=pl.BlockSpec(memory_space=pl.ANY),
            scratch_shapes=[pltpu.VMEM((2, FFN_ROWS * sub, LANES), F32), pltpu.VMEM((FFN_ROWS, d), F32),
                            pltpu.VMEM((2, FFN_ROWS * sub, LANES), F32),
                            pltpu.SemaphoreType.DMA((2,)), pltpu.SemaphoreType.DMA((2,)),
                            pltpu.SemaphoreType.DMA]),
        out_shape=jax.ShapeDtypeStruct((p * sub, LANES), F32),
        compiler_params=_cparams("arbitrary", "arbitrary"),
        name="swiglu_experts",
    )(block_e, n_used, slot_tok, slot_dst, t, w_gate, w_up, w_down)


def _combine_ln_kernel(x_ref, mod_ref, g_ref, b_ref, *rest, alpha):
    y_refs, p_ref, o_ref = rest[:-2], rest[-2], rest[-1]
    t = x_ref.shape[1]
    top_k = p_ref.shape[2]
    for bb in range(x_ref.shape[0]):
        f = None
        for kk in range(top_k):
            term = _load_row_tiles(y_refs[bb * top_k + kk], t) * p_ref[bb, :, kk:kk + 1]
            f = term if f is None else f + term
        m = mod_ref[bb, 0]
        o_ref[bb] = _layer_norm(alpha * x_ref[bb] + m[5:6] * f, g_ref[...], b_ref[...])


def _combine_ln(x1, mods, ln_g, ln_b, y, probs, alpha):
    b, rows, d = x1.shape
    nt = rows // ROW_TILE
    sub = d // LANES
    top_k = probs.shape[2]
    nbb = BATCH_BLOCK
    tok = lambda w: pl.BlockSpec((nbb, ROW_TILE, w), lambda bi, i: (bi, i, 0))
    const = lambda shp: pl.BlockSpec(shp, lambda bi, i: (0,) * len(shp))
    y_specs = [pl.BlockSpec((ROW_TILE * sub, LANES),
                            lambda bi, i, kk=kk, bb=bb: ((kk * b + bi * nbb + bb) * nt + i, 0))
               for bb in range(nbb) for kk in range(top_k)]
    return pl.pallas_call(
        functools.partial(_combine_ln_kernel, alpha=alpha),
        grid=(b // nbb, nt),
        in_specs=[tok(d), pl.BlockSpec((nbb, 1, SUBLANES, d), lambda bi, i: (bi, 1, 0, 0)),
                  const((1, d)), const((1, d))] + y_specs + [tok(top_k)],
        out_specs=tok(d),
        out_shape=jax.ShapeDtypeStruct((b, rows, d), F32),
        compiler_params=_cparams("parallel", "parallel"),
        name="combine_ln2",
    )(x1, mods, ln_g, ln_b, *([y] * (top_k * nbb)), probs)


def _moe(t, logits, w_gate, w_up, w_down, tf):
    n, n_e = logits.shape
    sub = t.shape[0] // n
    top_v, top_i = lax.top_k(logits, TOP_K)
    probs = jax.nn.softmax(top_v, axis=-1)
    m = n * TOP_K
    flat_e = top_i.reshape(m)
    counts = jnp.sum((flat_e[:, None] == jnp.arange(n_e, dtype=flat_e.dtype)[None, :]).astype(jnp.int32), axis=0)
    order = jnp.argsort(flat_e, stable=True).astype(jnp.int32)
    start = jnp.cumsum(counts) - counts
    padded = (counts + FFN_ROWS - 1) // FFN_ROWS * FFN_ROWS
    ends_p = jnp.cumsum(padded)
    start_p = ends_p - padded
    nb = -(-(m + n_e * (FFN_ROWS - 1)) // FFN_ROWS)
    p = (nb + 1) * FFN_ROWS
    block_e = jnp.minimum(jnp.searchsorted(ends_p, jnp.arange(nb + 1) * FFN_ROWS, side='right'),
                          n_e - 1).astype(jnp.int32)
    n_used = (ends_p[-1:] // FFN_ROWS).astype(jnp.int32)
    slot_e = jnp.repeat(block_e, FFN_ROWS)
    r = jnp.arange(p, dtype=jnp.int32) - start_p[slot_e]
    is_pad = r >= counts[slot_e]
    slot_pair = order[jnp.clip(start[slot_e] + r, 0, m - 1)]
    slot_tok = jnp.where(is_pad, 0, slot_pair // TOP_K) * sub
    pads_before = (start_p - start)[slot_e] + r - counts[slot_e]
    slot_dst = jnp.where(is_pad, m + pads_before, (slot_pair % TOP_K) * n + slot_pair // TOP_K) * sub
    y = _swiglu_experts(t, block_e, n_used, slot_tok, slot_dst, w_gate, w_up, w_down, tf)
    return y, probs


def _block_diag(w):
    nb, bi, bj = w.shape
    eye = jnp.eye(nb, dtype=w.dtype)
    return (eye[:, None, :, None] * w[:, :, None, :]).reshape(nb * bi, nb * bj)


def _rope_tables(ctx_len, s):
    nf = HEAD_DIM // 4
    inv = ROPE_THETA ** (-jnp.arange(nf, dtype=F32) / nf)
    t = jnp.arange(s)
    row = (t // GRID_W).astype(F32)[:, None] * inv
    col = (t % GRID_W).astype(F32)[:, None] * inv
    ang = jnp.concatenate([row, row, col, col], axis=1)
    sign = jnp.tile(jnp.concatenate([-jnp.ones((nf,), F32), jnp.ones((nf,), F32)]), 2)
    cos = jnp.concatenate([jnp.ones((ctx_len, HEAD_DIM), F32), jnp.cos(ang)], axis=0)
    sin = jnp.concatenate([jnp.zeros((ctx_len, HEAD_DIM), F32), jnp.sin(ang) * sign], axis=0)
    reps = LANES // HEAD_DIM
    return jnp.tile(cos, (1, reps)), jnp.tile(sin, (1, reps))


def _pad_rows(a, rows):
    return jnp.zeros((rows,) + a.shape[1:], a.dtype).at[:a.shape[0]].set(a)


def kernel(x, c, ctx, c_ctx, w_mod, b_mod, w_in, b_merge, pool_w, pool_scale, q_norm, k_norm, conv_w, conv_b, lru_wa, lru_ba, lru_wx, lru_bx, lru_lambda, w_branch, w_out, ln1_g, ln1_b, ffn_w_gate, ffn_w_up, ffn_w_down, moe_router, moe_router_b, moe_w_gate, moe_w_up, moe_w_down, ln2_g, ln2_b):
    b, s, d = x.shape
    ctx_len = ctx.shape[1]
    depth = w_in.shape[0]
    l = ctx_len + s
    assert ctx_len % ROW_TILE == 0 and s % ROW_TILE == 0 and (b * s) % FFN_ROWS == 0 and b % BATCH_BLOCK == 0
    assert (b * l) % FFN_ROWS == 0 and ctx_len % Q_TILE == 0 and s % Q_TILE == 0
    nct = ctx_len // ROW_TILE
    alpha = (2 * depth) ** 0.25

    cc = _pad_rows(jnp.concatenate([c, c_ctx[None, :]], axis=0), -(-(b + 1) // SUBLANES) * SUBLANES)
    mod_all = _modulation(cc, w_mod, b_mod)

    cos_t, sin_t = _rope_tables(ctx_len, s)
    bd = _block_diag(jnp.full((LANES // HEAD_DIM, HEAD_DIM, HEAD_DIM), 1.0 / HEAD_DIM, F32)).astype(BF16)
    reps = LANES // HEAD_DIM

    xs = None
    w_in_bf = w_in[0].astype(BF16)
    for li in range(depth):
        last = li == depth - 1
        dense = li % 2 == 0
        jf = li // 2
        ml = jnp.transpose(mod_all[li, :, :b], (1, 0, 2))
        mc = jnp.broadcast_to(mod_all[li, :, b][None], (b, 6, d))
        mods = jnp.stack([mc, ml], axis=1)
        mods = jnp.concatenate([mods, 0.5 * mods[:, :, 2:3], jnp.zeros((b, 2, 1, d), F32)], axis=2)

        proj_args = (mods, w_in_bf[:, :OFF_GATE], bd, jnp.tile(q_norm[li], reps)[None],
                     jnp.tile(k_norm[li], reps)[None], cos_t, sin_t, nct)
        if xs is None:
            p, q, k, v, lx, lg, xs = _project(x, *proj_args, ctx=ctx)
        else:
            p, q, k, v, lx, lg = _project(xs, *proj_args)

        slab = [w_branch[li].reshape(-1, d), w_out[li]]
        if dense:
            slab += [ffn_w_gate[jf], ffn_w_up[jf], ffn_w_down[jf]]
            step = []
        else:
            step = [moe_w_gate[jf], moe_w_up[jf], moe_w_down[jf]]
        if not last:
            slab.append(w_in[li + 1])
        if not all(w.shape[0] == b and w.shape[1] % ((l - (ctx_len if last else 0)) // Q_TILE * 16) == 0
                   for w in step):
            step_bf, step = [w.astype(BF16) for w in step], []
        attn, *made = _attention(q, k, v, ctx_len, with_ctx=not last, step_casts=step, slab_casts=slab)
        if step:
            step_bf = made[:len(step)]
        slab_bf = made[len(step):]
        wb_bf, wo_bf = slab_bf[0].reshape(w_branch[li].shape), slab_bf[1]

        cw = _pad_rows(0.5 * conv_w[li], SUBLANES)
        cb = 0.5 * conv_b[li][None]
        lru_w = [jnp.concatenate([_block_diag(lru_wa[li, dr]), _block_diag(lru_wx[li, dr])], axis=1).astype(BF16)
                 for dr in range(2)]
        hf = _lru(lx, lru_w[0], 0.5 * lru_ba[li, 0][None], 0.5 * lru_bx[li, 0][None], lru_lambda[li, 0][None],
                  cw, cb, nct, reverse=False)
        lru_y = _lru(lx, lru_w[1], 0.5 * lru_ba[li, 1][None], 0.5 * lru_bx[li, 1][None], lru_lambda[li, 1][None],
                     cw, cb, nct, reverse=True, hf=hf, lg=lg)

        merge_args = (xs, mods, p, attn, lru_y, w_in_bf[:, OFF_GATE:], _pad_rows(0.5 * b_merge[li], SUBLANES),
                      pool_w[li].astype(BF16), pool_scale[li][None], wb_bf, wo_bf, ln1_g[li][None], ln1_b[li][None])
        if dense:
            assert not last, "the dense layer is expected to carry the context tokens along"
            x1, h2 = _merge(*merge_args, nct, lat_only=False, alpha=alpha)
            xs = _swiglu_dense_ln(h2, x1, mods, ln2_g[li][None], ln2_b[li][None], *slab_bf[2:5],
                                  nct=nct, alpha=alpha)
        else:
            assert last, "the expert layer is expected to be the last layer (latent tokens only)"
            n_e = moe_router.shape[2]
            rw = jnp.zeros((d, LANES), F32).at[:, :n_e].set(moe_router[jf])
            rb = jnp.zeros((1, LANES), F32).at[0, :n_e].set(moe_router_b[jf])
            rw_hi = rw.astype(BF16)
            rw_split = jnp.stack([rw_hi, (rw - rw_hi.astype(F32)).astype(BF16)])
            x1, h2, logits = _merge(*merge_args, nct, lat_only=True, alpha=alpha, router=(rw_split, rb))
            y, probs = _moe(h2.reshape(-1, LANES), logits.reshape(b * s, LANES)[:, :n_e], *step_bf,
                            tf=moe_w_gate.shape[3] // 2)
            xs = _combine_ln(x1, mods, ln2_g[li][None], ln2_b[li][None], y, probs.reshape(b, s, -1), alpha)
        if not last:
            w_in_bf = slab_bf[-1]
    return xs
```

```python
import functools

import jax
import jax.numpy as jnp
from jax import lax
from jax.experimental import pallas as pl
from jax.experimental.pallas import tpu as pltpu

F32 = jnp.float32
BF16 = jnp.bfloat16

GRID_W = 64
POOL_WINDOWS = (2, 4, 8, 16)
BRANCH_WIDTH = 512
POOL_GROUP = BRANCH_WIDTH // len(POOL_WINDOWS)
N_Q_HEADS = 8
N_KV_HEADS = 2
HEAD_DIM = 64
GQA_GROUP = N_Q_HEADS // N_KV_HEADS
Q_WIDTH = N_Q_HEADS * HEAD_DIM
KV_WIDTH = N_KV_HEADS * HEAD_DIM
ROPE_THETA = 10000.0
ATTN_SCALE = HEAD_DIM ** -0.5
LOG2_E = 1.4426950408889634
LRU_WIDTH = BRANCH_WIDTH
LRU_BLOCKS = 8
CONV_WIDTH = 4
RG_C = 8.0
N_BRANCH = 3
N_EXPERTS = 8
TOP_K = 2
LN_EPS = 1e-5
RMS_EPS = 1e-6
OFF_GATE = BRANCH_WIDTH + Q_WIDTH + 2 * KV_WIDTH + 2 * LRU_WIDTH

LANES = 128
SUBLANES = 8
ROW_TILE = 256
BATCH_BLOCK = 2
LIGHT_BATCH_BLOCK = 4
Q_TILE = 256
ATTN_CHAIN = 256
FFN_ROWS = 512
HALO = SUBLANES
VMEM_LIMIT = 56 * 1024 * 1024


def _cparams(*sem):
    return pltpu.CompilerParams(dimension_semantics=sem, vmem_limit_bytes=VMEM_LIMIT)


def _sigmoid(x):
    return 0.5 * jnp.tanh(0.5 * x) + 0.5


def _store_row_tiles(ref, val):
    t, d = val.shape
    for j in range(d // LANES):
        ref[pl.ds(j, t, stride=d // LANES), :] = val[:, j * LANES:(j + 1) * LANES]


def _load_row_tiles(ref, t):
    n = ref.shape[0] // t
    return jnp.concatenate([ref[pl.ds(j, t, stride=n), :] for j in range(n)], axis=1)


def _silu(x):
    return x * _sigmoid(x)


def _layer_norm(v, g, b):
    mu = jnp.mean(v, axis=-1, keepdims=True)
    d = v - mu
    var = jnp.mean(d * d, axis=-1, keepdims=True)
    return d * lax.rsqrt(var + LN_EPS) * g + b


def _mod_kernel(c_ref, w_ref, b_ref, o_ref):
    s = _silu(c_ref[...])
    o_ref[0, 0] = jnp.dot(s, w_ref[0], precision=lax.Precision.HIGHEST,
                          preferred_element_type=F32) + b_ref[0, 0]


def _modulation(cc, w_mod, b_mod):
    depth, d, _ = w_mod.shape
    r = cc.shape[0]
    b6 = b_mod.reshape(depth, 6, 1, d)
    return pl.pallas_call(
        _mod_kernel,
        grid=(depth, 6),
        in_specs=[pl.BlockSpec((r, d), lambda l, j: (0, 0)),
                  pl.BlockSpec((1, d, d), lambda l, j: (l, 0, j)),
                  pl.BlockSpec((1, 1, 1, d), lambda l, j: (l, j, 0, 0))],
        out_specs=pl.BlockSpec((1, 1, r, d), lambda l, j: (l, j, 0, 0)),
        out_shape=jax.ShapeDtypeStruct((depth, 6, r, d), F32),
        compiler_params=_cparams("arbitrary", "arbitrary"),
        name="modulation",
    )(cc, w_mod, b6)


def _norm_rope(zc, bd, g, cos, sin):
    zz = zc * zc
    hi = zz.astype(BF16)
    lo = (zz - hi.astype(F32)).astype(BF16)
    ms = jnp.dot(hi, bd, preferred_element_type=F32) + jnp.dot(lo, bd, preferred_element_type=F32)
    y = zc * lax.rsqrt(ms + RMS_EPS) * g
    nf = HEAD_DIM // 4
    up = pltpu.roll(y, LANES - nf, 1)
    dn = pltpu.roll(y, nf, 1)
    lane = lax.broadcasted_iota(jnp.int32, y.shape, 1)
    partner = jnp.where((lane % (2 * nf)) < nf, up, dn)
    return y * cos + partner * sin


def _proj_kernel(*refs, nct, joined):
    if joined:
        x_ref, mod_ref, w_ref, bd_ref, qg_ref, kg_ref, cos_ref, sin_ref, p_ref, q_ref, k_ref, v_ref, lx_ref, lg_ref = refs
    else:
        (c_ref, x_ref, mod_ref, w_ref, bd_ref, qg_ref, kg_ref, cos_ref, sin_ref,
         p_ref, q_ref, k_ref, v_ref, lx_ref, lg_ref, xs_ref) = refs
    bd = bd_ref[...]
    cos = cos_ref[...]
    sin = sin_ref[...]
    for bb in range(x_ref.shape[0]):
        if joined:
            x = x_ref[bb]
        else:
            x = jnp.where(pl.program_id(1) < nct, c_ref[bb], x_ref[bb])
            xs_ref[bb] = x
        m = mod_ref[bb, 0]
        h = (x * (1.0 + m[1:2]) + m[0:1]).astype(BF16)
        z = jnp.dot(h, w_ref[...], preferred_element_type=F32)
        o = 0
        p_ref[bb] = z[:, o:o + BRANCH_WIDTH]
        o += BRANCH_WIDTH
        for c in range(Q_WIDTH // LANES):
            y = _norm_rope(z[:, o:o + LANES], bd, qg_ref[...], cos, sin) * (ATTN_SCALE * LOG2_E)
            q_ref[bb, 2 * c] = y[:, :HEAD_DIM].astype(BF16)
            q_ref[bb, 2 * c + 1] = y[:, HEAD_DIM:].astype(BF16)
            o += LANES
        for c in range(KV_WIDTH // LANES):
            y = _norm_rope(z[:, o:o + LANES], bd, kg_ref[...], cos, sin)
            k_ref[bb, 2 * c] = y[:, :HEAD_DIM].astype(BF16)
            k_ref[bb, 2 * c + 1] = y[:, HEAD_DIM:].astype(BF16)
            o += LANES
        for c in range(KV_WIDTH // LANES):
            y = z[:, o:o + LANES]
            ones = jnp.ones((y.shape[0], LANES - HEAD_DIM), F32)
            v_ref[bb, 2 * c] = jnp.concatenate([y[:, :HEAD_DIM], ones], axis=1).astype(BF16)
            v_ref[bb, 2 * c + 1] = jnp.concatenate([y[:, HEAD_DIM:], ones], axis=1).astype(BF16)
            o += LANES
        lx_ref[bb] = z[:, o:o + LRU_WIDTH]
        o += LRU_WIDTH
        lg_ref[bb] = z[:, o:o + LRU_WIDTH]


def _project(x, mods, w_a, bd, qg, kg, cos_t, sin_t, nct, ctx=None):
    b, _, d = x.shape
    l = x.shape[1] if ctx is None else x.shape[1] + ctx.shape[1]
    nt = l // ROW_TILE
    wa = w_a.shape[1]
    bb = LIGHT_BATCH_BLOCK
    tok = lambda w: pl.BlockSpec((bb, ROW_TILE, w), lambda bi, i: (bi, i, 0))
    head = lambda n, w=HEAD_DIM: pl.BlockSpec((bb, n, ROW_TILE, w), lambda bi, i: (bi, 0, i, 0))
    const = lambda shp: pl.BlockSpec(shp, lambda bi, i: (0,) * len(shp))
    if ctx is None:
        tokens, token_specs = [x], [tok(d)]
    else:
        tokens = [ctx, x]
        token_specs = [pl.BlockSpec((bb, ROW_TILE, d), lambda bi, i: (bi, jnp.minimum(i, nct - 1), 0)),
                       pl.BlockSpec((bb, ROW_TILE, d), lambda bi, i: (bi, jnp.maximum(i - nct, 0), 0))]
    out_specs = [tok(BRANCH_WIDTH), head(N_Q_HEADS), head(N_KV_HEADS), head(N_KV_HEADS, LANES),
                 tok(LRU_WIDTH), tok(LRU_WIDTH)]
    out_shape = [jax.ShapeDtypeStruct((b, l, BRANCH_WIDTH), F32),
                 jax.ShapeDtypeStruct((b, N_Q_HEADS, l, HEAD_DIM), BF16),
                 jax.ShapeDtypeStruct((b, N_KV_HEADS, l, HEAD_DIM), BF16),
                 jax.ShapeDtypeStruct((b, N_KV_HEADS, l, LANES), BF16),
                 jax.ShapeDtypeStruct((b, l, LRU_WIDTH), F32),
                 jax.ShapeDtypeStruct((b, l, LRU_WIDTH), F32)]
    if ctx is not None:
        out_specs.append(tok(d))
        out_shape.append(jax.ShapeDtypeStruct((b, l, d), F32))
    return pl.pallas_call(
        functools.partial(_proj_kernel, nct=nct, joined=ctx is None),
        grid=(b // bb, nt),
        in_specs=token_specs + [
            pl.BlockSpec((bb, 1, SUBLANES, d), lambda bi, i: (bi, (i >= nct).astype(jnp.int32), 0, 0)),
            const((d, wa)), const((LANES, LANES)), const((1, LANES)), const((1, LANES)),
            pl.BlockSpec((ROW_TILE, LANES), lambda bi, i: (i, 0)),
            pl.BlockSpec((ROW_TILE, LANES), lambda bi, i: (i, 0))],
        out_specs=out_specs,
        out_shape=out_shape,
        compiler_params=_cparams("parallel", "parallel"),
        name="in_proj",
    )(*tokens, mods, w_a, bd, qg, kg, cos_t, sin_t)


def _softmax_pv(q, k, v):
    hd = q.shape[1]
    s = lax.dot_general(q, k, (((1,), (1,)), ((), ())), preferred_element_type=F32)
    m = jnp.max(s, axis=1, keepdims=True)
    p = jnp.exp2(s - m).astype(BF16)
    r = jnp.dot(p, v, preferred_element_type=F32)
    return r[:, :hd] / r[:, hd:2 * hd]


def _attn_kernel(q_ref, k_ref, v_ref, *rest, n_ctx_tiles, ctx_len, n_step_casts, n_slab_casts):
    n_casts = n_step_casts + n_slab_casts
    cast_in, o_ref, cast_out = rest[:n_casts], rest[n_casts], rest[n_casts + 1:]
    for src, dst in zip(cast_in[:n_step_casts], cast_out[:n_step_casts]):
        dst[...] = src[...].astype(dst.dtype)

    @pl.when(pl.program_id(1) == 0)
    def _():
        for src, dst in zip(cast_in[n_step_casts:], cast_out[n_step_casts:]):
            dst[...] = src[...].astype(dst.dtype)

    nh, tq, hd = q_ref.shape[1:]
    group = nh // k_ref.shape[1]

    def run(lk):
        for j in range(nh):
            for r0 in range(0, tq, ATTN_CHAIN):
                o = _softmax_pv(q_ref[0, j, r0:r0 + ATTN_CHAIN, :], k_ref[0, j // group, :lk],
                                v_ref[0, j // group, :lk])
                o_ref[0, r0:r0 + ATTN_CHAIN, j * hd:(j + 1) * hd] = o.astype(o_ref.dtype)

    if n_ctx_tiles:
        @pl.when(pl.program_id(1) < n_ctx_tiles)
        def _():
            run(ctx_len)

        @pl.when(pl.program_id(1) >= n_ctx_tiles)
        def _():
            run(k_ref.shape[2])
    else:
        run(k_ref.shape[2])


def _attention(q, k, v, ctx_len, with_ctx, step_casts=(), slab_casts=()):
    b, nh, l, hd = q.shape
    nct = ctx_len // Q_TILE
    rows = l if with_ctx else l - ctx_len
    q_off = 0 if with_ctx else nct
    nq = rows // Q_TILE
    cast_specs, cast_shapes = [], []
    for w in step_casts:
        assert w.shape[0] == b and w.shape[1] % nq == 0
        cast_specs.append(pl.BlockSpec((1, w.shape[1] // nq, w.shape[2]), lambda bi, i: (bi, i, 0)))
        cast_shapes.append(jax.ShapeDtypeStruct(w.shape, BF16))
    for w in slab_casts:
        assert w.shape[0] % b == 0
        cast_specs.append(pl.BlockSpec((w.shape[0] // b, w.shape[1]), lambda bi, i: (bi, 0)))
        cast_shapes.append(jax.ShapeDtypeStruct(w.shape, BF16))
    return pl.pallas_call(
        functools.partial(_attn_kernel, n_ctx_tiles=nct if with_ctx else 0, ctx_len=ctx_len,
                          n_step_casts=len(step_casts), n_slab_casts=len(slab_casts)),
        grid=(b, nq),
        in_specs=[pl.BlockSpec((1, nh, Q_TILE, hd), lambda bi, i: (bi, 0, i + q_off, 0)),
                  pl.BlockSpec((1,) + k.shape[1:], lambda bi, i: (bi, 0, 0, 0)),
                  pl.BlockSpec((1,) + v.shape[1:], lambda bi, i: (bi, 0, 0, 0))] + cast_specs,
        out_specs=[pl.BlockSpec((1, Q_TILE, nh * hd), lambda bi, i: (bi, i, 0))] + cast_specs,
        out_shape=[jax.ShapeDtypeStruct((b, rows, nh * hd), BF16)] + cast_shapes,
        compiler_params=_cparams("parallel", "arbitrary"),
        name="attention",
    )(q, k, v, *step_casts, *slab_casts)


def _fill_ext(ext_ref, main, prev, nxt, first, last):
    t = main.shape[0]
    ext_ref[0:HALO] = jnp.where(first, 0.0, prev)
    ext_ref[HALO:HALO + t] = main
    ext_ref[HALO + t:2 * HALO + t] = jnp.where(last, 0.0, nxt)


def _seg_flags(tile, nct, nt):
    first = jnp.logical_or(tile == 0, tile == nct)
    last = jnp.logical_or(tile == nct - 1, tile == nt - 1)
    return first, last


def _halo_specs(width, tile_of, l, bb):
    rb = ROW_TILE // HALO
    last_blk = l // HALO - 1
    main = pl.BlockSpec((bb, ROW_TILE, width), lambda bi, i: (bi, tile_of(i), 0))
    prev = pl.BlockSpec((bb, HALO, width), lambda bi, i: (bi, jnp.maximum(tile_of(i) * rb - 1, 0), 0))
    nxt = pl.BlockSpec((bb, HALO, width), lambda bi, i: (bi, jnp.minimum((tile_of(i) + 1) * rb, last_blk), 0))
    return main, prev, nxt


def _softplus(z):
    return jnp.maximum(z, 0.0) + jnp.log1p(jnp.exp(-jnp.abs(z)))


def _gelu_tanh(x):
    return 0.5 * x * (1.0 + jnp.tanh(0.7978845608028654 * (x + 0.044715 * (x * x * x))))


def _lru_kernel(*refs, reverse, nct, nt):
    if reverse:
        (lx_ref, lxp_ref, lxn_ref, w_ref, ba_ref, bx_ref, lam_ref, cw_ref, cb_ref, hf_ref, lg_ref,
         o_ref, ext_ref, a_ref, u_ref, h_ref) = refs
    else:
        (lx_ref, lxp_ref, lxn_ref, w_ref, ba_ref, bx_ref, lam_ref, cw_ref, cb_ref,
         o_ref, ext_ref, a_ref, u_ref, h_ref) = refs
    j = pl.program_id(1)
    tile = _lru_tile(j, reverse, nct, nt)
    t = lx_ref.shape[1]
    nbb = lx_ref.shape[0]

    @pl.when(j == 0)
    def _():
        h_ref[...] = jnp.zeros_like(h_ref)

    first, last = _seg_flags(tile, nct, nt)
    lo = CONV_WIDTH // 2
    half_log2_a = (-0.5 * RG_C * LOG2_E) * _softplus(-lam_ref[...])
    for bb in range(nbb):
        ext = ext_ref.at[bb]
        _fill_ext(ext, lx_ref[bb], lxp_ref[bb], lxn_ref[bb], first, last)
        xh = cb_ref[...]
        for kk in range(CONV_WIDTH):
            xh = xh + ext[pl.ds(HALO + kk - lo, t), :] * cw_ref[kk:kk + 1, :]
        zz = jnp.dot(xh.astype(BF16), w_ref[...], preferred_element_type=F32)
        tr = jnp.tanh(zz[:, :LRU_WIDTH] + ba_ref[...])
        ti = jnp.tanh(zz[:, LRU_WIDTH:] + bx_ref[...])
        a = jnp.exp2(half_log2_a * (tr + 1.0))
        a_ref[bb] = a
        u_ref[bb] = jnp.sqrt(1.0 - a * a) * (ti + 1.0) * xh

    row = lax.broadcasted_iota(jnp.int32, (SUBLANES, LRU_WIDTH), 0)
    n_sub = t // SUBLANES

    def sub(s, hs):
        blk = (n_sub - 1 - s) if reverse else s
        r0 = pl.multiple_of(blk * SUBLANES, SUBLANES)
        out = []
        for bb in range(nbb):
            aa = a_ref[bb, pl.ds(r0, SUBLANES), :]
            uu = u_ref[bb, pl.ds(r0, SUBLANES), :]
            for dd in (1, 2, 4):
                if reverse:
                    keep = row < SUBLANES - dd
                    sh = SUBLANES - dd
                else:
                    keep = row >= dd
                    sh = dd
                a_sh = jnp.where(keep, pltpu.roll(aa, sh, 0), 1.0)
                u_sh = jnp.where(keep, pltpu.roll(uu, sh, 0), 0.0)
                uu = aa * u_sh + uu
                aa = aa * a_sh
            hh = aa * hs[bb] + uu
            if reverse:
                o_ref[bb, pl.ds(r0, SUBLANES), :] = (
                    (hf_ref[bb, pl.ds(r0, SUBLANES), :] + hh)
                    * _gelu_tanh(lg_ref[bb, pl.ds(r0, SUBLANES), :])).astype(o_ref.dtype)
                edge = hh[0:1]
            else:
                o_ref[bb, pl.ds(r0, SUBLANES), :] = hh
                edge = hh[SUBLANES - 1:SUBLANES]
            out.append(jnp.broadcast_to(edge, (SUBLANES, LRU_WIDTH)))
        return tuple(out)

    hs = lax.fori_loop(0, n_sub, sub, tuple(h_ref[bb] for bb in range(nbb)), unroll=4)
    for bb in range(nbb):
        h_ref[bb] = hs[bb]


def _lru_tile(j, reverse, nct, nt):
    if not reverse:
        return j
    return jnp.where(j < nct, nct - 1 - j, nt - 1 - (j - nct))


def _lru(lx, w, ba, bx, lam, cw, cb, nct, reverse, hf=None, lg=None):
    b, l, wd = lx.shape
    nt = l // ROW_TILE
    tile_of = lambda i: _lru_tile(i, reverse, nct, nt)
    bb = LIGHT_BATCH_BLOCK
    main, prev, nxt = _halo_specs(wd, tile_of, l, bb)
    const = lambda shp: pl.BlockSpec(shp, lambda bi, i: (0,) * len(shp))
    in_specs = [main, prev, nxt, const(w.shape), const((1, wd)), const((1, wd)), const((1, wd)),
                const((SUBLANES, wd)), const((1, wd))]
    args = [lx, lx, lx, w, ba, bx, lam, cw, cb]
    if reverse:
        in_specs += [main, main]
        args += [hf, lg]
    return pl.pallas_call(
        functools.partial(_lru_kernel, reverse=reverse, nct=nct, nt=nt),
        grid=(b // bb, nt),
        in_specs=in_specs,
        out_specs=main,
        out_shape=jax.ShapeDtypeStruct((b, l, wd), BF16 if reverse else F32),
        scratch_shapes=[pltpu.VMEM((bb, ROW_TILE + 2 * HALO, wd), F32), pltpu.VMEM((bb, ROW_TILE, wd), F32),
                        pltpu.VMEM((bb, ROW_TILE, wd), F32), pltpu.VMEM((bb, SUBLANES, wd), F32)],
        compiler_params=_cparams("parallel", "arbitrary"),
        name="lru_bwd" if reverse else "lru_fwd",
    )(*args)


def _merge_kernel(x_ref, mod_ref, p_ref, pp_ref, pn_ref, at_ref, lr_ref, wg_ref, bm_ref, pw_ref, ps_ref,
                  wb_ref, wo_ref, g_ref, b_ref, *rest, nct, nt, tile_off, ctx_len, alpha, route):
    if route:
        rw_ref, rb_ref, x1_ref, h2_ref, logit_ref, ext_ref = rest
    else:
        x1_ref, h2_ref, ext_ref = rest
    tile = pl.program_id(1) + tile_off
    t = x_ref.shape[1]
    d = x_ref.shape[2]
    first, last = _seg_flags(tile, nct, nt)
    in_ctx = tile < nct
    seg_len = jnp.where(in_ctx, ctx_len, nt * t - ctx_len)
    pos = (tile - jnp.where(in_ctx, 0, nct)) * t + lax.broadcasted_iota(jnp.int32, (t, 1), 0)

    for bb in range(x_ref.shape[0]):
        x = x_ref[bb]
        m = mod_ref[bb, 0]
        h_half = ((x * (1.0 + m[1:2]) + m[0:1]) * 0.5).astype(BF16)

        def branch(n, ys):
            tg = jnp.tanh(jnp.dot(h_half, wg_ref[:, n * d:(n + 1) * d], preferred_element_type=F32)
                          + bm_ref[n:n + 1, :])
            return (tg + 1.0) * jnp.dot(ys, wb_ref[n], preferred_element_type=F32)

        mix = branch(1, at_ref[bb]) + branch(2, lr_ref[bb])

        ext = ext_ref.at[bb]
        _fill_ext(ext, p_ref[bb], pp_ref[bb], pn_ref[bb], first, last)
        pooled = []
        for gi, w in enumerate(POOL_WINDOWS):
            lo = w // 2
            cols = slice(gi * POOL_GROUP, (gi + 1) * POOL_GROUP)
            acc = ext[pl.ds(HALO - lo, t), cols]
            for kk in range(1, w):
                acc = acc + ext[pl.ds(HALO - lo + kk, t), cols]
            cnt = jnp.clip(pos - lo + w, 0, seg_len) - jnp.clip(pos - lo, 0, seg_len)
            mean = acc / cnt.astype(F32)
            dlt = (mean - ext[pl.ds(HALO, t), cols]).astype(BF16)
            pooled.append(jnp.dot(dlt, pw_ref[gi], preferred_element_type=F32))
        pool_y = (jnp.concatenate(pooled, axis=1) * ps_ref[...]).astype(BF16)
        mix = branch(0, pool_y) + mix
        y2 = jnp.dot(mix.astype(BF16), wo_ref[...], preferred_element_type=F32)
        x1 = _layer_norm(alpha * x + m[6:7] * y2, g_ref[...], b_ref[...])
        x1_ref[bb] = x1
        h2 = x1 * (1.0 + m[4:5]) + m[3:4]
        if route:
            hi = h2.astype(BF16)
            mid = (h2 - hi.astype(F32)).astype(BF16)
            logit_ref[bb] = (jnp.dot(hi, rw_ref[0], preferred_element_type=F32)
                             + jnp.dot(mid, rw_ref[0], preferred_element_type=F32)
                             + jnp.dot(hi, rw_ref[1], preferred_element_type=F32) + rb_ref[...])
            _store_row_tiles(h2_ref.at[bb], h2)
        else:
            h2_ref[bb] = h2.astype(h2_ref.dtype)


def _merge(x, mods, p, attn, lru, w_gate, b_merge, pool_w, pool_scale, w_branch, w_out, ln_g, ln_b,
           nct, lat_only, alpha, router=None):
    b, l, d = x.shape
    nt = l // ROW_TILE
    off = nct if lat_only else 0
    rows = l - off * ROW_TILE
    tile_of = lambda i: i + off
    bb = BATCH_BLOCK
    tok_l = lambda w: pl.BlockSpec((bb, ROW_TILE, w), lambda bi, i: (bi, i + off, 0))
    tok_o = lambda w: pl.BlockSpec((bb, ROW_TILE, w), lambda bi, i: (bi, i, 0))
    const = lambda shp: pl.BlockSpec(shp, lambda bi, i: (0,) * len(shp))
    pm, pp, pn = _halo_specs(BRANCH_WIDTH, tile_of, l, bb)
    in_specs = [tok_l(d),
                pl.BlockSpec((bb, 1, SUBLANES, d), lambda bi, i: (bi, (i + off >= nct).astype(jnp.int32), 0, 0)),
                pm, pp, pn,
                tok_o(Q_WIDTH) if lat_only else tok_l(Q_WIDTH),
                tok_l(LRU_WIDTH),
                const(w_gate.shape), const(b_merge.shape), const(pool_w.shape), const(pool_scale.shape),
                const(w_branch.shape), const(w_out.shape), const((1, d)), const((1, d))]
    args = [x, mods, p, p, p, attn, lru, w_gate, b_merge, pool_w, pool_scale, w_branch, w_out, ln_g, ln_b]
    out_specs = [tok_o(d)]
    out_shape = [jax.ShapeDtypeStruct((b, rows, d), F32)]
    if router is None:
        out_specs.append(tok_o(d))
        out_shape.append(jax.ShapeDtypeStruct((b, rows, d), BF16))
    else:
        n_sub = d // LANES
        in_specs += [const(router[0].shape), const(router[1].shape)]
        args += list(router)
        out_specs += [pl.BlockSpec((bb, ROW_TILE * n_sub, LANES), lambda bi, i: (bi, i, 0)), tok_o(LANES)]
        out_shape += [jax.ShapeDtypeStruct((b, rows * n_sub, LANES), F32),
                      jax.ShapeDtypeStruct((b, rows, LANES), F32)]
    return pl.pallas_call(
        functools.partial(_merge_kernel, nct=nct, nt=nt, tile_off=off, ctx_len=nct * ROW_TILE, alpha=alpha,
                          route=router is not None),
        grid=(b // bb, rows // ROW_TILE),
        in_specs=in_specs,
        out_specs=out_specs,
        out_shape=out_shape,
        scratch_shapes=[pltpu.VMEM((bb, ROW_TILE + 2 * HALO, BRANCH_WIDTH), F32)],
        compiler_params=_cparams("parallel", "parallel"),
        name="merge_ln1",
    )(*args)


def _swiglu_kernel(x_ref, wg_ref, wu_ref, wd_ref, x1_ref, *rest, alpha):
    mod_refs, (g_ref, b_ref, o_ref) = rest[:-3], rest[-3:]
    x = x_ref[...]
    g = jnp.dot(x, wg_ref[...], preferred_element_type=F32)
    u = jnp.dot(x, wu_ref[...], preferred_element_type=F32)
    ff = jnp.dot((_silu(g) * u).astype(BF16), wd_ref[...], preferred_element_type=F32)
    for hh, mod_ref in enumerate(mod_refs):
        rs = slice(hh * ROW_TILE, (hh + 1) * ROW_TILE)
        o_ref[rs, :] = _layer_norm(alpha * x1_ref[rs, :] + mod_ref[0, 0, 5:6, :] * ff[rs], g_ref[...], b_ref[...])


def _swiglu_dense_ln(h2, x1, mods, ln_g, ln_b, w_gate, w_up, w_down, nct, alpha):
    b, l, d = x1.shape
    r = b * l
    ntl = l // ROW_TILE
    halves = FFN_ROWS // ROW_TILE

    def mod_spec(hh):
        def idx(i):
            tile = i * halves + hh
            return (tile // ntl, (tile % ntl >= nct).astype(jnp.int32), 0, 0)
        return pl.BlockSpec((1, 1, SUBLANES, d), idx)

    resident = lambda shp: pl.BlockSpec(shp, lambda i: (0, 0), pipeline_mode=pl.Buffered(1))
    out = pl.pallas_call(
        functools.partial(_swiglu_kernel, alpha=alpha),
        grid=(r // FFN_ROWS,),
        in_specs=[pl.BlockSpec((FFN_ROWS, d), lambda i: (i, 0)),
                  resident(w_gate.shape), resident(w_up.shape), resident(w_down.shape),
                  pl.BlockSpec((FFN_ROWS, d), lambda i: (i, 0))]
                 + [mod_spec(hh) for hh in range(halves)]
                 + [pl.BlockSpec((1, d), lambda i: (0, 0)), pl.BlockSpec((1, d), lambda i: (0, 0))],
        out_specs=pl.BlockSpec((FFN_ROWS, d), lambda i: (i, 0)),
        out_shape=jax.ShapeDtypeStruct((r, d), F32),
        compiler_params=_cparams("parallel"),
        name="swiglu_dense_ln2",
    )(h2.reshape(r, d), w_gate, w_up, w_down, x1.reshape(r, d), *([mods] * halves), ln_g, ln_b)
    return out.reshape(b, l, d)


def _moe_kernel(be_ref, nu_ref, tok_ref, dst_ref, t_hbm, wg_ref, wu_ref, wd_ref, y_hbm,
                xbuf, acc_ref, obuf, sem_in, sem_out, sem_fill):
    i = pl.program_id(0)
    f = pl.program_id(1)
    nb = pl.num_programs(0)
    nu = nu_ref[0]
    rows = acc_ref.shape[0]
    sub = xbuf.shape[1] // rows
    slot = i % 2

    def tile_rows(start):
        return pl.ds(pl.multiple_of(start, sub), sub)

    def gather_copy(blk, sl, r):
        return pltpu.make_async_copy(t_hbm.at[tile_rows(tok_ref[blk * rows + r])], xbuf.at[sl, tile_rows(r * sub)],
                                     sem_in.at[sl])

    def scatter_copy(blk, sl, r):
        return pltpu.make_async_copy(obuf.at[sl, tile_rows(r * sub)], y_hbm.at[tile_rows(dst_ref[blk * rows + r])],
                                     sem_out.at[sl])

    def gathers_done(sl):
        pltpu.make_async_copy(t_hbm.at[pl.ds(0, rows * sub)], xbuf.at[sl], sem_in.at[sl]).wait()

    def scatters_done(sl):
        pltpu.make_async_copy(obuf.at[sl], y_hbm.at[pl.ds(0, rows * sub)], sem_out.at[sl]).wait()

    def for_rows(fn):
        def body(r, c):
            fn(r)
            return c
        lax.fori_loop(0, rows, body, 0, unroll=8)

    def load_x():
        return _load_row_tiles(xbuf.at[slot], rows).astype(BF16)

    def swiglu_part(x):
        g = jnp.dot(x, wg_ref[0], preferred_element_type=F32)
        u = jnp.dot(x, wu_ref[0], preferred_element_type=F32)
        return jnp.dot((_silu(g) * u).astype(BF16), wd_ref[0], preferred_element_type=F32)

    @pl.when(jnp.logical_and(i == 0, f == 0))
    def _():
        obuf[1] = jnp.zeros(obuf.shape[1:], obuf.dtype)
        for_rows(lambda r: gather_copy(0, 0, r).start())

    @pl.when(jnp.logical_and(f == 0, i <= nu))
    def _():
        gathers_done(slot)

    @pl.when(jnp.logical_and(i < nu, f == 0))
    def _():
        x = load_x()
        for r in range(rows):
            gather_copy(i + 1, 1 - slot, r).start()
        acc_ref[...] = swiglu_part(x)

    @pl.when(jnp.logical_and(i < nu, f == 1))
    def _():
        @pl.when(i >= 1)
        def _():
            scatters_done(slot)

        x = load_x()
        prev = jnp.where(i >= 1, i - 1, nb)
        for r in range(rows):
            scatter_copy(prev, 1 - slot, r).start()
        _store_row_tiles(obuf.at[slot], acc_ref[...] + swiglu_part(x))

        @pl.when(i == nu - 1)
        def _():
            scatters_done(1 - slot)
            for_rows(lambda r: scatter_copy(i, slot, r).start())
            scatters_done(slot)

    @pl.when(jnp.logical_and(i >= nu, f == 1))
    def _():
        xbuf[0] = jnp.zeros(xbuf.shape[1:], xbuf.dtype)
        fill = pltpu.make_async_copy(xbuf.at[0], y_hbm.at[pl.ds(i * rows * sub, rows * sub)], sem_fill)
        fill.start()
        fill.wait()

    @pl.when(jnp.logical_and(jnp.logical_and(i == nb - 1, f == 1), nu == nb))
    def _():
        gathers_done(nb % 2)


def _swiglu_experts(t, block_e, n_used, slot_tok, slot_dst, w_gate, w_up, w_down, tf):
    d = w_gate.shape[1]
    sub = d // LANES
    p = slot_tok.shape[0]
    nf = w_gate.shape[2] // tf
    assert nf == 2

    def fcol(i, f, nu):
        return jnp.where(i < nu[0], f, nf - 1)

    return pl.pallas_call(
        _moe_kernel,
        grid_spec=pltpu.PrefetchScalarGridSpec(
            num_scalar_prefetch=4,
            grid=(p // FFN_ROWS - 1, nf),
            in_specs=[pl.BlockSpec(memory_space=pl.ANY),
                      pl.BlockSpec((1, d, tf), lambda i, f, be, nu, st, sd: (be[i], 0, fcol(i, f, nu))),
                      pl.BlockSpec((1, d, tf), lambda i, f, be, nu, st, sd: (be[i], 0, fcol(i, f, nu))),
                      pl.BlockSpec((1, tf, d), lambda i, f, be, nu, st, sd: (be[i], fcol(i, f, nu), 0))],
            out_specs=pl.BlockSpec(memory_space=pl.ANY),
            scratch_shapes=[pltpu.VMEM((2, FFN_ROWS * sub, LANES), F32), pltpu.VMEM((FFN_ROWS, d), F32),
                            pltpu.VMEM((2, FFN_ROWS * sub, LANES), F32),
                            pltpu.SemaphoreType.DMA((2,)), pltpu.SemaphoreType.DMA((2,)),
                            pltpu.SemaphoreType.DMA]),
        out_shape=jax.ShapeDtypeStruct((p * sub, LANES), F32),
        compiler_params=_cparams("arbitrary", "arbitrary"),
        name="swiglu_experts",
    )(block_e, n_used, slot_tok, slot_dst, t, w_gate, w_up, w_down)


def _combine_ln_kernel(x_ref, mod_ref, g_ref, b_ref, *rest, alpha):
    y_refs, p_ref, o_ref = rest[:-2], rest[-2], rest[-1]
    t = x_ref.shape[1]
    top_k = p_ref.shape[2]
    for bb in range(x_ref.shape[0]):
        f = None
        for kk in range(top_k):
            term = _load_row_tiles(y_refs[bb * top_k + kk], t) * p_ref[bb, :, kk:kk + 1]
            f = term if f is None else f + term
        m = mod_ref[bb, 0]
        o_ref[bb] = _layer_norm(alpha * x_ref[bb] + m[5:6] * f, g_ref[...], b_ref[...])


def _combine_ln(x1, mods, ln_g, ln_b, y, probs, alpha):
    b, rows, d = x1.shape
    nt = rows // ROW_TILE
    sub = d // LANES
    top_k = probs.shape[2]
    nbb = LIGHT_BATCH_BLOCK
    tok = lambda w: pl.BlockSpec((nbb, ROW_TILE, w), lambda bi, i: (bi, i, 0))
    const = lambda shp: pl.BlockSpec(shp, lambda bi, i: (0,) * len(shp))
    y_specs = [pl.BlockSpec((ROW_TILE * sub, LANES),
                            lambda bi, i, kk=kk, bb=bb: ((kk * b + bi * nbb + bb) * nt + i, 0))
               for bb in range(nbb) for kk in range(top_k)]
    return pl.pallas_call(
        functools.partial(_combine_ln_kernel, alpha=alpha),
        grid=(b // nbb, nt),
        in_specs=[tok(d), pl.BlockSpec((nbb, 1, SUBLANES, d), lambda bi, i: (bi, 1, 0, 0)),
                  const((1, d)), const((1, d))] + y_specs + [tok(top_k)],
        out_specs=tok(d),
        out_shape=jax.ShapeDtypeStruct((b, rows, d), F32),
        compiler_params=_cparams("parallel", "parallel"),
        name="combine_ln2",
    )(x1, mods, ln_g, ln_b, *([y] * (top_k * nbb)), probs)


def _moe(t, logits, w_gate, w_up, w_down, tf):
    n, n_e = logits.shape
    sub = t.shape[0] // n
    top_v, top_i = lax.top_k(logits, TOP_K)
    probs = jax.nn.softmax(top_v, axis=-1)
    m = n * TOP_K
    flat_e = top_i.reshape(m)
    counts = jnp.sum((flat_e[:, None] == jnp.arange(n_e, dtype=flat_e.dtype)[None, :]).astype(jnp.int32), axis=0)
    order = jnp.argsort(flat_e, stable=True).astype(jnp.int32)
    start = jnp.cumsum(counts) - counts
    padded = (counts + FFN_ROWS - 1) // FFN_ROWS * FFN_ROWS
    ends_p = jnp.cumsum(padded)
    start_p = ends_p - padded
    nb = -(-(m + n_e * (FFN_ROWS - 1)) // FFN_ROWS)
    p = (nb + 1) * FFN_ROWS
    blk_start = jnp.arange(nb + 1, dtype=jnp.int32) * FFN_ROWS
    block_e = jnp.minimum(jnp.sum((ends_p[None, :] <= blk_start[:, None]).astype(jnp.int32), axis=1), n_e - 1)
    n_used = (ends_p[-1:] // FFN_ROWS).astype(jnp.int32)
    slot_e = jnp.repeat(block_e, FFN_ROWS)
    r = jnp.arange(p, dtype=jnp.int32) - start_p[slot_e]
    is_pad = r >= counts[slot_e]
    slot_pair = order[jnp.clip(start[slot_e] + r, 0, m - 1)]
    slot_tok = jnp.where(is_pad, 0, slot_pair // TOP_K) * sub
    pads_before = (start_p - start)[slot_e] + r - counts[slot_e]
    slot_dst = jnp.where(is_pad, m + pads_before, (slot_pair % TOP_K) * n + slot_pair // TOP_K) * sub
    y = _swiglu_experts(t, block_e, n_used, slot_tok, slot_dst, w_gate, w_up, w_down, tf)
    return y, probs


def _block_diag(w):
    nb, bi, bj = w.shape
    eye = jnp.eye(nb, dtype=w.dtype)
    return (eye[:, None, :, None] * w[:, :, None, :]).reshape(nb * bi, nb * bj)


def _rope_tables(ctx_len, s):
    nf = HEAD_DIM // 4
    inv = ROPE_THETA ** (-jnp.arange(nf, dtype=F32) / nf)
    t = jnp.arange(s)
    row = (t // GRID_W).astype(F32)[:, None] * inv
    col = (t % GRID_W).astype(F32)[:, None] * inv
    ang = jnp.concatenate([row, row, col, col], axis=1)
    sign = jnp.tile(jnp.concatenate([-jnp.ones((nf,), F32), jnp.ones((nf,), F32)]), 2)
    cos = jnp.concatenate([jnp.ones((ctx_len, HEAD_DIM), F32), jnp.cos(ang)], axis=0)
    sin = jnp.concatenate([jnp.zeros((ctx_len, HEAD_DIM), F32), jnp.sin(ang) * sign], axis=0)
    reps = LANES // HEAD_DIM
    return jnp.tile(cos, (1, reps)), jnp.tile(sin, (1, reps))


def _pad_rows(a, rows):
    return jnp.zeros((rows,) + a.shape[1:], a.dtype).at[:a.shape[0]].set(a)


def kernel(x, c, ctx, c_ctx, w_mod, b_mod, w_in, b_merge, pool_w, pool_scale, q_norm, k_norm, conv_w, conv_b, lru_wa, lru_ba, lru_wx, lru_bx, lru_lambda, w_branch, w_out, ln1_g, ln1_b, ffn_w_gate, ffn_w_up, ffn_w_down, moe_router, moe_router_b, moe_w_gate, moe_w_up, moe_w_down, ln2_g, ln2_b):
    b, s, d = x.shape
    ctx_len = ctx.shape[1]
    depth = w_in.shape[0]
    l = ctx_len + s
    assert ctx_len % ROW_TILE == 0 and s % ROW_TILE == 0 and (b * s) % FFN_ROWS == 0 and b % LIGHT_BATCH_BLOCK == 0
    assert LIGHT_BATCH_BLOCK % BATCH_BLOCK == 0
    assert (b * l) % FFN_ROWS == 0 and ctx_len % Q_TILE == 0 and s % Q_TILE == 0
    nct = ctx_len // ROW_TILE
    alpha = (2 * depth) ** 0.25

    cc = _pad_rows(jnp.concatenate([c, c_ctx[None, :]], axis=0), -(-(b + 1) // SUBLANES) * SUBLANES)
    mod_all = _modulation(cc, w_mod, b_mod)

    cos_t, sin_t = _rope_tables(ctx_len, s)
    bd = _block_diag(jnp.full((LANES // HEAD_DIM, HEAD_DIM, HEAD_DIM), 1.0 / HEAD_DIM, F32)).astype(BF16)
    reps = LANES // HEAD_DIM

    xs = None
    w_in_bf = w_in[0].astype(BF16)
    for li in range(depth):
        last = li == depth - 1
        dense = li % 2 == 0
        jf = li // 2
        ml = jnp.transpose(mod_all[li, :, :b], (1, 0, 2))
        mc = jnp.broadcast_to(mod_all[li, :, b][None], (b, 6, d))
        mods = jnp.stack([mc, ml], axis=1)
        mods = jnp.concatenate([mods, 0.5 * mods[:, :, 2:3], jnp.zeros((b, 2, 1, d), F32)], axis=2)

        proj_args = (mods, w_in_bf[:, :OFF_GATE], bd, jnp.tile(q_norm[li], reps)[None],
                     jnp.tile(k_norm[li], reps)[None], cos_t, sin_t, nct)
        if xs is None:
            p, q, k, v, lx, lg, xs = _project(x, *proj_args, ctx=ctx)
        else:
            p, q, k, v, lx, lg = _project(xs, *proj_args)

        slab = [w_branch[li].reshape(-1, d), w_out[li]]
        if dense:
            slab += [ffn_w_gate[jf], ffn_w_up[jf], ffn_w_down[jf]]
            step = []
        else:
            step = [moe_w_gate[jf], moe_w_up[jf], moe_w_down[jf]]
        if not last:
            slab.append(w_in[li + 1])
        if not all(w.shape[0] == b and w.shape[1] % ((l - (ctx_len if last else 0)) // Q_TILE * 16) == 0
                   for w in step):
            step_bf, step = [w.astype(BF16) for w in step], []
        attn, *made = _attention(q, k, v, ctx_len, with_ctx=not last, step_casts=step, slab_casts=slab)
        if step:
            step_bf = made[:len(step)]
        slab_bf = made[len(step):]
        wb_bf, wo_bf = slab_bf[0].reshape(w_branch[li].shape), slab_bf[1]

        cw = _pad_rows(0.5 * conv_w[li], SUBLANES)
        cb = 0.5 * conv_b[li][None]
        lru_w = [jnp.concatenate([_block_diag(lru_wa[li, dr]), _block_diag(lru_wx[li, dr])], axis=1).astype(BF16)
                 for dr in range(2)]
        hf = _lru(lx, lru_w[0], 0.5 * lru_ba[li, 0][None], 0.5 * lru_bx[li, 0][None], lru_lambda[li, 0][None],
                  cw, cb, nct, reverse=False)
        lru_y = _lru(lx, lru_w[1], 0.5 * lru_ba[li, 1][None], 0.5 * lru_bx[li, 1][None], lru_lambda[li, 1][None],
                     cw, cb, nct, reverse=True, hf=hf, lg=lg)

        merge_args = (xs, mods, p, attn, lru_y, w_in_bf[:, OFF_GATE:], _pad_rows(0.5 * b_merge[li], SUBLANES),
                      pool_w[li].astype(BF16), pool_scale[li][None], wb_bf, wo_bf, ln1_g[li][None], ln1_b[li][None])
        if dense:
            assert not last, "the dense layer is expected to carry the context tokens along"
            x1, h2 = _merge(*merge_args, nct, lat_only=False, alpha=alpha)
            xs = _swiglu_dense_ln(h2, x1, mods, ln2_g[li][None], ln2_b[li][None], *slab_bf[2:5],
                                  nct=nct, alpha=alpha)
        else:
            assert last, "the expert layer is expected to be the last layer (latent tokens only)"
            n_e = moe_router.shape[2]
            rw = jnp.zeros((d, LANES), F32).at[:, :n_e].set(moe_router[jf])
            rb = jnp.zeros((1, LANES), F32).at[0, :n_e].set(moe_router_b[jf])
            rw_hi = rw.astype(BF16)
            rw_split = jnp.stack([rw_hi, (rw - rw_hi.astype(F32)).astype(BF16)])
            x1, h2, logits = _merge(*merge_args, nct, lat_only=True, alpha=alpha, router=(rw_split, rb))
            y, probs = _moe(h2.reshape(-1, LANES), logits.reshape(b * s, LANES)[:, :n_e], *step_bf,
                            tf=moe_w_gate.shape[3] // 2)
            xs = _combine_ln(x1, mods, ln2_g[li][None], ln2_b[li][None], y, probs.reshape(b, s, -1), alpha)
        if not last:
            w_in_bf = slab_bf[-1]
    return xs
```

```python
import functools

import jax
import jax.numpy as jnp
from jax import lax
from jax.experimental import pallas as pl
from jax.experimental.pallas import tpu as pltpu

F32 = jnp.float32
BF16 = jnp.bfloat16

GRID_W = 64
POOL_WINDOWS = (2, 4, 8, 16)
BRANCH_WIDTH = 512
POOL_GROUP = BRANCH_WIDTH // len(POOL_WINDOWS)
N_Q_HEADS = 8
N_KV_HEADS = 2
HEAD_DIM = 64
GQA_GROUP = N_Q_HEADS // N_KV_HEADS
Q_WIDTH = N_Q_HEADS * HEAD_DIM
KV_WIDTH = N_KV_HEADS * HEAD_DIM
ROPE_THETA = 10000.0
ATTN_SCALE = HEAD_DIM ** -0.5
LOG2_E = 1.4426950408889634
LRU_WIDTH = BRANCH_WIDTH
LRU_BLOCKS = 8
CONV_WIDTH = 4
RG_C = 8.0
N_BRANCH = 3
N_EXPERTS = 8
TOP_K = 2
LN_EPS = 1e-5
RMS_EPS = 1e-6
OFF_GATE = BRANCH_WIDTH + Q_WIDTH + 2 * KV_WIDTH + 2 * LRU_WIDTH

LANES = 128
SUBLANES = 8
ROW_TILE = 256
BATCH_BLOCK = 4
LIGHT_BATCH_BLOCK = 4
Q_TILE = 256
ATTN_CHAIN = 256
FFN_ROWS = 512
HALO = SUBLANES
VMEM_LIMIT = 56 * 1024 * 1024


def _cparams(*sem):
    return pltpu.CompilerParams(dimension_semantics=sem, vmem_limit_bytes=VMEM_LIMIT)


def _sigmoid(x):
    return 0.5 * jnp.tanh(0.5 * x) + 0.5


def _store_row_tiles(ref, val):
    t, d = val.shape
    for j in range(d // LANES):
        ref[pl.ds(j, t, stride=d // LANES), :] = val[:, j * LANES:(j + 1) * LANES]


def _load_row_tiles(ref, t):
    n = ref.shape[0] // t
    return jnp.concatenate([ref[pl.ds(j, t, stride=n), :] for j in range(n)], axis=1)


def _silu(x):
    return x * _sigmoid(x)


def _layer_norm(v, g, b):
    mu = jnp.mean(v, axis=-1, keepdims=True)
    d = v - mu
    var = jnp.mean(d * d, axis=-1, keepdims=True)
    return d * lax.rsqrt(var + LN_EPS) * g + b


def _mod_kernel(c_ref, w_ref, b_ref, o_ref):
    s = _silu(c_ref[...])
    o_ref[0, 0] = jnp.dot(s, w_ref[0], precision=lax.Precision.HIGHEST,
                          preferred_element_type=F32) + b_ref[0, 0]


def _modulation(cc, w_mod, b_mod):
    depth, d, _ = w_mod.shape
    r = cc.shape[0]
    b6 = b_mod.reshape(depth, 6, 1, d)
    return pl.pallas_call(
        _mod_kernel,
        grid=(depth, 6),
        in_specs=[pl.BlockSpec((r, d), lambda l, j: (0, 0)),
                  pl.BlockSpec((1, d, d), lambda l, j: (l, 0, j)),
                  pl.BlockSpec((1, 1, 1, d), lambda l, j: (l, j, 0, 0))],
        out_specs=pl.BlockSpec((1, 1, r, d), lambda l, j: (l, j, 0, 0)),
        out_shape=jax.ShapeDtypeStruct((depth, 6, r, d), F32),
        compiler_params=_cparams("arbitrary", "arbitrary"),
        name="modulation",
    )(cc, w_mod, b6)


def _norm_rope(zc, bd, g, cos, sin):
    zz = zc * zc
    hi = zz.astype(BF16)
    lo = (zz - hi.astype(F32)).astype(BF16)
    ms = jnp.dot(hi, bd, preferred_element_type=F32) + jnp.dot(lo, bd, preferred_element_type=F32)
    y = zc * lax.rsqrt(ms + RMS_EPS) * g
    nf = HEAD_DIM // 4
    up = pltpu.roll(y, LANES - nf, 1)
    dn = pltpu.roll(y, nf, 1)
    lane = lax.broadcasted_iota(jnp.int32, y.shape, 1)
    partner = jnp.where((lane % (2 * nf)) < nf, up, dn)
    return y * cos + partner * sin


def _proj_kernel(*refs, nct, joined):
    if joined:
        x_ref, mod_ref, w_ref, bd_ref, qg_ref, kg_ref, cos_ref, sin_ref, p_ref, q_ref, k_ref, v_ref, lx_ref, lg_ref = refs
    else:
        (c_ref, x_ref, mod_ref, w_ref, bd_ref, qg_ref, kg_ref, cos_ref, sin_ref,
         p_ref, q_ref, k_ref, v_ref, lx_ref, lg_ref, xs_ref) = refs
    bd = bd_ref[...]
    cos = cos_ref[...]
    sin = sin_ref[...]
    for bb in range(x_ref.shape[0]):
        if joined:
            x = x_ref[bb]
        else:
            x = jnp.where(pl.program_id(1) < nct, c_ref[bb], x_ref[bb])
            xs_ref[bb] = x
        m = mod_ref[bb, 0]
        h = (x * (1.0 + m[1:2]) + m[0:1]).astype(BF16)
        z = jnp.dot(h, w_ref[...], preferred_element_type=F32)
        o = 0
        p_ref[bb] = z[:, o:o + BRANCH_WIDTH]
        o += BRANCH_WIDTH
        for c in range(Q_WIDTH // LANES):
            y = _norm_rope(z[:, o:o + LANES], bd, qg_ref[...], cos, sin) * (ATTN_SCALE * LOG2_E)
            q_ref[bb, 2 * c] = y[:, :HEAD_DIM].astype(BF16)
            q_ref[bb, 2 * c + 1] = y[:, HEAD_DIM:].astype(BF16)
            o += LANES
        for c in range(KV_WIDTH // LANES):
            y = _norm_rope(z[:, o:o + LANES], bd, kg_ref[...], cos, sin)
            k_ref[bb, 2 * c] = y[:, :HEAD_DIM].astype(BF16)
            k_ref[bb, 2 * c + 1] = y[:, HEAD_DIM:].astype(BF16)
            o += LANES
        for c in range(KV_WIDTH // LANES):
            y = z[:, o:o + LANES]
            ones = jnp.ones((y.shape[0], LANES - HEAD_DIM), F32)
            v_ref[bb, 2 * c] = jnp.concatenate([y[:, :HEAD_DIM], ones], axis=1).astype(BF16)
            v_ref[bb, 2 * c + 1] = jnp.concatenate([y[:, HEAD_DIM:], ones], axis=1).astype(BF16)
            o += LANES
        lx_ref[bb] = z[:, o:o + LRU_WIDTH]
        o += LRU_WIDTH
        lg_ref[bb] = z[:, o:o + LRU_WIDTH]


def _project(x, mods, w_a, bd, qg, kg, cos_t, sin_t, nct, ctx=None):
    b, _, d = x.shape
    l = x.shape[1] if ctx is None else x.shape[1] + ctx.shape[1]
    nt = l // ROW_TILE
    wa = w_a.shape[1]
    bb = LIGHT_BATCH_BLOCK
    tok = lambda w: pl.BlockSpec((bb, ROW_TILE, w), lambda bi, i: (bi, i, 0))
    head = lambda n, w=HEAD_DIM: pl.BlockSpec((bb, n, ROW_TILE, w), lambda bi, i: (bi, 0, i, 0))
    const = lambda shp: pl.BlockSpec(shp, lambda bi, i: (0,) * len(shp))
    if ctx is None:
        tokens, token_specs = [x], [tok(d)]
    else:
        tokens = [ctx, x]
        token_specs = [pl.BlockSpec((bb, ROW_TILE, d), lambda bi, i: (bi, jnp.minimum(i, nct - 1), 0)),
                       pl.BlockSpec((bb, ROW_TILE, d), lambda bi, i: (bi, jnp.maximum(i - nct, 0), 0))]
    out_specs = [tok(BRANCH_WIDTH), head(N_Q_HEADS), head(N_KV_HEADS), head(N_KV_HEADS, LANES),
                 tok(LRU_WIDTH), tok(LRU_WIDTH)]
    out_shape = [jax.ShapeDtypeStruct((b, l, BRANCH_WIDTH), F32),
                 jax.ShapeDtypeStruct((b, N_Q_HEADS, l, HEAD_DIM), BF16),
                 jax.ShapeDtypeStruct((b, N_KV_HEADS, l, HEAD_DIM), BF16),
                 jax.ShapeDtypeStruct((b, N_KV_HEADS, l, LANES), BF16),
                 jax.ShapeDtypeStruct((b, l, LRU_WIDTH), F32),
                 jax.ShapeDtypeStruct((b, l, LRU_WIDTH), F32)]
    if ctx is not None:
        out_specs.append(tok(d))
        out_shape.append(jax.ShapeDtypeStruct((b, l, d), F32))
    return pl.pallas_call(
        functools.partial(_proj_kernel, nct=nct, joined=ctx is None),
        grid=(b // bb, nt),
        in_specs=token_specs + [
            pl.BlockSpec((bb, 1, SUBLANES, d), lambda bi, i: (bi, (i >= nct).astype(jnp.int32), 0, 0)),
            const((d, wa)), const((LANES, LANES)), const((1, LANES)), const((1, LANES)),
            pl.BlockSpec((ROW_TILE, LANES), lambda bi, i: (i, 0)),
            pl.BlockSpec((ROW_TILE, LANES), lambda bi, i: (i, 0))],
        out_specs=out_specs,
        out_shape=out_shape,
        compiler_params=_cparams("parallel", "parallel"),
        name="in_proj",
    )(*tokens, mods, w_a, bd, qg, kg, cos_t, sin_t)


def _softmax_pv(q, k, v):
    hd = q.shape[1]
    s = lax.dot_general(q, k, (((1,), (1,)), ((), ())), preferred_element_type=F32)
    m = jnp.max(s, axis=1, keepdims=True)
    p = jnp.exp2(s - m).astype(BF16)
    r = jnp.dot(p, v, preferred_element_type=F32)
    return r[:, :hd] / r[:, hd:2 * hd]


def _attn_kernel(q_ref, k_ref, v_ref, *rest, n_ctx_tiles, ctx_len, n_step_casts, n_slab_casts):
    n_casts = n_step_casts + n_slab_casts
    cast_in, o_ref, cast_out = rest[:n_casts], rest[n_casts], rest[n_casts + 1:]
    for src, dst in zip(cast_in[:n_step_casts], cast_out[:n_step_casts]):
        dst[...] = src[...].astype(dst.dtype)

    @pl.when(pl.program_id(1) == 0)
    def _():
        for src, dst in zip(cast_in[n_step_casts:], cast_out[n_step_casts:]):
            dst[...] = src[...].astype(dst.dtype)

    nh, tq, hd = q_ref.shape[1:]
    group = nh // k_ref.shape[1]

    def run(lk):
        for j in range(nh):
            for r0 in range(0, tq, ATTN_CHAIN):
                o = _softmax_pv(q_ref[0, j, r0:r0 + ATTN_CHAIN, :], k_ref[0, j // group, :lk],
                                v_ref[0, j // group, :lk])
                o_ref[0, r0:r0 + ATTN_CHAIN, j * hd:(j + 1) * hd] = o.astype(o_ref.dtype)

    if n_ctx_tiles:
        @pl.when(pl.program_id(1) < n_ctx_tiles)
        def _():
            run(ctx_len)

        @pl.when(pl.program_id(1) >= n_ctx_tiles)
        def _():
            run(k_ref.shape[2])
    else:
        run(k_ref.shape[2])


def _attention(q, k, v, ctx_len, with_ctx, step_casts=(), slab_casts=()):
    b, nh, l, hd = q.shape
    nct = ctx_len // Q_TILE
    rows = l if with_ctx else l - ctx_len
    q_off = 0 if with_ctx else nct
    nq = rows // Q_TILE
    cast_specs, cast_shapes = [], []
    for w in step_casts:
        assert w.shape[0] == b and w.shape[1] % nq == 0
        cast_specs.append(pl.BlockSpec((1, w.shape[1] // nq, w.shape[2]), lambda bi, i: (bi, i, 0)))
        cast_shapes.append(jax.ShapeDtypeStruct(w.shape, BF16))
    for w in slab_casts:
        assert w.shape[0] % b == 0
        cast_specs.append(pl.BlockSpec((w.shape[0] // b, w.shape[1]), lambda bi, i: (bi, 0)))
        cast_shapes.append(jax.ShapeDtypeStruct(w.shape, BF16))
    return pl.pallas_call(
        functools.partial(_attn_kernel, n_ctx_tiles=nct if with_ctx else 0, ctx_len=ctx_len,
                          n_step_casts=len(step_casts), n_slab_casts=len(slab_casts)),
        grid=(b, nq),
        in_specs=[pl.BlockSpec((1, nh, Q_TILE, hd), lambda bi, i: (bi, 0, i + q_off, 0)),
                  pl.BlockSpec((1,) + k.shape[1:], lambda bi, i: (bi, 0, 0, 0)),
                  pl.BlockSpec((1,) + v.shape[1:], lambda bi, i: (bi, 0, 0, 0))] + cast_specs,
        out_specs=[pl.BlockSpec((1, Q_TILE, nh * hd), lambda bi, i: (bi, i, 0))] + cast_specs,
        out_shape=[jax.ShapeDtypeStruct((b, rows, nh * hd), BF16)] + cast_shapes,
        compiler_params=_cparams("parallel", "arbitrary"),
        name="attention",
    )(q, k, v, *step_casts, *slab_casts)


def _fill_ext(ext_ref, main, prev, nxt, first, last):
    t = main.shape[0]
    ext_ref[0:HALO] = jnp.where(first, 0.0, prev)
    ext_ref[HALO:HALO + t] = main
    ext_ref[HALO + t:2 * HALO + t] = jnp.where(last, 0.0, nxt)


def _seg_flags(tile, nct, nt):
    first = jnp.logical_or(tile == 0, tile == nct)
    last = jnp.logical_or(tile == nct - 1, tile == nt - 1)
    return first, last


def _halo_specs(width, tile_of, l, bb):
    rb = ROW_TILE // HALO
    last_blk = l // HALO - 1
    main = pl.BlockSpec((bb, ROW_TILE, width), lambda bi, i: (bi, tile_of(i), 0))
    prev = pl.BlockSpec((bb, HALO, width), lambda bi, i: (bi, jnp.maximum(tile_of(i) * rb - 1, 0), 0))
    nxt = pl.BlockSpec((bb, HALO, width), lambda bi, i: (bi, jnp.minimum((tile_of(i) + 1) * rb, last_blk), 0))
    return main, prev, nxt


def _softplus(z):
    return jnp.maximum(z, 0.0) + jnp.log1p(jnp.exp(-jnp.abs(z)))


def _gelu_tanh(x):
    return 0.5 * x * (1.0 + jnp.tanh(0.7978845608028654 * (x + 0.044715 * (x * x * x))))


def _lru_kernel(*refs, reverse, nct, nt):
    if reverse:
        (lx_ref, lxp_ref, lxn_ref, w_ref, ba_ref, bx_ref, lam_ref, cw_ref, cb_ref, hf_ref, lg_ref,
         o_ref, ext_ref, a_ref, u_ref, h_ref) = refs
    else:
        (lx_ref, lxp_ref, lxn_ref, w_ref, ba_ref, bx_ref, lam_ref, cw_ref, cb_ref,
         o_ref, ext_ref, a_ref, u_ref, h_ref) = refs
    j = pl.program_id(1)
    tile = _lru_tile(j, reverse, nct, nt)
    t = lx_ref.shape[1]
    nbb = lx_ref.shape[0]

    @pl.when(j == 0)
    def _():
        h_ref[...] = jnp.zeros_like(h_ref)

    first, last = _seg_flags(tile, nct, nt)
    lo = CONV_WIDTH // 2
    half_log2_a = (-0.5 * RG_C * LOG2_E) * _softplus(-lam_ref[...])
    for bb in range(nbb):
        ext = ext_ref.at[bb]
        _fill_ext(ext, lx_ref[bb], lxp_ref[bb], lxn_ref[bb], first, last)
        xh = cb_ref[...]
        for kk in range(CONV_WIDTH):
            xh = xh + ext[pl.ds(HALO + kk - lo, t), :] * cw_ref[kk:kk + 1, :]
        zz = jnp.dot(xh.astype(BF16), w_ref[...], preferred_element_type=F32)
        tr = jnp.tanh(zz[:, :LRU_WIDTH] + ba_ref[...])
        ti = jnp.tanh(zz[:, LRU_WIDTH:] + bx_ref[...])
        a = jnp.exp2(half_log2_a * (tr + 1.0))
        a_ref[bb] = a
        u_ref[bb] = jnp.sqrt(1.0 - a * a) * (ti + 1.0) * xh

    row = lax.broadcasted_iota(jnp.int32, (SUBLANES, LRU_WIDTH), 0)
    n_sub = t // SUBLANES

    def sub(s, hs):
        blk = (n_sub - 1 - s) if reverse else s
        r0 = pl.multiple_of(blk * SUBLANES, SUBLANES)
        out = []
        for bb in range(nbb):
            aa = a_ref[bb, pl.ds(r0, SUBLANES), :]
            uu = u_ref[bb, pl.ds(r0, SUBLANES), :]
            for dd in (1, 2, 4):
                if reverse:
                    keep = row < SUBLANES - dd
                    sh = SUBLANES - dd
                else:
                    keep = row >= dd
                    sh = dd
                a_sh = jnp.where(keep, pltpu.roll(aa, sh, 0), 1.0)
                u_sh = jnp.where(keep, pltpu.roll(uu, sh, 0), 0.0)
                uu = aa * u_sh + uu
                aa = aa * a_sh
            hh = aa * hs[bb] + uu
            if reverse:
                o_ref[bb, pl.ds(r0, SUBLANES), :] = (
                    (hf_ref[bb, pl.ds(r0, SUBLANES), :] + hh)
                    * _gelu_tanh(lg_ref[bb, pl.ds(r0, SUBLANES), :])).astype(o_ref.dtype)
                edge = hh[0:1]
            else:
                o_ref[bb, pl.ds(r0, SUBLANES), :] = hh
                edge = hh[SUBLANES - 1:SUBLANES]
            out.append(jnp.broadcast_to(edge, (SUBLANES, LRU_WIDTH)))
        return tuple(out)

    hs = lax.fori_loop(0, n_sub, sub, tuple(h_ref[bb] for bb in range(nbb)), unroll=4)
    for bb in range(nbb):
        h_ref[bb] = hs[bb]


def _lru_tile(j, reverse, nct, nt):
    if not reverse:
        return j
    return jnp.where(j < nct, nct - 1 - j, nt - 1 - (j - nct))


def _lru(lx, w, ba, bx, lam, cw, cb, nct, reverse, hf=None, lg=None):
    b, l, wd = lx.shape
    nt = l // ROW_TILE
    tile_of = lambda i: _lru_tile(i, reverse, nct, nt)
    bb = LIGHT_BATCH_BLOCK
    main, prev, nxt = _halo_specs(wd, tile_of, l, bb)
    const = lambda shp: pl.BlockSpec(shp, lambda bi, i: (0,) * len(shp))
    in_specs = [main, prev, nxt, const(w.shape), const((1, wd)), const((1, wd)), const((1, wd)),
                const((SUBLANES, wd)), const((1, wd))]
    args = [lx, lx, lx, w, ba, bx, lam, cw, cb]
    if reverse:
        in_specs += [main, main]
        args += [hf, lg]
    return pl.pallas_call(
        functools.partial(_lru_kernel, reverse=reverse, nct=nct, nt=nt),
        grid=(b // bb, nt),
        in_specs=in_specs,
        out_specs=main,
        out_shape=jax.ShapeDtypeStruct((b, l, wd), BF16 if reverse else F32),
        scratch_shapes=[pltpu.VMEM((bb, ROW_TILE + 2 * HALO, wd), F32), pltpu.VMEM((bb, ROW_TILE, wd), F32),
                        pltpu.VMEM((bb, ROW_TILE, wd), F32), pltpu.VMEM((bb, SUBLANES, wd), F32)],
        compiler_params=_cparams("parallel", "arbitrary"),
        name="lru_bwd" if reverse else "lru_fwd",
    )(*args)


def _merge_kernel(x_ref, mod_ref, p_ref, pp_ref, pn_ref, at_ref, lr_ref, wg_ref, bm_ref, pw_ref, ps_ref,
                  wb_ref, wo_ref, g_ref, b_ref, *rest, nct, nt, tile_off, ctx_len, alpha, route):
    if route:
        rw_ref, rb_ref, x1_ref, h2_ref, logit_ref, ext_ref = rest
    else:
        x1_ref, h2_ref, ext_ref = rest
    tile = pl.program_id(1) + tile_off
    t = x_ref.shape[1]
    d = x_ref.shape[2]
    first, last = _seg_flags(tile, nct, nt)
    in_ctx = tile < nct
    seg_len = jnp.where(in_ctx, ctx_len, nt * t - ctx_len)
    pos = (tile - jnp.where(in_ctx, 0, nct)) * t + lax.broadcasted_iota(jnp.int32, (t, 1), 0)

    for bb in range(x_ref.shape[0]):
        x = x_ref[bb]
        m = mod_ref[bb, 0]
        h_half = ((x * (1.0 + m[1:2]) + m[0:1]) * 0.5).astype(BF16)

        def branch(n, ys):
            tg = jnp.tanh(jnp.dot(h_half, wg_ref[:, n * d:(n + 1) * d], preferred_element_type=F32)
                          + bm_ref[n:n + 1, :])
            return (tg + 1.0) * jnp.dot(ys, wb_ref[n], preferred_element_type=F32)

        mix = branch(1, at_ref[bb]) + branch(2, lr_ref[bb])

        ext = ext_ref.at[bb]
        _fill_ext(ext, p_ref[bb], pp_ref[bb], pn_ref[bb], first, last)
        pooled = []
        for gi, w in enumerate(POOL_WINDOWS):
            lo = w // 2
            cols = slice(gi * POOL_GROUP, (gi + 1) * POOL_GROUP)
            acc = ext[pl.ds(HALO - lo, t), cols]
            for kk in range(1, w):
                acc = acc + ext[pl.ds(HALO - lo + kk, t), cols]
            cnt = jnp.clip(pos - lo + w, 0, seg_len) - jnp.clip(pos - lo, 0, seg_len)
            mean = acc / cnt.astype(F32)
            dlt = (mean - ext[pl.ds(HALO, t), cols]).astype(BF16)
            pooled.append(jnp.dot(dlt, pw_ref[gi], preferred_element_type=F32))
        pool_y = (jnp.concatenate(pooled, axis=1) * ps_ref[...]).astype(BF16)
        mix = branch(0, pool_y) + mix
        y2 = jnp.dot(mix.astype(BF16), wo_ref[...], preferred_element_type=F32)
        x1 = _layer_norm(alpha * x + m[6:7] * y2, g_ref[...], b_ref[...])
        x1_ref[bb] = x1
        h2 = x1 * (1.0 + m[4:5]) + m[3:4]
        if route:
            hi = h2.astype(BF16)
            mid = (h2 - hi.astype(F32)).astype(BF16)
            logit_ref[bb] = (jnp.dot(hi, rw_ref[0], preferred_element_type=F32)
                             + jnp.dot(mid, rw_ref[0], preferred_element_type=F32)
                             + jnp.dot(hi, rw_ref[1], preferred_element_type=F32) + rb_ref[...])
            _store_row_tiles(h2_ref.at[bb], h2)
        else:
            h2_ref[bb] = h2.astype(h2_ref.dtype)


def _merge(x, mods, p, attn, lru, w_gate, b_merge, pool_w, pool_scale, w_branch, w_out, ln_g, ln_b,
           nct, lat_only, alpha, router=None):
    b, l, d = x.shape
    nt = l // ROW_TILE
    off = nct if lat_only else 0
    rows = l - off * ROW_TILE
    tile_of = lambda i: i + off
    bb = BATCH_BLOCK
    tok_l = lambda w: pl.BlockSpec((bb, ROW_TILE, w), lambda bi, i: (bi, i + off, 0))
    tok_o = lambda w: pl.BlockSpec((bb, ROW_TILE, w), lambda bi, i: (bi, i, 0))
    const = lambda shp: pl.BlockSpec(shp, lambda bi, i: (0,) * len(shp), pipeline_mode=pl.Buffered(1))
    pm, pp, pn = _halo_specs(BRANCH_WIDTH, tile_of, l, bb)
    in_specs = [tok_l(d),
                pl.BlockSpec((bb, 1, SUBLANES, d), lambda bi, i: (bi, (i + off >= nct).astype(jnp.int32), 0, 0)),
                pm, pp, pn,
                tok_o(Q_WIDTH) if lat_only else tok_l(Q_WIDTH),
                tok_l(LRU_WIDTH),
                const(w_gate.shape), const(b_merge.shape), const(pool_w.shape), const(pool_scale.shape),
                const(w_branch.shape), const(w_out.shape), const((1, d)), const((1, d))]
    args = [x, mods, p, p, p, attn, lru, w_gate, b_merge, pool_w, pool_scale, w_branch, w_out, ln_g, ln_b]
    out_specs = [tok_o(d)]
    out_shape = [jax.ShapeDtypeStruct((b, rows, d), F32)]
    if router is None:
        out_specs.append(tok_o(d))
        out_shape.append(jax.ShapeDtypeStruct((b, rows, d), BF16))
    else:
        n_sub = d // LANES
        in_specs += [const(router[0].shape), const(router[1].shape)]
        args += list(router)
        out_specs += [pl.BlockSpec((bb, ROW_TILE * n_sub, LANES), lambda bi, i: (bi, i, 0)), tok_o(LANES)]
        out_shape += [jax.ShapeDtypeStruct((b, rows * n_sub, LANES), F32),
                      jax.ShapeDtypeStruct((b, rows, LANES), F32)]
    return pl.pallas_call(
        functools.partial(_merge_kernel, nct=nct, nt=nt, tile_off=off, ctx_len=nct * ROW_TILE, alpha=alpha,
                          route=router is not None),
        grid=(b // bb, rows // ROW_TILE),
        in_specs=in_specs,
        out_specs=out_specs,
        out_shape=out_shape,
        scratch_shapes=[pltpu.VMEM((bb, ROW_TILE + 2 * HALO, BRANCH_WIDTH), F32)],
        compiler_params=_cparams("parallel", "parallel"),
        name="merge_ln1",
    )(*args)


def _swiglu_kernel(x_ref, wg_ref, wu_ref, wd_ref, x1_ref, *rest, alpha):
    mod_refs, (g_ref, b_ref, o_ref) = rest[:-3], rest[-3:]
    x = x_ref[...]
    g = jnp.dot(x, wg_ref[...], preferred_element_type=F32)
    u = jnp.dot(x, wu_ref[...], preferred_element_type=F32)
    ff = jnp.dot((_silu(g) * u).astype(BF16), wd_ref[...], preferred_element_type=F32)
    for hh, mod_ref in enumerate(mod_refs):
        rs = slice(hh * ROW_TILE, (hh + 1) * ROW_TILE)
        o_ref[rs, :] = _layer_norm(alpha * x1_ref[rs, :] + mod_ref[0, 0, 5:6, :] * ff[rs], g_ref[...], b_ref[...])


def _swiglu_dense_ln(h2, x1, mods, ln_g, ln_b, w_gate, w_up, w_down, nct, alpha):
    b, l, d = x1.shape
    r = b * l
    ntl = l // ROW_TILE
    halves = FFN_ROWS // ROW_TILE

    def mod_spec(hh):
        def idx(i):
            tile = i * halves + hh
            return (tile // ntl, (tile % ntl >= nct).astype(jnp.int32), 0, 0)
        return pl.BlockSpec((1, 1, SUBLANES, d), idx)

    resident = lambda shp: pl.BlockSpec(shp, lambda i: (0, 0), pipeline_mode=pl.Buffered(1))
    out = pl.pallas_call(
        functools.partial(_swiglu_kernel, alpha=alpha),
        grid=(r // FFN_ROWS,),
        in_specs=[pl.BlockSpec((FFN_ROWS, d), lambda i: (i, 0)),
                  resident(w_gate.shape), resident(w_up.shape), resident(w_down.shape),
                  pl.BlockSpec((FFN_ROWS, d), lambda i: (i, 0))]
                 + [mod_spec(hh) for hh in range(halves)]
                 + [pl.BlockSpec((1, d), lambda i: (0, 0)), pl.BlockSpec((1, d), lambda i: (0, 0))],
        out_specs=pl.BlockSpec((FFN_ROWS, d), lambda i: (i, 0)),
        out_shape=jax.ShapeDtypeStruct((r, d), F32),
        compiler_params=_cparams("parallel"),
        name="swiglu_dense_ln2",
    )(h2.reshape(r, d), w_gate, w_up, w_down, x1.reshape(r, d), *([mods] * halves), ln_g, ln_b)
    return out.reshape(b, l, d)


def _moe_kernel(be_ref, nu_ref, tok_ref, dst_ref, t_hbm, wg_ref, wu_ref, wd_ref, y_hbm,
                xbuf, acc_ref, obuf, sem_in, sem_out, sem_fill):
    i = pl.program_id(0)
    f = pl.program_id(1)
    nb = pl.num_programs(0)
    nu = nu_ref[0]
    rows = acc_ref.shape[0]
    sub = xbuf.shape[1] // rows
    slot = i % 2

    def tile_rows(start):
        return pl.ds(pl.multiple_of(start, sub), sub)

    def gather_copy(blk, sl, r):
        return pltpu.make_async_copy(t_hbm.at[tile_rows(tok_ref[blk * rows + r])], xbuf.at[sl, tile_rows(r * sub)],
                                     sem_in.at[sl])

    def scatter_copy(blk, sl, r):
        return pltpu.make_async_copy(obuf.at[sl, tile_rows(r * sub)], y_hbm.at[tile_rows(dst_ref[blk * rows + r])],
                                     sem_out.at[sl])

    def gathers_done(sl):
        pltpu.make_async_copy(t_hbm.at[pl.ds(0, rows * sub)], xbuf.at[sl], sem_in.at[sl]).wait()

    def scatters_done(sl):
        pltpu.make_async_copy(obuf.at[sl], y_hbm.at[pl.ds(0, rows * sub)], sem_out.at[sl]).wait()

    def for_rows(fn):
        def body(r, c):
            fn(r)
            return c
        lax.fori_loop(0, rows, body, 0, unroll=8)

    def load_x():
        return _load_row_tiles(xbuf.at[slot], rows).astype(BF16)

    def swiglu_part(x):
        g = jnp.dot(x, wg_ref[0], preferred_element_type=F32)
        u = jnp.dot(x, wu_ref[0], preferred_element_type=F32)
        return jnp.dot((_silu(g) * u).astype(BF16), wd_ref[0], preferred_element_type=F32)

    @pl.when(jnp.logical_and(i == 0, f == 0))
    def _():
        obuf[1] = jnp.zeros(obuf.shape[1:], obuf.dtype)
        for_rows(lambda r: gather_copy(0, 0, r).start())

    @pl.when(jnp.logical_and(f == 0, i <= nu))
    def _():
        gathers_done(slot)

    @pl.when(jnp.logical_and(i < nu, f == 0))
    def _():
        x = load_x()
        for r in range(rows):
            gather_copy(i + 1, 1 - slot, r).start()
        acc_ref[...] = swiglu_part(x)

    @pl.when(jnp.logical_and(i < nu, f == 1))
    def _():
        @pl.when(i >= 1)
        def _():
            scatters_done(slot)

        x = load_x()
        prev = jnp.where(i >= 1, i - 1, nb)
        for r in range(rows):
            scatter_copy(prev, 1 - slot, r).start()
        _store_row_tiles(obuf.at[slot], acc_ref[...] + swiglu_part(x))

        @pl.when(i == nu - 1)
        def _():
            scatters_done(1 - slot)
            for_rows(lambda r: scatter_copy(i, slot, r).start())
            scatters_done(slot)

    @pl.when(jnp.logical_and(i >= nu, f == 1))
    def _():
        xbuf[0] = jnp.zeros(xbuf.shape[1:], xbuf.dtype)
        fill = pltpu.make_async_copy(xbuf.at[0], y_hbm.at[pl.ds(i * rows * sub, rows * sub)], sem_fill)
        fill.start()
        fill.wait()

    @pl.when(jnp.logical_and(jnp.logical_and(i == nb - 1, f == 1), nu == nb))
    def _():
        gathers_done(nb % 2)


def _swiglu_experts(t, block_e, n_used, slot_tok, slot_dst, w_gate, w_up, w_down, tf):
    d = w_gate.shape[1]
    sub = d // LANES
    p = slot_tok.shape[0]
    nf = w_gate.shape[2] // tf
    assert nf == 2

    def fcol(i, f, nu):
        return jnp.where(i < nu[0], f, nf - 1)

    return pl.pallas_call(
        _moe_kernel,
        grid_spec=pltpu.PrefetchScalarGridSpec(
            num_scalar_prefetch=4,
            grid=(p // FFN_ROWS - 1, nf),
            in_specs=[pl.BlockSpec(memory_space=pl.ANY),
                      pl.BlockSpec((1, d, tf), lambda i, f, be, nu, st, sd: (be[i], 0, fcol(i, f, nu))),
                      pl.BlockSpec((1, d, tf), lambda i, f, be, nu, st, sd: (be[i], 0, fcol(i, f, nu))),
                      pl.BlockSpec((1, tf, d), lambda i, f, be, nu, st, sd: (be[i], fcol(i, f, nu), 0))],
            out_specs=pl.BlockSpec(memory_space=pl.ANY),
            scratch_shapes=[pltpu.VMEM((2, FFN_ROWS * sub, LANES), F32), pltpu.VMEM((FFN_ROWS, d), F32),
                            pltpu.VMEM((2, FFN_ROWS * sub, LANES), F32),
                            pltpu.SemaphoreType.DMA((2,)), pltpu.SemaphoreType.DMA((2,)),
                            pltpu.SemaphoreType.DMA]),
        out_shape=jax.ShapeDtypeStruct((p * sub, LANES), F32),
        compiler_params=_cparams("arbitrary", "arbitrary"),
        name="swiglu_experts",
    )(block_e, n_used, slot_tok, slot_dst, t, w_gate, w_up, w_down)


def _combine_ln_kernel(x_ref, mod_ref, g_ref, b_ref, *rest, alpha):
    y_refs, p_ref, o_ref = rest[:-2], rest[-2], rest[-1]
    t = x_ref.shape[1]
    top_k = p_ref.shape[2]
    for bb in range(x_ref.shape[0]):
        f = None
        for kk in range(top_k):
            term = _load_row_tiles(y_refs[bb * top_k + kk], t) * p_ref[bb, :, kk:kk + 1]
            f = term if f is None else f + term
        m = mod_ref[bb, 0]
        o_ref[bb] = _layer_norm(alpha * x_ref[bb] + m[5:6] * f, g_ref[...], b_ref[...])


def _combine_ln(x1, mods, ln_g, ln_b, y, probs, alpha):
    b, rows, d = x1.shape
    nt = rows // ROW_TILE
    sub = d // LANES
    top_k = probs.shape[2]
    nbb = LIGHT_BATCH_BLOCK
    tok = lambda w: pl.BlockSpec((nbb, ROW_TILE, w), lambda bi, i: (bi, i, 0))
    const = lambda shp: pl.BlockSpec(shp, lambda bi, i: (0,) * len(shp))
    y_specs = [pl.BlockSpec((ROW_TILE * sub, LANES),
                            lambda bi, i, kk=kk, bb=bb: ((kk * b + bi * nbb + bb) * nt + i, 0))
               for bb in range(nbb) for kk in range(top_k)]
    return pl.pallas_call(
        functools.partial(_combine_ln_kernel, alpha=alpha),
        grid=(b // nbb, nt),
        in_specs=[tok(d), pl.BlockSpec((nbb, 1, SUBLANES, d), lambda bi, i: (bi, 1, 0, 0)),
                  const((1, d)), const((1, d))] + y_specs + [tok(top_k)],
        out_specs=tok(d),
        out_shape=jax.ShapeDtypeStruct((b, rows, d), F32),
        compiler_params=_cparams("parallel", "parallel"),
        name="combine_ln2",
    )(x1, mods, ln_g, ln_b, *([y] * (top_k * nbb)), probs)


def _moe(t, logits, w_gate, w_up, w_down, tf):
    n, n_e = logits.shape
    sub = t.shape[0] // n
    top_v, top_i = lax.top_k(logits, TOP_K)
    probs = jax.nn.softmax(top_v, axis=-1)
    m = n * TOP_K
    flat_e = top_i.reshape(m)
    counts = jnp.sum((flat_e[:, None] == jnp.arange(n_e, dtype=flat_e.dtype)[None, :]).astype(jnp.int32), axis=0)
    order = jnp.argsort(flat_e, stable=True).astype(jnp.int32)
    start = jnp.cumsum(counts) - counts
    padded = (counts + FFN_ROWS - 1) // FFN_ROWS * FFN_ROWS
    ends_p = jnp.cumsum(padded)
    start_p = ends_p - padded
    nb = -(-(m + n_e * (FFN_ROWS - 1)) // FFN_ROWS)
    p = (nb + 1) * FFN_ROWS
    blk_start = jnp.arange(nb + 1, dtype=jnp.int32) * FFN_ROWS
    block_e = jnp.minimum(jnp.sum((ends_p[None, :] <= blk_start[:, None]).astype(jnp.int32), axis=1), n_e - 1)
    n_used = (ends_p[-1:] // FFN_ROWS).astype(jnp.int32)
    slot_e = jnp.repeat(block_e, FFN_ROWS)
    r = jnp.arange(p, dtype=jnp.int32) - start_p[slot_e]
    is_pad = r >= counts[slot_e]
    slot_pair = order[jnp.clip(start[slot_e] + r, 0, m - 1)]
    slot_tok = jnp.where(is_pad, 0, slot_pair // TOP_K) * sub
    pads_before = (start_p - start)[slot_e] + r - counts[slot_e]
    slot_dst = jnp.where(is_pad, m + pads_before, (slot_pair % TOP_K) * n + slot_pair // TOP_K) * sub
    y = _swiglu_experts(t, block_e, n_used, slot_tok, slot_dst, w_gate, w_up, w_down, tf)
    return y, probs


def _block_diag(w):
    nb, bi, bj = w.shape
    eye = jnp.eye(nb, dtype=w.dtype)
    return (eye[:, None, :, None] * w[:, :, None, :]).reshape(nb * bi, nb * bj)


def _rope_tables(ctx_len, s):
    nf = HEAD_DIM // 4
    inv = ROPE_THETA ** (-jnp.arange(nf, dtype=F32) / nf)
    t = jnp.arange(s)
    row = (t // GRID_W).astype(F32)[:, None] * inv
    col = (t % GRID_W).astype(F32)[:, None] * inv
    ang = jnp.concatenate([row, row, col, col], axis=1)
    sign = jnp.tile(jnp.concatenate([-jnp.ones((nf,), F32), jnp.ones((nf,), F32)]), 2)
    cos = jnp.concatenate([jnp.ones((ctx_len, HEAD_DIM), F32), jnp.cos(ang)], axis=0)
    sin = jnp.concatenate([jnp.zeros((ctx_len, HEAD_DIM), F32), jnp.sin(ang) * sign], axis=0)
    reps = LANES // HEAD_DIM
    return jnp.tile(cos, (1, reps)), jnp.tile(sin, (1, reps))


def _pad_rows(a, rows):
    return jnp.zeros((rows,) + a.shape[1:], a.dtype).at[:a.shape[0]].set(a)


def kernel(x, c, ctx, c_ctx, w_mod, b_mod, w_in, b_merge, pool_w, pool_scale, q_norm, k_norm, conv_w, conv_b, lru_wa, lru_ba, lru_wx, lru_bx, lru_lambda, w_branch, w_out, ln1_g, ln1_b, ffn_w_gate, ffn_w_up, ffn_w_down, moe_router, moe_router_b, moe_w_gate, moe_w_up, moe_w_down, ln2_g, ln2_b):
    b, s, d = x.shape
    ctx_len = ctx.shape[1]
    depth = w_in.shape[0]
    l = ctx_len + s
    assert ctx_len % ROW_TILE == 0 and s % ROW_TILE == 0 and (b * s) % FFN_ROWS == 0 and b % LIGHT_BATCH_BLOCK == 0
    assert LIGHT_BATCH_BLOCK % BATCH_BLOCK == 0
    assert (b * l) % FFN_ROWS == 0 and ctx_len % Q_TILE == 0 and s % Q_TILE == 0
    nct = ctx_len // ROW_TILE
    alpha = (2 * depth) ** 0.25

    cc = _pad_rows(jnp.concatenate([c, c_ctx[None, :]], axis=0), -(-(b + 1) // SUBLANES) * SUBLANES)
    mod_all = _modulation(cc, w_mod, b_mod)

    cos_t, sin_t = _rope_tables(ctx_len, s)
    bd = _block_diag(jnp.full((LANES // HEAD_DIM, HEAD_DIM, HEAD_DIM), 1.0 / HEAD_DIM, F32)).astype(BF16)
    reps = LANES // HEAD_DIM

    xs = None
    w_in_bf = w_in[0].astype(BF16)
    for li in range(depth):
        last = li == depth - 1
        dense = li % 2 == 0
        jf = li // 2
        ml = jnp.transpose(mod_all[li, :, :b], (1, 0, 2))
        mc = jnp.broadcast_to(mod_all[li, :, b][None], (b, 6, d))
        mods = jnp.stack([mc, ml], axis=1)
        mods = jnp.concatenate([mods, 0.5 * mods[:, :, 2:3], jnp.zeros((b, 2, 1, d), F32)], axis=2)

        proj_args = (mods, w_in_bf[:, :OFF_GATE], bd, jnp.tile(q_norm[li], reps)[None],
                     jnp.tile(k_norm[li], reps)[None], cos_t, sin_t, nct)
        if xs is None:
            p, q, k, v, lx, lg, xs = _project(x, *proj_args, ctx=ctx)
        else:
            p, q, k, v, lx, lg = _project(xs, *proj_args)

        slab = [w_branch[li].reshape(-1, d), w_out[li]]
        if dense:
            slab += [ffn_w_gate[jf], ffn_w_up[jf], ffn_w_down[jf]]
            step = []
        else:
            step = [moe_w_gate[jf], moe_w_up[jf], moe_w_down[jf]]
        if not last:
            slab.append(w_in[li + 1])
        if not all(w.shape[0] == b and w.shape[1] % ((l - (ctx_len if last else 0)) // Q_TILE * 16) == 0
                   for w in step):
            step_bf, step = [w.astype(BF16) for w in step], []
        attn, *made = _attention(q, k, v, ctx_len, with_ctx=not last, step_casts=step, slab_casts=slab)
        if step:
            step_bf = made[:len(step)]
        slab_bf = made[len(step):]
        wb_bf, wo_bf = slab_bf[0].reshape(w_branch[li].shape), slab_bf[1]

        cw = _pad_rows(0.5 * conv_w[li], SUBLANES)
        cb = 0.5 * conv_b[li][None]
        lru_w = [jnp.concatenate([_block_diag(lru_wa[li, dr]), _block_diag(lru_wx[li, dr])], axis=1).astype(BF16)
                 for dr in range(2)]
        hf = _lru(lx, lru_w[0], 0.5 * lru_ba[li, 0][None], 0.5 * lru_bx[li, 0][None], lru_lambda[li, 0][None],
                  cw, cb, nct, reverse=False)
        lru_y = _lru(lx, lru_w[1], 0.5 * lru_ba[li, 1][None], 0.5 * lru_bx[li, 1][None], lru_lambda[li, 1][None],
                     cw, cb, nct, reverse=True, hf=hf, lg=lg)

        merge_args = (xs, mods, p, attn, lru_y, w_in_bf[:, OFF_GATE:], _pad_rows(0.5 * b_merge[li], SUBLANES),
                      pool_w[li].astype(BF16), pool_scale[li][None], wb_bf, wo_bf, ln1_g[li][None], ln1_b[li][None])
        if dense:
            assert not last, "the dense layer is expected to carry the context tokens along"
            x1, h2 = _merge(*merge_args, nct, lat_only=False, alpha=alpha)
            xs = _swiglu_dense_ln(h2, x1, mods, ln2_g[li][None], ln2_b[li][None], *slab_bf[2:5],
                                  nct=nct, alpha=alpha)
        else:
            assert last, "the expert layer is expected to be the last layer (latent tokens only)"
            n_e = moe_router.shape[2]
            rw = jnp.zeros((d, LANES), F32).at[:, :n_e].set(moe_router[jf])
            rb = jnp.zeros((1, LANES), F32).at[0, :n_e].set(moe_router_b[jf])
            rw_hi = rw.astype(BF16)
            rw_split = jnp.stack([rw_hi, (rw - rw_hi.astype(F32)).astype(BF16)])
            x1, h2, logits = _merge(*merge_args, nct, lat_only=True, alpha=alpha, router=(rw_split, rb))
            y, probs = _moe(h2.reshape(-1, LANES), logits.reshape(b * s, LANES)[:, :n_e], *step_bf,
                            tf=moe_w_gate.shape[3] // 2)
            xs = _combine_ln(x1, mods, ln2_g[li][None], ln2_b[li][None], y, probs.reshape(b, s, -1), alpha)
        if not last:
            w_in_bf = slab_bf[-1]
    return xs
```

```python
import functools

import jax
import jax.numpy as jnp
from jax import lax
from jax.experimental import pallas as pl
from jax.experimental.pallas import tpu as pltpu

F32 = jnp.float32
BF16 = jnp.bfloat16

GRID_W = 64
POOL_WINDOWS = (2, 4, 8, 16)
BRANCH_WIDTH = 512
POOL_GROUP = BRANCH_WIDTH // len(POOL_WINDOWS)
N_Q_HEADS = 8
N_KV_HEADS = 2
HEAD_DIM = 64
Q_WIDTH = N_Q_HEADS * HEAD_DIM
KV_WIDTH = N_KV_HEADS * HEAD_DIM
ROPE_THETA = 10000.0
ATTN_SCALE = HEAD_DIM ** -0.5
LOG2_E = 1.4426950408889634
LRU_WIDTH = BRANCH_WIDTH
CONV_WIDTH = 4
RG_C = 8.0
TOP_K = 2
LN_EPS = 1e-5
RMS_EPS = 1e-6
OFF_GATE = BRANCH_WIDTH + Q_WIDTH + 2 * KV_WIDTH + 2 * LRU_WIDTH

LANES = 128
SUBLANES = 8
V7X_VMEM_BYTES = 64 * 1024 * 1024
VMEM_RESERVE = 8 * 1024 * 1024
VMEM_LIMIT = V7X_VMEM_BYTES - VMEM_RESERVE

ROW_TILE = 256
BATCH_BLOCK = 4
LRU_BATCH_BLOCK = 8
Q_TILE = 256
ATTN_CHAIN = 256
FFN_ROWS = 512
HALO = SUBLANES


def _cparams(*sem):
    return pltpu.CompilerParams(dimension_semantics=sem, vmem_limit_bytes=VMEM_LIMIT)


def _sigmoid(x):
    return 0.5 * jnp.tanh(0.5 * x) + 0.5


def _store_row_tiles(ref, val):
    t, d = val.shape
    for j in range(d // LANES):
        ref[pl.ds(j, t, stride=d // LANES), :] = val[:, j * LANES:(j + 1) * LANES]


def _load_row_tiles(ref, t):
    n = ref.shape[0] // t
    return jnp.concatenate([ref[pl.ds(j, t, stride=n), :] for j in range(n)], axis=1)


def _silu(x):
    return x * _sigmoid(x)


def _layer_norm(v, g, b):
    mu = jnp.mean(v, axis=-1, keepdims=True)
    d = v - mu
    var = jnp.mean(d * d, axis=-1, keepdims=True)
    return d * lax.rsqrt(var + LN_EPS) * g + b


def _mod_kernel(c_ref, w_ref, b_ref, o_ref):
    s = _silu(c_ref[...])
    o_ref[0, 0] = jnp.dot(s, w_ref[0], precision=lax.Precision.HIGHEST,
                          preferred_element_type=F32) + b_ref[0, 0]


def _modulation(cc, w_mod, b_mod):
    depth, d, _ = w_mod.shape
    r = cc.shape[0]
    b6 = b_mod.reshape(depth, 6, 1, d)
    return pl.pallas_call(
        _mod_kernel,
        grid=(depth, 6),
        in_specs=[pl.BlockSpec((r, d), lambda l, j: (0, 0)),
                  pl.BlockSpec((1, d, d), lambda l, j: (l, 0, j)),
                  pl.BlockSpec((1, 1, 1, d), lambda l, j: (l, j, 0, 0))],
        out_specs=pl.BlockSpec((1, 1, r, d), lambda l, j: (l, j, 0, 0)),
        out_shape=jax.ShapeDtypeStruct((depth, 6, r, d), F32),
        compiler_params=_cparams("arbitrary", "arbitrary"),
        name="modulation",
    )(cc, w_mod, b6)


def _norm_rope(zc, bd, g, cos, sin):
    zz = zc * zc
    hi = zz.astype(BF16)
    lo = (zz - hi.astype(F32)).astype(BF16)
    ms = jnp.dot(hi, bd, preferred_element_type=F32) + jnp.dot(lo, bd, preferred_element_type=F32)
    y = zc * lax.rsqrt(ms + RMS_EPS) * g
    nf = HEAD_DIM // 4
    up = pltpu.roll(y, LANES - nf, 1)
    dn = pltpu.roll(y, nf, 1)
    lane = lax.broadcasted_iota(jnp.int32, y.shape, 1)
    partner = jnp.where((lane % (2 * nf)) < nf, up, dn)
    return y * cos + partner * sin


def _proj_kernel(*refs, nct, joined):
    if joined:
        x_ref, mod_ref, w_ref, bd_ref, qg_ref, kg_ref, cos_ref, sin_ref, p_ref, q_ref, k_ref, v_ref, lx_ref, lg_ref = refs
    else:
        (c_ref, x_ref, mod_ref, w_ref, bd_ref, qg_ref, kg_ref, cos_ref, sin_ref,
         p_ref, q_ref, k_ref, v_ref, lx_ref, lg_ref, xs_ref) = refs
    bd = bd_ref[...]
    cos = cos_ref[...]
    sin = sin_ref[...]
    for bb in range(x_ref.shape[0]):
        if joined:
            x = x_ref[bb]
        else:
            x = jnp.where(pl.program_id(1) < nct, c_ref[bb], x_ref[bb])
            xs_ref[bb] = x
        m = mod_ref[bb, 0]
        h = (x * (1.0 + m[1:2]) + m[0:1]).astype(BF16)
        z = jnp.dot(h, w_ref[...], preferred_element_type=F32)
        o = 0
        p_ref[bb] = z[:, o:o + BRANCH_WIDTH]
        o += BRANCH_WIDTH
        for c in range(Q_WIDTH // LANES):
            y = _norm_rope(z[:, o:o + LANES], bd, qg_ref[...], cos, sin) * (ATTN_SCALE * LOG2_E)
            q_ref[bb, 2 * c] = y[:, :HEAD_DIM].astype(BF16)
            q_ref[bb, 2 * c + 1] = y[:, HEAD_DIM:].astype(BF16)
            o += LANES
        for c in range(KV_WIDTH // LANES):
            y = _norm_rope(z[:, o:o + LANES], bd, kg_ref[...], cos, sin)
            k_ref[bb, 2 * c] = y[:, :HEAD_DIM].astype(BF16)
            k_ref[bb, 2 * c + 1] = y[:, HEAD_DIM:].astype(BF16)
            o += LANES
        for c in range(KV_WIDTH // LANES):
            y = z[:, o:o + LANES]
            ones = jnp.ones((y.shape[0], LANES - HEAD_DIM), F32)
            v_ref[bb, 2 * c] = jnp.concatenate([y[:, :HEAD_DIM], ones], axis=1).astype(BF16)
            v_ref[bb, 2 * c + 1] = jnp.concatenate([y[:, HEAD_DIM:], ones], axis=1).astype(BF16)
            o += LANES
        lx_ref[bb] = z[:, o:o + LRU_WIDTH]
        o += LRU_WIDTH
        lg_ref[bb] = z[:, o:o + LRU_WIDTH]


def _project(x, mods, w_a, bd, qg, kg, cos_t, sin_t, nct, ctx=None):
    b, _, d = x.shape
    l = x.shape[1] if ctx is None else x.shape[1] + ctx.shape[1]
    nt = l // ROW_TILE
    wa = w_a.shape[1]
    bb = BATCH_BLOCK
    tok = lambda w: pl.BlockSpec((bb, ROW_TILE, w), lambda bi, i: (bi, i, 0))
    head = lambda n, w=HEAD_DIM: pl.BlockSpec((bb, n, ROW_TILE, w), lambda bi, i: (bi, 0, i, 0))
    const = lambda shp: pl.BlockSpec(shp, lambda bi, i: (0,) * len(shp))
    if ctx is None:
        tokens, token_specs = [x], [tok(d)]
    else:
        tokens = [ctx, x]
        token_specs = [pl.BlockSpec((bb, ROW_TILE, d), lambda bi, i: (bi, jnp.minimum(i, nct - 1), 0)),
                       pl.BlockSpec((bb, ROW_TILE, d), lambda bi, i: (bi, jnp.maximum(i - nct, 0), 0))]
    out_specs = [tok(BRANCH_WIDTH), head(N_Q_HEADS), head(N_KV_HEADS), head(N_KV_HEADS, LANES),
                 tok(LRU_WIDTH), tok(LRU_WIDTH)]
    out_shape = [jax.ShapeDtypeStruct((b, l, BRANCH_WIDTH), F32),
                 jax.ShapeDtypeStruct((b, N_Q_HEADS, l, HEAD_DIM), BF16),
                 jax.ShapeDtypeStruct((b, N_KV_HEADS, l, HEAD_DIM), BF16),
                 jax.ShapeDtypeStruct((b, N_KV_HEADS, l, LANES), BF16),
                 jax.ShapeDtypeStruct((b, l, LRU_WIDTH), F32),
                 jax.ShapeDtypeStruct((b, l, LRU_WIDTH), F32)]
    if ctx is not None:
        out_specs.append(tok(d))
        out_shape.append(jax.ShapeDtypeStruct((b, l, d), F32))
    return pl.pallas_call(
        functools.partial(_proj_kernel, nct=nct, joined=ctx is None),
        grid=(b // bb, nt),
        in_specs=token_specs + [
            pl.BlockSpec((bb, 1, SUBLANES, d), lambda bi, i: (bi, (i >= nct).astype(jnp.int32), 0, 0)),
            const((d, wa)), const((LANES, LANES)), const((1, LANES)), const((1, LANES)),
            pl.BlockSpec((ROW_TILE, LANES), lambda bi, i: (i, 0)),
            pl.BlockSpec((ROW_TILE, LANES), lambda bi, i: (i, 0))],
        out_specs=out_specs,
        out_shape=out_shape,
        compiler_params=_cparams("parallel", "parallel"),
        name="in_proj",
    )(*tokens, mods, w_a, bd, qg, kg, cos_t, sin_t)


def _softmax_pv(q, k, v):
    hd = q.shape[1]
    s = lax.dot_general(q, k, (((1,), (1,)), ((), ())), preferred_element_type=F32)
    m = jnp.max(s, axis=1, keepdims=True)
    p = jnp.exp2(s - m).astype(BF16)
    r = jnp.dot(p, v, preferred_element_type=F32)
    return r[:, :hd] / r[:, hd:2 * hd]


def _attn_kernel(q_ref, k_ref, v_ref, *rest, n_ctx_tiles, ctx_len, n_step_casts, n_slab_casts):
    n_casts = n_step_casts + n_slab_casts
    cast_in, o_ref, cast_out = rest[:n_casts], rest[n_casts], rest[n_casts + 1:]
    for src, dst in zip(cast_in[:n_step_casts], cast_out[:n_step_casts]):
        dst[...] = src[...].astype(dst.dtype)

    @pl.when(pl.program_id(1) == 0)
    def _():
        for src, dst in zip(cast_in[n_step_casts:], cast_out[n_step_casts:]):
            dst[...] = src[...].astype(dst.dtype)

    nh, tq, hd = q_ref.shape[1:]
    group = nh // k_ref.shape[1]

    def run(lk):
        for j in range(nh):
            for r0 in range(0, tq, ATTN_CHAIN):
                o = _softmax_pv(q_ref[0, j, r0:r0 + ATTN_CHAIN, :], k_ref[0, j // group, :lk],
                                v_ref[0, j // group, :lk])
                o_ref[0, r0:r0 + ATTN_CHAIN, j * hd:(j + 1) * hd] = o.astype(o_ref.dtype)

    if n_ctx_tiles:
        @pl.when(pl.program_id(1) < n_ctx_tiles)
        def _():
            run(ctx_len)

        @pl.when(pl.program_id(1) >= n_ctx_tiles)
        def _():
            run(k_ref.shape[2])
    else:
        run(k_ref.shape[2])


def _attention(q, k, v, ctx_len, with_ctx, step_casts=(), slab_casts=()):
    b, nh, l, hd = q.shape
    nct = ctx_len // Q_TILE
    rows = l if with_ctx else l - ctx_len
    q_off = 0 if with_ctx else nct
    nq = rows // Q_TILE
    cast_specs, cast_shapes = [], []
    for w in step_casts:
        assert w.shape[0] == b and w.shape[1] % nq == 0
        cast_specs.append(pl.BlockSpec((1, w.shape[1] // nq, w.shape[2]), lambda bi, i: (bi, i, 0)))
        cast_shapes.append(jax.ShapeDtypeStruct(w.shape, BF16))
    for w in slab_casts:
        assert w.shape[0] % b == 0
        cast_specs.append(pl.BlockSpec((w.shape[0] // b, w.shape[1]), lambda bi, i: (bi, 0)))
        cast_shapes.append(jax.ShapeDtypeStruct(w.shape, BF16))
    return pl.pallas_call(
        functools.partial(_attn_kernel, n_ctx_tiles=nct if with_ctx else 0, ctx_len=ctx_len,
                          n_step_casts=len(step_casts), n_slab_casts=len(slab_casts)),
        grid=(b, nq),
        in_specs=[pl.BlockSpec((1, nh, Q_TILE, hd), lambda bi, i: (bi, 0, i + q_off, 0)),
                  pl.BlockSpec((1,) + k.shape[1:], lambda bi, i: (bi, 0, 0, 0)),
                  pl.BlockSpec((1,) + v.shape[1:], lambda bi, i: (bi, 0, 0, 0))] + cast_specs,
        out_specs=[pl.BlockSpec((1, Q_TILE, nh * hd), lambda bi, i: (bi, i, 0))] + cast_specs,
        out_shape=[jax.ShapeDtypeStruct((b, rows, nh * hd), BF16)] + cast_shapes,
        compiler_params=_cparams("parallel", "arbitrary"),
        name="attention",
    )(q, k, v, *step_casts, *slab_casts)


def _fill_ext(ext_ref, main, prev, nxt, first, last):
    t = main.shape[0]
    ext_ref[0:HALO] = jnp.where(first, 0.0, prev)
    ext_ref[HALO:HALO + t] = main
    ext_ref[HALO + t:2 * HALO + t] = jnp.where(last, 0.0, nxt)


def _seg_flags(tile, nct, nt):
    first = jnp.logical_or(tile == 0, tile == nct)
    last = jnp.logical_or(tile == nct - 1, tile == nt - 1)
    return first, last


def _halo_specs(width, tile_of, l, bb):
    rb = ROW_TILE // HALO
    last_blk = l // HALO - 1
    main = pl.BlockSpec((bb, ROW_TILE, width), lambda bi, i: (bi, tile_of(i), 0))
    prev = pl.BlockSpec((bb, HALO, width), lambda bi, i: (bi, jnp.maximum(tile_of(i) * rb - 1, 0), 0))
    nxt = pl.BlockSpec((bb, HALO, width), lambda bi, i: (bi, jnp.minimum((tile_of(i) + 1) * rb, last_blk), 0))
    return main, prev, nxt


def _softplus(z):
    return jnp.maximum(z, 0.0) + jnp.log1p(jnp.exp(-jnp.abs(z)))


def _gelu_tanh(x):
    return 0.5 * x * (1.0 + jnp.tanh(0.7978845608028654 * (x + 0.044715 * (x * x * x))))


def _lru_kernel(*refs, reverse, nct, nt):
    if reverse:
        (lx_ref, lxp_ref, lxn_ref, w_ref, ba_ref, bx_ref, lam_ref, cw_ref, cb_ref, hf_ref, lg_ref,
         o_ref, ext_ref, a_ref, u_ref, h_ref) = refs
    else:
        (lx_ref, lxp_ref, lxn_ref, w_ref, ba_ref, bx_ref, lam_ref, cw_ref, cb_ref,
         o_ref, ext_ref, a_ref, u_ref, h_ref) = refs
    j = pl.program_id(1)
    tile = _lru_tile(j, reverse, nct, nt)
    t = lx_ref.shape[1]
    nbb = lx_ref.shape[0]

    @pl.when(j == 0)
    def _():
        h_ref[...] = jnp.zeros_like(h_ref)

    first, last = _seg_flags(tile, nct, nt)
    lo = CONV_WIDTH // 2
    half_log2_a = (-0.5 * RG_C * LOG2_E) * _softplus(-lam_ref[...])
    for bb in range(nbb):
        ext = ext_ref.at[bb]
        _fill_ext(ext, lx_ref[bb], lxp_ref[bb], lxn_ref[bb], first, last)
        xh = cb_ref[...]
        for kk in range(CONV_WIDTH):
            xh = xh + ext[pl.ds(HALO + kk - lo, t), :] * cw_ref[kk:kk + 1, :]
        zz = jnp.dot(xh.astype(BF16), w_ref[...], preferred_element_type=F32)
        tr = jnp.tanh(zz[:, :LRU_WIDTH] + ba_ref[...])
        ti = jnp.tanh(zz[:, LRU_WIDTH:] + bx_ref[...])
        a = jnp.exp2(half_log2_a * (tr + 1.0))
        a_ref[bb] = a
        u_ref[bb] = jnp.sqrt(1.0 - a * a) * (ti + 1.0) * xh

    row = lax.broadcasted_iota(jnp.int32, (SUBLANES, LRU_WIDTH), 0)
    n_sub = t // SUBLANES

    def sub(s, hs):
        blk = (n_sub - 1 - s) if reverse else s
        r0 = pl.multiple_of(blk * SUBLANES, SUBLANES)
        out = []
        for bb in range(nbb):
            aa = a_ref[bb, pl.ds(r0, SUBLANES), :]
            uu = u_ref[bb, pl.ds(r0, SUBLANES), :]
            for dd in (1, 2, 4):
                if reverse:
                    keep = row < SUBLANES - dd
                    sh = SUBLANES - dd
                else:
                    keep = row >= dd
                    sh = dd
                a_sh = jnp.where(keep, pltpu.roll(aa, sh, 0), 1.0)
                u_sh = jnp.where(keep, pltpu.roll(uu, sh, 0), 0.0)
                uu = aa * u_sh + uu
                aa = aa * a_sh
            hh = aa * hs[bb] + uu
            if reverse:
                o_ref[bb, pl.ds(r0, SUBLANES), :] = (
                    (hf_ref[bb, pl.ds(r0, SUBLANES), :] + hh)
                    * _gelu_tanh(lg_ref[bb, pl.ds(r0, SUBLANES), :])).astype(o_ref.dtype)
                edge = hh[0:1]
            else:
                o_ref[bb, pl.ds(r0, SUBLANES), :] = hh
                edge = hh[SUBLANES - 1:SUBLANES]
            out.append(jnp.broadcast_to(edge, (SUBLANES, LRU_WIDTH)))
        return tuple(out)

    hs = lax.fori_loop(0, n_sub, sub, tuple(h_ref[bb] for bb in range(nbb)), unroll=4)
    for bb in range(nbb):
        h_ref[bb] = hs[bb]


def _lru_tile(j, reverse, nct, nt):
    if not reverse:
        return j
    return jnp.where(j < nct, nct - 1 - j, nt - 1 - (j - nct))


def _lru(lx, w, ba, bx, lam, cw, cb, nct, reverse, hf=None, lg=None):
    b, l, wd = lx.shape
    nt = l // ROW_TILE
    tile_of = lambda i: _lru_tile(i, reverse, nct, nt)
    bb = LRU_BATCH_BLOCK
    main, prev, nxt = _halo_specs(wd, tile_of, l, bb)
    const = lambda shp: pl.BlockSpec(shp, lambda bi, i: (0,) * len(shp))
    in_specs = [main, prev, nxt, const(w.shape), const((1, wd)), const((1, wd)), const((1, wd)),
                const((SUBLANES, wd)), const((1, wd))]
    args = [lx, lx, lx, w, ba, bx, lam, cw, cb]
    if reverse:
        in_specs += [main, main]
        args += [hf, lg]
    return pl.pallas_call(
        functools.partial(_lru_kernel, reverse=reverse, nct=nct, nt=nt),
        grid=(b // bb, nt),
        in_specs=in_specs,
        out_specs=main,
        out_shape=jax.ShapeDtypeStruct((b, l, wd), BF16 if reverse else F32),
        scratch_shapes=[pltpu.VMEM((bb, ROW_TILE + 2 * HALO, wd), F32), pltpu.VMEM((bb, ROW_TILE, wd), F32),
                        pltpu.VMEM((bb, ROW_TILE, wd), F32), pltpu.VMEM((bb, SUBLANES, wd), F32)],
        compiler_params=_cparams("parallel", "arbitrary"),
        name="lru_bwd" if reverse else "lru_fwd",
    )(*args)


def _merge_kernel(x_ref, mod_ref, p_ref, pp_ref, pn_ref, at_ref, lr_ref, wg_ref, bm_ref, pw_ref, ps_ref,
                  wb_ref, wo_ref, g_ref, b_ref, *rest, nct, nt, tile_off, ctx_len, alpha, route):
    if route:
        rw_ref, rb_ref, x1_ref, h2_ref, logit_ref, ext_ref = rest
    else:
        x1_ref, h2_ref, ext_ref = rest
    tile = pl.program_id(1) + tile_off
    t = x_ref.shape[1]
    d = x_ref.shape[2]
    first, last = _seg_flags(tile, nct, nt)
    in_ctx = tile < nct
    seg_len = jnp.where(in_ctx, ctx_len, nt * t - ctx_len)
    pos = (tile - jnp.where(in_ctx, 0, nct)) * t + lax.broadcasted_iota(jnp.int32, (t, 1), 0)

    for bb in range(x_ref.shape[0]):
        x = x_ref[bb]
        m = mod_ref[bb, 0]
        h_half = ((x * (1.0 + m[1:2]) + m[0:1]) * 0.5).astype(BF16)

        def branch(n, ys):
            tg = jnp.tanh(jnp.dot(h_half, wg_ref[:, n * d:(n + 1) * d], preferred_element_type=F32)
                          + bm_ref[n:n + 1, :])
            return (tg + 1.0) * jnp.dot(ys, wb_ref[n], preferred_element_type=F32)

        mix = branch(1, at_ref[bb]) + branch(2, lr_ref[bb])

        ext = ext_ref.at[bb]
        _fill_ext(ext, p_ref[bb], pp_ref[bb], pn_ref[bb], first, last)
        pooled = []
        for gi, w in enumerate(POOL_WINDOWS):
            lo = w // 2
            cols = slice(gi * POOL_GROUP, (gi + 1) * POOL_GROUP)
            acc = ext[pl.ds(HALO - lo, t), cols]
            for kk in range(1, w):
                acc = acc + ext[pl.ds(HALO - lo + kk, t), cols]
            cnt = jnp.clip(pos - lo + w, 0, seg_len) - jnp.clip(pos - lo, 0, seg_len)
            mean = acc / cnt.astype(F32)
            dlt = (mean - ext[pl.ds(HALO, t), cols]).astype(BF16)
            pooled.append(jnp.dot(dlt, pw_ref[gi], preferred_element_type=F32))
        pool_y = (jnp.concatenate(pooled, axis=1) * ps_ref[...]).astype(BF16)
        mix = branch(0, pool_y) + mix
        y2 = jnp.dot(mix.astype(BF16), wo_ref[...], preferred_element_type=F32)
        x1 = _layer_norm(alpha * x + m[6:7] * y2, g_ref[...], b_ref[...])
        x1_ref[bb] = x1
        h2 = x1 * (1.0 + m[4:5]) + m[3:4]
        if route:
            hi = h2.astype(BF16)
            mid = (h2 - hi.astype(F32)).astype(BF16)
            logit_ref[bb] = (jnp.dot(hi, rw_ref[0], preferred_element_type=F32)
                             + jnp.dot(mid, rw_ref[0], preferred_element_type=F32)
                             + jnp.dot(hi, rw_ref[1], preferred_element_type=F32) + rb_ref[...])
            _store_row_tiles(h2_ref.at[bb], h2)
        else:
            h2_ref[bb] = h2.astype(h2_ref.dtype)


def _merge(x, mods, p, attn, lru, w_gate, b_merge, pool_w, pool_scale, w_branch, w_out, ln_g, ln_b,
           nct, lat_only, alpha, router=None):
    b, l, d = x.shape
    nt = l // ROW_TILE
    off = nct if lat_only else 0
    rows = l - off * ROW_TILE
    tile_of = lambda i: i + off
    bb = BATCH_BLOCK
    tok_l = lambda w: pl.BlockSpec((bb, ROW_TILE, w), lambda bi, i: (bi, i + off, 0))
    tok_o = lambda w: pl.BlockSpec((bb, ROW_TILE, w), lambda bi, i: (bi, i, 0))
    const = lambda shp: pl.BlockSpec(shp, lambda bi, i: (0,) * len(shp), pipeline_mode=pl.Buffered(1))
    pm, pp, pn = _halo_specs(BRANCH_WIDTH, tile_of, l, bb)
    in_specs = [tok_l(d),
                pl.BlockSpec((bb, 1, SUBLANES, d), lambda bi, i: (bi, (i + off >= nct).astype(jnp.int32), 0, 0)),
                pm, pp, pn,
                tok_o(Q_WIDTH) if lat_only else tok_l(Q_WIDTH),
                tok_l(LRU_WIDTH),
                const(w_gate.shape), const(b_merge.shape), const(pool_w.shape), const(pool_scale.shape),
                const(w_branch.shape), const(w_out.shape), const((1, d)), const((1, d))]
    args = [x, mods, p, p, p, attn, lru, w_gate, b_merge, pool_w, pool_scale, w_branch, w_out, ln_g, ln_b]
    out_specs = [tok_o(d)]
    out_shape = [jax.ShapeDtypeStruct((b, rows, d), F32)]
    if router is None:
        out_specs.append(tok_o(d))
        out_shape.append(jax.ShapeDtypeStruct((b, rows, d), BF16))
    else:
        n_sub = d // LANES
        in_specs += [const(router[0].shape), const(router[1].shape)]
        args += list(router)
        out_specs += [pl.BlockSpec((bb, ROW_TILE * n_sub, LANES), lambda bi, i: (bi, i, 0)), tok_o(LANES)]
        out_shape += [jax.ShapeDtypeStruct((b, rows * n_sub, LANES), F32),
                      jax.ShapeDtypeStruct((b, rows, LANES), F32)]
    return pl.pallas_call(
        functools.partial(_merge_kernel, nct=nct, nt=nt, tile_off=off, ctx_len=nct * ROW_TILE, alpha=alpha,
                          route=router is not None),
        grid=(b // bb, rows // ROW_TILE),
        in_specs=in_specs,
        out_specs=out_specs,
        out_shape=out_shape,
        scratch_shapes=[pltpu.VMEM((bb, ROW_TILE + 2 * HALO, BRANCH_WIDTH), F32)],
        compiler_params=_cparams("parallel", "parallel"),
        name="merge_ln1",
    )(*args)


def _swiglu_kernel(x_ref, wg_ref, wu_ref, wd_ref, x1_ref, *rest, alpha):
    mod_refs, (g_ref, b_ref, o_ref) = rest[:-3], rest[-3:]
    x = x_ref[...]
    g = jnp.dot(x, wg_ref[...], preferred_element_type=F32)
    u = jnp.dot(x, wu_ref[...], preferred_element_type=F32)
    ff = jnp.dot((_silu(g) * u).astype(BF16), wd_ref[...], preferred_element_type=F32)
    for hh, mod_ref in enumerate(mod_refs):
        rs = slice(hh * ROW_TILE, (hh + 1) * ROW_TILE)
        o_ref[rs, :] = _layer_norm(alpha * x1_ref[rs, :] + mod_ref[0, 0, 5:6, :] * ff[rs], g_ref[...], b_ref[...])


def _swiglu_dense_ln(h2, x1, mods, ln_g, ln_b, w_gate, w_up, w_down, nct, alpha):
    b, l, d = x1.shape
    r = b * l
    ntl = l // ROW_TILE
    halves = FFN_ROWS // ROW_TILE

    def mod_spec(hh):
        def idx(i):
            tile = i * halves + hh
            return (tile // ntl, (tile % ntl >= nct).astype(jnp.int32), 0, 0)
        return pl.BlockSpec((1, 1, SUBLANES, d), idx)

    resident = lambda shp: pl.BlockSpec(shp, lambda i: (0, 0), pipeline_mode=pl.Buffered(1))
    out = pl.pallas_call(
        functools.partial(_swiglu_kernel, alpha=alpha),
        grid=(r // FFN_ROWS,),
        in_specs=[pl.BlockSpec((FFN_ROWS, d), lambda i: (i, 0)),
                  resident(w_gate.shape), resident(w_up.shape), resident(w_down.shape),
                  pl.BlockSpec((FFN_ROWS, d), lambda i: (i, 0))]
                 + [mod_spec(hh) for hh in range(halves)]
                 + [pl.BlockSpec((1, d), lambda i: (0, 0)), pl.BlockSpec((1, d), lambda i: (0, 0))],
        out_specs=pl.BlockSpec((FFN_ROWS, d), lambda i: (i, 0)),
        out_shape=jax.ShapeDtypeStruct((r, d), F32),
        compiler_params=_cparams("parallel"),
        name="swiglu_dense_ln2",
    )(h2.reshape(r, d), w_gate, w_up, w_down, x1.reshape(r, d), *([mods] * halves), ln_g, ln_b)
    return out.reshape(b, l, d)


def _moe_kernel(be_ref, nu_ref, tok_ref, dst_ref, t_hbm, wg_ref, wu_ref, wd_ref, y_hbm,
                xbuf, acc_ref, obuf, sem_in, sem_out, sem_fill):
    i = pl.program_id(0)
    f = pl.program_id(1)
    nb = pl.num_programs(0)
    nu = nu_ref[0]
    rows = acc_ref.shape[0]
    sub = xbuf.shape[1] // rows
    slot = i % 2

    def tile_rows(start):
        return pl.ds(pl.multiple_of(start, sub), sub)

    def gather_copy(blk, sl, r):
        return pltpu.make_async_copy(t_hbm.at[tile_rows(tok_ref[blk * rows + r])], xbuf.at[sl, tile_rows(r * sub)],
                                     sem_in.at[sl])

    def scatter_copy(blk, sl, r):
        return pltpu.make_async_copy(obuf.at[sl, tile_rows(r * sub)], y_hbm.at[tile_rows(dst_ref[blk * rows + r])],
                                     sem_out.at[sl])

    def gathers_done(sl):
        pltpu.make_async_copy(t_hbm.at[pl.ds(0, rows * sub)], xbuf.at[sl], sem_in.at[sl]).wait()

    def scatters_done(sl):
        pltpu.make_async_copy(obuf.at[sl], y_hbm.at[pl.ds(0, rows * sub)], sem_out.at[sl]).wait()

    def for_rows(fn):
        def body(r, c):
            fn(r)
            return c
        lax.fori_loop(0, rows, body, 0, unroll=8)

    def load_x():
        return _load_row_tiles(xbuf.at[slot], rows).astype(BF16)

    def swiglu_part(x):
        g = jnp.dot(x, wg_ref[0], preferred_element_type=F32)
        u = jnp.dot(x, wu_ref[0], preferred_element_type=F32)
        return jnp.dot((_silu(g) * u).astype(BF16), wd_ref[0], preferred_element_type=F32)

    @pl.when(jnp.logical_and(i == 0, f == 0))
    def _():
        obuf[1] = jnp.zeros(obuf.shape[1:], obuf.dtype)
        for_rows(lambda r: gather_copy(0, 0, r).start())

    @pl.when(jnp.logical_and(f == 0, i <= nu))
    def _():
        gathers_done(slot)

    @pl.when(jnp.logical_and(i < nu, f == 0))
    def _():
        x = load_x()
        for r in range(rows):
            gather_copy(i + 1, 1 - slot, r).start()
        acc_ref[...] = swiglu_part(x)

    @pl.when(jnp.logical_and(i < nu, f == 1))
    def _():
        @pl.when(i >= 1)
        def _():
            scatters_done(slot)

        x = load_x()
        prev = jnp.where(i >= 1, i - 1, nb)
        for r in range(rows):
            scatter_copy(prev, 1 - slot, r).start()
        _store_row_tiles(obuf.at[slot], acc_ref[...] + swiglu_part(x))

        @pl.when(i == nu - 1)
        def _():
            scatters_done(1 - slot)
            for_rows(lambda r: scatter_copy(i, slot, r).start())
            scatters_done(slot)

    @pl.when(jnp.logical_and(i >= nu, f == 1))
    def _():
        xbuf[0] = jnp.zeros(xbuf.shape[1:], xbuf.dtype)
        fill = pltpu.make_async_copy(xbuf.at[0], y_hbm.at[pl.ds(i * rows * sub, rows * sub)], sem_fill)
        fill.start()
        fill.wait()

    @pl.when(jnp.logical_and(jnp.logical_and(i == nb - 1, f == 1), nu == nb))
    def _():
        gathers_done(nb % 2)


def _swiglu_experts(t, block_e, n_used, slot_tok, slot_dst, w_gate, w_up, w_down, tf):
    d = w_gate.shape[1]
    sub = d // LANES
    p = slot_tok.shape[0]
    nf = w_gate.shape[2] // tf
    assert nf == 2

    def fcol(i, f, nu):
        return jnp.where(i < nu[0], f, nf - 1)

    return pl.pallas_call(
        _moe_kernel,
        grid_spec=pltpu.PrefetchScalarGridSpec(
            num_scalar_prefetch=4,
            grid=(p // FFN_ROWS - 1, nf),
            in_specs=[pl.BlockSpec(memory_space=pl.ANY),
                      pl.BlockSpec((1, d, tf), lambda i, f, be, nu, st, sd: (be[i], 0, fcol(i, f, nu))),
                      pl.BlockSpec((1, d, tf), lambda i, f, be, nu, st, sd: (be[i], 0, fcol(i, f, nu))),
                      pl.BlockSpec((1, tf, d), lambda i, f, be, nu, st, sd: (be[i], fcol(i, f, nu), 0))],
            out_specs=pl.BlockSpec(memory_space=pl.ANY),
            scratch_shapes=[pltpu.VMEM((2, FFN_ROWS * sub, LANES), F32), pltpu.VMEM((FFN_ROWS, d), F32),
                            pltpu.VMEM((2, FFN_ROWS * sub, LANES), F32),
                            pltpu.SemaphoreType.DMA((2,)), pltpu.SemaphoreType.DMA((2,)),
                            pltpu.SemaphoreType.DMA]),
        out_shape=jax.ShapeDtypeStruct((p * sub, LANES), F32),
        compiler_params=_cparams("arbitrary", "arbitrary"),
        name="swiglu_experts",
    )(block_e, n_used, slot_tok, slot_dst, t, w_gate, w_up, w_down)


def _combine_ln_kernel(x_ref, mod_ref, g_ref, b_ref, *rest, alpha):
    y_refs, p_ref, o_ref = rest[:-2], rest[-2], rest[-1]
    t = x_ref.shape[1]
    top_k = p_ref.shape[2]
    for bb in range(x_ref.shape[0]):
        f = None
        for kk in range(top_k):
            term = _load_row_tiles(y_refs[bb * top_k + kk], t) * p_ref[bb, :, kk:kk + 1]
            f = term if f is None else f + term
        m = mod_ref[bb, 0]
        o_ref[bb] = _layer_norm(alpha * x_ref[bb] + m[5:6] * f, g_ref[...], b_ref[...])


def _combine_ln(x1, mods, ln_g, ln_b, y, probs, alpha):
    b, rows, d = x1.shape
    nt = rows // ROW_TILE
    sub = d // LANES
    top_k = probs.shape[2]
    nbb = BATCH_BLOCK
    tok = lambda w: pl.BlockSpec((nbb, ROW_TILE, w), lambda bi, i: (bi, i, 0))
    const = lambda shp: pl.BlockSpec(shp, lambda bi, i: (0,) * len(shp))
    y_specs = [pl.BlockSpec((ROW_TILE * sub, LANES),
                            lambda bi, i, kk=kk, bb=bb: ((kk * b + bi * nbb + bb) * nt + i, 0))
               for bb in range(nbb) for kk in range(top_k)]
    return pl.pallas_call(
        functools.partial(_combine_ln_kernel, alpha=alpha),
        grid=(b // nbb, nt),
        in_specs=[tok(d), pl.BlockSpec((nbb, 1, SUBLANES, d), lambda bi, i: (bi, 1, 0, 0)),
                  const((1, d)), const((1, d))] + y_specs + [tok(top_k)],
        out_specs=tok(d),
        out_shape=jax.ShapeDtypeStruct((b, rows, d), F32),
        compiler_params=_cparams("parallel", "parallel"),
        name="combine_ln2",
    )(x1, mods, ln_g, ln_b, *([y] * (top_k * nbb)), probs)


def _moe(t, logits, w_gate, w_up, w_down, tf):
    n, n_e = logits.shape
    sub = t.shape[0] // n
    top_v, top_i = lax.top_k(logits, TOP_K)
    probs = jax.nn.softmax(top_v, axis=-1)
    m = n * TOP_K
    flat_e = top_i.reshape(m)
    counts = jnp.sum((flat_e[:, None] == jnp.arange(n_e, dtype=flat_e.dtype)[None, :]).astype(jnp.int32), axis=0)
    order = jnp.argsort(flat_e, stable=True).astype(jnp.int32)
    start = jnp.cumsum(counts) - counts
    padded = (counts + FFN_ROWS - 1) // FFN_ROWS * FFN_ROWS
    ends_p = jnp.cumsum(padded)
    start_p = ends_p - padded
    nb = -(-(m + n_e * (FFN_ROWS - 1)) // FFN_ROWS)
    p = (nb + 1) * FFN_ROWS
    blk_start = jnp.arange(nb + 1, dtype=jnp.int32) * FFN_ROWS
    block_e = jnp.minimum(jnp.sum((ends_p[None, :] <= blk_start[:, None]).astype(jnp.int32), axis=1), n_e - 1)
    n_used = (ends_p[-1:] // FFN_ROWS).astype(jnp.int32)
    slot_e = jnp.repeat(block_e, FFN_ROWS)
    r = jnp.arange(p, dtype=jnp.int32) - start_p[slot_e]
    is_pad = r >= counts[slot_e]
    slot_pair = order[jnp.clip(start[slot_e] + r, 0, m - 1)]
    slot_tok = jnp.where(is_pad, 0, slot_pair // TOP_K) * sub
    pads_before = (start_p - start)[slot_e] + r - counts[slot_e]
    slot_dst = jnp.where(is_pad, m + pads_before, (slot_pair % TOP_K) * n + slot_pair // TOP_K) * sub
    y = _swiglu_experts(t, block_e, n_used, slot_tok, slot_dst, w_gate, w_up, w_down, tf)
    return y, probs


def _block_diag(w):
    nb, bi, bj = w.shape
    eye = jnp.eye(nb, dtype=w.dtype)
    return (eye[:, None, :, None] * w[:, :, None, :]).reshape(nb * bi, nb * bj)


def _rope_tables(ctx_len, s):
    nf = HEAD_DIM // 4
    inv = ROPE_THETA ** (-jnp.arange(nf, dtype=F32) / nf)
    t = jnp.arange(s)
    row = (t // GRID_W).astype(F32)[:, None] * inv
    col = (t % GRID_W).astype(F32)[:, None] * inv
    ang = jnp.concatenate([row, row, col, col], axis=1)
    sign = jnp.tile(jnp.concatenate([-jnp.ones((nf,), F32), jnp.ones((nf,), F32)]), 2)
    cos = jnp.concatenate([jnp.ones((ctx_len, HEAD_DIM), F32), jnp.cos(ang)], axis=0)
    sin = jnp.concatenate([jnp.zeros((ctx_len, HEAD_DIM), F32), jnp.sin(ang) * sign], axis=0)
    reps = LANES // HEAD_DIM
    return jnp.tile(cos, (1, reps)), jnp.tile(sin, (1, reps))


def _pad_rows(a, rows):
    return jnp.zeros((rows,) + a.shape[1:], a.dtype).at[:a.shape[0]].set(a)


def kernel(x, c, ctx, c_ctx, w_mod, b_mod, w_in, b_merge, pool_w, pool_scale, q_norm, k_norm, conv_w, conv_b, lru_wa, lru_ba, lru_wx, lru_bx, lru_lambda, w_branch, w_out, ln1_g, ln1_b, ffn_w_gate, ffn_w_up, ffn_w_down, moe_router, moe_router_b, moe_w_gate, moe_w_up, moe_w_down, ln2_g, ln2_b):
    b, s, d = x.shape
    ctx_len = ctx.shape[1]
    depth = w_in.shape[0]
    l = ctx_len + s
    assert ctx_len % ROW_TILE == 0 and s % ROW_TILE == 0 and (b * s) % FFN_ROWS == 0
    assert b % BATCH_BLOCK == 0 and b % LRU_BATCH_BLOCK == 0
    assert (b * l) % FFN_ROWS == 0 and ctx_len % Q_TILE == 0 and s % Q_TILE == 0
    nct = ctx_len // ROW_TILE
    alpha = (2 * depth) ** 0.25

    cc = _pad_rows(jnp.concatenate([c, c_ctx[None, :]], axis=0), -(-(b + 1) // SUBLANES) * SUBLANES)
    mod_all = _modulation(cc, w_mod, b_mod)

    cos_t, sin_t = _rope_tables(ctx_len, s)
    bd = _block_diag(jnp.full((LANES // HEAD_DIM, HEAD_DIM, HEAD_DIM), 1.0 / HEAD_DIM, F32)).astype(BF16)
    reps = LANES // HEAD_DIM

    xs = None
    w_in_bf = w_in[0].astype(BF16)
    for li in range(depth):
        last = li == depth - 1
        dense = li % 2 == 0
        jf = li // 2
        ml = jnp.transpose(mod_all[li, :, :b], (1, 0, 2))
        mc = jnp.broadcast_to(mod_all[li, :, b][None], (b, 6, d))
        mods = jnp.stack([mc, ml], axis=1)
        mods = jnp.concatenate([mods, 0.5 * mods[:, :, 2:3], jnp.zeros((b, 2, 1, d), F32)], axis=2)

        proj_args = (mods, w_in_bf[:, :OFF_GATE], bd, jnp.tile(q_norm[li], reps)[None],
                     jnp.tile(k_norm[li], reps)[None], cos_t, sin_t, nct)
        if xs is None:
            p, q, k, v, lx, lg, xs = _project(x, *proj_args, ctx=ctx)
        else:
            p, q, k, v, lx, lg = _project(xs, *proj_args)

        slab = [w_branch[li].reshape(-1, d), w_out[li]]
        if dense:
            slab += [ffn_w_gate[jf], ffn_w_up[jf], ffn_w_down[jf]]
            step = []
        else:
            step = [moe_w_gate[jf], moe_w_up[jf], moe_w_down[jf]]
        if not last:
            slab.append(w_in[li + 1])
        if not all(w.shape[0] == b and w.shape[1] % ((l - (ctx_len if last else 0)) // Q_TILE * 16) == 0
                   for w in step):
            step_bf, step = [w.astype(BF16) for w in step], []
        attn, *made = _attention(q, k, v, ctx_len, with_ctx=not last, step_casts=step, slab_casts=slab)
        if step:
            step_bf = made[:len(step)]
        slab_bf = made[len(step):]
        wb_bf, wo_bf = slab_bf[0].reshape(w_branch[li].shape), slab_bf[1]

        cw = _pad_rows(0.5 * conv_w[li], SUBLANES)
        cb = 0.5 * conv_b[li][None]
        lru_w = [jnp.concatenate([_block_diag(lru_wa[li, dr]), _block_diag(lru_wx[li, dr])], axis=1).astype(BF16)
                 for dr in range(2)]
        hf = _lru(lx, lru_w[0], 0.5 * lru_ba[li, 0][None], 0.5 * lru_bx[li, 0][None], lru_lambda[li, 0][None],
                  cw, cb, nct, reverse=False)
        lru_y = _lru(lx, lru_w[1], 0.5 * lru_ba[li, 1][None], 0.5 * lru_bx[li, 1][None], lru_lambda[li, 1][None],
                     cw, cb, nct, reverse=True, hf=hf, lg=lg)

        merge_args = (xs, mods, p, attn, lru_y, w_in_bf[:, OFF_GATE:], _pad_rows(0.5 * b_merge[li], SUBLANES),
                      pool_w[li].astype(BF16), pool_scale[li][None], wb_bf, wo_bf, ln1_g[li][None], ln1_b[li][None])
        if dense:
            assert not last, "the dense layer is expected to carry the context tokens along"
            x1, h2 = _merge(*merge_args, nct, lat_only=False, alpha=alpha)
            xs = _swiglu_dense_ln(h2, x1, mods, ln2_g[li][None], ln2_b[li][None], *slab_bf[2:5],
                                  nct=nct, alpha=alpha)
        else:
            assert last, "the expert layer is expected to be the last layer (latent tokens only)"
            n_e = moe_router.shape[2]
            rw = jnp.zeros((d, LANES), F32).at[:, :n_e].set(moe_router[jf])
            rb = jnp.zeros((1, LANES), F32).at[0, :n_e].set(moe_router_b[jf])
            rw_hi = rw.astype(BF16)
            rw_split = jnp.stack([rw_hi, (rw - rw_hi.astype(F32)).astype(BF16)])
            x1, h2, logits = _merge(*merge_args, nct, lat_only=True, alpha=alpha, router=(rw_split, rb))
            y, probs = _moe(h2.reshape(-1, LANES), logits.reshape(b * s, LANES)[:, :n_e], *step_bf,
                            tf=moe_w_gate.shape[3] // 2)
            xs = _combine_ln(x1, mods, ln2_g[li][None], ln2_b[li][None], y, probs.reshape(b, s, -1), alpha)
        if not last:
            w_in_bf = slab_bf[-1]
    return xs
```

```python
import functools

import jax
import jax.numpy as jnp
from jax import lax
from jax.experimental import pallas as pl
from jax.experimental.pallas import tpu as pltpu

F32 = jnp.float32
BF16 = jnp.bfloat16

GRID_W = 64
POOL_WINDOWS = (2, 4, 8, 16)
BRANCH_WIDTH = 512
POOL_GROUP = BRANCH_WIDTH // len(POOL_WINDOWS)
N_Q_HEADS = 8
N_KV_HEADS = 2
HEAD_DIM = 64
Q_WIDTH = N_Q_HEADS * HEAD_DIM
KV_WIDTH = N_KV_HEADS * HEAD_DIM
ROPE_THETA = 10000.0
ATTN_SCALE = HEAD_DIM ** -0.5
LOG2_E = 1.4426950408889634
LRU_WIDTH = BRANCH_WIDTH
CONV_WIDTH = 4
RG_C = 8.0
TOP_K = 2
LN_EPS = 1e-5
RMS_EPS = 1e-6
OFF_GATE = BRANCH_WIDTH + Q_WIDTH + 2 * KV_WIDTH + 2 * LRU_WIDTH

LANES = 128
SUBLANES = 8
V7X_VMEM_BYTES = 64 * 1024 * 1024
VMEM_RESERVE = 8 * 1024 * 1024
VMEM_LIMIT = V7X_VMEM_BYTES - VMEM_RESERVE

ROW_TILE = 256
BATCH_BLOCK = 4
LRU_BATCH_BLOCK = 8
Q_TILE = 256
ATTN_CHAIN = 256
FFN_ROWS = 512
HALO = SUBLANES


def _cparams(*sem):
    return pltpu.CompilerParams(dimension_semantics=sem, vmem_limit_bytes=VMEM_LIMIT)


def _sigmoid(x):
    return 0.5 * jnp.tanh(0.5 * x) + 0.5


def _store_row_tiles(ref, val):
    t, d = val.shape
    for j in range(d // LANES):
        ref[pl.ds(j, t, stride=d // LANES), :] = val[:, j * LANES:(j + 1) * LANES]


def _load_row_tiles(ref, t):
    n = ref.shape[0] // t
    return jnp.concatenate([ref[pl.ds(j, t, stride=n), :] for j in range(n)], axis=1)


def _silu(x):
    return x * _sigmoid(x)


def _layer_norm(v, g, b):
    mu = jnp.mean(v, axis=-1, keepdims=True)
    d = v - mu
    var = jnp.mean(d * d, axis=-1, keepdims=True)
    return d * lax.rsqrt(var + LN_EPS) * g + b


def _mod_kernel(c_ref, w_ref, b_ref, o_ref):
    s = _silu(c_ref[...])
    o_ref[0, 0] = jnp.dot(s, w_ref[0], precision=lax.Precision.HIGHEST,
                          preferred_element_type=F32) + b_ref[0, 0]


def _modulation(cc, w_mod, b_mod):
    depth, d, _ = w_mod.shape
    r = cc.shape[0]
    b6 = b_mod.reshape(depth, 6, 1, d)
    return pl.pallas_call(
        _mod_kernel,
        grid=(depth, 6),
        in_specs=[pl.BlockSpec((r, d), lambda l, j: (0, 0)),
                  pl.BlockSpec((1, d, d), lambda l, j: (l, 0, j)),
                  pl.BlockSpec((1, 1, 1, d), lambda l, j: (l, j, 0, 0))],
        out_specs=pl.BlockSpec((1, 1, r, d), lambda l, j: (l, j, 0, 0)),
        out_shape=jax.ShapeDtypeStruct((depth, 6, r, d), F32),
        compiler_params=_cparams("arbitrary", "arbitrary"),
        name="modulation",
    )(cc, w_mod, b6)


def _norm_rope(zc, bd, g, cos, sin):
    zz = zc * zc
    hi = zz.astype(BF16)
    lo = (zz - hi.astype(F32)).astype(BF16)
    ms = jnp.dot(hi, bd, preferred_element_type=F32) + jnp.dot(lo, bd, preferred_element_type=F32)
    y = zc * lax.rsqrt(ms + RMS_EPS) * g
    nf = HEAD_DIM // 4
    up = pltpu.roll(y, LANES - nf, 1)
    dn = pltpu.roll(y, nf, 1)
    lane = lax.broadcasted_iota(jnp.int32, y.shape, 1)
    partner = jnp.where((lane % (2 * nf)) < nf, up, dn)
    return y * cos + partner * sin


def _proj_kernel(*refs, nct, joined):
    if joined:
        x_ref, mod_ref, w_ref, bd_ref, qg_ref, kg_ref, cos_ref, sin_ref, p_ref, q_ref, k_ref, v_ref, lx_ref, lg_ref = refs
    else:
        (c_ref, x_ref, mod_ref, w_ref, bd_ref, qg_ref, kg_ref, cos_ref, sin_ref,
         p_ref, q_ref, k_ref, v_ref, lx_ref, lg_ref, xs_ref) = refs
    bd = bd_ref[...]
    cos = cos_ref[...]
    sin = sin_ref[...]
    for bb in range(x_ref.shape[0]):
        if joined:
            x = x_ref[bb]
        else:
            x = jnp.where(pl.program_id(1) < nct, c_ref[bb], x_ref[bb])
            xs_ref[bb] = x
        m = mod_ref[bb, 0]
        h = (x * (1.0 + m[1:2]) + m[0:1]).astype(BF16)
        z = jnp.dot(h, w_ref[...], preferred_element_type=F32)
        o = 0
        p_ref[bb] = z[:, o:o + BRANCH_WIDTH]
        o += BRANCH_WIDTH
        for c in range(Q_WIDTH // LANES):
            y = _norm_rope(z[:, o:o + LANES], bd, qg_ref[...], cos, sin) * (ATTN_SCALE * LOG2_E)
            q_ref[bb, 2 * c] = y[:, :HEAD_DIM].astype(BF16)
            q_ref[bb, 2 * c + 1] = y[:, HEAD_DIM:].astype(BF16)
            o += LANES
        for c in range(KV_WIDTH // LANES):
            y = _norm_rope(z[:, o:o + LANES], bd, kg_ref[...], cos, sin)
            k_ref[bb, 2 * c] = y[:, :HEAD_DIM].astype(BF16)
            k_ref[bb, 2 * c + 1] = y[:, HEAD_DIM:].astype(BF16)
            o += LANES
        for c in range(KV_WIDTH // LANES):
            y = z[:, o:o + LANES]
            ones = jnp.ones((y.shape[0], LANES - HEAD_DIM), F32)
            v_ref[bb, 2 * c] = jnp.concatenate([y[:, :HEAD_DIM], ones], axis=1).astype(BF16)
            v_ref[bb, 2 * c + 1] = jnp.concatenate([y[:, HEAD_DIM:], ones], axis=1).astype(BF16)
            o += LANES
        lx_ref[bb] = z[:, o:o + LRU_WIDTH]
        o += LRU_WIDTH
        lg_ref[bb] = z[:, o:o + LRU_WIDTH]


def _project(x, mods, w_a, bd, qg, kg, cos_t, sin_t, nct, ctx=None):
    b, _, d = x.shape
    l = x.shape[1] if ctx is None else x.shape[1] + ctx.shape[1]
    nt = l // ROW_TILE
    wa = w_a.shape[1]
    bb = BATCH_BLOCK
    tok = lambda w: pl.BlockSpec((bb, ROW_TILE, w), lambda bi, i: (bi, i, 0))
    head = lambda n, w=HEAD_DIM: pl.BlockSpec((bb, n, ROW_TILE, w), lambda bi, i: (bi, 0, i, 0))
    const = lambda shp: pl.BlockSpec(shp, lambda bi, i: (0,) * len(shp))
    if ctx is None:
        tokens, token_specs = [x], [tok(d)]
    else:
        tokens = [ctx, x]
        token_specs = [pl.BlockSpec((bb, ROW_TILE, d), lambda bi, i: (bi, jnp.minimum(i, nct - 1), 0)),
                       pl.BlockSpec((bb, ROW_TILE, d), lambda bi, i: (bi, jnp.maximum(i - nct, 0), 0))]
    out_specs = [tok(BRANCH_WIDTH), head(N_Q_HEADS), head(N_KV_HEADS), head(N_KV_HEADS, LANES),
                 tok(LRU_WIDTH), tok(LRU_WIDTH)]
    out_shape = [jax.ShapeDtypeStruct((b, l, BRANCH_WIDTH), F32),
                 jax.ShapeDtypeStruct((b, N_Q_HEADS, l, HEAD_DIM), BF16),
                 jax.ShapeDtypeStruct((b, N_KV_HEADS, l, HEAD_DIM), BF16),
                 jax.ShapeDtypeStruct((b, N_KV_HEADS, l, LANES), BF16),
                 jax.ShapeDtypeStruct((b, l, LRU_WIDTH), F32),
                 jax.ShapeDtypeStruct((b, l, LRU_WIDTH), F32)]
    if ctx is not None:
        out_specs.append(tok(d))
        out_shape.append(jax.ShapeDtypeStruct((b, l, d), F32))
    return pl.pallas_call(
        functools.partial(_proj_kernel, nct=nct, joined=ctx is None),
        grid=(b // bb, nt),
        in_specs=token_specs + [
            pl.BlockSpec((bb, 1, SUBLANES, d), lambda bi, i: (bi, (i >= nct).astype(jnp.int32), 0, 0)),
            const((d, wa)), const((LANES, LANES)), const((1, LANES)), const((1, LANES)),
            pl.BlockSpec((ROW_TILE, LANES), lambda bi, i: (i, 0)),
            pl.BlockSpec((ROW_TILE, LANES), lambda bi, i: (i, 0))],
        out_specs=out_specs,
        out_shape=out_shape,
        compiler_params=_cparams("parallel", "parallel"),
        name="in_proj",
    )(*tokens, mods, w_a, bd, qg, kg, cos_t, sin_t)


def _softmax_pv(q, k, v):
    hd = q.shape[1]
    s = lax.dot_general(q, k, (((1,), (1,)), ((), ())), preferred_element_type=F32)
    m = jnp.max(s, axis=1, keepdims=True)
    p = jnp.exp2(s - m).astype(BF16)
    r = jnp.dot(p, v, preferred_element_type=F32)
    return r[:, :hd] / r[:, hd:2 * hd]


def _attn_kernel(q_ref, k_ref, v_ref, *rest, n_ctx_tiles, ctx_len, n_step_casts, n_slab_casts):
    n_casts = n_step_casts + n_slab_casts
    cast_in, o_ref, cast_out = rest[:n_casts], rest[n_casts], rest[n_casts + 1:]
    for src, dst in zip(cast_in[:n_step_casts], cast_out[:n_step_casts]):
        dst[...] = src[...].astype(dst.dtype)

    @pl.when(pl.program_id(1) == 0)
    def _():
        for src, dst in zip(cast_in[n_step_casts:], cast_out[n_step_casts:]):
            dst[...] = src[...].astype(dst.dtype)

    nh, tq, hd = q_ref.shape[1:]
    group = nh // k_ref.shape[1]

    def run(lk):
        for j in range(nh):
            for r0 in range(0, tq, ATTN_CHAIN):
                o = _softmax_pv(q_ref[0, j, r0:r0 + ATTN_CHAIN, :], k_ref[0, j // group, :lk],
                                v_ref[0, j // group, :lk])
                o_ref[0, r0:r0 + ATTN_CHAIN, j * hd:(j + 1) * hd] = o.astype(o_ref.dtype)

    if n_ctx_tiles:
        @pl.when(pl.program_id(1) < n_ctx_tiles)
        def _():
            run(ctx_len)

        @pl.when(pl.program_id(1) >= n_ctx_tiles)
        def _():
            run(k_ref.shape[2])
    else:
        run(k_ref.shape[2])


def _attention(q, k, v, ctx_len, with_ctx, step_casts=(), slab_casts=()):
    b, nh, l, hd = q.shape
    nct = ctx_len // Q_TILE
    rows = l if with_ctx else l - ctx_len
    q_off = 0 if with_ctx else nct
    nq = rows // Q_TILE
    cast_specs, cast_shapes = [], []
    for w in step_casts:
        assert w.shape[0] == b and w.shape[1] % nq == 0
        cast_specs.append(pl.BlockSpec((1, w.shape[1] // nq, w.shape[2]), lambda bi, i: (bi, i, 0)))
        cast_shapes.append(jax.ShapeDtypeStruct(w.shape, BF16))
    for w in slab_casts:
        assert w.shape[0] % b == 0
        cast_specs.append(pl.BlockSpec((w.shape[0] // b, w.shape[1]), lambda bi, i: (bi, 0)))
        cast_shapes.append(jax.ShapeDtypeStruct(w.shape, BF16))
    return pl.pallas_call(
        functools.partial(_attn_kernel, n_ctx_tiles=nct if with_ctx else 0, ctx_len=ctx_len,
                          n_step_casts=len(step_casts), n_slab_casts=len(slab_casts)),
        grid=(b, nq),
        in_specs=[pl.BlockSpec((1, nh, Q_TILE, hd), lambda bi, i: (bi, 0, i + q_off, 0)),
                  pl.BlockSpec((1,) + k.shape[1:], lambda bi, i: (bi, 0, 0, 0)),
                  pl.BlockSpec((1,) + v.shape[1:], lambda bi, i: (bi, 0, 0, 0))] + cast_specs,
        out_specs=[pl.BlockSpec((1, Q_TILE, nh * hd), lambda bi, i: (bi, i, 0))] + cast_specs,
        out_shape=[jax.ShapeDtypeStruct((b, rows, nh * hd), BF16)] + cast_shapes,
        compiler_params=_cparams("parallel", "arbitrary"),
        name="attention",
    )(q, k, v, *step_casts, *slab_casts)


def _fill_ext(ext_ref, main, prev, nxt, first, last):
    t = main.shape[0]
    ext_ref[0:HALO] = jnp.where(first, 0.0, prev)
    ext_ref[HALO:HALO + t] = main
    ext_ref[HALO + t:2 * HALO + t] = jnp.where(last, 0.0, nxt)


def _seg_flags(tile, nct, nt):
    first = jnp.logical_or(tile == 0, tile == nct)
    last = jnp.logical_or(tile == nct - 1, tile == nt - 1)
    return first, last


def _halo_specs(width, tile_of, l, bb):
    rb = ROW_TILE // HALO
    last_blk = l // HALO - 1
    main = pl.BlockSpec((bb, ROW_TILE, width), lambda bi, i: (bi, tile_of(i), 0))
    prev = pl.BlockSpec((bb, HALO, width), lambda bi, i: (bi, jnp.maximum(tile_of(i) * rb - 1, 0), 0))
    nxt = pl.BlockSpec((bb, HALO, width), lambda bi, i: (bi, jnp.minimum((tile_of(i) + 1) * rb, last_blk), 0))
    return main, prev, nxt


def _softplus(z):
    return jnp.maximum(z, 0.0) + jnp.log1p(jnp.exp(-jnp.abs(z)))


def _gelu_tanh(x):
    return 0.5 * x * (1.0 + jnp.tanh(0.7978845608028654 * (x + 0.044715 * (x * x * x))))


def _lru_kernel(*refs, reverse, nct, nt):
    if reverse:
        (lx_ref, lxp_ref, lxn_ref, w_ref, ba_ref, bx_ref, lam_ref, cw_ref, cb_ref, hf_ref, lg_ref,
         o_ref, ext_ref, a_ref, u_ref, h_ref) = refs
    else:
        (lx_ref, lxp_ref, lxn_ref, w_ref, ba_ref, bx_ref, lam_ref, cw_ref, cb_ref,
         o_ref, ext_ref, a_ref, u_ref, h_ref) = refs
    j = pl.program_id(1)
    tile = _lru_tile(j, reverse, nct, nt)
    t = lx_ref.shape[1]
    nbb = lx_ref.shape[0]

    @pl.when(j == 0)
    def _():
        h_ref[...] = jnp.zeros_like(h_ref)

    first, last = _seg_flags(tile, nct, nt)
    lo = CONV_WIDTH // 2
    half_log2_a = (-0.5 * RG_C * LOG2_E) * _softplus(-lam_ref[...])
    for bb in range(nbb):
        ext = ext_ref.at[bb]
        _fill_ext(ext, lx_ref[bb], lxp_ref[bb], lxn_ref[bb], first, last)
        ev = ext[...]
        xh = cb_ref[...]
        for kk in range(CONV_WIDTH):
            tap = ev if kk == lo else pltpu.roll(ev, (lo - kk) % ev.shape[0], 0)
            xh = xh + tap[HALO:HALO + t] * cw_ref[kk:kk + 1, :]
        zz = jnp.dot(xh.astype(BF16), w_ref[...], preferred_element_type=F32)
        tr = jnp.tanh(zz[:, :LRU_WIDTH] + ba_ref[...])
        ti = jnp.tanh(zz[:, LRU_WIDTH:] + bx_ref[...])
        a = jnp.exp2(half_log2_a * (tr + 1.0))
        a_ref[bb] = a
        u_ref[bb] = jnp.sqrt(1.0 - a * a) * (ti + 1.0) * xh

    row = lax.broadcasted_iota(jnp.int32, (SUBLANES, LRU_WIDTH), 0)
    n_sub = t // SUBLANES

    def sub(s, hs):
        blk = (n_sub - 1 - s) if reverse else s
        r0 = pl.multiple_of(blk * SUBLANES, SUBLANES)
        out = []
        for bb in range(nbb):
            aa = a_ref[bb, pl.ds(r0, SUBLANES), :]
            uu = u_ref[bb, pl.ds(r0, SUBLANES), :]
            for dd in (1, 2, 4):
                if reverse:
                    keep = row < SUBLANES - dd
                    sh = SUBLANES - dd
                else:
                    keep = row >= dd
                    sh = dd
                a_sh = jnp.where(keep, pltpu.roll(aa, sh, 0), 1.0)
                u_sh = jnp.where(keep, pltpu.roll(uu, sh, 0), 0.0)
                uu = aa * u_sh + uu
                aa = aa * a_sh
            hh = aa * hs[bb] + uu
            if reverse:
                o_ref[bb, pl.ds(r0, SUBLANES), :] = (
                    (hf_ref[bb, pl.ds(r0, SUBLANES), :] + hh)
                    * _gelu_tanh(lg_ref[bb, pl.ds(r0, SUBLANES), :])).astype(o_ref.dtype)
                edge = hh[0:1]
            else:
                o_ref[bb, pl.ds(r0, SUBLANES), :] = hh
                edge = hh[SUBLANES - 1:SUBLANES]
            out.append(jnp.broadcast_to(edge, (SUBLANES, LRU_WIDTH)))
        return tuple(out)

    hs = lax.fori_loop(0, n_sub, sub, tuple(h_ref[bb] for bb in range(nbb)), unroll=4)
    for bb in range(nbb):
        h_ref[bb] = hs[bb]


def _lru_tile(j, reverse, nct, nt):
    if not reverse:
        return j
    return jnp.where(j < nct, nct - 1 - j, nt - 1 - (j - nct))


def _lru(lx, w, ba, bx, lam, cw, cb, nct, reverse, hf=None, lg=None):
    b, l, wd = lx.shape
    nt = l // ROW_TILE
    tile_of = lambda i: _lru_tile(i, reverse, nct, nt)
    bb = LRU_BATCH_BLOCK
    main, prev, nxt = _halo_specs(wd, tile_of, l, bb)
    const = lambda shp: pl.BlockSpec(shp, lambda bi, i: (0,) * len(shp))
    in_specs = [main, prev, nxt, const(w.shape), const((1, wd)), const((1, wd)), const((1, wd)),
                const((SUBLANES, wd)), const((1, wd))]
    args = [lx, lx, lx, w, ba, bx, lam, cw, cb]
    if reverse:
        in_specs += [main, main]
        args += [hf, lg]
    return pl.pallas_call(
        functools.partial(_lru_kernel, reverse=reverse, nct=nct, nt=nt),
        grid=(b // bb, nt),
        in_specs=in_specs,
        out_specs=main,
        out_shape=jax.ShapeDtypeStruct((b, l, wd), BF16 if reverse else F32),
        scratch_shapes=[pltpu.VMEM((bb, ROW_TILE + 2 * HALO, wd), F32), pltpu.VMEM((bb, ROW_TILE, wd), F32),
                        pltpu.VMEM((bb, ROW_TILE, wd), F32), pltpu.VMEM((bb, SUBLANES, wd), F32)],
        compiler_params=_cparams("parallel", "arbitrary"),
        name="lru_bwd" if reverse else "lru_fwd",
    )(*args)


def _merge_kernel(x_ref, mod_ref, p_ref, pp_ref, pn_ref, at_ref, lr_ref, wg_ref, bm_ref, pw_ref, ps_ref,
                  wb_ref, wo_ref, g_ref, b_ref, *rest, nct, nt, tile_off, ctx_len, alpha, route):
    if route:
        rw_ref, rb_ref, x1_ref, h2_ref, logit_ref, ext_ref = rest
    else:
        x1_ref, h2_ref, ext_ref = rest
    tile = pl.program_id(1) + tile_off
    t = x_ref.shape[1]
    d = x_ref.shape[2]
    first, last = _seg_flags(tile, nct, nt)
    in_ctx = tile < nct
    seg_len = jnp.where(in_ctx, ctx_len, nt * t - ctx_len)
    pos = (tile - jnp.where(in_ctx, 0, nct)) * t + lax.broadcasted_iota(jnp.int32, (t, 1), 0)

    for bb in range(x_ref.shape[0]):
        x = x_ref[bb]
        m = mod_ref[bb, 0]
        h_half = ((x * (1.0 + m[1:2]) + m[0:1]) * 0.5).astype(BF16)

        def branch(n, ys):
            tg = jnp.tanh(jnp.dot(h_half, wg_ref[:, n * d:(n + 1) * d], preferred_element_type=F32)
                          + bm_ref[n:n + 1, :])
            return (tg + 1.0) * jnp.dot(ys, wb_ref[n], preferred_element_type=F32)

        mix = branch(1, at_ref[bb]) + branch(2, lr_ref[bb])

        ext = ext_ref.at[bb]
        _fill_ext(ext, p_ref[bb], pp_ref[bb], pn_ref[bb], first, last)
        pooled = []
        for gi, w in enumerate(POOL_WINDOWS):
            lo = w // 2
            ev = ext[:, gi * POOL_GROUP:(gi + 1) * POOL_GROUP]
            n = ev.shape[0]
            run, span = ev, 1
            while span < lo:
                run = run + pltpu.roll(run, span, 0)
                span *= 2
            acc = pltpu.roll(run, 1, 0) + pltpu.roll(run, (1 - lo) % n, 0) if lo > 1 else run + pltpu.roll(run, 1, 0)
            cnt = jnp.clip(pos - lo + w, 0, seg_len) - jnp.clip(pos - lo, 0, seg_len)
            mean = acc[HALO:HALO + t] / cnt.astype(F32)
            dlt = (mean - ev[HALO:HALO + t]).astype(BF16)
            pooled.append(jnp.dot(dlt, pw_ref[gi], preferred_element_type=F32))
        pool_y = (jnp.concatenate(pooled, axis=1) * ps_ref[...]).astype(BF16)
        mix = branch(0, pool_y) + mix
        y2 = jnp.dot(mix.astype(BF16), wo_ref[...], preferred_element_type=F32)
        x1 = _layer_norm(alpha * x + m[6:7] * y2, g_ref[...], b_ref[...])
        x1_ref[bb] = x1
        h2 = x1 * (1.0 + m[4:5]) + m[3:4]
        if route:
            hi = h2.astype(BF16)
            mid = (h2 - hi.astype(F32)).astype(BF16)
            logit_ref[bb] = (jnp.dot(hi, rw_ref[0], preferred_element_type=F32)
                             + jnp.dot(mid, rw_ref[0], preferred_element_type=F32)
                             + jnp.dot(hi, rw_ref[1], preferred_element_type=F32) + rb_ref[...])
            _store_row_tiles(h2_ref.at[bb], h2)
        else:
            h2_ref[bb] = h2.astype(h2_ref.dtype)


def _merge(x, mods, p, attn, lru, w_gate, b_merge, pool_w, pool_scale, w_branch, w_out, ln_g, ln_b,
           nct, lat_only, alpha, router=None):
    b, l, d = x.shape
    nt = l // ROW_TILE
    off = nct if lat_only else 0
    rows = l - off * ROW_TILE
    tile_of = lambda i: i + off
    bb = BATCH_BLOCK
    tok_l = lambda w: pl.BlockSpec((bb, ROW_TILE, w), lambda bi, i: (bi, i + off, 0))
    tok_o = lambda w: pl.BlockSpec((bb, ROW_TILE, w), lambda bi, i: (bi, i, 0))
    const = lambda shp: pl.BlockSpec(shp, lambda bi, i: (0,) * len(shp), pipeline_mode=pl.Buffered(1))
    pm, pp, pn = _halo_specs(BRANCH_WIDTH, tile_of, l, bb)
    in_specs = [tok_l(d),
                pl.BlockSpec((bb, 1, SUBLANES, d), lambda bi, i: (bi, (i + off >= nct).astype(jnp.int32), 0, 0)),
                pm, pp, pn,
                tok_o(Q_WIDTH) if lat_only else tok_l(Q_WIDTH),
                tok_l(LRU_WIDTH),
                const(w_gate.shape), const(b_merge.shape), const(pool_w.shape), const(pool_scale.shape),
                const(w_branch.shape), const(w_out.shape), const((1, d)), const((1, d))]
    args = [x, mods, p, p, p, attn, lru, w_gate, b_merge, pool_w, pool_scale, w_branch, w_out, ln_g, ln_b]
    out_specs = [tok_o(d)]
    out_shape = [jax.ShapeDtypeStruct((b, rows, d), F32)]
    if router is None:
        out_specs.append(tok_o(d))
        out_shape.append(jax.ShapeDtypeStruct((b, rows, d), BF16))
    else:
        n_sub = d // LANES
        in_specs += [const(router[0].shape), const(router[1].shape)]
        args += list(router)
        out_specs += [pl.BlockSpec((bb, ROW_TILE * n_sub, LANES), lambda bi, i: (bi, i, 0)), tok_o(LANES)]
        out_shape += [jax.ShapeDtypeStruct((b, rows * n_sub, LANES), F32),
                      jax.ShapeDtypeStruct((b, rows, LANES), F32)]
    return pl.pallas_call(
        functools.partial(_merge_kernel, nct=nct, nt=nt, tile_off=off, ctx_len=nct * ROW_TILE, alpha=alpha,
                          route=router is not None),
        grid=(b // bb, rows // ROW_TILE),
        in_specs=in_specs,
        out_specs=out_specs,
        out_shape=out_shape,
        scratch_shapes=[pltpu.VMEM((bb, ROW_TILE + 2 * HALO, BRANCH_WIDTH), F32)],
        compiler_params=_cparams("parallel", "parallel"),
        name="merge_ln1",
    )(*args)


def _swiglu_kernel(x_ref, wg_ref, wu_ref, wd_ref, x1_ref, *rest, alpha):
    mod_refs, (g_ref, b_ref, o_ref) = rest[:-3], rest[-3:]
    x = x_ref[...]
    g = jnp.dot(x, wg_ref[...], preferred_element_type=F32)
    u = jnp.dot(x, wu_ref[...], preferred_element_type=F32)
    ff = jnp.dot((_silu(g) * u).astype(BF16), wd_ref[...], preferred_element_type=F32)
    for hh, mod_ref in enumerate(mod_refs):
        rs = slice(hh * ROW_TILE, (hh + 1) * ROW_TILE)
        o_ref[rs, :] = _layer_norm(alpha * x1_ref[rs, :] + mod_ref[0, 0, 5:6, :] * ff[rs], g_ref[...], b_ref[...])


def _swiglu_dense_ln(h2, x1, mods, ln_g, ln_b, w_gate, w_up, w_down, nct, alpha):
    b, l, d = x1.shape
    r = b * l
    ntl = l // ROW_TILE
    halves = FFN_ROWS // ROW_TILE

    def mod_spec(hh):
        def idx(i):
            tile = i * halves + hh
            return (tile // ntl, (tile % ntl >= nct).astype(jnp.int32), 0, 0)
        return pl.BlockSpec((1, 1, SUBLANES, d), idx)

    resident = lambda shp: pl.BlockSpec(shp, lambda i: (0, 0), pipeline_mode=pl.Buffered(1))
    out = pl.pallas_call(
        functools.partial(_swiglu_kernel, alpha=alpha),
        grid=(r // FFN_ROWS,),
        in_specs=[pl.BlockSpec((FFN_ROWS, d), lambda i: (i, 0)),
                  resident(w_gate.shape), resident(w_up.shape), resident(w_down.shape),
                  pl.BlockSpec((FFN_ROWS, d), lambda i: (i, 0))]
                 + [mod_spec(hh) for hh in range(halves)]
                 + [pl.BlockSpec((1, d), lambda i: (0, 0)), pl.BlockSpec((1, d), lambda i: (0, 0))],
        out_specs=pl.BlockSpec((FFN_ROWS, d), lambda i: (i, 0)),
        out_shape=jax.ShapeDtypeStruct((r, d), F32),
        compiler_params=_cparams("parallel"),
        name="swiglu_dense_ln2",
    )(h2.reshape(r, d), w_gate, w_up, w_down, x1.reshape(r, d), *([mods] * halves), ln_g, ln_b)
    return out.reshape(b, l, d)


def _moe_kernel(be_ref, nu_ref, tok_ref, dst_ref, t_hbm, wg_ref, wu_ref, wd_ref, y_hbm,
                xbuf, acc_ref, obuf, sem_in, sem_out, sem_fill):
    i = pl.program_id(0)
    f = pl.program_id(1)
    nb = pl.num_programs(0)
    nu = nu_ref[0]
    rows = acc_ref.shape[0]
    sub = xbuf.shape[1] // rows
    slot = i % 2

    def tile_rows(start):
        return pl.ds(pl.multiple_of(start, sub), sub)

    def gather_copy(blk, sl, r):
        return pltpu.make_async_copy(t_hbm.at[tile_rows(tok_ref[blk * rows + r])], xbuf.at[sl, tile_rows(r * sub)],
                                     sem_in.at[sl])

    def scatter_copy(blk, sl, r):
        return pltpu.make_async_copy(obuf.at[sl, tile_rows(r * sub)], y_hbm.at[tile_rows(dst_ref[blk * rows + r])],
                                     sem_out.at[sl])

    def gathers_done(sl):
        pltpu.make_async_copy(t_hbm.at[pl.ds(0, rows * sub)], xbuf.at[sl], sem_in.at[sl]).wait()

    def scatters_done(sl):
        pltpu.make_async_copy(obuf.at[sl], y_hbm.at[pl.ds(0, rows * sub)], sem_out.at[sl]).wait()

    def for_rows(fn):
        def body(r, c):
            fn(r)
            return c
        lax.fori_loop(0, rows, body, 0, unroll=8)

    def load_x():
        return _load_row_tiles(xbuf.at[slot], rows).astype(BF16)

    def swiglu_part(x):
        g = jnp.dot(x, wg_ref[0], preferred_element_type=F32)
        u = jnp.dot(x, wu_ref[0], preferred_element_type=F32)
        return jnp.dot((_silu(g) * u).astype(BF16), wd_ref[0], preferred_element_type=F32)

    @pl.when(jnp.logical_and(i == 0, f == 0))
    def _():
        obuf[1] = jnp.zeros(obuf.shape[1:], obuf.dtype)
        for_rows(lambda r: gather_copy(0, 0, r).start())

    @pl.when(jnp.logical_and(f == 0, i <= nu))
    def _():
        gathers_done(slot)

    @pl.when(jnp.logical_and(i < nu, f == 0))
    def _():
        x = load_x()
        for r in range(rows):
            gather_copy(i + 1, 1 - slot, r).start()
        acc_ref[...] = swiglu_part(x)

    @pl.when(jnp.logical_and(i < nu, f == 1))
    def _():
        @pl.when(i >= 1)
        def _():
            scatters_done(slot)

        x = load_x()
        prev = jnp.where(i >= 1, i - 1, nb)
        for r in range(rows):
            scatter_copy(prev, 1 - slot, r).start()
        _store_row_tiles(obuf.at[slot], acc_ref[...] + swiglu_part(x))

        @pl.when(i == nu - 1)
        def _():
            scatters_done(1 - slot)
            for_rows(lambda r: scatter_copy(i, slot, r).start())
            scatters_done(slot)

    @pl.when(jnp.logical_and(i >= nu, f == 1))
    def _():
        xbuf[0] = jnp.zeros(xbuf.shape[1:], xbuf.dtype)
        fill = pltpu.make_async_copy(xbuf.at[0], y_hbm.at[pl.ds(i * rows * sub, rows * sub)], sem_fill)
        fill.start()
        fill.wait()

    @pl.when(jnp.logical_and(jnp.logical_and(i == nb - 1, f == 1), nu == nb))
    def _():
        gathers_done(nb % 2)


def _swiglu_experts(t, block_e, n_used, slot_tok, slot_dst, w_gate, w_up, w_down, tf):
    d = w_gate.shape[1]
    sub = d // LANES
    p = slot_tok.shape[0]
    nf = w_gate.shape[2] // tf
    assert nf == 2

    def fcol(i, f, nu):
        return jnp.where(i < nu[0], f, nf - 1)

    return pl.pallas_call(
        _moe_kernel,
        grid_spec=pltpu.PrefetchScalarGridSpec(
            num_scalar_prefetch=4,
            grid=(p // FFN_ROWS - 1, nf),
            in_specs=[pl.BlockSpec(memory_space=pl.ANY),
                      pl.BlockSpec((1, d, tf), lambda i, f, be, nu, st, sd: (be[i], 0, fcol(i, f, nu))),
                      pl.BlockSpec((1, d, tf), lambda i, f, be, nu, st, sd: (be[i], 0, fcol(i, f, nu))),
                      pl.BlockSpec((1, tf, d), lambda i, f, be, nu, st, sd: (be[i], fcol(i, f, nu), 0))],
            out_specs=pl.BlockSpec(memory_space=pl.ANY),
            scratch_shapes=[pltpu.VMEM((2, FFN_ROWS * sub, LANES), F32), pltpu.VMEM((FFN_ROWS, d), F32),
                            pltpu.VMEM((2, FFN_ROWS * sub, LANES), F32),
                            pltpu.SemaphoreType.DMA((2,)), pltpu.SemaphoreType.DMA((2,)),
                            pltpu.SemaphoreType.DMA]),
        out_shape=jax.ShapeDtypeStruct((p * sub, LANES), F32),
        compiler_params=_cparams("arbitrary", "arbitrary"),
        name="swiglu_experts",
    )(block_e, n_used, slot_tok, slot_dst, t, w_gate, w_up, w_down)


def _combine_ln_kernel(x_ref, mod_ref, g_ref, b_ref, *rest, alpha):
    y_refs, p_ref, o_ref = rest[:-2], rest[-2], rest[-1]
    t = x_ref.shape[1]
    top_k = p_ref.shape[2]
    for bb in range(x_ref.shape[0]):
        f = None
        for kk in range(top_k):
            term = _load_row_tiles(y_refs[bb * top_k + kk], t) * p_ref[bb, :, kk:kk + 1]
            f = term if f is None else f + term
        m = mod_ref[bb, 0]
        o_ref[bb] = _layer_norm(alpha * x_ref[bb] + m[5:6] * f, g_ref[...], b_ref[...])


def _combine_ln(x1, mods, ln_g, ln_b, y, probs, alpha):
    b, rows, d = x1.shape
    nt = rows // ROW_TILE
    sub = d // LANES
    top_k = probs.shape[2]
    nbb = BATCH_BLOCK
    tok = lambda w: pl.BlockSpec((nbb, ROW_TILE, w), lambda bi, i: (bi, i, 0))
    const = lambda shp: pl.BlockSpec(shp, lambda bi, i: (0,) * len(shp))
    y_specs = [pl.BlockSpec((ROW_TILE * sub, LANES),
                            lambda bi, i, kk=kk, bb=bb: ((kk * b + bi * nbb + bb) * nt + i, 0))
               for bb in range(nbb) for kk in range(top_k)]
    return pl.pallas_call(
        functools.partial(_combine_ln_kernel, alpha=alpha),
        grid=(b // nbb, nt),
        in_specs=[tok(d), pl.BlockSpec((nbb, 1, SUBLANES, d), lambda bi, i: (bi, 1, 0, 0)),
                  const((1, d)), const((1, d))] + y_specs + [tok(top_k)],
        out_specs=tok(d),
        out_shape=jax.ShapeDtypeStruct((b, rows, d), F32),
        compiler_params=_cparams("parallel", "parallel"),
        name="combine_ln2",
    )(x1, mods, ln_g, ln_b, *([y] * (top_k * nbb)), probs)


def _moe(t, logits, w_gate, w_up, w_down, tf):
    n, n_e = logits.shape
    sub = t.shape[0] // n
    top_v, top_i = lax.top_k(logits, TOP_K)
    probs = jax.nn.softmax(top_v, axis=-1)
    m = n * TOP_K
    flat_e = top_i.reshape(m)
    counts = jnp.sum((flat_e[:, None] == jnp.arange(n_e, dtype=flat_e.dtype)[None, :]).astype(jnp.int32), axis=0)
    order = jnp.argsort(flat_e, stable=True).astype(jnp.int32)
    start = jnp.cumsum(counts) - counts
    padded = (counts + FFN_ROWS - 1) // FFN_ROWS * FFN_ROWS
    ends_p = jnp.cumsum(padded)
    start_p = ends_p - padded
    nb = -(-(m + n_e * (FFN_ROWS - 1)) // FFN_ROWS)
    p = (nb + 1) * FFN_ROWS
    blk_start = jnp.arange(nb + 1, dtype=jnp.int32) * FFN_ROWS
    block_e = jnp.minimum(jnp.sum((ends_p[None, :] <= blk_start[:, None]).astype(jnp.int32), axis=1), n_e - 1)
    n_used = (ends_p[-1:] // FFN_ROWS).astype(jnp.int32)
    slot_e = jnp.repeat(block_e, FFN_ROWS)
    r = jnp.arange(p, dtype=jnp.int32) - start_p[slot_e]
    is_pad = r >= counts[slot_e]
    slot_pair = order[jnp.clip(start[slot_e] + r, 0, m - 1)]
    slot_tok = jnp.where(is_pad, 0, slot_pair // TOP_K) * sub
    pads_before = (start_p - start)[slot_e] + r - counts[slot_e]
    slot_dst = jnp.where(is_pad, m + pads_before, (slot_pair % TOP_K) * n + slot_pair // TOP_K) * sub
    y = _swiglu_experts(t, block_e, n_used, slot_tok, slot_dst, w_gate, w_up, w_down, tf)
    return y, probs


def _block_diag(w):
    nb, bi, bj = w.shape
    eye = jnp.eye(nb, dtype=w.dtype)
    return (eye[:, None, :, None] * w[:, :, None, :]).reshape(nb * bi, nb * bj)


def _rope_tables(ctx_len, s):
    nf = HEAD_DIM // 4
    inv = ROPE_THETA ** (-jnp.arange(nf, dtype=F32) / nf)
    t = jnp.arange(s)
    row = (t // GRID_W).astype(F32)[:, None] * inv
    col = (t % GRID_W).astype(F32)[:, None] * inv
    ang = jnp.concatenate([row, row, col, col], axis=1)
    sign = jnp.tile(jnp.concatenate([-jnp.ones((nf,), F32), jnp.ones((nf,), F32)]), 2)
    cos = jnp.concatenate([jnp.ones((ctx_len, HEAD_DIM), F32), jnp.cos(ang)], axis=0)
    sin = jnp.concatenate([jnp.zeros((ctx_len, HEAD_DIM), F32), jnp.sin(ang) * sign], axis=0)
    reps = LANES // HEAD_DIM
    return jnp.tile(cos, (1, reps)), jnp.tile(sin, (1, reps))


def _pad_rows(a, rows):
    return jnp.zeros((rows,) + a.shape[1:], a.dtype).at[:a.shape[0]].set(a)


def kernel(x, c, ctx, c_ctx, w_mod, b_mod, w_in, b_merge, pool_w, pool_scale, q_norm, k_norm, conv_w, conv_b, lru_wa, lru_ba, lru_wx, lru_bx, lru_lambda, w_branch, w_out, ln1_g, ln1_b, ffn_w_gate, ffn_w_up, ffn_w_down, moe_router, moe_router_b, moe_w_gate, moe_w_up, moe_w_down, ln2_g, ln2_b):
    b, s, d = x.shape
    ctx_len = ctx.shape[1]
    depth = w_in.shape[0]
    l = ctx_len + s
    assert ctx_len % ROW_TILE == 0 and s % ROW_TILE == 0 and (b * s) % FFN_ROWS == 0
    assert b % BATCH_BLOCK == 0 and b % LRU_BATCH_BLOCK == 0
    assert (b * l) % FFN_ROWS == 0 and ctx_len % Q_TILE == 0 and s % Q_TILE == 0
    nct = ctx_len // ROW_TILE
    alpha = (2 * depth) ** 0.25

    cc = _pad_rows(jnp.concatenate([c, c_ctx[None, :]], axis=0), -(-(b + 1) // SUBLANES) * SUBLANES)
    mod_all = _modulation(cc, w_mod, b_mod)

    cos_t, sin_t = _rope_tables(ctx_len, s)
    bd = _block_diag(jnp.full((LANES // HEAD_DIM, HEAD_DIM, HEAD_DIM), 1.0 / HEAD_DIM, F32)).astype(BF16)
    reps = LANES // HEAD_DIM

    xs = None
    w_in_bf = w_in[0].astype(BF16)
    for li in range(depth):
        last = li == depth - 1
        dense = li % 2 == 0
        jf = li // 2
        ml = jnp.transpose(mod_all[li, :, :b], (1, 0, 2))
        mc = jnp.broadcast_to(mod_all[li, :, b][None], (b, 6, d))
        mods = jnp.stack([mc, ml], axis=1)
        mods = jnp.concatenate([mods, 0.5 * mods[:, :, 2:3], jnp.zeros((b, 2, 1, d), F32)], axis=2)

        proj_args = (mods, w_in_bf[:, :OFF_GATE], bd, jnp.tile(q_norm[li], reps)[None],
                     jnp.tile(k_norm[li], reps)[None], cos_t, sin_t, nct)
        if xs is None:
            p, q, k, v, lx, lg, xs = _project(x, *proj_args, ctx=ctx)
        else:
            p, q, k, v, lx, lg = _project(xs, *proj_args)

        slab = [w_branch[li].reshape(-1, d), w_out[li]]
        if dense:
            slab += [ffn_w_gate[jf], ffn_w_up[jf], ffn_w_down[jf]]
            step = []
        else:
            step = [moe_w_gate[jf], moe_w_up[jf], moe_w_down[jf]]
        if not last:
            slab.append(w_in[li + 1])
        if not all(w.shape[0] == b and w.shape[1] % ((l - (ctx_len if last else 0)) // Q_TILE * 16) == 0
                   for w in step):
            step_bf, step = [w.astype(BF16) for w in step], []
        attn, *made = _attention(q, k, v, ctx_len, with_ctx=not last, step_casts=step, slab_casts=slab)
        if step:
            step_bf = made[:len(step)]
        slab_bf = made[len(step):]
        wb_bf, wo_bf = slab_bf[0].reshape(w_branch[li].shape), slab_bf[1]

        cw = _pad_rows(0.5 * conv_w[li], SUBLANES)
        cb = 0.5 * conv_b[li][None]
        lru_w = [jnp.concatenate([_block_diag(lru_wa[li, dr]), _block_diag(lru_wx[li, dr])], axis=1).astype(BF16)
                 for dr in range(2)]
        hf = _lru(lx, lru_w[0], 0.5 * lru_ba[li, 0][None], 0.5 * lru_bx[li, 0][None], lru_lambda[li, 0][None],
                  cw, cb, nct, reverse=False)
        lru_y = _lru(lx, lru_w[1], 0.5 * lru_ba[li, 1][None], 0.5 * lru_bx[li, 1][None], lru_lambda[li, 1][None],
                     cw, cb, nct, reverse=True, hf=hf, lg=lg)

        merge_args = (xs, mods, p, attn, lru_y, w_in_bf[:, OFF_GATE:], _pad_rows(0.5 * b_merge[li], SUBLANES),
                      pool_w[li].astype(BF16), pool_scale[li][None], wb_bf, wo_bf, ln1_g[li][None], ln1_b[li][None])
        if dense:
            assert not last, "the dense layer is expected to carry the context tokens along"
            x1, h2 = _merge(*merge_args, nct, lat_only=False, alpha=alpha)
            xs = _swiglu_dense_ln(h2, x1, mods, ln2_g[li][None], ln2_b[li][None], *slab_bf[2:5],
                                  nct=nct, alpha=alpha)
        else:
            assert last, "the expert layer is expected to be the last layer (latent tokens only)"
            n_e = moe_router.shape[2]
            rw = jnp.zeros((d, LANES), F32).at[:, :n_e].set(moe_router[jf])
            rb = jnp.zeros((1, LANES), F32).at[0, :n_e].set(moe_router_b[jf])
            rw_hi = rw.astype(BF16)
            rw_split = jnp.stack([rw_hi, (rw - rw_hi.astype(F32)).astype(BF16)])
            x1, h2, logits = _merge(*merge_args, nct, lat_only=True, alpha=alpha, router=(rw_split, rb))
            y, probs = _moe(h2.reshape(-1, LANES), logits.reshape(b * s, LANES)[:, :n_e], *step_bf,
                            tf=moe_w_gate.shape[3] // 2)
            xs = _combine_ln(x1, mods, ln2_g[li][None], ln2_b[li][None], y, probs.reshape(b, s, -1), alpha)
        if not last:
            w_in_bf = slab_bf[-1]
    return xs
```

```python
import functools

import jax
import jax.numpy as jnp
from jax import lax
from jax.experimental import pallas as pl
from jax.experimental.pallas import tpu as pltpu

F32 = jnp.float32
BF16 = jnp.bfloat16

GRID_W = 64
POOL_WINDOWS = (2, 4, 8, 16)
BRANCH_WIDTH = 512
POOL_GROUP = BRANCH_WIDTH // len(POOL_WINDOWS)
N_Q_HEADS = 8
N_KV_HEADS = 2
HEAD_DIM = 64
Q_WIDTH = N_Q_HEADS * HEAD_DIM
KV_WIDTH = N_KV_HEADS * HEAD_DIM
ROPE_THETA = 10000.0
ATTN_SCALE = HEAD_DIM ** -0.5
LOG2_E = 1.4426950408889634
LRU_WIDTH = BRANCH_WIDTH
CONV_WIDTH = 4
RG_C = 8.0
TOP_K = 2
LN_EPS = 1e-5
RMS_EPS = 1e-6
OFF_GATE = BRANCH_WIDTH + Q_WIDTH + 2 * KV_WIDTH + 2 * LRU_WIDTH

LANES = 128
SUBLANES = 8
V7X_VMEM_BYTES = 64 * 1024 * 1024
VMEM_RESERVE = 8 * 1024 * 1024
VMEM_LIMIT = V7X_VMEM_BYTES - VMEM_RESERVE

ROW_TILE = 256
BATCH_BLOCK = 4
LRU_BATCH_BLOCK = 4
Q_TILE = 256
ATTN_CHAIN = 256
FFN_ROWS = 512
HALO = SUBLANES


def _cparams(*sem):
    return pltpu.CompilerParams(dimension_semantics=sem, vmem_limit_bytes=VMEM_LIMIT)


def _sigmoid(x):
    return 0.5 * jnp.tanh(0.5 * x) + 0.5


def _store_row_tiles(ref, val):
    t, d = val.shape
    for j in range(d // LANES):
        ref[pl.ds(j, t, stride=d // LANES), :] = val[:, j * LANES:(j + 1) * LANES]


def _load_row_tiles(ref, t):
    n = ref.shape[0] // t
    return jnp.concatenate([ref[pl.ds(j, t, stride=n), :] for j in range(n)], axis=1)


def _silu(x):
    return x * _sigmoid(x)


def _layer_norm(v, g, b):
    mu = jnp.mean(v, axis=-1, keepdims=True)
    d = v - mu
    var = jnp.mean(d * d, axis=-1, keepdims=True)
    return d * lax.rsqrt(var + LN_EPS) * g + b


def _mod_kernel(c_ref, w_ref, b_ref, o_ref):
    s = _silu(c_ref[...])
    o_ref[0, 0] = jnp.dot(s, w_ref[0], precision=lax.Precision.HIGHEST,
                          preferred_element_type=F32) + b_ref[0, 0]


def _modulation(cc, w_mod, b_mod):
    depth, d, _ = w_mod.shape
    r = cc.shape[0]
    b6 = b_mod.reshape(depth, 6, 1, d)
    return pl.pallas_call(
        _mod_kernel,
        grid=(depth, 6),
        in_specs=[pl.BlockSpec((r, d), lambda l, j: (0, 0)),
                  pl.BlockSpec((1, d, d), lambda l, j: (l, 0, j)),
                  pl.BlockSpec((1, 1, 1, d), lambda l, j: (l, j, 0, 0))],
        out_specs=pl.BlockSpec((1, 1, r, d), lambda l, j: (l, j, 0, 0)),
        out_shape=jax.ShapeDtypeStruct((depth, 6, r, d), F32),
        compiler_params=_cparams("arbitrary", "arbitrary"),
        name="modulation",
    )(cc, w_mod, b6)


def _norm_rope(zc, bd, g, cos, sin):
    zz = zc * zc
    hi = zz.astype(BF16)
    lo = (zz - hi.astype(F32)).astype(BF16)
    ms = jnp.dot(hi, bd, preferred_element_type=F32) + jnp.dot(lo, bd, preferred_element_type=F32)
    y = zc * lax.rsqrt(ms + RMS_EPS) * g
    nf = HEAD_DIM // 4
    up = pltpu.roll(y, LANES - nf, 1)
    dn = pltpu.roll(y, nf, 1)
    lane = lax.broadcasted_iota(jnp.int32, y.shape, 1)
    partner = jnp.where((lane % (2 * nf)) < nf, up, dn)
    return y * cos + partner * sin


def _proj_kernel(*refs, nct, joined):
    if joined:
        x_ref, mod_ref, w_ref, bd_ref, qg_ref, kg_ref, cos_ref, sin_ref, p_ref, q_ref, k_ref, v_ref, lx_ref, lg_ref = refs
    else:
        (c_ref, x_ref, mod_ref, w_ref, bd_ref, qg_ref, kg_ref, cos_ref, sin_ref,
         p_ref, q_ref, k_ref, v_ref, lx_ref, lg_ref, xs_ref) = refs
    bd = bd_ref[...]
    cos = cos_ref[...]
    sin = sin_ref[...]
    for bb in range(x_ref.shape[0]):
        if joined:
            x = x_ref[bb]
        else:
            x = jnp.where(pl.program_id(1) < nct, c_ref[bb], x_ref[bb])
            xs_ref[bb] = x
        m = mod_ref[bb, 0]
        h = (x * (1.0 + m[1:2]) + m[0:1]).astype(BF16)
        z = jnp.dot(h, w_ref[...], preferred_element_type=F32)
        o = 0
        p_ref[bb] = z[:, o:o + BRANCH_WIDTH]
        o += BRANCH_WIDTH
        for c in range(Q_WIDTH // LANES):
            y = _norm_rope(z[:, o:o + LANES], bd, qg_ref[...], cos, sin) * (ATTN_SCALE * LOG2_E)
            q_ref[bb, 2 * c] = y[:, :HEAD_DIM].astype(BF16)
            q_ref[bb, 2 * c + 1] = y[:, HEAD_DIM:].astype(BF16)
            o += LANES
        for c in range(KV_WIDTH // LANES):
            y = _norm_rope(z[:, o:o + LANES], bd, kg_ref[...], cos, sin)
            k_ref[bb, 2 * c] = y[:, :HEAD_DIM].astype(BF16)
            k_ref[bb, 2 * c + 1] = y[:, HEAD_DIM:].astype(BF16)
            o += LANES
        for c in range(KV_WIDTH // LANES):
            y = z[:, o:o + LANES]
            ones = jnp.ones((y.shape[0], LANES - HEAD_DIM), F32)
            v_ref[bb, 2 * c] = jnp.concatenate([y[:, :HEAD_DIM], ones], axis=1).astype(BF16)
            v_ref[bb, 2 * c + 1] = jnp.concatenate([y[:, HEAD_DIM:], ones], axis=1).astype(BF16)
            o += LANES
        lx_ref[bb] = z[:, o:o + LRU_WIDTH]
        o += LRU_WIDTH
        lg_ref[bb] = z[:, o:o + LRU_WIDTH]


def _project(x, mods, w_a, bd, qg, kg, cos_t, sin_t, nct, ctx=None):
    b, _, d = x.shape
    l = x.shape[1] if ctx is None else x.shape[1] + ctx.shape[1]
    nt = l // ROW_TILE
    wa = w_a.shape[1]
    bb = BATCH_BLOCK
    tok = lambda w: pl.BlockSpec((bb, ROW_TILE, w), lambda bi, i: (bi, i, 0))
    head = lambda n, w=HEAD_DIM: pl.BlockSpec((bb, n, ROW_TILE, w), lambda bi, i: (bi, 0, i, 0))
    const = lambda shp: pl.BlockSpec(shp, lambda bi, i: (0,) * len(shp))
    if ctx is None:
        tokens, token_specs = [x], [tok(d)]
    else:
        tokens = [ctx, x]
        token_specs = [pl.BlockSpec((bb, ROW_TILE, d), lambda bi, i: (bi, jnp.minimum(i, nct - 1), 0)),
                       pl.BlockSpec((bb, ROW_TILE, d), lambda bi, i: (bi, jnp.maximum(i - nct, 0), 0))]
    out_specs = [tok(BRANCH_WIDTH), head(N_Q_HEADS), head(N_KV_HEADS), head(N_KV_HEADS, LANES),
                 tok(LRU_WIDTH), tok(LRU_WIDTH)]
    out_shape = [jax.ShapeDtypeStruct((b, l, BRANCH_WIDTH), F32),
                 jax.ShapeDtypeStruct((b, N_Q_HEADS, l, HEAD_DIM), BF16),
                 jax.ShapeDtypeStruct((b, N_KV_HEADS, l, HEAD_DIM), BF16),
                 jax.ShapeDtypeStruct((b, N_KV_HEADS, l, LANES), BF16),
                 jax.ShapeDtypeStruct((b, l, LRU_WIDTH), F32),
                 jax.ShapeDtypeStruct((b, l, LRU_WIDTH), F32)]
    if ctx is not None:
        out_specs.append(tok(d))
        out_shape.append(jax.ShapeDtypeStruct((b, l, d), F32))
    return pl.pallas_call(
        functools.partial(_proj_kernel, nct=nct, joined=ctx is None),
        grid=(b // bb, nt),
        in_specs=token_specs + [
            pl.BlockSpec((bb, 1, SUBLANES, d), lambda bi, i: (bi, (i >= nct).astype(jnp.int32), 0, 0)),
            const((d, wa)), const((LANES, LANES)), const((1, LANES)), const((1, LANES)),
            pl.BlockSpec((ROW_TILE, LANES), lambda bi, i: (i, 0)),
            pl.BlockSpec((ROW_TILE, LANES), lambda bi, i: (i, 0))],
        out_specs=out_specs,
        out_shape=out_shape,
        compiler_params=_cparams("parallel", "parallel"),
        name="in_proj",
    )(*tokens, mods, w_a, bd, qg, kg, cos_t, sin_t)


def _softmax_pv(q, k, v):
    hd = q.shape[1]
    s = lax.dot_general(q, k, (((1,), (1,)), ((), ())), preferred_element_type=F32)
    m = jnp.max(s, axis=1, keepdims=True)
    p = jnp.exp2(s - m).astype(BF16)
    r = jnp.dot(p, v, preferred_element_type=F32)
    return r[:, :hd] / r[:, hd:2 * hd]


def _attn_kernel(q_ref, k_ref, v_ref, *rest, n_ctx_tiles, ctx_len, n_step_casts, n_slab_casts):
    n_casts = n_step_casts + n_slab_casts
    cast_in, o_ref, cast_out = rest[:n_casts], rest[n_casts], rest[n_casts + 1:]
    for src, dst in zip(cast_in[:n_step_casts], cast_out[:n_step_casts]):
        dst[...] = src[...].astype(dst.dtype)

    @pl.when(pl.program_id(1) == 0)
    def _():
        for src, dst in zip(cast_in[n_step_casts:], cast_out[n_step_casts:]):
            dst[...] = src[...].astype(dst.dtype)

    nh, tq, hd = q_ref.shape[1:]
    group = nh // k_ref.shape[1]

    def run(lk):
        for j in range(nh):
            for r0 in range(0, tq, ATTN_CHAIN):
                o = _softmax_pv(q_ref[0, j, r0:r0 + ATTN_CHAIN, :], k_ref[0, j // group, :lk],
                                v_ref[0, j // group, :lk])
                o_ref[0, r0:r0 + ATTN_CHAIN, j * hd:(j + 1) * hd] = o.astype(o_ref.dtype)

    if n_ctx_tiles:
        @pl.when(pl.program_id(1) < n_ctx_tiles)
        def _():
            run(ctx_len)

        @pl.when(pl.program_id(1) >= n_ctx_tiles)
        def _():
            run(k_ref.shape[2])
    else:
        run(k_ref.shape[2])


def _attention(q, k, v, ctx_len, with_ctx, step_casts=(), slab_casts=()):
    b, nh, l, hd = q.shape
    nct = ctx_len // Q_TILE
    rows = l if with_ctx else l - ctx_len
    q_off = 0 if with_ctx else nct
    nq = rows // Q_TILE
    cast_specs, cast_shapes = [], []
    for w in step_casts:
        assert w.shape[0] == b and w.shape[1] % nq == 0
        cast_specs.append(pl.BlockSpec((1, w.shape[1] // nq, w.shape[2]), lambda bi, i: (bi, i, 0)))
        cast_shapes.append(jax.ShapeDtypeStruct(w.shape, BF16))
    for w in slab_casts:
        assert w.shape[0] % b == 0
        cast_specs.append(pl.BlockSpec((w.shape[0] // b, w.shape[1]), lambda bi, i: (bi, 0)))
        cast_shapes.append(jax.ShapeDtypeStruct(w.shape, BF16))
    return pl.pallas_call(
        functools.partial(_attn_kernel, n_ctx_tiles=nct if with_ctx else 0, ctx_len=ctx_len,
                          n_step_casts=len(step_casts), n_slab_casts=len(slab_casts)),
        grid=(b, nq),
        in_specs=[pl.BlockSpec((1, nh, Q_TILE, hd), lambda bi, i: (bi, 0, i + q_off, 0)),
                  pl.BlockSpec((1,) + k.shape[1:], lambda bi, i: (bi, 0, 0, 0)),
                  pl.BlockSpec((1,) + v.shape[1:], lambda bi, i: (bi, 0, 0, 0))] + cast_specs,
        out_specs=[pl.BlockSpec((1, Q_TILE, nh * hd), lambda bi, i: (bi, i, 0))] + cast_specs,
        out_shape=[jax.ShapeDtypeStruct((b, rows, nh * hd), BF16)] + cast_shapes,
        compiler_params=_cparams("parallel", "arbitrary"),
        name="attention",
    )(q, k, v, *step_casts, *slab_casts)


def _fill_ext(ext_ref, main, prev, nxt, first, last):
    t = main.shape[0]
    ext_ref[0:HALO] = jnp.where(first, 0.0, prev)
    ext_ref[HALO:HALO + t] = main
    ext_ref[HALO + t:2 * HALO + t] = jnp.where(last, 0.0, nxt)


def _seg_flags(tile, nct, nt):
    first = jnp.logical_or(tile == 0, tile == nct)
    last = jnp.logical_or(tile == nct - 1, tile == nt - 1)
    return first, last


def _halo_specs(width, tile_of, l, bb):
    rb = ROW_TILE // HALO
    last_blk = l // HALO - 1
    main = pl.BlockSpec((bb, ROW_TILE, width), lambda bi, i: (bi, tile_of(i), 0))
    prev = pl.BlockSpec((bb, HALO, width), lambda bi, i: (bi, jnp.maximum(tile_of(i) * rb - 1, 0), 0))
    nxt = pl.BlockSpec((bb, HALO, width), lambda bi, i: (bi, jnp.minimum((tile_of(i) + 1) * rb, last_blk), 0))
    return main, prev, nxt


def _softplus(z):
    return jnp.maximum(z, 0.0) + jnp.log1p(jnp.exp(-jnp.abs(z)))


def _gelu_tanh(x):
    return 0.5 * x * (1.0 + jnp.tanh(0.7978845608028654 * (x + 0.044715 * (x * x * x))))


def _lru_scan(a_of, u_of, emit, h_ref, reverse, nbb, t):
    row = lax.broadcasted_iota(jnp.int32, (SUBLANES, LRU_WIDTH), 0)
    n_sub = t // SUBLANES

    def sub(s, hs):
        blk = (n_sub - 1 - s) if reverse else s
        r0 = pl.multiple_of(blk * SUBLANES, SUBLANES)
        out = []
        for bb in range(nbb):
            aa = a_of(bb, r0)
            uu = u_of(bb, r0)
            for dd in (1, 2, 4):
                if reverse:
                    keep = row < SUBLANES - dd
                    sh = SUBLANES - dd
                else:
                    keep = row >= dd
                    sh = dd
                a_sh = jnp.where(keep, pltpu.roll(aa, sh, 0), 1.0)
                u_sh = jnp.where(keep, pltpu.roll(uu, sh, 0), 0.0)
                uu = aa * u_sh + uu
                aa = aa * a_sh
            hh = aa * hs[bb] + uu
            emit(bb, r0, hh)
            edge = hh[0:1] if reverse else hh[SUBLANES - 1:SUBLANES]
            out.append(jnp.broadcast_to(edge, (SUBLANES, LRU_WIDTH)))
        return tuple(out)

    hs = lax.fori_loop(0, n_sub, sub, tuple(h_ref[bb] for bb in range(nbb)), unroll=4)
    for bb in range(nbb):
        h_ref[bb] = hs[bb]


def _lru_fwd_kernel(lx_ref, lxp_ref, lxn_ref, w_ref, ba_ref, bx_ref, lam_ref, cw_ref, cb_ref,
                    hf_ref, ab_ref, ub_ref, ext_ref, a_ref, u_ref, h_ref, *, nct, nt):
    j = pl.program_id(1)
    t = lx_ref.shape[1]
    nbb = lx_ref.shape[0]

    @pl.when(j == 0)
    def _():
        h_ref[...] = jnp.zeros_like(h_ref)

    first, last = _seg_flags(j, nct, nt)
    lo = CONV_WIDTH // 2
    half_log2_a = (-0.5 * RG_C * LOG2_E) * _softplus(-lam_ref[...])
    for bb in range(nbb):
        ext = ext_ref.at[bb]
        _fill_ext(ext, lx_ref[bb], lxp_ref[bb], lxn_ref[bb], first, last)
        ev = ext[...]
        xh = cb_ref[...]
        for kk in range(CONV_WIDTH):
            tap = ev if kk == lo else pltpu.roll(ev, (lo - kk) % ev.shape[0], 0)
            xh = xh + tap[HALO:HALO + t] * cw_ref[kk:kk + 1, :]
        zz = jnp.dot(xh.astype(BF16), w_ref[...], preferred_element_type=F32)
        for dr, (a_out, u_out) in enumerate(((a_ref, u_ref), (ab_ref, ub_ref))):
            c0 = 2 * dr * LRU_WIDTH
            tr = jnp.tanh(zz[:, c0:c0 + LRU_WIDTH] + ba_ref[dr:dr + 1, :])
            ti = jnp.tanh(zz[:, c0 + LRU_WIDTH:c0 + 2 * LRU_WIDTH] + bx_ref[dr:dr + 1, :])
            a = jnp.exp2(half_log2_a[dr:dr + 1, :] * (tr + 1.0))
            a_out[bb] = a
            u_out[bb] = jnp.sqrt(1.0 - a * a) * (ti + 1.0) * xh

    def emit(bb, r0, hh):
        hf_ref[bb, pl.ds(r0, SUBLANES), :] = hh

    _lru_scan(lambda bb, r0: a_ref[bb, pl.ds(r0, SUBLANES), :], lambda bb, r0: u_ref[bb, pl.ds(r0, SUBLANES), :],
              emit, h_ref, False, nbb, t)


def _lru_bwd_kernel(ab_ref, ub_ref, hf_ref, lg_ref, o_ref, h_ref):
    @pl.when(pl.program_id(1) == 0)
    def _():
        h_ref[...] = jnp.zeros_like(h_ref)

    def emit(bb, r0, hh):
        rows = pl.ds(r0, SUBLANES)
        o_ref[bb, rows, :] = ((hf_ref[bb, rows, :] + hh) * _gelu_tanh(lg_ref[bb, rows, :])).astype(o_ref.dtype)

    _lru_scan(lambda bb, r0: ab_ref[bb, pl.ds(r0, SUBLANES), :], lambda bb, r0: ub_ref[bb, pl.ds(r0, SUBLANES), :],
              emit, h_ref, True, ab_ref.shape[0], ab_ref.shape[1])


def _lru_tile(j, reverse, nct, nt):
    if not reverse:
        return j
    return jnp.where(j < nct, nct - 1 - j, nt - 1 - (j - nct))


def _lru(lx, lg, w, ba, bx, lam, cw, cb, nct):
    b, l, wd = lx.shape
    nt = l // ROW_TILE
    bb = LRU_BATCH_BLOCK
    main, prev, nxt = _halo_specs(wd, lambda i: i, l, bb)
    const = lambda shp: pl.BlockSpec(shp, lambda bi, i: (0,) * len(shp))
    state = jax.ShapeDtypeStruct((b, l, wd), F32)
    hf, ab, ub = pl.pallas_call(
        functools.partial(_lru_fwd_kernel, nct=nct, nt=nt),
        grid=(b // bb, nt),
        in_specs=[main, prev, nxt, const(w.shape), const(ba.shape), const(bx.shape), const(lam.shape),
                  const((SUBLANES, wd)), const((1, wd))],
        out_specs=[main, main, main],
        out_shape=[state, state, state],
        scratch_shapes=[pltpu.VMEM((bb, ROW_TILE + 2 * HALO, wd), F32), pltpu.VMEM((bb, ROW_TILE, wd), F32),
                        pltpu.VMEM((bb, ROW_TILE, wd), F32), pltpu.VMEM((bb, SUBLANES, wd), F32)],
        compiler_params=_cparams("parallel", "arbitrary"),
        name="lru_fwd",
    )(lx, lx, lx, w, ba, bx, lam, cw, cb)
    rev = pl.BlockSpec((bb, ROW_TILE, wd), lambda bi, i: (bi, _lru_tile(i, True, nct, nt), 0))
    return pl.pallas_call(
        _lru_bwd_kernel,
        grid=(b // bb, nt),
        in_specs=[rev, rev, rev, rev],
        out_specs=rev,
        out_shape=jax.ShapeDtypeStruct((b, l, wd), BF16),
        scratch_shapes=[pltpu.VMEM((bb, SUBLANES, wd), F32)],
        compiler_params=_cparams("parallel", "arbitrary"),
        name="lru_bwd",
    )(ab, ub, hf, lg)


def _merge_kernel(x_ref, mod_ref, p_ref, pp_ref, pn_ref, at_ref, lr_ref, wg_ref, bm_ref, pw_ref, ps_ref,
                  wb_ref, wo_ref, g_ref, b_ref, *rest, nct, nt, tile_off, ctx_len, alpha, route):
    if route:
        rw_ref, rb_ref, x1_ref, h2_ref, logit_ref, ext_ref = rest
    else:
        x1_ref, h2_ref, ext_ref = rest
    tile = pl.program_id(1) + tile_off
    t = x_ref.shape[1]
    d = x_ref.shape[2]
    first, last = _seg_flags(tile, nct, nt)
    in_ctx = tile < nct
    seg_len = jnp.where(in_ctx, ctx_len, nt * t - ctx_len)
    pos = (tile - jnp.where(in_ctx, 0, nct)) * t + lax.broadcasted_iota(jnp.int32, (t, 1), 0)

    for bb in range(x_ref.shape[0]):
        x = x_ref[bb]
        m = mod_ref[bb, 0]
        h_half = ((x * (1.0 + m[1:2]) + m[0:1]) * 0.5).astype(BF16)

        def branch(n, ys):
            tg = jnp.tanh(jnp.dot(h_half, wg_ref[:, n * d:(n + 1) * d], preferred_element_type=F32)
                          + bm_ref[n:n + 1, :])
            return (tg + 1.0) * jnp.dot(ys, wb_ref[n], preferred_element_type=F32)

        mix = branch(1, at_ref[bb]) + branch(2, lr_ref[bb])

        ext = ext_ref.at[bb]
        _fill_ext(ext, p_ref[bb], pp_ref[bb], pn_ref[bb], first, last)
        pooled = []
        for gi, w in enumerate(POOL_WINDOWS):
            lo = w // 2
            ev = ext[:, gi * POOL_GROUP:(gi + 1) * POOL_GROUP]
            n = ev.shape[0]
            run, span = ev, 1
            while span < lo:
                run = run + pltpu.roll(run, span, 0)
                span *= 2
            acc = pltpu.roll(run, 1, 0) + pltpu.roll(run, (1 - lo) % n, 0) if lo > 1 else run + pltpu.roll(run, 1, 0)
            cnt = jnp.clip(pos - lo + w, 0, seg_len) - jnp.clip(pos - lo, 0, seg_len)
            mean = acc[HALO:HALO + t] / cnt.astype(F32)
            dlt = (mean - ev[HALO:HALO + t]).astype(BF16)
            pooled.append(jnp.dot(dlt, pw_ref[gi], preferred_element_type=F32))
        pool_y = (jnp.concatenate(pooled, axis=1) * ps_ref[...]).astype(BF16)
        mix = branch(0, pool_y) + mix
        y2 = jnp.dot(mix.astype(BF16), wo_ref[...], preferred_element_type=F32)
        x1 = _layer_norm(alpha * x + m[6:7] * y2, g_ref[...], b_ref[...])
        x1_ref[bb] = x1
        h2 = x1 * (1.0 + m[4:5]) + m[3:4]
        if route:
            hi = h2.astype(BF16)
            mid = (h2 - hi.astype(F32)).astype(BF16)
            logit_ref[bb] = (jnp.dot(hi, rw_ref[0], preferred_element_type=F32)
                             + jnp.dot(mid, rw_ref[0], preferred_element_type=F32)
                             + jnp.dot(hi, rw_ref[1], preferred_element_type=F32) + rb_ref[...])
            _store_row_tiles(h2_ref.at[bb], h2)
        else:
            h2_ref[bb] = h2.astype(h2_ref.dtype)


def _merge(x, mods, p, attn, lru, w_gate, b_merge, pool_w, pool_scale, w_branch, w_out, ln_g, ln_b,
           nct, lat_only, alpha, router=None):
    b, l, d = x.shape
    nt = l // ROW_TILE
    off = nct if lat_only else 0
    rows = l - off * ROW_TILE
    tile_of = lambda i: i + off
    bb = BATCH_BLOCK
    tok_l = lambda w: pl.BlockSpec((bb, ROW_TILE, w), lambda bi, i: (bi, i + off, 0))
    tok_o = lambda w: pl.BlockSpec((bb, ROW_TILE, w), lambda bi, i: (bi, i, 0))
    const = lambda shp: pl.BlockSpec(shp, lambda bi, i: (0,) * len(shp), pipeline_mode=pl.Buffered(1))
    pm, pp, pn = _halo_specs(BRANCH_WIDTH, tile_of, l, bb)
    in_specs = [tok_l(d),
                pl.BlockSpec((bb, 1, SUBLANES, d), lambda bi, i: (bi, (i + off >= nct).astype(jnp.int32), 0, 0)),
                pm, pp, pn,
                tok_o(Q_WIDTH) if lat_only else tok_l(Q_WIDTH),
                tok_l(LRU_WIDTH),
                const(w_gate.shape), const(b_merge.shape), const(pool_w.shape), const(pool_scale.shape),
                const(w_branch.shape), const(w_out.shape), const((1, d)), const((1, d))]
    args = [x, mods, p, p, p, attn, lru, w_gate, b_merge, pool_w, pool_scale, w_branch, w_out, ln_g, ln_b]
    out_specs = [tok_o(d)]
    out_shape = [jax.ShapeDtypeStruct((b, rows, d), F32)]
    if router is None:
        out_specs.append(tok_o(d))
        out_shape.append(jax.ShapeDtypeStruct((b, rows, d), BF16))
    else:
        n_sub = d // LANES
        in_specs += [const(router[0].shape), const(router[1].shape)]
        args += list(router)
        out_specs += [pl.BlockSpec((bb, ROW_TILE * n_sub, LANES), lambda bi, i: (bi, i, 0)), tok_o(LANES)]
        out_shape += [jax.ShapeDtypeStruct((b, rows * n_sub, LANES), F32),
                      jax.ShapeDtypeStruct((b, rows, LANES), F32)]
    return pl.pallas_call(
        functools.partial(_merge_kernel, nct=nct, nt=nt, tile_off=off, ctx_len=nct * ROW_TILE, alpha=alpha,
                          route=router is not None),
        grid=(b // bb, rows // ROW_TILE),
        in_specs=in_specs,
        out_specs=out_specs,
        out_shape=out_shape,
        scratch_shapes=[pltpu.VMEM((bb, ROW_TILE + 2 * HALO, BRANCH_WIDTH), F32)],
        compiler_params=_cparams("parallel", "parallel"),
        name="merge_ln1",
    )(*args)


def _swiglu_kernel(x_ref, wg_ref, wu_ref, wd_ref, x1_ref, *rest, alpha):
    mod_refs, (g_ref, b_ref, o_ref) = rest[:-3], rest[-3:]
    x = x_ref[...]
    g = jnp.dot(x, wg_ref[...], preferred_element_type=F32)
    u = jnp.dot(x, wu_ref[...], preferred_element_type=F32)
    ff = jnp.dot((_silu(g) * u).astype(BF16), wd_ref[...], preferred_element_type=F32)
    for hh, mod_ref in enumerate(mod_refs):
        rs = slice(hh * ROW_TILE, (hh + 1) * ROW_TILE)
        o_ref[rs, :] = _layer_norm(alpha * x1_ref[rs, :] + mod_ref[0, 0, 5:6, :] * ff[rs], g_ref[...], b_ref[...])


def _swiglu_dense_ln(h2, x1, mods, ln_g, ln_b, w_gate, w_up, w_down, nct, alpha):
    b, l, d = x1.shape
    r = b * l
    ntl = l // ROW_TILE
    halves = FFN_ROWS // ROW_TILE

    def mod_spec(hh):
        def idx(i):
            tile = i * halves + hh
            return (tile // ntl, (tile % ntl >= nct).astype(jnp.int32), 0, 0)
        return pl.BlockSpec((1, 1, SUBLANES, d), idx)

    resident = lambda shp: pl.BlockSpec(shp, lambda i: (0, 0), pipeline_mode=pl.Buffered(1))
    out = pl.pallas_call(
        functools.partial(_swiglu_kernel, alpha=alpha),
        grid=(r // FFN_ROWS,),
        in_specs=[pl.BlockSpec((FFN_ROWS, d), lambda i: (i, 0)),
                  resident(w_gate.shape), resident(w_up.shape), resident(w_down.shape),
                  pl.BlockSpec((FFN_ROWS, d), lambda i: (i, 0))]
                 + [mod_spec(hh) for hh in range(halves)]
                 + [pl.BlockSpec((1, d), lambda i: (0, 0)), pl.BlockSpec((1, d), lambda i: (0, 0))],
        out_specs=pl.BlockSpec((FFN_ROWS, d), lambda i: (i, 0)),
        out_shape=jax.ShapeDtypeStruct((r, d), F32),
        compiler_params=_cparams("parallel"),
        name="swiglu_dense_ln2",
    )(h2.reshape(r, d), w_gate, w_up, w_down, x1.reshape(r, d), *([mods] * halves), ln_g, ln_b)
    return out.reshape(b, l, d)


def _moe_kernel(be_ref, nu_ref, tok_ref, dst_ref, t_hbm, wg_ref, wu_ref, wd_ref, y_hbm,
                xbuf, acc_ref, obuf, sem_in, sem_out, sem_fill):
    i = pl.program_id(0)
    f = pl.program_id(1)
    nb = pl.num_programs(0)
    nu = nu_ref[0]
    rows = acc_ref.shape[0]
    sub = xbuf.shape[1] // rows
    slot = i % 2

    def tile_rows(start):
        return pl.ds(pl.multiple_of(start, sub), sub)

    def gather_copy(blk, sl, r):
        return pltpu.make_async_copy(t_hbm.at[tile_rows(tok_ref[blk * rows + r])], xbuf.at[sl, tile_rows(r * sub)],
                                     sem_in.at[sl])

    def scatter_copy(blk, sl, r):
        return pltpu.make_async_copy(obuf.at[sl, tile_rows(r * sub)], y_hbm.at[tile_rows(dst_ref[blk * rows + r])],
                                     sem_out.at[sl])

    def gathers_done(sl):
        pltpu.make_async_copy(t_hbm.at[pl.ds(0, rows * sub)], xbuf.at[sl], sem_in.at[sl]).wait()

    def scatters_done(sl):
        pltpu.make_async_copy(obuf.at[sl], y_hbm.at[pl.ds(0, rows * sub)], sem_out.at[sl]).wait()

    def for_rows(fn):
        def body(r, c):
            fn(r)
            return c
        lax.fori_loop(0, rows, body, 0, unroll=8)

    def load_x():
        return _load_row_tiles(xbuf.at[slot], rows).astype(BF16)

    def swiglu_part(x):
        g = jnp.dot(x, wg_ref[0], preferred_element_type=F32)
        u = jnp.dot(x, wu_ref[0], preferred_element_type=F32)
        return jnp.dot((_silu(g) * u).astype(BF16), wd_ref[0], preferred_element_type=F32)

    @pl.when(jnp.logical_and(i == 0, f == 0))
    def _():
        obuf[1] = jnp.zeros(obuf.shape[1:], obuf.dtype)
        for_rows(lambda r: gather_copy(0, 0, r).start())

    @pl.when(jnp.logical_and(f == 0, i <= nu))
    def _():
        gathers_done(slot)

    @pl.when(jnp.logical_and(i < nu, f == 0))
    def _():
        x = load_x()
        for r in range(rows):
            gather_copy(i + 1, 1 - slot, r).start()
        acc_ref[...] = swiglu_part(x)

    @pl.when(jnp.logical_and(i < nu, f == 1))
    def _():
        @pl.when(i >= 1)
        def _():
            scatters_done(slot)

        x = load_x()
        prev = jnp.where(i >= 1, i - 1, nb)
        for r in range(rows):
            scatter_copy(prev, 1 - slot, r).start()
        _store_row_tiles(obuf.at[slot], acc_ref[...] + swiglu_part(x))

        @pl.when(i == nu - 1)
        def _():
            scatters_done(1 - slot)
            for_rows(lambda r: scatter_copy(i, slot, r).start())
            scatters_done(slot)

    @pl.when(jnp.logical_and(i >= nu, f == 1))
    def _():
        xbuf[0] = jnp.zeros(xbuf.shape[1:], xbuf.dtype)
        fill = pltpu.make_async_copy(xbuf.at[0], y_hbm.at[pl.ds(i * rows * sub, rows * sub)], sem_fill)
        fill.start()
        fill.wait()

    @pl.when(jnp.logical_and(jnp.logical_and(i == nb - 1, f == 1), nu == nb))
    def _():
        gathers_done(nb % 2)


def _swiglu_experts(t, block_e, n_used, slot_tok, slot_dst, w_gate, w_up, w_down, tf):
    d = w_gate.shape[1]
    sub = d // LANES
    p = slot_tok.shape[0]
    nf = w_gate.shape[2] // tf
    assert nf == 2

    def fcol(i, f, nu):
        return jnp.where(i < nu[0], f, nf - 1)

    return pl.pallas_call(
        _moe_kernel,
        grid_spec=pltpu.PrefetchScalarGridSpec(
            num_scalar_prefetch=4,
            grid=(p // FFN_ROWS - 1, nf),
            in_specs=[pl.BlockSpec(memory_space=pl.ANY),
                      pl.BlockSpec((1, d, tf), lambda i, f, be, nu, st, sd: (be[i], 0, fcol(i, f, nu))),
                      pl.BlockSpec((1, d, tf), lambda i, f, be, nu, st, sd: (be[i], 0, fcol(i, f, nu))),
                      pl.BlockSpec((1, tf, d), lambda i, f, be, nu, st, sd: (be[i], fcol(i, f, nu), 0))],
            out_specs=pl.BlockSpec(memory_space=pl.ANY),
            scratch_shapes=[pltpu.VMEM((2, FFN_ROWS * sub, LANES), F32), pltpu.VMEM((FFN_ROWS, d), F32),
                            pltpu.VMEM((2, FFN_ROWS * sub, LANES), F32),
                            pltpu.SemaphoreType.DMA((2,)), pltpu.SemaphoreType.DMA((2,)),
                            pltpu.SemaphoreType.DMA]),
        out_shape=jax.ShapeDtypeStruct((p * sub, LANES), F32),
        compiler_params=_cparams("arbitrary", "arbitrary"),
        name="swiglu_experts",
    )(block_e, n_used, slot_tok, slot_dst, t, w_gate, w_up, w_down)


def _combine_ln_kernel(x_ref, mod_ref, g_ref, b_ref, *rest, alpha):
    y_refs, p_ref, o_ref = rest[:-2], rest[-2], rest[-1]
    t = x_ref.shape[1]
    top_k = p_ref.shape[2]
    for bb in range(x_ref.shape[0]):
        f = None
        for kk in range(top_k):
            term = _load_row_tiles(y_refs[bb * top_k + kk], t) * p_ref[bb, :, kk:kk + 1]
            f = term if f is None else f + term
        m = mod_ref[bb, 0]
        o_ref[bb] = _layer_norm(alpha * x_ref[bb] + m[5:6] * f, g_ref[...], b_ref[...])


def _combine_ln(x1, mods, ln_g, ln_b, y, probs, alpha):
    b, rows, d = x1.shape
    nt = rows // ROW_TILE
    sub = d // LANES
    top_k = probs.shape[2]
    nbb = BATCH_BLOCK
    tok = lambda w: pl.BlockSpec((nbb, ROW_TILE, w), lambda bi, i: (bi, i, 0))
    const = lambda shp: pl.BlockSpec(shp, lambda bi, i: (0,) * len(shp))
    y_specs = [pl.BlockSpec((ROW_TILE * sub, LANES),
                            lambda bi, i, kk=kk, bb=bb: ((kk * b + bi * nbb + bb) * nt + i, 0))
               for bb in range(nbb) for kk in range(top_k)]
    return pl.pallas_call(
        functools.partial(_combine_ln_kernel, alpha=alpha),
        grid=(b // nbb, nt),
        in_specs=[tok(d), pl.BlockSpec((nbb, 1, SUBLANES, d), lambda bi, i: (bi, 1, 0, 0)),
                  const((1, d)), const((1, d))] + y_specs + [tok(top_k)],
        out_specs=tok(d),
        out_shape=jax.ShapeDtypeStruct((b, rows, d), F32),
        compiler_params=_cparams("parallel", "parallel"),
        name="combine_ln2",
    )(x1, mods, ln_g, ln_b, *([y] * (top_k * nbb)), probs)


def _moe(t, logits, w_gate, w_up, w_down, tf):
    n, n_e = logits.shape
    sub = t.shape[0] // n
    top_v, top_i = lax.top_k(logits, TOP_K)
    probs = jax.nn.softmax(top_v, axis=-1)
    m = n * TOP_K
    flat_e = top_i.reshape(m)
    counts = jnp.sum((flat_e[:, None] == jnp.arange(n_e, dtype=flat_e.dtype)[None, :]).astype(jnp.int32), axis=0)
    order = jnp.argsort(flat_e, stable=True).astype(jnp.int32)
    start = jnp.cumsum(counts) - counts
    padded = (counts + FFN_ROWS - 1) // FFN_ROWS * FFN_ROWS
    ends_p = jnp.cumsum(padded)
    start_p = ends_p - padded
    nb = -(-(m + n_e * (FFN_ROWS - 1)) // FFN_ROWS)
    p = (nb + 1) * FFN_ROWS
    blk_start = jnp.arange(nb + 1, dtype=jnp.int32) * FFN_ROWS
    block_e = jnp.minimum(jnp.sum((ends_p[None, :] <= blk_start[:, None]).astype(jnp.int32), axis=1), n_e - 1)
    n_used = (ends_p[-1:] // FFN_ROWS).astype(jnp.int32)
    slot_e = jnp.repeat(block_e, FFN_ROWS)
    r = jnp.arange(p, dtype=jnp.int32) - start_p[slot_e]
    is_pad = r >= counts[slot_e]
    slot_pair = order[jnp.clip(start[slot_e] + r, 0, m - 1)]
    slot_tok = jnp.where(is_pad, 0, slot_pair // TOP_K) * sub
    pads_before = (start_p - start)[slot_e] + r - counts[slot_e]
    slot_dst = jnp.where(is_pad, m + pads_before, (slot_pair % TOP_K) * n + slot_pair // TOP_K) * sub
    y = _swiglu_experts(t, block_e, n_used, slot_tok, slot_dst, w_gate, w_up, w_down, tf)
    return y, probs


def _block_diag(w):
    nb, bi, bj = w.shape
    eye = jnp.eye(nb, dtype=w.dtype)
    return (eye[:, None, :, None] * w[:, :, None, :]).reshape(nb * bi, nb * bj)


def _rope_tables(ctx_len, s):
    nf = HEAD_DIM // 4
    inv = ROPE_THETA ** (-jnp.arange(nf, dtype=F32) / nf)
    t = jnp.arange(s)
    row = (t // GRID_W).astype(F32)[:, None] * inv
    col = (t % GRID_W).astype(F32)[:, None] * inv
    ang = jnp.concatenate([row, row, col, col], axis=1)
    sign = jnp.tile(jnp.concatenate([-jnp.ones((nf,), F32), jnp.ones((nf,), F32)]), 2)
    cos = jnp.concatenate([jnp.ones((ctx_len, HEAD_DIM), F32), jnp.cos(ang)], axis=0)
    sin = jnp.concatenate([jnp.zeros((ctx_len, HEAD_DIM), F32), jnp.sin(ang) * sign], axis=0)
    reps = LANES // HEAD_DIM
    return jnp.tile(cos, (1, reps)), jnp.tile(sin, (1, reps))


def _pad_rows(a, rows):
    return jnp.zeros((rows,) + a.shape[1:], a.dtype).at[:a.shape[0]].set(a)


def kernel(x, c, ctx, c_ctx, w_mod, b_mod, w_in, b_merge, pool_w, pool_scale, q_norm, k_norm, conv_w, conv_b, lru_wa, lru_ba, lru_wx, lru_bx, lru_lambda, w_branch, w_out, ln1_g, ln1_b, ffn_w_gate, ffn_w_up, ffn_w_down, moe_router, moe_router_b, moe_w_gate, moe_w_up, moe_w_down, ln2_g, ln2_b):
    b, s, d = x.shape
    ctx_len = ctx.shape[1]
    depth = w_in.shape[0]
    l = ctx_len + s
    assert ctx_len % ROW_TILE == 0 and s % ROW_TILE == 0 and (b * s) % FFN_ROWS == 0
    assert b % BATCH_BLOCK == 0 and b % LRU_BATCH_BLOCK == 0
    assert (b * l) % FFN_ROWS == 0 and ctx_len % Q_TILE == 0 and s % Q_TILE == 0
    nct = ctx_len // ROW_TILE
    alpha = (2 * depth) ** 0.25

    cc = _pad_rows(jnp.concatenate([c, c_ctx[None, :]], axis=0), -(-(b + 1) // SUBLANES) * SUBLANES)
    mod_all = _modulation(cc, w_mod, b_mod)

    cos_t, sin_t = _rope_tables(ctx_len, s)
    bd = _block_diag(jnp.full((LANES // HEAD_DIM, HEAD_DIM, HEAD_DIM), 1.0 / HEAD_DIM, F32)).astype(BF16)
    reps = LANES // HEAD_DIM

    xs = None
    w_in_bf = w_in[0].astype(BF16)
    for li in range(depth):
        last = li == depth - 1
        dense = li % 2 == 0
        jf = li // 2
        ml = jnp.transpose(mod_all[li, :, :b], (1, 0, 2))
        mc = jnp.broadcast_to(mod_all[li, :, b][None], (b, 6, d))
        mods = jnp.stack([mc, ml], axis=1)
        mods = jnp.concatenate([mods, 0.5 * mods[:, :, 2:3], jnp.zeros((b, 2, 1, d), F32)], axis=2)

        proj_args = (mods, w_in_bf[:, :OFF_GATE], bd, jnp.tile(q_norm[li], reps)[None],
                     jnp.tile(k_norm[li], reps)[None], cos_t, sin_t, nct)
        if xs is None:
            p, q, k, v, lx, lg, xs = _project(x, *proj_args, ctx=ctx)
        else:
            p, q, k, v, lx, lg = _project(xs, *proj_args)

        slab = [w_branch[li].reshape(-1, d), w_out[li]]
        if dense:
            slab += [ffn_w_gate[jf], ffn_w_up[jf], ffn_w_down[jf]]
            step = []
        else:
            step = [moe_w_gate[jf], moe_w_up[jf], moe_w_down[jf]]
        if not last:
            slab.append(w_in[li + 1])
        if not all(w.shape[0] == b and w.shape[1] % ((l - (ctx_len if last else 0)) // Q_TILE * 16) == 0
                   for w in step):
            step_bf, step = [w.astype(BF16) for w in step], []
        attn, *made = _attention(q, k, v, ctx_len, with_ctx=not last, step_casts=step, slab_casts=slab)
        if step:
            step_bf = made[:len(step)]
        slab_bf = made[len(step):]
        wb_bf, wo_bf = slab_bf[0].reshape(w_branch[li].shape), slab_bf[1]

        cw = _pad_rows(0.5 * conv_w[li], SUBLANES)
        cb = 0.5 * conv_b[li][None]
        lru_w = jnp.concatenate([_block_diag(lru_wa[li, dr]) if kind == 0 else _block_diag(lru_wx[li, dr])
                                 for dr in range(2) for kind in range(2)], axis=1).astype(BF16)
        lru_y = _lru(lx, lg, lru_w, 0.5 * lru_ba[li], 0.5 * lru_bx[li], lru_lambda[li], cw, cb, nct)

        merge_args = (xs, mods, p, attn, lru_y, w_in_bf[:, OFF_GATE:], _pad_rows(0.5 * b_merge[li], SUBLANES),
                      pool_w[li].astype(BF16), pool_scale[li][None], wb_bf, wo_bf, ln1_g[li][None], ln1_b[li][None])
        if dense:
            assert not last, "the dense layer is expected to carry the context tokens along"
            x1, h2 = _merge(*merge_args, nct, lat_only=False, alpha=alpha)
            xs = _swiglu_dense_ln(h2, x1, mods, ln2_g[li][None], ln2_b[li][None], *slab_bf[2:5],
                                  nct=nct, alpha=alpha)
        else:
            assert last, "the expert layer is expected to be the last layer (latent tokens only)"
            n_e = moe_router.shape[2]
            rw = jnp.zeros((d, LANES), F32).at[:, :n_e].set(moe_router[jf])
            rb = jnp.zeros((1, LANES), F32).at[0, :n_e].set(moe_router_b[jf])
            rw_hi = rw.astype(BF16)
            rw_split = jnp.stack([rw_hi, (rw - rw_hi.astype(F32)).astype(BF16)])
            x1, h2, logits = _merge(*merge_args, nct, lat_only=True, alpha=alpha, router=(rw_split, rb))
            y, probs = _moe(h2.reshape(-1, LANES), logits.reshape(b * s, LANES)[:, :n_e], *step_bf,
                            tf=moe_w_gate.shape[3] // 2)
            xs = _combine_ln(x1, mods, ln2_g[li][None], ln2_b[li][None], y, probs.reshape(b, s, -1), alpha)
        if not last:
            w_in_bf = slab_bf[-1]
    return xs
```

```python
import functools

import jax
import jax.numpy as jnp
from jax import lax
from jax.experimental import pallas as pl
from jax.experimental.pallas import tpu as pltpu

F32 = jnp.float32
BF16 = jnp.bfloat16

GRID_W = 64
POOL_WINDOWS = (2, 4, 8, 16)
BRANCH_WIDTH = 512
POOL_GROUP = BRANCH_WIDTH // len(POOL_WINDOWS)
N_Q_HEADS = 8
N_KV_HEADS = 2
HEAD_DIM = 64
Q_WIDTH = N_Q_HEADS * HEAD_DIM
KV_WIDTH = N_KV_HEADS * HEAD_DIM
ROPE_THETA = 10000.0
ATTN_SCALE = HEAD_DIM ** -0.5
LOG2_E = 1.4426950408889634
LRU_WIDTH = BRANCH_WIDTH
CONV_WIDTH = 4
RG_C = 8.0
TOP_K = 2
LN_EPS = 1e-5
RMS_EPS = 1e-6
OFF_GATE = BRANCH_WIDTH + Q_WIDTH + 2 * KV_WIDTH + 2 * LRU_WIDTH

LANES = 128
SUBLANES = 8
V7X_VMEM_BYTES = 64 * 1024 * 1024
VMEM_RESERVE = 8 * 1024 * 1024
VMEM_LIMIT = V7X_VMEM_BYTES - VMEM_RESERVE

ROW_TILE = 256
BATCH_BLOCK = 4
LRU_BATCH_BLOCK = 4
Q_TILE = 256
ATTN_CHAIN = 256
FFN_ROWS = 512
HALO = SUBLANES


def _cparams(*sem):
    return pltpu.CompilerParams(dimension_semantics=sem, vmem_limit_bytes=VMEM_LIMIT)


def _sigmoid(x):
    return 0.5 * jnp.tanh(0.5 * x) + 0.5


def _store_row_tiles(ref, val):
    t, d = val.shape
    for j in range(d // LANES):
        ref[pl.ds(j, t, stride=d // LANES), :] = val[:, j * LANES:(j + 1) * LANES]


def _load_row_tiles(ref, t):
    n = ref.shape[0] // t
    return jnp.concatenate([ref[pl.ds(j, t, stride=n), :] for j in range(n)], axis=1)


def _silu(x):
    return x * _sigmoid(x)


def _layer_norm(v, g, b):
    mu = jnp.mean(v, axis=-1, keepdims=True)
    d = v - mu
    var = jnp.mean(d * d, axis=-1, keepdims=True)
    return d * lax.rsqrt(var + LN_EPS) * g + b


def _mod_kernel(c_ref, w_ref, b_ref, o_ref):
    s = _silu(c_ref[...])
    o_ref[0, 0] = jnp.dot(s, w_ref[0], precision=lax.Precision.HIGHEST,
                          preferred_element_type=F32) + b_ref[0, 0]


def _modulation(cc, w_mod, b_mod):
    depth, d, _ = w_mod.shape
    r = cc.shape[0]
    b6 = b_mod.reshape(depth, 6, 1, d)
    return pl.pallas_call(
        _mod_kernel,
        grid=(depth, 6),
        in_specs=[pl.BlockSpec((r, d), lambda l, j: (0, 0)),
                  pl.BlockSpec((1, d, d), lambda l, j: (l, 0, j)),
                  pl.BlockSpec((1, 1, 1, d), lambda l, j: (l, j, 0, 0))],
        out_specs=pl.BlockSpec((1, 1, r, d), lambda l, j: (l, j, 0, 0)),
        out_shape=jax.ShapeDtypeStruct((depth, 6, r, d), F32),
        compiler_params=_cparams("arbitrary", "arbitrary"),
        name="modulation",
    )(cc, w_mod, b6)


def _norm_rope(zc, bd, g, cos, sin):
    zz = zc * zc
    hi = zz.astype(BF16)
    lo = (zz - hi.astype(F32)).astype(BF16)
    ms = jnp.dot(hi, bd, preferred_element_type=F32) + jnp.dot(lo, bd, preferred_element_type=F32)
    y = zc * lax.rsqrt(ms + RMS_EPS) * g
    nf = HEAD_DIM // 4
    up = pltpu.roll(y, LANES - nf, 1)
    dn = pltpu.roll(y, nf, 1)
    lane = lax.broadcasted_iota(jnp.int32, y.shape, 1)
    partner = jnp.where((lane % (2 * nf)) < nf, up, dn)
    return y * cos + partner * sin


def _proj_kernel(*refs, nct, joined):
    if joined:
        x_ref, mod_ref, w_ref, bd_ref, qg_ref, kg_ref, cos_ref, sin_ref, p_ref, q_ref, k_ref, v_ref, lx_ref, lg_ref = refs
    else:
        (c_ref, x_ref, mod_ref, w_ref, bd_ref, qg_ref, kg_ref, cos_ref, sin_ref,
         p_ref, q_ref, k_ref, v_ref, lx_ref, lg_ref, xs_ref) = refs
    bd = bd_ref[...]
    cos = cos_ref[...]
    sin = sin_ref[...]
    for bb in range(x_ref.shape[0]):
        if joined:
            x = x_ref[bb]
        else:
            x = jnp.where(pl.program_id(1) < nct, c_ref[bb], x_ref[bb])
            xs_ref[bb] = x
        m = mod_ref[bb, 0]
        h = (x * (1.0 + m[1:2]) + m[0:1]).astype(BF16)
        z = jnp.dot(h, w_ref[...], preferred_element_type=F32)
        o = 0
        p_ref[bb] = z[:, o:o + BRANCH_WIDTH]
        o += BRANCH_WIDTH
        for c in range(Q_WIDTH // LANES):
            y = _norm_rope(z[:, o:o + LANES], bd, qg_ref[...], cos, sin) * (ATTN_SCALE * LOG2_E)
            q_ref[bb, 2 * c] = y[:, :HEAD_DIM].astype(BF16)
            q_ref[bb, 2 * c + 1] = y[:, HEAD_DIM:].astype(BF16)
            o += LANES
        for c in range(KV_WIDTH // LANES):
            y = _norm_rope(z[:, o:o + LANES], bd, kg_ref[...], cos, sin)
            k_ref[bb, 2 * c] = y[:, :HEAD_DIM].astype(BF16)
            k_ref[bb, 2 * c + 1] = y[:, HEAD_DIM:].astype(BF16)
            o += LANES
        for c in range(KV_WIDTH // LANES):
            y = z[:, o:o + LANES]
            ones = jnp.ones((y.shape[0], LANES - HEAD_DIM), F32)
            v_ref[bb, 2 * c] = jnp.concatenate([y[:, :HEAD_DIM], ones], axis=1).astype(BF16)
            v_ref[bb, 2 * c + 1] = jnp.concatenate([y[:, HEAD_DIM:], ones], axis=1).astype(BF16)
            o += LANES
        lx_ref[bb] = z[:, o:o + LRU_WIDTH]
        o += LRU_WIDTH
        lg_ref[bb] = z[:, o:o + LRU_WIDTH]


def _project(x, mods, w_a, bd, qg, kg, cos_t, sin_t, nct, ctx=None):
    b, _, d = x.shape
    l = x.shape[1] if ctx is None else x.shape[1] + ctx.shape[1]
    nt = l // ROW_TILE
    wa = w_a.shape[1]
    bb = BATCH_BLOCK
    tok = lambda w: pl.BlockSpec((bb, ROW_TILE, w), lambda bi, i: (bi, i, 0))
    head = lambda n, w=HEAD_DIM: pl.BlockSpec((bb, n, ROW_TILE, w), lambda bi, i: (bi, 0, i, 0))
    const = lambda shp: pl.BlockSpec(shp, lambda bi, i: (0,) * len(shp))
    if ctx is None:
        tokens, token_specs = [x], [tok(d)]
    else:
        tokens = [ctx, x]
        token_specs = [pl.BlockSpec((bb, ROW_TILE, d), lambda bi, i: (bi, jnp.minimum(i, nct - 1), 0)),
                       pl.BlockSpec((bb, ROW_TILE, d), lambda bi, i: (bi, jnp.maximum(i - nct, 0), 0))]
    out_specs = [tok(BRANCH_WIDTH), head(N_Q_HEADS), head(N_KV_HEADS), head(N_KV_HEADS, LANES),
                 tok(LRU_WIDTH), tok(LRU_WIDTH)]
    out_shape = [jax.ShapeDtypeStruct((b, l, BRANCH_WIDTH), F32),
                 jax.ShapeDtypeStruct((b, N_Q_HEADS, l, HEAD_DIM), BF16),
                 jax.ShapeDtypeStruct((b, N_KV_HEADS, l, HEAD_DIM), BF16),
                 jax.ShapeDtypeStruct((b, N_KV_HEADS, l, LANES), BF16),
                 jax.ShapeDtypeStruct((b, l, LRU_WIDTH), F32),
                 jax.ShapeDtypeStruct((b, l, LRU_WIDTH), F32)]
    if ctx is not None:
        out_specs.append(tok(d))
        out_shape.append(jax.ShapeDtypeStruct((b, l, d), F32))
    return pl.pallas_call(
        functools.partial(_proj_kernel, nct=nct, joined=ctx is None),
        grid=(b // bb, nt),
        in_specs=token_specs + [
            pl.BlockSpec((bb, 1, SUBLANES, d), lambda bi, i: (bi, (i >= nct).astype(jnp.int32), 0, 0)),
            const((d, wa)), const((LANES, LANES)), const((1, LANES)), const((1, LANES)),
            pl.BlockSpec((ROW_TILE, LANES), lambda bi, i: (i, 0)),
            pl.BlockSpec((ROW_TILE, LANES), lambda bi, i: (i, 0))],
        out_specs=out_specs,
        out_shape=out_shape,
        compiler_params=_cparams("parallel", "parallel"),
        name="in_proj",
    )(*tokens, mods, w_a, bd, qg, kg, cos_t, sin_t)


def _softmax_pv(q, k, v):
    hd = q.shape[1]
    s = lax.dot_general(q, k, (((1,), (1,)), ((), ())), preferred_element_type=F32)
    m = jnp.max(s, axis=1, keepdims=True)
    p = jnp.exp2(s - m).astype(BF16)
    r = jnp.dot(p, v, preferred_element_type=F32)
    return r[:, :hd] / r[:, hd:2 * hd]


def _attn_kernel(q_ref, k_ref, v_ref, *rest, n_ctx_tiles, ctx_len, n_step_casts, n_slab_casts):
    n_casts = n_step_casts + n_slab_casts
    cast_in, o_ref, cast_out = rest[:n_casts], rest[n_casts], rest[n_casts + 1:]
    for src, dst in zip(cast_in[:n_step_casts], cast_out[:n_step_casts]):
        dst[...] = src[...].astype(dst.dtype)

    @pl.when(pl.program_id(1) == 0)
    def _():
        for src, dst in zip(cast_in[n_step_casts:], cast_out[n_step_casts:]):
            dst[...] = src[...].astype(dst.dtype)

    nh, tq, hd = q_ref.shape[1:]
    group = nh // k_ref.shape[1]

    def run(lk):
        for j in range(nh):
            for r0 in range(0, tq, ATTN_CHAIN):
                o = _softmax_pv(q_ref[0, j, r0:r0 + ATTN_CHAIN, :], k_ref[0, j // group, :lk],
                                v_ref[0, j // group, :lk])
                o_ref[0, r0:r0 + ATTN_CHAIN, j * hd:(j + 1) * hd] = o.astype(o_ref.dtype)

    if n_ctx_tiles:
        @pl.when(pl.program_id(1) < n_ctx_tiles)
        def _():
            run(ctx_len)

        @pl.when(pl.program_id(1) >= n_ctx_tiles)
        def _():
            run(k_ref.shape[2])
    else:
        run(k_ref.shape[2])


def _attention(q, k, v, ctx_len, with_ctx, step_casts=(), slab_casts=()):
    b, nh, l, hd = q.shape
    nct = ctx_len // Q_TILE
    rows = l if with_ctx else l - ctx_len
    q_off = 0 if with_ctx else nct
    nq = rows // Q_TILE
    cast_specs, cast_shapes = [], []
    for w in step_casts:
        assert w.shape[0] == b and w.shape[1] % nq == 0
        cast_specs.append(pl.BlockSpec((1, w.shape[1] // nq, w.shape[2]), lambda bi, i: (bi, i, 0)))
        cast_shapes.append(jax.ShapeDtypeStruct(w.shape, BF16))
    for w in slab_casts:
        assert w.shape[0] % b == 0
        cast_specs.append(pl.BlockSpec((w.shape[0] // b, w.shape[1]), lambda bi, i: (bi, 0)))
        cast_shapes.append(jax.ShapeDtypeStruct(w.shape, BF16))
    return pl.pallas_call(
        functools.partial(_attn_kernel, n_ctx_tiles=nct if with_ctx else 0, ctx_len=ctx_len,
                          n_step_casts=len(step_casts), n_slab_casts=len(slab_casts)),
        grid=(b, nq),
        in_specs=[pl.BlockSpec((1, nh, Q_TILE, hd), lambda bi, i: (bi, 0, i + q_off, 0)),
                  pl.BlockSpec((1,) + k.shape[1:], lambda bi, i: (bi, 0, 0, 0)),
                  pl.BlockSpec((1,) + v.shape[1:], lambda bi, i: (bi, 0, 0, 0))] + cast_specs,
        out_specs=[pl.BlockSpec((1, Q_TILE, nh * hd), lambda bi, i: (bi, i, 0))] + cast_specs,
        out_shape=[jax.ShapeDtypeStruct((b, rows, nh * hd), BF16)] + cast_shapes,
        compiler_params=_cparams("parallel", "arbitrary"),
        name="attention",
    )(q, k, v, *step_casts, *slab_casts)


def _fill_ext(ext_ref, main, prev, nxt, first, last):
    t = main.shape[0]
    ext_ref[0:HALO] = jnp.where(first, 0.0, prev)
    ext_ref[HALO:HALO + t] = main
    ext_ref[HALO + t:2 * HALO + t] = jnp.where(last, 0.0, nxt)


def _seg_flags(tile, nct, nt):
    first = jnp.logical_or(tile == 0, tile == nct)
    last = jnp.logical_or(tile == nct - 1, tile == nt - 1)
    return first, last


def _halo_specs(width, tile_of, l, bb):
    rb = ROW_TILE // HALO
    last_blk = l // HALO - 1
    main = pl.BlockSpec((bb, ROW_TILE, width), lambda bi, i: (bi, tile_of(i), 0))
    prev = pl.BlockSpec((bb, HALO, width), lambda bi, i: (bi, jnp.maximum(tile_of(i) * rb - 1, 0), 0))
    nxt = pl.BlockSpec((bb, HALO, width), lambda bi, i: (bi, jnp.minimum((tile_of(i) + 1) * rb, last_blk), 0))
    return main, prev, nxt


def _softplus(z):
    return jnp.maximum(z, 0.0) + jnp.log1p(jnp.exp(-jnp.abs(z)))


def _gelu_tanh(x):
    return 0.5 * x * (1.0 + jnp.tanh(0.7978845608028654 * (x + 0.044715 * (x * x * x))))


def _lru_scan(a_of, u_of, emit, h_ref, reverse, nbb, t):
    row = lax.broadcasted_iota(jnp.int32, (SUBLANES, LRU_WIDTH), 0)
    n_sub = t // SUBLANES

    def sub(s, hs):
        blk = (n_sub - 1 - s) if reverse else s
        r0 = pl.multiple_of(blk * SUBLANES, SUBLANES)
        out = []
        for bb in range(nbb):
            aa = a_of(bb, r0)
            uu = u_of(bb, r0)
            for dd in (1, 2, 4):
                if reverse:
                    keep = row < SUBLANES - dd
                    sh = SUBLANES - dd
                else:
                    keep = row >= dd
                    sh = dd
                a_sh = jnp.where(keep, pltpu.roll(aa, sh, 0), 1.0)
                u_sh = jnp.where(keep, pltpu.roll(uu, sh, 0), 0.0)
                uu = aa * u_sh + uu
                aa = aa * a_sh
            hh = aa * hs[bb] + uu
            emit(bb, r0, hh)
            edge = hh[0:1] if reverse else hh[SUBLANES - 1:SUBLANES]
            out.append(jnp.broadcast_to(edge, (SUBLANES, LRU_WIDTH)))
        return tuple(out)

    hs = lax.fori_loop(0, n_sub, sub, tuple(h_ref[bb] for bb in range(nbb)), unroll=4)
    for bb in range(nbb):
        h_ref[bb] = hs[bb]


def _lru_fwd_kernel(lx_ref, lxp_ref, lxn_ref, w_ref, ba_ref, bx_ref, lam_ref, cw_ref, cb_ref,
                    hf_ref, ab_ref, ub_ref, ext_ref, a_ref, u_ref, h_ref, *, nct, nt):
    j = pl.program_id(1)
    t = lx_ref.shape[1]
    nbb = lx_ref.shape[0]

    @pl.when(j == 0)
    def _():
        h_ref[...] = jnp.zeros_like(h_ref)

    first, last = _seg_flags(j, nct, nt)
    lo = CONV_WIDTH // 2
    half_log2_a = (-0.5 * RG_C * LOG2_E) * _softplus(-lam_ref[...])
    for bb in range(nbb):
        ext = ext_ref.at[bb]
        _fill_ext(ext, lx_ref[bb], lxp_ref[bb], lxn_ref[bb], first, last)
        ev = ext[...]
        xh = cb_ref[...]
        for kk in range(CONV_WIDTH):
            tap = ev if kk == lo else pltpu.roll(ev, (lo - kk) % ev.shape[0], 0)
            xh = xh + tap[HALO:HALO + t] * cw_ref[kk:kk + 1, :]
        zz = jnp.dot(xh.astype(BF16), w_ref[...], preferred_element_type=F32)
        for dr, (a_out, u_out) in enumerate(((a_ref, u_ref), (ab_ref, ub_ref))):
            c0 = 2 * dr * LRU_WIDTH
            tr = jnp.tanh(zz[:, c0:c0 + LRU_WIDTH] + ba_ref[dr:dr + 1, :])
            ti = jnp.tanh(zz[:, c0 + LRU_WIDTH:c0 + 2 * LRU_WIDTH] + bx_ref[dr:dr + 1, :])
            a = jnp.exp2(half_log2_a[dr:dr + 1, :] * (tr + 1.0))
            a_out[bb] = a
            u_out[bb] = jnp.sqrt(1.0 - a * a) * (ti + 1.0) * xh

    def emit(bb, r0, hh):
        hf_ref[bb, pl.ds(r0, SUBLANES), :] = hh

    _lru_scan(lambda bb, r0: a_ref[bb, pl.ds(r0, SUBLANES), :], lambda bb, r0: u_ref[bb, pl.ds(r0, SUBLANES), :],
              emit, h_ref, False, nbb, t)


def _lru_bwd_kernel(ab_ref, ub_ref, hf_ref, lg_ref, o_ref, h_ref):
    @pl.when(pl.program_id(1) == 0)
    def _():
        h_ref[...] = jnp.zeros_like(h_ref)

    def emit(bb, r0, hh):
        rows = pl.ds(r0, SUBLANES)
        o_ref[bb, rows, :] = ((hf_ref[bb, rows, :] + hh) * _gelu_tanh(lg_ref[bb, rows, :])).astype(o_ref.dtype)

    _lru_scan(lambda bb, r0: ab_ref[bb, pl.ds(r0, SUBLANES), :], lambda bb, r0: ub_ref[bb, pl.ds(r0, SUBLANES), :],
              emit, h_ref, True, ab_ref.shape[0], ab_ref.shape[1])


def _lru_tile(j, reverse, nct, nt):
    if not reverse:
        return j
    return jnp.where(j < nct, nct - 1 - j, nt - 1 - (j - nct))


def _lru(lx, lg, w, ba, bx, lam, cw, cb, nct):
    b, l, wd = lx.shape
    nt = l // ROW_TILE
    bb = LRU_BATCH_BLOCK
    main, prev, nxt = _halo_specs(wd, lambda i: i, l, bb)
    const = lambda shp: pl.BlockSpec(shp, lambda bi, i: (0,) * len(shp))
    state = jax.ShapeDtypeStruct((b, l, wd), F32)
    hf, ab, ub = pl.pallas_call(
        functools.partial(_lru_fwd_kernel, nct=nct, nt=nt),
        grid=(b // bb, nt),
        in_specs=[main, prev, nxt, const(w.shape), const(ba.shape), const(bx.shape), const(lam.shape),
                  const((SUBLANES, wd)), const((1, wd))],
        out_specs=[main, main, main],
        out_shape=[state, state, state],
        scratch_shapes=[pltpu.VMEM((bb, ROW_TILE + 2 * HALO, wd), F32), pltpu.VMEM((bb, ROW_TILE, wd), F32),
                        pltpu.VMEM((bb, ROW_TILE, wd), F32), pltpu.VMEM((bb, SUBLANES, wd), F32)],
        compiler_params=_cparams("parallel", "arbitrary"),
        name="lru_fwd",
    )(lx, lx, lx, w, ba, bx, lam, cw, cb)
    rev = pl.BlockSpec((bb, ROW_TILE, wd), lambda bi, i: (bi, _lru_tile(i, True, nct, nt), 0))
    return pl.pallas_call(
        _lru_bwd_kernel,
        grid=(b // bb, nt),
        in_specs=[rev, rev, rev, rev],
        out_specs=rev,
        out_shape=jax.ShapeDtypeStruct((b, l, wd), BF16),
        scratch_shapes=[pltpu.VMEM((bb, SUBLANES, wd), F32)],
        compiler_params=_cparams("parallel", "arbitrary"),
        name="lru_bwd",
    )(ab, ub, hf, lg)


def _merge_kernel(x_ref, mod_ref, p_ref, pp_ref, pn_ref, at_ref, lr_ref, wg_ref, bm_ref, pw_ref, ps_ref,
                  wb_ref, wo_ref, g_ref, b_ref, *rest, nct, nt, tile_off, ctx_len, alpha, route):
    if route:
        rw_ref, rb_ref, x1_ref, h2_ref, logit_ref, ext_ref = rest
    else:
        x1_ref, h2_ref, ext_ref = rest
    tile = pl.program_id(1) + tile_off
    t = x_ref.shape[1]
    d = x_ref.shape[2]
    first, last = _seg_flags(tile, nct, nt)
    in_ctx = tile < nct
    seg_len = jnp.where(in_ctx, ctx_len, nt * t - ctx_len)
    pos = (tile - jnp.where(in_ctx, 0, nct)) * t + lax.broadcasted_iota(jnp.int32, (t, 1), 0)

    for bb in range(x_ref.shape[0]):
        x = x_ref[bb]
        m = mod_ref[bb, 0]
        h_half = ((x * (1.0 + m[1:2]) + m[0:1]) * 0.5).astype(BF16)

        def branch(n, ys):
            tg = jnp.tanh(jnp.dot(h_half, wg_ref[:, n * d:(n + 1) * d], preferred_element_type=F32)
                          + bm_ref[n:n + 1, :])
            return (tg + 1.0) * jnp.dot(ys, wb_ref[n], preferred_element_type=F32)

        mix = branch(1, at_ref[bb]) + branch(2, lr_ref[bb])

        ext = ext_ref.at[bb]
        _fill_ext(ext, p_ref[bb], pp_ref[bb], pn_ref[bb], first, last)
        pooled = []
        for gi, w in enumerate(POOL_WINDOWS):
            lo = w // 2
            ev = ext[:, gi * POOL_GROUP:(gi + 1) * POOL_GROUP]
            n = ev.shape[0]
            run, span = ev, 1
            while span < lo:
                run = run + pltpu.roll(run, span, 0)
                span *= 2
            acc = pltpu.roll(run, 1, 0) + pltpu.roll(run, (1 - lo) % n, 0) if lo > 1 else run + pltpu.roll(run, 1, 0)
            cnt = jnp.clip(pos - lo + w, 0, seg_len) - jnp.clip(pos - lo, 0, seg_len)
            mean = acc[HALO:HALO + t] / cnt.astype(F32)
            dlt = (mean - ev[HALO:HALO + t]).astype(BF16)
            pooled.append(jnp.dot(dlt, pw_ref[gi], preferred_element_type=F32))
        pool_y = (jnp.concatenate(pooled, axis=1) * ps_ref[...]).astype(BF16)
        mix = branch(0, pool_y) + mix
        y2 = jnp.dot(mix.astype(BF16), wo_ref[...], preferred_element_type=F32)
        x1 = _layer_norm(alpha * x + m[6:7] * y2, g_ref[...], b_ref[...])
        x1_ref[bb] = x1
        h2 = x1 * (1.0 + m[4:5]) + m[3:4]
        if route:
            hi = h2.astype(BF16)
            mid = (h2 - hi.astype(F32)).astype(BF16)
            logit_ref[bb] = (jnp.dot(hi, rw_ref[0], preferred_element_type=F32)
                             + jnp.dot(mid, rw_ref[0], preferred_element_type=F32)
                             + jnp.dot(hi, rw_ref[1], preferred_element_type=F32) + rb_ref[...])
            _store_row_tiles(h2_ref.at[bb], h2)
        else:
            h2_ref[bb] = h2.astype(h2_ref.dtype)


def _merge(x, mods, p, attn, lru, w_gate, b_merge, pool_w, pool_scale, w_branch, w_out, ln_g, ln_b,
           nct, lat_only, alpha, router=None):
    b, l, d = x.shape
    nt = l // ROW_TILE
    off = nct if lat_only else 0
    rows = l - off * ROW_TILE
    tile_of = lambda i: i + off
    bb = BATCH_BLOCK
    tok_l = lambda w: pl.BlockSpec((bb, ROW_TILE, w), lambda bi, i: (bi, i + off, 0))
    tok_o = lambda w: pl.BlockSpec((bb, ROW_TILE, w), lambda bi, i: (bi, i, 0))
    const = lambda shp: pl.BlockSpec(shp, lambda bi, i: (0,) * len(shp), pipeline_mode=pl.Buffered(1))
    pm, pp, pn = _halo_specs(BRANCH_WIDTH, tile_of, l, bb)
    in_specs = [tok_l(d),
                pl.BlockSpec((bb, 1, SUBLANES, d), lambda bi, i: (bi, (i + off >= nct).astype(jnp.int32), 0, 0)),
                pm, pp, pn,
                tok_o(Q_WIDTH) if lat_only else tok_l(Q_WIDTH),
                tok_l(LRU_WIDTH),
                const(w_gate.shape), const(b_merge.shape), const(pool_w.shape), const(pool_scale.shape),
                const(w_branch.shape), const(w_out.shape), const((1, d)), const((1, d))]
    args = [x, mods, p, p, p, attn, lru, w_gate, b_merge, pool_w, pool_scale, w_branch, w_out, ln_g, ln_b]
    out_specs = [tok_o(d)]
    out_shape = [jax.ShapeDtypeStruct((b, rows, d), F32)]
    if router is None:
        out_specs.append(tok_o(d))
        out_shape.append(jax.ShapeDtypeStruct((b, rows, d), BF16))
    else:
        n_sub = d // LANES
        in_specs += [const(router[0].shape), const(router[1].shape)]
        args += list(router)
        out_specs += [pl.BlockSpec((bb, ROW_TILE * n_sub, LANES), lambda bi, i: (bi, i, 0)), tok_o(LANES)]
        out_shape += [jax.ShapeDtypeStruct((b, rows * n_sub, LANES), F32),
                      jax.ShapeDtypeStruct((b, rows, LANES), F32)]
    return pl.pallas_call(
        functools.partial(_merge_kernel, nct=nct, nt=nt, tile_off=off, ctx_len=nct * ROW_TILE, alpha=alpha,
                          route=router is not None),
        grid=(b // bb, rows // ROW_TILE),
        in_specs=in_specs,
        out_specs=out_specs,
        out_shape=out_shape,
        scratch_shapes=[pltpu.VMEM((bb, ROW_TILE + 2 * HALO, BRANCH_WIDTH), F32)],
        compiler_params=_cparams("parallel", "parallel"),
        name="merge_ln1",
    )(*args)


def _swiglu_kernel(x_ref, wg_ref, wu_ref, wd_ref, x1_ref, *rest, alpha):
    mod_refs, (g_ref, b_ref, o_ref) = rest[:-3], rest[-3:]
    x = x_ref[...]
    g = jnp.dot(x, wg_ref[...], preferred_element_type=F32)
    u = jnp.dot(x, wu_ref[...], preferred_element_type=F32)
    ff = jnp.dot((_silu(g) * u).astype(BF16), wd_ref[...], preferred_element_type=F32)
    for hh, mod_ref in enumerate(mod_refs):
        rs = slice(hh * ROW_TILE, (hh + 1) * ROW_TILE)
        o_ref[rs, :] = _layer_norm(alpha * x1_ref[rs, :] + mod_ref[0, 0, 5:6, :] * ff[rs], g_ref[...], b_ref[...])


def _swiglu_dense_ln(h2, x1, mods, ln_g, ln_b, w_gate, w_up, w_down, nct, alpha):
    b, l, d = x1.shape
    r = b * l
    ntl = l // ROW_TILE
    halves = FFN_ROWS // ROW_TILE

    def mod_spec(hh):
        def idx(i):
            tile = i * halves + hh
            return (tile // ntl, (tile % ntl >= nct).astype(jnp.int32), 0, 0)
        return pl.BlockSpec((1, 1, SUBLANES, d), idx)

    resident = lambda shp: pl.BlockSpec(shp, lambda i: (0, 0), pipeline_mode=pl.Buffered(1))
    out = pl.pallas_call(
        functools.partial(_swiglu_kernel, alpha=alpha),
        grid=(r // FFN_ROWS,),
        in_specs=[pl.BlockSpec((FFN_ROWS, d), lambda i: (i, 0)),
                  resident(w_gate.shape), resident(w_up.shape), resident(w_down.shape),
                  pl.BlockSpec((FFN_ROWS, d), lambda i: (i, 0))]
                 + [mod_spec(hh) for hh in range(halves)]
                 + [pl.BlockSpec((1, d), lambda i: (0, 0)), pl.BlockSpec((1, d), lambda i: (0, 0))],
        out_specs=pl.BlockSpec((FFN_ROWS, d), lambda i: (i, 0)),
        out_shape=jax.ShapeDtypeStruct((r, d), F32),
        compiler_params=_cparams("parallel"),
        name="swiglu_dense_ln2",
    )(h2.reshape(r, d), w_gate, w_up, w_down, x1.reshape(r, d), *([mods] * halves), ln_g, ln_b)
    return out.reshape(b, l, d)


def _moe_kernel(be_ref, nu_ref, tok_ref, dst_ref, t_hbm, wg_ref, wu_ref, wd_ref, y_hbm,
                xbuf, acc_ref, obuf, sem_in, sem_out, sem_fill):
    i = pl.program_id(0)
    f = pl.program_id(1)
    nb = pl.num_programs(0)
    nu = nu_ref[0]
    rows = acc_ref.shape[0]
    sub = xbuf.shape[1] // rows
    slot = i % 2

    def tile_rows(start):
        return pl.ds(pl.multiple_of(start, sub), sub)

    def gather_copy(blk, sl, r):
        return pltpu.make_async_copy(t_hbm.at[tile_rows(tok_ref[blk * rows + r])], xbuf.at[sl, tile_rows(r * sub)],
                                     sem_in.at[sl])

    def scatter_copy(blk, sl, r):
        return pltpu.make_async_copy(obuf.at[sl, tile_rows(r * sub)], y_hbm.at[tile_rows(dst_ref[blk * rows + r])],
                                     sem_out.at[sl])

    def gathers_done(sl):
        pltpu.make_async_copy(t_hbm.at[pl.ds(0, rows * sub)], xbuf.at[sl], sem_in.at[sl]).wait()

    def scatters_done(sl):
        pltpu.make_async_copy(obuf.at[sl], y_hbm.at[pl.ds(0, rows * sub)], sem_out.at[sl]).wait()

    def for_rows(fn):
        def body(r, c):
            fn(r)
            return c
        lax.fori_loop(0, rows, body, 0, unroll=8)

    def load_x():
        return _load_row_tiles(xbuf.at[slot], rows).astype(BF16)

    def swiglu_part(x):
        g = jnp.dot(x, wg_ref[0], preferred_element_type=F32)
        u = jnp.dot(x, wu_ref[0], preferred_element_type=F32)
        return jnp.dot((_silu(g) * u).astype(BF16), wd_ref[0], preferred_element_type=F32)

    @pl.when(jnp.logical_and(i == 0, f == 0))
    def _():
        obuf[1] = jnp.zeros(obuf.shape[1:], obuf.dtype)
        for_rows(lambda r: gather_copy(0, 0, r).start())

    @pl.when(jnp.logical_and(f == 0, i <= nu))
    def _():
        gathers_done(slot)

    @pl.when(jnp.logical_and(i < nu, f == 0))
    def _():
        x = load_x()
        for r in range(rows):
            gather_copy(i + 1, 1 - slot, r).start(priority=r % 2)
        acc_ref[...] = swiglu_part(x)

    @pl.when(jnp.logical_and(i < nu, f == 1))
    def _():
        @pl.when(i >= 1)
        def _():
            scatters_done(slot)

        x = load_x()
        prev = jnp.where(i >= 1, i - 1, nb)
        for r in range(rows):
            scatter_copy(prev, 1 - slot, r).start(priority=r % 2)
        _store_row_tiles(obuf.at[slot], acc_ref[...] + swiglu_part(x))

        @pl.when(i == nu - 1)
        def _():
            scatters_done(1 - slot)
            for_rows(lambda r: scatter_copy(i, slot, r).start())
            scatters_done(slot)

    @pl.when(jnp.logical_and(i >= nu, f == 1))
    def _():
        xbuf[0] = jnp.zeros(xbuf.shape[1:], xbuf.dtype)
        fill = pltpu.make_async_copy(xbuf.at[0], y_hbm.at[pl.ds(i * rows * sub, rows * sub)], sem_fill)
        fill.start()
        fill.wait()

    @pl.when(jnp.logical_and(jnp.logical_and(i == nb - 1, f == 1), nu == nb))
    def _():
        gathers_done(nb % 2)


def _swiglu_experts(t, block_e, n_used, slot_tok, slot_dst, w_gate, w_up, w_down, tf):
    d = w_gate.shape[1]
    sub = d // LANES
    p = slot_tok.shape[0]
    nf = w_gate.shape[2] // tf
    assert nf == 2

    def fcol(i, f, nu):
        return jnp.where(i < nu[0], f, nf - 1)

    return pl.pallas_call(
        _moe_kernel,
        grid_spec=pltpu.PrefetchScalarGridSpec(
            num_scalar_prefetch=4,
            grid=(p // FFN_ROWS - 1, nf),
            in_specs=[pl.BlockSpec(memory_space=pl.ANY),
                      pl.BlockSpec((1, d, tf), lambda i, f, be, nu, st, sd: (be[i], 0, fcol(i, f, nu))),
                      pl.BlockSpec((1, d, tf), lambda i, f, be, nu, st, sd: (be[i], 0, fcol(i, f, nu))),
                      pl.BlockSpec((1, tf, d), lambda i, f, be, nu, st, sd: (be[i], fcol(i, f, nu), 0))],
            out_specs=pl.BlockSpec(memory_space=pl.ANY),
            scratch_shapes=[pltpu.VMEM((2, FFN_ROWS * sub, LANES), F32), pltpu.VMEM((FFN_ROWS, d), F32),
                            pltpu.VMEM((2, FFN_ROWS * sub, LANES), F32),
                            pltpu.SemaphoreType.DMA((2,)), pltpu.SemaphoreType.DMA((2,)),
                            pltpu.SemaphoreType.DMA]),
        out_shape=jax.ShapeDtypeStruct((p * sub, LANES), F32),
        compiler_params=_cparams("arbitrary", "arbitrary"),
        name="swiglu_experts",
    )(block_e, n_used, slot_tok, slot_dst, t, w_gate, w_up, w_down)


def _combine_ln_kernel(x_ref, mod_ref, g_ref, b_ref, *rest, alpha):
    y_refs, p_ref, o_ref = rest[:-2], rest[-2], rest[-1]
    t = x_ref.shape[1]
    top_k = p_ref.shape[2]
    for bb in range(x_ref.shape[0]):
        f = None
        for kk in range(top_k):
            term = _load_row_tiles(y_refs[bb * top_k + kk], t) * p_ref[bb, :, kk:kk + 1]
            f = term if f is None else f + term
        m = mod_ref[bb, 0]
        o_ref[bb] = _layer_norm(alpha * x_ref[bb] + m[5:6] * f, g_ref[...], b_ref[...])


def _combine_ln(x1, mods, ln_g, ln_b, y, probs, alpha):
    b, rows, d = x1.shape
    nt = rows // ROW_TILE
    sub = d // LANES
    top_k = probs.shape[2]
    nbb = BATCH_BLOCK
    tok = lambda w: pl.BlockSpec((nbb, ROW_TILE, w), lambda bi, i: (bi, i, 0))
    const = lambda shp: pl.BlockSpec(shp, lambda bi, i: (0,) * len(shp))
    y_specs = [pl.BlockSpec((ROW_TILE * sub, LANES),
                            lambda bi, i, kk=kk, bb=bb: ((kk * b + bi * nbb + bb) * nt + i, 0))
               for bb in range(nbb) for kk in range(top_k)]
    return pl.pallas_call(
        functools.partial(_combine_ln_kernel, alpha=alpha),
        grid=(b // nbb, nt),
        in_specs=[tok(d), pl.BlockSpec((nbb, 1, SUBLANES, d), lambda bi, i: (bi, 1, 0, 0)),
                  const((1, d)), const((1, d))] + y_specs + [tok(top_k)],
        out_specs=tok(d),
        out_shape=jax.ShapeDtypeStruct((b, rows, d), F32),
        compiler_params=_cparams("parallel", "parallel"),
        name="combine_ln2",
    )(x1, mods, ln_g, ln_b, *([y] * (top_k * nbb)), probs)


def _moe(t, logits, w_gate, w_up, w_down, tf):
    n, n_e = logits.shape
    sub = t.shape[0] // n
    top_v, top_i = lax.top_k(logits, TOP_K)
    probs = jax.nn.softmax(top_v, axis=-1)
    m = n * TOP_K
    flat_e = top_i.reshape(m)
    counts = jnp.sum((flat_e[:, None] == jnp.arange(n_e, dtype=flat_e.dtype)[None, :]).astype(jnp.int32), axis=0)
    order = jnp.argsort(flat_e, stable=True).astype(jnp.int32)
    start = jnp.cumsum(counts) - counts
    padded = (counts + FFN_ROWS - 1) // FFN_ROWS * FFN_ROWS
    ends_p = jnp.cumsum(padded)
    start_p = ends_p - padded
    nb = -(-(m + n_e * (FFN_ROWS - 1)) // FFN_ROWS)
    p = (nb + 1) * FFN_ROWS
    blk_start = jnp.arange(nb + 1, dtype=jnp.int32) * FFN_ROWS
    block_e = jnp.minimum(jnp.sum((ends_p[None, :] <= blk_start[:, None]).astype(jnp.int32), axis=1), n_e - 1)
    n_used = (ends_p[-1:] // FFN_ROWS).astype(jnp.int32)
    slot_e = jnp.repeat(block_e, FFN_ROWS)
    r = jnp.arange(p, dtype=jnp.int32) - start_p[slot_e]
    is_pad = r >= counts[slot_e]
    slot_pair = order[jnp.clip(start[slot_e] + r, 0, m - 1)]
    slot_tok = jnp.where(is_pad, 0, slot_pair // TOP_K) * sub
    pads_before = (start_p - start)[slot_e] + r - counts[slot_e]
    slot_dst = jnp.where(is_pad, m + pads_before, (slot_pair % TOP_K) * n + slot_pair // TOP_K) * sub
    y = _swiglu_experts(t, block_e, n_used, slot_tok, slot_dst, w_gate, w_up, w_down, tf)
    return y, probs


def _block_diag(w):
    nb, bi, bj = w.shape
    eye = jnp.eye(nb, dtype=w.dtype)
    return (eye[:, None, :, None] * w[:, :, None, :]).reshape(nb * bi, nb * bj)


def _rope_tables(ctx_len, s):
    nf = HEAD_DIM // 4
    inv = ROPE_THETA ** (-jnp.arange(nf, dtype=F32) / nf)
    t = jnp.arange(s)
    row = (t // GRID_W).astype(F32)[:, None] * inv
    col = (t % GRID_W).astype(F32)[:, None] * inv
    ang = jnp.concatenate([row, row, col, col], axis=1)
    sign = jnp.tile(jnp.concatenate([-jnp.ones((nf,), F32), jnp.ones((nf,), F32)]), 2)
    cos = jnp.concatenate([jnp.ones((ctx_len, HEAD_DIM), F32), jnp.cos(ang)], axis=0)
    sin = jnp.concatenate([jnp.zeros((ctx_len, HEAD_DIM), F32), jnp.sin(ang) * sign], axis=0)
    reps = LANES // HEAD_DIM
    return jnp.tile(cos, (1, reps)), jnp.tile(sin, (1, reps))


def _pad_rows(a, rows):
    return jnp.zeros((rows,) + a.shape[1:], a.dtype).at[:a.shape[0]].set(a)


def kernel(x, c, ctx, c_ctx, w_mod, b_mod, w_in, b_merge, pool_w, pool_scale, q_norm, k_norm, conv_w, conv_b, lru_wa, lru_ba, lru_wx, lru_bx, lru_lambda, w_branch, w_out, ln1_g, ln1_b, ffn_w_gate, ffn_w_up, ffn_w_down, moe_router, moe_router_b, moe_w_gate, moe_w_up, moe_w_down, ln2_g, ln2_b):
    b, s, d = x.shape
    ctx_len = ctx.shape[1]
    depth = w_in.shape[0]
    l = ctx_len + s
    assert ctx_len % ROW_TILE == 0 and s % ROW_TILE == 0 and (b * s) % FFN_ROWS == 0
    assert b % BATCH_BLOCK == 0 and b % LRU_BATCH_BLOCK == 0
    assert (b * l) % FFN_ROWS == 0 and ctx_len % Q_TILE == 0 and s % Q_TILE == 0
    nct = ctx_len // ROW_TILE
    alpha = (2 * depth) ** 0.25

    cc = _pad_rows(jnp.concatenate([c, c_ctx[None, :]], axis=0), -(-(b + 1) // SUBLANES) * SUBLANES)
    mod_all = _modulation(cc, w_mod, b_mod)

    cos_t, sin_t = _rope_tables(ctx_len, s)
    bd = _block_diag(jnp.full((LANES // HEAD_DIM, HEAD_DIM, HEAD_DIM), 1.0 / HEAD_DIM, F32)).astype(BF16)
    reps = LANES // HEAD_DIM

    xs = None
    w_in_bf = w_in[0].astype(BF16)
    for li in range(depth):
        last = li == depth - 1
        dense = li % 2 == 0
        jf = li // 2
        ml = jnp.transpose(mod_all[li, :, :b], (1, 0, 2))
        mc = jnp.broadcast_to(mod_all[li, :, b][None], (b, 6, d))
        mods = jnp.stack([mc, ml], axis=1)
        mods = jnp.concatenate([mods, 0.5 * mods[:, :, 2:3], jnp.zeros((b, 2, 1, d), F32)], axis=2)

        proj_args = (mods, w_in_bf[:, :OFF_GATE], bd, jnp.tile(q_norm[li], reps)[None],
                     jnp.tile(k_norm[li], reps)[None], cos_t, sin_t, nct)
        if xs is None:
            p, q, k, v, lx, lg, xs = _project(x, *proj_args, ctx=ctx)
        else:
            p, q, k, v, lx, lg = _project(xs, *proj_args)

        slab = [w_branch[li].reshape(-1, d), w_out[li]]
        if dense:
            slab += [ffn_w_gate[jf], ffn_w_up[jf], ffn_w_down[jf]]
            step = []
        else:
            step = [moe_w_gate[jf], moe_w_up[jf], moe_w_down[jf]]
        if not last:
            slab.append(w_in[li + 1])
        if not all(w.shape[0] == b and w.shape[1] % ((l - (ctx_len if last else 0)) // Q_TILE * 16) == 0
                   for w in step):
            step_bf, step = [w.astype(BF16) for w in step], []
        attn, *made = _attention(q, k, v, ctx_len, with_ctx=not last, step_casts=step, slab_casts=slab)
        if step:
            step_bf = made[:len(step)]
        slab_bf = made[len(step):]
        wb_bf, wo_bf = slab_bf[0].reshape(w_branch[li].shape), slab_bf[1]

        cw = _pad_rows(0.5 * conv_w[li], SUBLANES)
        cb = 0.5 * conv_b[li][None]
        lru_w = jnp.concatenate([_block_diag(lru_wa[li, dr]) if kind == 0 else _block_diag(lru_wx[li, dr])
                                 for dr in range(2) for kind in range(2)], axis=1).astype(BF16)
        lru_y = _lru(lx, lg, lru_w, 0.5 * lru_ba[li], 0.5 * lru_bx[li], lru_lambda[li], cw, cb, nct)

        merge_args = (xs, mods, p, attn, lru_y, w_in_bf[:, OFF_GATE:], _pad_rows(0.5 * b_merge[li], SUBLANES),
                      pool_w[li].astype(BF16), pool_scale[li][None], wb_bf, wo_bf, ln1_g[li][None], ln1_b[li][None])
        if dense:
            assert not last, "the dense layer is expected to carry the context tokens along"
            x1, h2 = _merge(*merge_args, nct, lat_only=False, alpha=alpha)
            xs = _swiglu_dense_ln(h2, x1, mods, ln2_g[li][None], ln2_b[li][None], *slab_bf[2:5],
                                  nct=nct, alpha=alpha)
        else:
            assert last, "the expert layer is expected to be the last layer (latent tokens only)"
            n_e = moe_router.shape[2]
            rw = jnp.zeros((d, LANES), F32).at[:, :n_e].set(moe_router[jf])
            rb = jnp.zeros((1, LANES), F32).at[0, :n_e].set(moe_router_b[jf])
            rw_hi = rw.astype(BF16)
            rw_split = jnp.stack([rw_hi, (rw - rw_hi.astype(F32)).astype(BF16)])
            x1, h2, logits = _merge(*merge_args, nct, lat_only=True, alpha=alpha, router=(rw_split, rb))
            y, probs = _moe(h2.reshape(-1, LANES), logits.reshape(b * s, LANES)[:, :n_e], *step_bf,
                            tf=moe_w_gate.shape[3] // 2)
            xs = _combine_ln(x1, mods, ln2_g[li][None], ln2_b[li][None], y, probs.reshape(b, s, -1), alpha)
        if not last:
            w_in_bf = slab_bf[-1]
    return xs
```
